```python
import math
import jax, jax.numpy as jnp
from jax import lax
import numpy as np

D_MODEL = 1024
BATCH = 2
SEQ = 8192
DEPTH = 2

CTX_LEN = 256
GRID_W = 64
EPS = 1e-6
N_DIR = 2

W_BRANCH = D_MODEL // 4
D_MIX = 4 * W_BRANCH

ML_HEADS = 4
ML_DH = W_BRANCH // ML_HEADS
ML_CHUNK = 128

DA_HEADS = 4
DA_V = W_BRANCH // DA_HEADS
DA_QK = DA_V // 2
Q_BLOCK = 128
ROPE_BASE = 10000.0
ROPE_AXIS = DA_QK // 2

SSM_HEADS = 4
SSM_P = W_BRANCH // SSM_HEADS
SSM_D_INNER = SSM_HEADS * SSM_P
SSM_GROUPS = 2
SSM_N = 128
SSM_CONV = 3
SSM_CHUNK = 128
SSM_XBC = SSM_D_INNER + 2 * SSM_GROUPS * SSM_N

HG_HEADS = 4
HG_DK = W_BRANCH // HG_HEADS
HG_DV = W_BRANCH // HG_HEADS
HG_CHUNK = 16

SEGMENTS = (
    ("ml_q", ML_HEADS * ML_DH), ("ml_k", ML_HEADS * ML_DH), ("ml_v", ML_HEADS * ML_DH),
    ("ml_o", ML_HEADS * ML_DH), ("ml_i", N_DIR * ML_HEADS), ("ml_f", N_DIR * ML_HEADS), ("ml_z", W_BRANCH),
    ("da_q", DA_HEADS * 2 * DA_QK), ("da_k", DA_HEADS * 2 * DA_QK), ("da_v", DA_HEADS * DA_V), ("da_z", W_BRANCH),
    ("ss_xbc", SSM_XBC), ("ss_dt", N_DIR * SSM_HEADS), ("ss_z", W_BRANCH),
    ("hg_q", HG_HEADS * HG_DK), ("hg_f", N_DIR * HG_HEADS * HG_DK), ("hg_i", HG_HEADS * HG_DV), ("hg_z", W_BRANCH),
)
D_IN = sum(n for _, n in SEGMENTS)

kernel_name = "hybrid_mlstm_diffattn_ssd_hgrn2_prefix_dit"

F32 = jnp.float32


def rmsnorm(x, g):
    xf = x.astype(F32)
    y = xf * lax.rsqrt(jnp.mean(xf * xf, axis=-1, keepdims=True) + EPS)
    return (y * g).astype(x.dtype)


def head_layernorm(x, g):
    xf = x.astype(F32)
    mu = jnp.mean(xf, axis=-1, keepdims=True)
    var = jnp.mean(jnp.square(xf - mu), axis=-1, keepdims=True)
    return (xf - mu) * lax.rsqrt(var + EPS) * g


def split_cols(u):
    out, off = {}, 0
    for name, n in SEGMENTS:
        out[name] = u[..., off:off + n]
        off += n
    return out


def axial_rope(n_rows):
    rows = jnp.repeat(jnp.arange(n_rows), GRID_W).astype(F32)
    cols = (jnp.arange(n_rows * GRID_W) % GRID_W).astype(F32)
    inv = ROPE_BASE ** (-jnp.arange(0, ROPE_AXIS, 2, dtype=F32) / ROPE_AXIS)
    ang = jnp.concatenate([rows[:, None] * inv, cols[:, None] * inv], axis=-1)
    return jnp.cos(ang), jnp.sin(ang)


def apply_rope(x, cos, sin):
    half = x.shape[-1] // 2
    x1, x2 = x[..., :half], x[..., half:]
    cos = cos[None, :, None, None, :]
    sin = sin[None, :, None, None, :]
    return jnp.concatenate([x1 * cos - x2 * sin, x1 * sin + x2 * cos], axis=-1)


def dwconv_centred(x, w, b):
    k = w.shape[0]
    y = lax.conv_general_dilated(x, w[:, None, :].astype(x.dtype), window_strides=(1,),
                                 padding=[((k - 1) // 2, k // 2)],
                                 dimension_numbers=("NWC", "WIO", "NWC"),
                                 feature_group_count=x.shape[-1])
    return y + b


def chunk_state_scan(decay, incr, s0):
    def step(s, inp):
        d, u = inp
        return d * s + u, s
    final, starts = lax.scan(step, s0, (jnp.moveaxis(decay, 1, 0), jnp.moveaxis(incr, 1, 0)))
    return jnp.moveaxis(starts, 0, 1), final


def bidirectional(run, ctx_fwd, lat_fwd, ctx_bwd, lat_bwd, s0, ctx_out):
    flip = lambda args: tuple(jnp.flip(a, axis=1) for a in args)
    yc_f, sc_f = run(ctx_fwd, s0, ctx_out)
    yl_f, _ = run(lat_fwd, sc_f, True)
    yc_b, sc_b = run(flip(ctx_bwd), s0, ctx_out)
    yl_b, _ = run(flip(lat_bwd), sc_b, True)
    y_lat = yl_f + jnp.flip(yl_b, axis=1)
    y_ctx = yc_f + jnp.flip(yc_b, axis=1) if ctx_out else None
    return y_lat, y_ctx


def mlstm_run(args, state, with_out):
    q, k, v, li, lf = args
    bsz, t, h, d = q.shape
    L = ML_CHUNK
    nc = t // L
    r = lambda a: a.reshape(bsz, nc, L, *a.shape[2:])
    q, k, v, li, lf = (r(a) for a in (q, k, v, li, lf))
    F = jnp.cumsum(lf, axis=2)
    FL = F[:, :, -1]
    w_end = FL[:, :, None] - F + li
    b_end = jnp.max(w_end, axis=2)
    e = jnp.exp(w_end - b_end[:, :, None])
    dC = jnp.einsum('bclh,bclhk,bclhv->bchkv', e, k, v)
    dn = jnp.einsum('bclh,bclhk->bchk', e, k)

    def step(carry, inp):
        C, n, m = carry
        fl, bb, dc, dnn = inp
        m_new = jnp.maximum(fl + m, bb)
        a = jnp.exp(fl + m - m_new)
        g = jnp.exp(bb - m_new)
        return (a[..., None, None] * C + g[..., None, None] * dc, a[..., None] * n + g[..., None] * dnn, m_new), (C, n, m)

    tm = lambda a: jnp.moveaxis(a, 1, 0)
    final, (Cs, ns, ms) = lax.scan(step, state, (tm(FL), tm(b_end), tm(dC), tm(dn)))
    if not with_out:
        return None, final
    Cs, ns, ms = tm(Cs), tm(ns), tm(ms)
    mask = jnp.tril(jnp.ones((L, L), bool))
    Dm = jnp.where(mask[:, :, None], F[:, :, :, None, :] - F[:, :, None, :, :] + li[:, :, None, :, :], -jnp.inf)
    a_int = F + ms[:, :, None, :]
    m_i = jnp.maximum(a_int, jnp.max(Dm, axis=3))
    P = jnp.exp(Dm - m_i[:, :, :, None, :])
    s = jnp.einsum('bcihd,bcjhd->bcijh', q, k) * P
    w_int = jnp.exp(a_int - m_i)
    num = jnp.einsum('bcijh,bcjhv->bcihv', s, v) + w_int[..., None] * jnp.einsum('bcihk,bchkv->bcihv', q, Cs)
    den = jnp.sum(s, axis=3) + w_int * jnp.einsum('bcihk,bchk->bcih', q, ns)
    hh = num / jnp.maximum(jnp.abs(den), jnp.exp(-m_i))[..., None]
    return hh.reshape(bsz, t, h, d), final


def mlstm_branch(s_lat, s_ctx, gate_b, norm_g, ctx_out):
    def prep(s):
        bsz, t, _ = s["ml_q"].shape
        heads = lambda a: a.reshape(bsz, t, ML_HEADS, ML_DH)
        q, k, v = heads(s["ml_q"]), heads(s["ml_k"]) * (ML_DH ** -0.5), heads(s["ml_v"])
        ig = s["ml_i"].reshape(bsz, t, N_DIR, ML_HEADS).astype(F32) + gate_b[:, 0]
        lf = jax.nn.log_sigmoid(s["ml_f"].reshape(bsz, t, N_DIR, ML_HEADS).astype(F32) + gate_b[:, 1])
        return (q, k, v, ig[:, :, 0], lf[:, :, 0]), (q, k, v, ig[:, :, 1], lf[:, :, 1])
    lat_f, lat_b = prep(s_lat)
    ctx_f, ctx_b = prep(s_ctx)
    bsz = s_lat["ml_q"].shape[0]
    s0 = (jnp.zeros((bsz, ML_HEADS, ML_DH, ML_DH), F32), jnp.zeros((bsz, ML_HEADS, ML_DH), F32),
          jnp.zeros((bsz, ML_HEADS), F32))
    h_lat, h_ctx = bidirectional(mlstm_run, ctx_f, lat_f, ctx_b, lat_b, s0, ctx_out)

    def finish(hh, s):
        o = jax.nn.sigmoid(s["ml_o"].astype(F32)).reshape(hh.shape)
        y = head_layernorm(o * hh, norm_g.reshape(ML_HEADS, ML_DH))
        return y.reshape(*y.shape[:2], -1) * jax.nn.silu(s["ml_z"])
    return finish(h_lat, s_lat), (finish(h_ctx, s_ctx) if ctx_out else None)


def diff_attend(q, k, v, lam):
    s = jnp.einsum('bqhmd,bkhmd->bhmqk', q, k).astype(F32) * (DA_QK ** -0.5)
    p = jax.nn.softmax(s, axis=-1)
    w = p[:, :, 0] - lam * p[:, :, 1]
    return jnp.einsum('bhqk,bkhv->bqhv', w, v)


def diff_attn_branch(s_lat, s_ctx, lam_vecs, norm_g, lam_init, rope_cos, rope_sin, ctx_out):
    def prep(s):
        bsz, t, _ = s["da_q"].shape
        q = s["da_q"].reshape(bsz, t, DA_HEADS, 2, DA_QK)
        k = s["da_k"].reshape(bsz, t, DA_HEADS, 2, DA_QK)
        v = s["da_v"].reshape(bsz, t, DA_HEADS, DA_V)
        return q, k, v
    ql, kl, vl = prep(s_lat)
    qc, kc, vc = prep(s_ctx)
    lv = lam_vecs.astype(F32)
    lam = jnp.exp(jnp.sum(lv[0] * lv[1])) - jnp.exp(jnp.sum(lv[2] * lv[3])) + lam_init
    ql = apply_rope(ql, rope_cos, rope_sin)
    kl = apply_rope(kl, rope_cos, rope_sin)
    k_all = jnp.concatenate([kl, kc], axis=1)
    v_all = jnp.concatenate([vl, vc], axis=1)
    bsz, t = ql.shape[:2]
    n_blk = t // Q_BLOCK
    q_blocks = jnp.moveaxis(ql.reshape(bsz, n_blk, Q_BLOCK, DA_HEADS, 2, DA_QK), 1, 0)
    o_lat = lax.map(lambda qb: diff_attend(qb, k_all, v_all, lam), q_blocks)
    o_lat = jnp.moveaxis(o_lat, 0, 1).reshape(bsz, t, DA_HEADS, DA_V)

    def finish(o, s):
        o = rmsnorm(o, norm_g) * (1.0 - lam_init)
        return o.reshape(*o.shape[:2], -1) * jax.nn.silu(s["da_z"])
    y_ctx = finish(diff_attend(qc, kc, vc, lam), s_ctx) if ctx_out else None
    return finish(o_lat, s_lat), y_ctx


def ssd_run(args, s0, with_out):
    xh, dt, a, Bh, Ch = args
    bsz, t, h, p = xh.shape
    L = SSM_CHUNK
    nc = t // L
    r = lambda z: z.reshape(bsz, nc, L, *z.shape[2:])
    xh, dt, a, Bh, Ch = (r(z) for z in (xh, dt, a, Bh, Ch))
    Acum = jnp.cumsum(a, axis=2)
    AL = Acum[:, :, -1]
    dS = jnp.einsum('bclh,bclhn,bclhp->bchnp', jnp.exp(AL[:, :, None] - Acum) * dt, Bh, xh)
    starts, final = chunk_state_scan(jnp.exp(AL)[..., None, None], dS, s0)
    if not with_out:
        return None, final
    mask = jnp.tril(jnp.ones((L, L), bool))
    Lm = jnp.exp(jnp.where(mask[:, :, None], Acum[:, :, :, None, :] - Acum[:, :, None, :, :], -jnp.inf))
    sc = jnp.einsum('bcihn,bcjhn->bcijh', Ch, Bh) * Lm * dt[:, :, None]
    y = jnp.einsum('bcijh,bcjhp->bcihp', sc, xh) + jnp.exp(Acum)[..., None] * jnp.einsum('bcihn,bchnp->bcihp', Ch, starts)
    return y.reshape(bsz, t, h, p), final


def ssd_branch(s_lat, s_ctx, conv_w, conv_b, dt_bias, a_log, d_skip, norm_g, ctx_out):
    A = -jnp.exp(a_log.astype(F32))
    rep = SSM_HEADS // SSM_GROUPS

    def prep(s):
        xbc = jax.nn.silu(dwconv_centred(s["ss_xbc"], conv_w, conv_b))
        bsz, t, _ = xbc.shape
        xh = xbc[..., :SSM_D_INNER].reshape(bsz, t, SSM_HEADS, SSM_P)
        Bm = xbc[..., SSM_D_INNER:SSM_D_INNER + SSM_GROUPS * SSM_N].reshape(bsz, t, SSM_GROUPS, SSM_N)
        Cm = xbc[..., SSM_D_INNER + SSM_GROUPS * SSM_N:].reshape(bsz, t, SSM_GROUPS, SSM_N)
        Bh = jnp.repeat(Bm, rep, axis=2)
        Ch = jnp.repeat(Cm, rep, axis=2)
        dt = jax.nn.softplus(s["ss_dt"].reshape(bsz, t, N_DIR, SSM_HEADS).astype(F32) + dt_bias)
        fwd = (xh, dt[:, :, 0], dt[:, :, 0] * A[0], Bh, Ch)
        bwd = (xh, dt[:, :, 1], dt[:, :, 1] * A[1], Bh, Ch)
        return fwd, bwd, xh
    lat_f, lat_b, xl = prep(s_lat)
    ctx_f, ctx_b, xc = prep(s_ctx)
    bsz = xl.shape[0]
    s0 = jnp.zeros((bsz, SSM_HEADS, SSM_N, SSM_P), F32)
    y_lat, y_ctx = bidirectional(ssd_run, ctx_f, lat_f, ctx_b, lat_b, s0, ctx_out)

    def finish(y, xh, s):
        y = y + d_skip[:, None] * xh
        bs, t = y.shape[:2]
        y = y.reshape(bs, t, SSM_D_INNER) * jax.nn.silu(s["ss_z"])
        y = rmsnorm(y.reshape(bs, t, SSM_GROUPS, SSM_D_INNER // SSM_GROUPS), norm_g.reshape(SSM_GROUPS, -1))
        return y.reshape(bs, t, SSM_D_INNER)
    return finish(y_lat, xl, s_lat), (finish(y_ctx, xc, s_ctx) if ctx_out else None)


def hgrn2_run(args, s0, with_out):
    q, lg, kk, v = args
    bsz, t, h, dk = q.shape
    L = HG_CHUNK
    nc = t // L
    r = lambda z: z.reshape(bsz, nc, L, *z.shape[2:])
    q, lg, kk, v = (r(z) for z in (q, lg, kk, v))
    G = jnp.cumsum(lg, axis=2)
    GL = G[:, :, -1]
    dS = jnp.einsum('bclhk,bclhv->bchkv', kk * jnp.exp(GL[:, :, None] - G), v)
    starts, final = chunk_state_scan(jnp.exp(GL)[..., None], dS, s0)
    if not with_out:
        return None, final
    mask = jnp.tril(jnp.ones((L, L), bool))
    dec = jnp.exp(jnp.where(mask[:, :, None, None], G[:, :, :, None] - G[:, :, None, :], -jnp.inf))
    Aw = jnp.einsum('bcihk,bcjhk,bcijhk->bcijh', q, kk, dec)
    o = jnp.einsum('bcijh,bcjhv->bcihv', Aw, v) + jnp.einsum('bcihk,bchkv->bcihv', q * jnp.exp(G), starts)
    return o.reshape(bsz, t, h, -1), final


def hgrn2_branch(s_lat, s_ctx, lb, norm_g, ctx_out):
    lbh = lb.astype(F32).reshape(N_DIR, HG_HEADS, HG_DK)
    log_lb, log_1mlb = jnp.log(lbh), jnp.log1p(-lbh)

    def prep(s):
        bsz, t, _ = s["hg_q"].shape
        q = jax.nn.silu(s["hg_q"]).reshape(bsz, t, HG_HEADS, HG_DK)
        z = s["hg_f"].reshape(bsz, t, N_DIR, HG_HEADS, HG_DK).astype(F32)
        logf = jnp.logaddexp(log_lb, log_1mlb + jax.nn.log_sigmoid(z))
        kk = (1.0 - lbh) * jax.nn.sigmoid(-z)
        v = s["hg_i"].reshape(bsz, t, HG_HEADS, HG_DV)
        return (q, logf[:, :, 0], kk[:, :, 0], v), (q, logf[:, :, 1], kk[:, :, 1], v)
    lat_f, lat_b = prep(s_lat)
    ctx_f, ctx_b = prep(s_ctx)
    bsz = s_lat["hg_q"].shape[0]
    s0 = jnp.zeros((bsz, HG_HEADS, HG_DK, HG_DV), F32)
    o_lat, o_ctx = bidirectional(hgrn2_run, ctx_f, lat_f, ctx_b, lat_b, s0, ctx_out)

    def finish(o, s):
        o = rmsnorm(o, norm_g.reshape(HG_HEADS, HG_DV))
        return o.reshape(*o.shape[:2], -1) * jax.nn.silu(s["hg_z"])
    return finish(o_lat, s_lat), (finish(o_ctx, s_ctx) if ctx_out else None)


def hybrid_layer(x, ctx, c, c_ctx, w_mod, b_mod, norm_g, w_in, w_out,
                 ml_gate_b, ml_norm_g, da_lambda, da_norm_g,
                 ss_conv_w, ss_conv_b, ss_dt_bias, ss_a_log, ss_d, ss_norm_g,
                 hg_lb, hg_norm_g, lam_init, rope_cos, rope_sin, ctx_out):
    shift, scale, gate = jnp.split(jax.nn.silu(c) @ w_mod + b_mod, 3, axis=-1)
    shift_c, scale_c, gate_c = jnp.split(jax.nn.silu(c_ctx) @ w_mod + b_mod, 3, axis=-1)
    h_lat = rmsnorm(x, norm_g) * (1.0 + scale[:, None]) + shift[:, None]
    h_ctx = rmsnorm(ctx, norm_g) * (1.0 + scale_c) + shift_c
    s_lat = split_cols(h_lat @ w_in)
    s_ctx = split_cols(h_ctx @ w_in)
    branches = (
        mlstm_branch(s_lat, s_ctx, ml_gate_b, ml_norm_g, ctx_out),
        diff_attn_branch(s_lat, s_ctx, da_lambda, da_norm_g, lam_init, rope_cos, rope_sin, ctx_out),
        ssd_branch(s_lat, s_ctx, ss_conv_w, ss_conv_b, ss_dt_bias, ss_a_log, ss_d, ss_norm_g, ctx_out),
        hgrn2_branch(s_lat, s_ctx, hg_lb, hg_norm_g, ctx_out),
    )
    y_lat = jnp.concatenate([br[0] for br in branches], axis=-1) @ w_out
    x = x + gate[:, None] * y_lat
    if ctx_out:
        y_ctx = jnp.concatenate([br[1] for br in branches], axis=-1) @ w_out
        ctx = ctx + gate_c * y_ctx
    return x, ctx


def setup_inputs(seed: int = 0) -> dict:
    key = jax.random.key(seed)
    ks = jax.random.split(key, 24)
    nrm = lambda k, shape, s: jax.random.normal(k, shape, F32) * s
    x = nrm(ks[0], (BATCH, SEQ, D_MODEL), 1.0)
    c = nrm(ks[1], (BATCH, D_MODEL), 1.0)
    ctx = nrm(ks[2], (BATCH, CTX_LEN, D_MODEL), 1.0)
    c_ctx = nrm(ks[3], (D_MODEL,), 1.0)
    w_mod = nrm(ks[4], (DEPTH, D_MODEL, 3 * D_MODEL), 0.5 * D_MODEL ** -0.5)
    b_mod = nrm(ks[5], (DEPTH, 3 * D_MODEL), 0.02)
    norm_g = 1.0 + nrm(ks[6], (DEPTH, D_MODEL), 0.02)
    w_in = nrm(ks[7], (DEPTH, D_MODEL, D_IN), D_MODEL ** -0.5)
    w_out = nrm(ks[8], (DEPTH, D_MIX, D_MODEL), D_MIX ** -0.5)
    f_bias = jnp.linspace(3.0, 6.0, ML_HEADS, dtype=F32)
    ml_gate_b = jnp.stack([nrm(ks[9], (DEPTH, N_DIR, ML_HEADS), 0.1),
                           f_bias + nrm(ks[10], (DEPTH, N_DIR, ML_HEADS), 0.1)], axis=2)
    ml_norm_g = 1.0 + nrm(ks[11], (DEPTH, W_BRANCH), 0.02)
    da_lambda = nrm(ks[12], (DEPTH, 4, DA_QK), 0.1)
    da_norm_g = 1.0 + nrm(ks[13], (DEPTH, DA_V), 0.02)
    ss_conv_w = nrm(ks[14], (DEPTH, SSM_CONV, SSM_XBC), SSM_CONV ** -0.5)
    ss_conv_b = nrm(ks[15], (DEPTH, SSM_XBC), 0.02)
    dt0 = jnp.exp(jax.random.uniform(ks[16], (DEPTH, N_DIR, SSM_HEADS), F32, math.log(1e-3), math.log(1e-1)))
    ss_dt_bias = dt0 + jnp.log(-jnp.expm1(-dt0))
    ss_a_log = jnp.log(jax.random.uniform(ks[17], (DEPTH, N_DIR, SSM_HEADS), F32, 1.0, 16.0))
    ss_d = 1.0 + nrm(ks[18], (DEPTH, SSM_HEADS), 0.1)
    ss_norm_g = 1.0 + nrm(ks[19], (DEPTH, W_BRANCH), 0.02)
    hg_lower = nrm(ks[20], (N_DIR, DEPTH, HG_HEADS * HG_DK), 1.0)
    hg_norm_g = 1.0 + nrm(ks[21], (DEPTH, W_BRANCH), 0.02)
    final_g = 1.0 + nrm(ks[22], (D_MODEL,), 0.02)
    return {"x": x, "c": c, "ctx": ctx, "c_ctx": c_ctx, "w_mod": w_mod, "b_mod": b_mod,
            "norm_g": norm_g, "w_in": w_in, "w_out": w_out, "ml_gate_b": ml_gate_b,
            "ml_norm_g": ml_norm_g, "da_lambda": da_lambda, "da_norm_g": da_norm_g,
            "ss_conv_w": ss_conv_w, "ss_conv_b": ss_conv_b, "ss_dt_bias": ss_dt_bias,
            "ss_a_log": ss_a_log, "ss_d": ss_d, "ss_norm_g": ss_norm_g, "hg_lower": hg_lower,
            "hg_norm_g": hg_norm_g, "final_g": final_g}


def reference(x, c, ctx, c_ctx, w_mod, b_mod, norm_g, w_in, w_out, ml_gate_b, ml_norm_g,
              da_lambda, da_norm_g, ss_conv_w, ss_conv_b, ss_dt_bias, ss_a_log, ss_d, ss_norm_g,
              hg_lower, hg_norm_g, final_g):
    n_rows = x.shape[1] // GRID_W
    rope_cos, rope_sin = axial_rope(n_rows)
    lb_all = jnp.cumsum(jax.nn.softmax(hg_lower.astype(F32), axis=1), axis=1)
    lb_all = lb_all - lb_all[:, :1]
    for l in range(DEPTH):
        lam_init = 0.8 - 0.6 * math.exp(-0.3 * l)
        x, ctx = hybrid_layer(x, ctx, c, c_ctx, w_mod[l], b_mod[l], norm_g[l], w_in[l], w_out[l],
                              ml_gate_b[l], ml_norm_g[l], da_lambda[l], da_norm_g[l],
                              ss_conv_w[l], ss_conv_b[l], ss_dt_bias[l], ss_a_log[l], ss_d[l], ss_norm_g[l],
                              lb_all[:, l], hg_norm_g[l], lam_init, rope_cos, rope_sin,
                              l < DEPTH - 1)
    return rmsnorm(x, final_g)
```

```python
import functools
import math

import jax
import jax.numpy as jnp
from jax import lax
from jax.experimental import pallas as pl
from jax.experimental.pallas import tpu as pltpu

F32 = jnp.float32
BF16 = jnp.bfloat16

EPS = 1e-6
GRID_W = 64
ROPE_BASE = 10000.0
N_HEADS = 4
HEAD_W = 64
W_BRANCH = N_HEADS * HEAD_W
DA_QK = 32
SSM_N = 128
CHUNK = 128
ROW_TILE = 256
SUB = 16
LANES = 128
VMEM_LIMIT = 56 * 1024 * 1024

C_ML = 0
C_DAQK = 1280
C_DAV = 1792
C_DAZ = 2048
C_DAROT = 2304
C_SS = 2816
C_HG = 3840
C_GATE = 5120
C_END = 5248

NT = (((1,), (1,)), ((), ()))
TN = (((0,), (0,)), ((), ()))


def _mm(a, b):
    return jnp.dot(a, b, preferred_element_type=F32)


def _mm_nt(a, b):
    return lax.dot_general(a, b, NT, preferred_element_type=F32)


def _mm_tn(a, b):
    return lax.dot_general(a, b, TN, preferred_element_type=F32)


def _split3(x):
    x0 = x.astype(BF16)
    r = x - x0.astype(F32)
    x1 = r.astype(BF16)
    x2 = (r - x1.astype(F32)).astype(BF16)
    return x0, x1, x2


def _mm_exact_l(t, x):
    x0, x1, x2 = _split3(x)
    return _mm(t, x0) + _mm(t, x1) + _mm(t, x2)


def _mm_exact_r(x, t):
    x0, x1, x2 = _split3(x)
    return _mm(x0, t) + _mm(x1, t) + _mm(x2, t)


def _sigmoid(x):
    return 1.0 / (1.0 + jnp.exp(-x))


def _silu(x):
    return x * _sigmoid(x)


def _log_sigmoid(x):
    return jnp.minimum(x, 0.0) - jnp.log1p(jnp.exp(-jnp.abs(x)))


def _softplus(x):
    return jnp.maximum(x, 0.0) + jnp.log1p(jnp.exp(-jnp.abs(x)))


def _iota2(shape, axis):
    return lax.broadcasted_iota(jnp.int32, shape, axis)


def _cparams(sem):
    return pltpu.CompilerParams(dimension_semantics=sem, vmem_limit_bytes=VMEM_LIMIT)


def _mod_kernel(c_ref, w_ref, b_ref, o_ref):
    c = c_ref[...]
    o_ref[...] = jnp.dot(_silu(c), w_ref[...], precision=lax.Precision.HIGHEST,
                         preferred_element_type=F32) + b_ref[...]


def _modulation(cc, w_mod, b_mod):
    d = cc.shape[1]
    n = w_mod.shape[1]
    tn = 512
    return pl.pallas_call(
        _mod_kernel,
        grid=(n // tn,),
        in_specs=[pl.BlockSpec((8, d), lambda j: (0, 0)),
                  pl.BlockSpec((d, tn), lambda j: (0, j)),
                  pl.BlockSpec((1, tn), lambda j: (0, j))],
        out_specs=pl.BlockSpec((8, tn), lambda j: (0, j)),
        out_shape=jax.ShapeDtypeStruct((8, n), F32),
        compiler_params=_cparams(("arbitrary",)),
        name="modulation",
    )(cc, w_mod, b_mod.reshape(1, n))


def _in_kernel(n_ctx_tiles, ctx_row, x_ref, mo_ref, g_ref, cos_ref, sin_ref, w_ref, wgt_ref,
               ml_ref, daq_ref, dak_ref, dav_ref, daz_ref, ss_ref, hg_ref, gc_ref, gt_ref):
    b = pl.program_id(0)
    i = pl.program_id(1)
    d = x_ref.shape[2]
    r = jnp.where(i < n_ctx_tiles, ctx_row, b)
    mo = mo_ref[pl.ds(r, 1), :]
    shift = mo[:, 0:d]
    scale = mo[:, d:2 * d]
    x = x_ref[0]
    y = x * lax.rsqrt(jnp.mean(x * x, axis=-1, keepdims=True) + EPS) * g_ref[...]
    h = (y * (1.0 + scale) + shift).astype(BF16)

    ml_ref[0] = _mm(h, w_ref[:, C_ML:C_DAQK])
    qk = _mm(h, w_ref[:, C_DAQK:C_DAV])
    qk_rot = _mm(h, w_ref[:, C_DAROT:C_SS])
    cos = cos_ref[...]
    sin = sin_ref[...]
    q = qk[:, 0:W_BRANCH] * cos + qk_rot[:, 0:W_BRANCH] * sin
    k = qk[:, W_BRANCH:] * cos + qk_rot[:, W_BRANCH:] * sin
    daq_ref[0] = (q * (DA_QK ** -0.5)).astype(BF16)
    dak_ref[0] = k.astype(BF16)
    dav_ref[0] = _mm(h, w_ref[:, C_DAV:C_DAZ]).astype(BF16)
    daz_ref[0] = _mm(h, w_ref[:, C_DAZ:C_DAROT])
    ss_ref[0] = _mm(h, w_ref[:, C_SS:C_HG])
    hg_ref[0] = _mm(h, w_ref[:, C_HG:C_GATE])
    gc_ref[0] = _mm(h, w_ref[:, C_GATE:C_END])
    gt_ref[0] = _mm_nt(wgt_ref[...], h)


def _in_proj(xs, mo, g, cos, sin, w2, wgt, n_ctx_tiles, ctx_row):
    bsz, t, d = xs.shape
    tm = ROW_TILE
    row = lambda b, i: (b, i, 0)
    const2 = lambda b, i: (0, 0)
    out_shape = [
        jax.ShapeDtypeStruct((bsz, t, 1280), F32),
        jax.ShapeDtypeStruct((bsz, t, W_BRANCH), BF16),
        jax.ShapeDtypeStruct((bsz, t, W_BRANCH), BF16),
        jax.ShapeDtypeStruct((bsz, t, W_BRANCH), BF16),
        jax.ShapeDtypeStruct((bsz, t, W_BRANCH), F32),
        jax.ShapeDtypeStruct((bsz, t, 1024), F32),
        jax.ShapeDtypeStruct((bsz, t, 1280), F32),
        jax.ShapeDtypeStruct((bsz, t, LANES), F32),
        jax.ShapeDtypeStruct((bsz, LANES, t), F32),
    ]
    out_specs = [
        pl.BlockSpec((1, tm, 1280), row),
        pl.BlockSpec((1, tm, W_BRANCH), row),
        pl.BlockSpec((1, tm, W_BRANCH), row),
        pl.BlockSpec((1, tm, W_BRANCH), row),
        pl.BlockSpec((1, tm, W_BRANCH), row),
        pl.BlockSpec((1, tm, 1024), row),
        pl.BlockSpec((1, tm, 1280), row),
        pl.BlockSpec((1, tm, LANES), row),
        pl.BlockSpec((1, LANES, tm), lambda b, i: (b, 0, i)),
    ]
    return pl.pallas_call(
        functools.partial(_in_kernel, n_ctx_tiles, ctx_row),
        grid=(bsz, t // tm),
        in_specs=[pl.BlockSpec((1, tm, d), row),
                  pl.BlockSpec(mo.shape, const2),
                  pl.BlockSpec((1, d), const2),
                  pl.BlockSpec((tm, W_BRANCH), lambda b, i: (i, 0)),
                  pl.BlockSpec((tm, W_BRANCH), lambda b, i: (i, 0)),
                  pl.BlockSpec(w2.shape, const2),
                  pl.BlockSpec(wgt.shape, const2)],
        out_specs=out_specs,
        out_shape=out_shape,
        compiler_params=_cparams(("parallel", "parallel")),
        name="in_proj",
    )(xs, mo, g.reshape(1, d), cos, sin, w2, wgt)


def _bwd_chunk(c, n_ctx_chunks, n_chunks):
    return jnp.where(c < n_ctx_chunks, n_ctx_chunks - 1 - c, n_chunks + n_ctx_chunks - 1 - c)


def _tri_masks(d):
    ri = _iota2((CHUNK, CHUNK), 0)
    ci = _iota2((CHUNK, CHUNK), 1)
    if d == 0:
        return ci <= ri, ri <= ci
    return ci >= ri, ri >= ci


def _mlstm_kernel(mlf_ref, mlb_ref, gcf_ref, gcb_ref, gtf_ref, gtb_ref, brow_ref, bcol_ref,
                  hf_ref, hb_ref, cn_ref, m_ref):
    @pl.when(pl.program_id(1) == 0)
    def _():
        cn_ref[...] = jnp.zeros_like(cn_ref)
        m_ref[...] = jnp.zeros_like(m_ref)

    lane = _iota2((CHUNK, LANES), 1)
    ninf = jnp.float32(-jnp.inf)
    for d in (0, 1):
        ml_ref = (mlf_ref, mlb_ref)[d]
        out_ref = (hf_ref, hb_ref)[d]
        vis, vis_t = _tri_masks(d)
        tc = jnp.where(vis, 1.0, 0.0).astype(BF16)
        tr = jnp.where(vis_t, 1.0, 0.0).astype(BF16)
        last = CHUNK - 1 if d == 0 else 0

        gcol = (gcf_ref, gcb_ref)[d][0] + brow_ref[...]
        fc_all = _mm_exact_l(tc, _log_sigmoid(gcol))
        grow = (gtf_ref, gtb_ref)[d][0] + bcol_ref[...]
        fr_all = _mm_exact_r(_log_sigmoid(grow), tr)

        for p in range(2):
            q_t = ml_ref[0, :, 128 * p:128 * p + 128]
            k_t = ml_ref[0, :, 256 + 128 * p:256 + 128 * p + 128] * (HEAD_W ** -0.5)
            v_t = ml_ref[0, :, 512 + 128 * p:512 + 128 * p + 128]
            k_b = k_t.astype(BF16)
            out_pair = jnp.zeros((CHUNK, LANES), F32)
            for half in range(2):
                h = 2 * p + half
                s_idx = 4 * d + h
                gi, gf = 4 * d + h, 8 + 4 * d + h
                hmask = (lane >= 64) if half else (lane < 64)
                den_lane = 0 if half else 64
                li_c = gcol[:, gi:gi + 1]
                f_c = fc_all[:, gf:gf + 1]
                li_r = grow[gi:gi + 1, :]
                f_r = fr_all[gf:gf + 1, :]
                f_last = f_c[last:last + 1, :]
                m_old = m_ref[s_idx:s_idx + 1, 0:1]

                qm = jnp.where(hmask, q_t, 0.0).astype(BF16)
                v_ext = jnp.where(hmask, v_t, jnp.where(lane == den_lane, 1.0, 0.0)).astype(BF16)
                cn = cn_ref[s_idx]

                dm = jnp.where(vis, f_c - f_r + li_r, ninf)
                a_int = f_c + m_old
                m_i = jnp.maximum(a_int, jnp.max(dm, axis=1, keepdims=True))
                s = _mm_nt(qm, k_b) * jnp.exp(dm - m_i)
                w_int = jnp.exp(a_int - m_i)
                tot = _mm(s.astype(BF16), v_ext) + w_int * _mm(qm, cn.astype(BF16))
                den = tot[:, den_lane:den_lane + 1]
                hh = tot / jnp.maximum(jnp.abs(den), jnp.exp(-m_i))
                out_pair = jnp.where(hmask, hh, out_pair)

                w_end = f_last - f_c + li_c
                b_end = jnp.max(w_end, axis=0, keepdims=True)
                e = jnp.exp(w_end - b_end)
                ke = (jnp.where(hmask, k_t, 0.0) * e).astype(BF16)
                dcn = _mm_tn(ke, v_ext)
                m_new = jnp.maximum(f_last + m_old, b_end)
                a = jnp.exp(f_last + m_old - m_new)
                g = jnp.exp(b_end - m_new)
                cn_ref[s_idx] = a * cn + g * dcn
                m_ref[s_idx:s_idx + 1, :] = jnp.broadcast_to(m_new, (1, LANES))
            out_ref[0, :, 128 * p:128 * p + 128] = out_pair


def _mlstm(ml, gc, gt, brow, bcol, n_ctx_chunks):
    bsz, t, _ = ml.shape
    nc = t // CHUNK
    fwd = lambda b, c: (b, c, 0)
    bwd = lambda b, c: (b, _bwd_chunk(c, n_ctx_chunks, nc), 0)
    fwd_t = lambda b, c: (b, 0, c)
    bwd_t = lambda b, c: (b, 0, _bwd_chunk(c, n_ctx_chunks, nc))
    const2 = lambda b, c: (0, 0)
    out = jax.ShapeDtypeStruct((bsz, t, W_BRANCH), F32)
    return pl.pallas_call(
        _mlstm_kernel,
        grid=(bsz, nc),
        in_specs=[pl.BlockSpec((1, CHUNK, 768), fwd), pl.BlockSpec((1, CHUNK, 768), bwd),
                  pl.BlockSpec((1, CHUNK, LANES), fwd), pl.BlockSpec((1, CHUNK, LANES), bwd),
                  pl.BlockSpec((1, LANES, CHUNK), fwd_t), pl.BlockSpec((1, LANES, CHUNK), bwd_t),
                  pl.BlockSpec((1, LANES), const2), pl.BlockSpec((LANES, 1), const2)],
        out_specs=[pl.BlockSpec((1, CHUNK, W_BRANCH), fwd), pl.BlockSpec((1, CHUNK, W_BRANCH), bwd)],
        out_shape=[out, out],
        scratch_shapes=[pltpu.VMEM((8, LANES, LANES), F32), pltpu.VMEM((8, LANES), F32)],
        compiler_params=_cparams(("parallel", "arbitrary")),
        name="mlstm",
    )(ml, ml, gc, gc, gt, gt, brow, bcol)


def _ssd_kernel(n_ctx_chunks, n_chunks,
                xf_ref, xfp_ref, xfn_ref, xb_ref, xbp_ref, xbn_ref,
                gcf_ref, gcb_ref, gtf_ref, gtb_ref, brow_ref, bcol_ref, arow_ref, acol_ref,
                cw_ref, cb_ref, dskip_ref, yf_ref, yb_ref, s_ref):
    c = pl.program_id(1)

    @pl.when(c == 0)
    def _():
        s_ref[...] = jnp.zeros_like(s_ref)

    lane = _iota2((CHUNK, LANES), 1)
    row768 = _iota2((CHUNK, 768), 0)
    ninf = jnp.float32(-jnp.inf)
    for d in (0, 1):
        x_ref, xp_ref, xn_ref = ((xf_ref, xfp_ref, xfn_ref), (xb_ref, xbp_ref, xbn_ref))[d]
        out_ref = (yf_ref, yb_ref)[d]
        j = c if d == 0 else _bwd_chunk(c, n_ctx_chunks, n_chunks)
        seg_first = jnp.logical_or(j == 0, j == n_ctx_chunks)
        seg_last = jnp.logical_or(j == n_ctx_chunks - 1, j == n_chunks - 1)
        vis, vis_t = _tri_masks(d)
        tc = jnp.where(vis, 1.0, 0.0).astype(BF16)
        tr = jnp.where(vis_t, 1.0, 0.0).astype(BF16)
        last = CHUNK - 1 if d == 0 else 0

        x = x_ref[0]
        prev = jnp.where(seg_first, 0.0, xp_ref[0, 7:8, :])
        nxt = jnp.where(seg_last, 0.0, xn_ref[0, 0:1, :])
        x_dn = jnp.where(row768 == 0, prev, pltpu.roll(x, 1, 0))
        x_up = jnp.where(row768 == CHUNK - 1, nxt, pltpu.roll(x, CHUNK - 1, 0))
        xc = x_dn * cw_ref[0:1, :] + x * cw_ref[1:2, :] + x_up * cw_ref[2:3, :] + cb_ref[...]
        xa = _silu(xc)

        gcol = (gcf_ref, gcb_ref)[d][0] + brow_ref[...]
        dtc_all = _softplus(gcol)
        ac_all = _mm_exact_l(tc, dtc_all * arow_ref[...])
        grow = (gtf_ref, gtb_ref)[d][0] + bcol_ref[...]
        dtr_all = _softplus(grow)
        ar_all = _mm_exact_r(dtr_all * acol_ref[...], tr)

        for g in range(2):
            x_pair = xa[:, 128 * g:128 * g + 128]
            b_g = xa[:, 256 + 128 * g:256 + 128 * g + 128]
            c_g = xa[:, 512 + 128 * g:512 + 128 * g + 128]
            c_gb = c_g.astype(BF16)
            cb = _mm_nt(c_gb, b_g.astype(BF16))
            y_pair = jnp.zeros((CHUNK, LANES), F32)
            for half in range(2):
                h = 2 * g + half
                s_idx = 4 * d + h
                gl = 16 + 4 * d + h
                hmask = (lane >= 64) if half else (lane < 64)
                a_c = ac_all[:, gl:gl + 1]
                dt_c = dtc_all[:, gl:gl + 1]
                a_r = ar_all[gl:gl + 1, :]
                dt_r = dtr_all[gl:gl + 1, :]
                a_last = a_c[last:last + 1, :]
                st = s_ref[s_idx]

                xm = jnp.where(hmask, x_pair, 0.0).astype(BF16)
                sc = cb * jnp.exp(jnp.where(vis, a_c - a_r, ninf)) * dt_r
                y_pair = y_pair + _mm(sc.astype(BF16), xm) + jnp.exp(a_c) * _mm(c_gb, st.astype(BF16))

                wb = jnp.exp(a_last - a_c) * dt_c
                ds = _mm_tn((b_g * wb).astype(BF16), xm)
                s_ref[s_idx] = jnp.exp(a_last) * st + ds
            if d == 0:
                y_pair = y_pair + dskip_ref[:, 128 * g:128 * g + 128] * x_pair
            out_ref[0, :, 128 * g:128 * g + 128] = y_pair


def _ssd(ss, gc, gt, brow, bcol, arow, acol, conv_w, conv_b, dskip, n_ctx_chunks):
    bsz, t, _ = ss.shape
    nc = t // CHUNK
    sub = CHUNK // 8
    nsub = t // 8
    bc = lambda c: _bwd_chunk(c, n_ctx_chunks, nc)
    fwd = lambda b, c: (b, c, 0)
    bwd = lambda b, c: (b, bc(c), 0)
    fwd_p = lambda b, c: (b, jnp.maximum(c * sub - 1, 0), 0)
    fwd_n = lambda b, c: (b, jnp.minimum((c + 1) * sub, nsub - 1), 0)
    bwd_p = lambda b, c: (b, jnp.maximum(bc(c) * sub - 1, 0), 0)
    bwd_n = lambda b, c: (b, jnp.minimum((bc(c) + 1) * sub, nsub - 1), 0)
    fwd_t = lambda b, c: (b, 0, c)
    bwd_t = lambda b, c: (b, 0, bc(c))
    const2 = lambda b, c: (0, 0)
    out = jax.ShapeDtypeStruct((bsz, t, W_BRANCH), F32)
    return pl.pallas_call(
        functools.partial(_ssd_kernel, n_ctx_chunks, nc),
        grid=(bsz, nc),
        in_specs=[pl.BlockSpec((1, CHUNK, 768), fwd), pl.BlockSpec((1, 8, 768), fwd_p),
                  pl.BlockSpec((1, 8, 768), fwd_n),
                  pl.BlockSpec((1, CHUNK, 768), bwd), pl.BlockSpec((1, 8, 768), bwd_p),
                  pl.BlockSpec((1, 8, 768), bwd_n),
                  pl.BlockSpec((1, CHUNK, LANES), fwd), pl.BlockSpec((1, CHUNK, LANES), bwd),
                  pl.BlockSpec((1, LANES, CHUNK), fwd_t), pl.BlockSpec((1, LANES, CHUNK), bwd_t),
                  pl.BlockSpec((1, LANES), const2), pl.BlockSpec((LANES, 1), const2),
                  pl.BlockSpec((1, LANES), const2), pl.BlockSpec((LANES, 1), const2),
                  pl.BlockSpec((3, 768), const2), pl.BlockSpec((1, 768), const2),
                  pl.BlockSpec((1, W_BRANCH), const2)],
        out_specs=[pl.BlockSpec((1, CHUNK, W_BRANCH), fwd), pl.BlockSpec((1, CHUNK, W_BRANCH), bwd)],
        out_shape=[out, out],
        scratch_shapes=[pltpu.VMEM((8, SSM_N, LANES), F32)],
        compiler_params=_cparams(("parallel", "arbitrary")),
        name="ssd",
    )(ss, ss, ss, ss, ss, ss, gc, gc, gt, gt, brow, bcol, arow, acol, conv_w, conv_b, dskip)


def _hgrn2_kernel(hf_ref, hb_ref, loglb_ref, log1m_ref, onem_ref, e_ref, of_ref, ob_ref,
                  st_ref, g_s, q_s, k_s, v_s, o_s):
    @pl.when(pl.program_id(1) == 0)
    def _():
        st_ref[...] = jnp.zeros_like(st_ref)

    ri = _iota2((CHUNK, CHUNK), 0)
    ci = _iota2((CHUNK, CHUNK), 1)
    lane = _iota2((CHUNK, LANES), 1)
    rw = _iota2((CHUNK, W_BRANCH), 0)
    blockdiag = (ri >> 6) == (ci >> 6)
    ninf = jnp.float32(-jnp.inf)
    for d in (0, 1):
        h_ref = (hf_ref, hb_ref)[d]
        out_ref = (of_ref, ob_ref)[d]
        vis, _ = _tri_masks(d)
        tc = jnp.where(vis, 1.0, 0.0).astype(BF16)
        last = CHUNK - 1 if d == 0 else 0

        q = _silu(h_ref[0, :, 0:W_BRANCH])
        z = h_ref[0, :, W_BRANCH * (1 + d):W_BRANCH * (2 + d)]
        v = h_ref[0, :, 3 * W_BRANCH:4 * W_BRANCH]
        la = loglb_ref[d:d + 1, :]
        lb_ = log1m_ref[d:d + 1, :] + _log_sigmoid(z)
        logf = jnp.maximum(la, lb_) + jnp.log1p(jnp.exp(-jnp.abs(la - lb_)))
        kk = onem_ref[d:d + 1, :] * _sigmoid(-z)
        gcum = _mm_exact_l(tc, logf)
        g_last = gcum[last:last + 1, :]

        qg = (q * jnp.exp(gcum)).astype(BF16)
        kg = (kk * jnp.exp(g_last - gcum)).astype(BF16)
        vb = v.astype(BF16)

        a_mats = [jnp.zeros((CHUNK, CHUNK), F32) for _ in range(N_HEADS)]
        blk = CHUNK // 2
        while blk >= SUB:
            first = (rw & (2 * blk - 1)) < blk
            brow_i = (blk - 1) if d == 0 else blk
            gb = gcum.reshape(CHUNK // (2 * blk), 2 * blk, W_BRANCH)[:, brow_i:brow_i + 1, :]
            gb = jnp.broadcast_to(gb, (CHUNK // (2 * blk), 2 * blk, W_BRANCH)).reshape(CHUNK, W_BRANCH)
            q_side = jnp.logical_not(first) if d == 0 else first
            qt = q * jnp.exp(jnp.where(q_side, gcum - gb, ninf))
            kt = (kk * jnp.exp(jnp.where(q_side, ninf, gb - gcum))).astype(BF16)
            same = (ri >> int(math.log2(2 * blk))) == (ci >> int(math.log2(2 * blk)))
            for h in range(N_HEADS):
                p, half = divmod(h, 2)
                hmask = (lane >= 64) if half else (lane < 64)
                qh = jnp.where(hmask, qt[:, 128 * p:128 * p + 128], 0.0).astype(BF16)
                a_mats[h] = a_mats[h] + jnp.where(same, _mm_nt(qh, kt[:, 128 * p:128 * p + 128]), 0.0)
            blk //= 2

        for p in range(2):
            sl = slice(128 * p, 128 * p + 128)
            st = st_ref[2 * d + p]
            o_inter = _mm_nt(qg[:, sl], st.astype(BF16))
            a_cat = jnp.concatenate([a_mats[2 * p], a_mats[2 * p + 1]], axis=1).astype(BF16)
            v_p = v[:, sl]
            v_cat = jnp.concatenate([jnp.where(lane < 64, v_p, 0.0), jnp.where(lane >= 64, v_p, 0.0)],
                                    axis=0).astype(BF16)
            o_s[:, sl] = o_inter + _mm(a_cat, v_cat)
            dst = _mm_tn(vb[:, sl], kg[:, sl])
            st_ref[2 * d + p] = st * jnp.exp(g_last[:, sl]) + jnp.where(blockdiag, dst, 0.0)

        g_s[...] = gcum
        q_s[...] = q
        k_s[...] = kk
        v_s[...] = v
        rs = _iota2((SUB, W_BRANCH), 0)

        def diag_block(r, carry):
            r0 = pl.multiple_of(r * SUB, SUB)
            g_blk = g_s[pl.ds(r0, SUB), :]
            q_blk = q_s[pl.ds(r0, SUB), :]
            parts = []
            for j in range(SUB):
                g_j = g_s[pl.ds(r0 + j, 1), :]
                k_j = k_s[pl.ds(r0 + j, 1), :]
                ok = (rs >= j) if d == 0 else (rs <= j)
                parts.append((q_blk * jnp.exp(jnp.where(ok, g_blk - g_j, ninf)) * k_j).astype(BF16))
            pst = jnp.concatenate(parts, axis=0)
            red = _mm(pst, e_ref[...])
            acc = o_s[pl.ds(r0, SUB), :]
            for j in range(SUB):
                acc = acc + red[SUB * j:SUB * j + SUB, :] * v_s[pl.ds(r0 + j, 1), :]
            o_s[pl.ds(r0, SUB), :] = acc
            return carry

        lax.fori_loop(0, CHUNK // SUB, diag_block, 0)
        out_ref[0] = o_s[...]


def _hgrn2(hg, loglb, log1m, onem, e64, n_ctx_chunks):
    bsz, t, _ = hg.shape
    nc = t // CHUNK
    fwd = lambda b, c: (b, c, 0)
    bwd = lambda b, c: (b, _bwd_chunk(c, n_ctx_chunks, nc), 0)
    const2 = lambda b, c: (0, 0)
    out = jax.ShapeDtypeStruct((bsz, t, W_BRANCH), F32)
    return pl.pallas_call(
        _hgrn2_kernel,
        grid=(bsz, nc),
        in_specs=[pl.BlockSpec((1, CHUNK, 1024), fwd), pl.BlockSpec((1, CHUNK, 1024), bwd),
                  pl.BlockSpec((2, W_BRANCH), const2), pl.BlockSpec((2, W_BRANCH), const2),
                  pl.BlockSpec((2, W_BRANCH), const2), pl.BlockSpec((W_BRANCH, W_BRANCH), const2)],
        out_specs=[pl.BlockSpec((1, CHUNK, W_BRANCH), fwd), pl.BlockSpec((1, CHUNK, W_BRANCH), bwd)],
        out_shape=[out, out],
        scratch_shapes=[pltpu.VMEM((4, LANES, LANES), F32)] +
                       [pltpu.VMEM((CHUNK, W_BRANCH), F32) for _ in range(5)],
        compiler_params=_cparams(("parallel", "arbitrary")),
        name="hgrn2",
    )(hg, hg, loglb, log1m, onem, e64)


def _attn(daq, dak, dav, lam_vecs, lam_init, n_ctx_tiles, skip_ctx):
    bsz, t, _ = daq.shape
    tq = ROW_TILE
    n_k_tiles = t // ROW_TILE
    q_off = n_ctx_tiles if skip_ctx else 0
    nq = t // tq - q_off
    n_maps = W_BRANCH // DA_QK

    def kern(q_ref, k_ref, v_ref, lv_ref, o_ref, q8_ref, m_ref, l_ref, acc_ref):
        qi = pl.program_id(1) + q_off
        lane = _iota2((tq, W_BRANCH), 1)
        q = q_ref[0]
        for j in range(n_maps):
            q8_ref[j * tq:(j + 1) * tq, :] = jnp.where((lane >> 5) == j, q, jnp.zeros_like(q))
        m_ref[...] = jnp.full_like(m_ref, -jnp.inf)
        l_ref[...] = jnp.zeros_like(l_ref)
        acc_ref[...] = jnp.zeros_like(acc_ref)
        nk = jnp.where(qi < n_ctx_tiles, n_ctx_tiles, n_k_tiles)

        def body(kt, carry):
            k0 = pl.multiple_of(kt * ROW_TILE, ROW_TILE)
            k = k_ref[0, pl.ds(k0, ROW_TILE), :]
            v = v_ref[0, pl.ds(k0, ROW_TILE), :]
            s = _mm_nt(q8_ref[...], k)
            m_old = m_ref[...]
            m_new = jnp.maximum(m_old, jnp.max(s, axis=1, keepdims=True))
            alpha = jnp.exp(m_old - m_new)
            p = jnp.exp(s - m_new)
            l_ref[...] = alpha * l_ref[...] + jnp.sum(p, axis=1, keepdims=True)
            acc_ref[...] = alpha * acc_ref[...] + _mm(p.astype(BF16), v)
            m_ref[...] = m_new
            return carry

        lax.fori_loop(0, nk, body, 0)

        lv = lv_ref[...]
        lam = (jnp.exp(jnp.sum(lv[0:1, :] * lv[1:2, :], axis=1, keepdims=True))
               - jnp.exp(jnp.sum(lv[2:3, :] * lv[3:4, :], axis=1, keepdims=True)) + lam_init)
        out = jnp.zeros((tq, W_BRANCH), F32)
        for h in range(N_HEADS):
            j0, j1 = 2 * h, 2 * h + 1
            o0 = acc_ref[j0 * tq:(j0 + 1) * tq, :] / l_ref[j0 * tq:(j0 + 1) * tq, :]
            o1 = acc_ref[j1 * tq:(j1 + 1) * tq, :] / l_ref[j1 * tq:(j1 + 1) * tq, :]
            out = jnp.where((lane >> 6) == h, o0 - lam * o1, out)
        o_ref[0] = out

    return pl.pallas_call(
        kern,
        grid=(bsz, nq),
        in_specs=[pl.BlockSpec((1, tq, W_BRANCH), lambda b, i: (b, i + q_off, 0)),
                  pl.BlockSpec((1, t, W_BRANCH), lambda b, i: (b, 0, 0)),
                  pl.BlockSpec((1, t, W_BRANCH), lambda b, i: (b, 0, 0)),
                  pl.BlockSpec(lam_vecs.shape, lambda b, i: (0, 0))],
        out_specs=pl.BlockSpec((1, tq, W_BRANCH), lambda b, i: (b, i, 0)),
        out_shape=jax.ShapeDtypeStruct((bsz, nq * tq, W_BRANCH), F32),
        scratch_shapes=[pltpu.VMEM((n_maps * tq, W_BRANCH), BF16),
                        pltpu.VMEM((n_maps * tq, 1), F32),
                        pltpu.VMEM((n_maps * tq, 1), F32),
                        pltpu.VMEM((n_maps * tq, W_BRANCH), F32)],
        compiler_params=_cparams(("parallel", "parallel")),
        name="diff_attn",
    )(daq, dak, dav, lam_vecs)


def _seg_mean(x, e_ref):
    x0 = x.astype(BF16)
    x1 = (x - x0.astype(F32)).astype(BF16)
    return (_mm(x0, e_ref[...]) + _mm(x1, e_ref[...])) * (1.0 / HEAD_W)


def _out_kernel(n_ctx_tiles, ctx_row, q_off, lam_init, final,
                x_ref, mo_ref, mlo_ref, mlz_ref, mhf_ref, mhb_ref, dao_ref, daz_ref,
                syf_ref, syb_ref, ssz_ref, hof_ref, hob_ref, hgz_ref,
                mlg_ref, dag_ref, ssg_ref, hgg_ref, e_ref, w_ref, fg_ref, o_ref):
    b = pl.program_id(0)
    i = pl.program_id(1) + q_off
    d = x_ref.shape[2]
    r = jnp.where(i < n_ctx_tiles, ctx_row, b)
    gate = mo_ref[pl.ds(r, 1), :][:, 2 * d:3 * d]

    u = _sigmoid(mlo_ref[0]) * (mhf_ref[0] + mhb_ref[0])
    dev = u - _seg_mean(u, e_ref)
    y_ml = dev * lax.rsqrt(_seg_mean(dev * dev, e_ref) + EPS) * mlg_ref[...] * _silu(mlz_ref[0])

    o = dao_ref[0]
    y_da = (o * lax.rsqrt(_seg_mean(o * o, e_ref) + EPS) * dag_ref[...]) * (1.0 - lam_init) * _silu(daz_ref[0])

    ys = (syf_ref[0] + syb_ref[0]) * _silu(ssz_ref[0])
    parts = []
    for g in range(2):
        yg = ys[:, 128 * g:128 * g + 128]
        parts.append(yg * lax.rsqrt(jnp.mean(yg * yg, axis=-1, keepdims=True) + EPS))
    y_ss = jnp.concatenate(parts, axis=1) * ssg_ref[...]

    oh = hof_ref[0] + hob_ref[0]
    y_hg = (oh * lax.rsqrt(_seg_mean(oh * oh, e_ref) + EPS) * hgg_ref[...]) * _silu(hgz_ref[0])

    acc = _mm(y_ml.astype(BF16), w_ref[0:W_BRANCH, :])
    acc = acc + _mm(y_da.astype(BF16), w_ref[W_BRANCH:2 * W_BRANCH, :])
    acc = acc + _mm(y_ss.astype(BF16), w_ref[2 * W_BRANCH:3 * W_BRANCH, :])
    acc = acc + _mm(y_hg.astype(BF16), w_ref[3 * W_BRANCH:4 * W_BRANCH, :])
    x_new = x_ref[0] + gate * acc
    if final:
        x_new = x_new * lax.rsqrt(jnp.mean(x_new * x_new, axis=-1, keepdims=True) + EPS) * fg_ref[...]
    o_ref[0] = x_new


def _out_proj(xs, mo, ml, mhf, mhb, dao, daz, ss, syf, syb, hg, hof, hob,
              mlg, dag, ssg, hgg, e64, w_out, final_g, lam_init, n_ctx_tiles, ctx_row, final):
    bsz, t, d = xs.shape
    tm = ROW_TILE
    q_off = n_ctx_tiles if final else 0
    nrow = t // tm - q_off
    row = lambda b, i: (b, i + q_off, 0)
    col = lambda k: (lambda b, i: (b, i + q_off, k))
    const2 = lambda b, i: (0, 0)
    wb = pl.BlockSpec((1, tm, W_BRANCH), row)
    return pl.pallas_call(
        functools.partial(_out_kernel, n_ctx_tiles, ctx_row, q_off, lam_init, final),
        grid=(bsz, nrow),
        in_specs=[pl.BlockSpec((1, tm, d), row),
                  pl.BlockSpec(mo.shape, const2),
                  pl.BlockSpec((1, tm, W_BRANCH), col(3)), pl.BlockSpec((1, tm, W_BRANCH), col(4)),
                  wb, wb,
                  pl.BlockSpec((1, tm, W_BRANCH), lambda b, i: (b, i, 0)) if final else wb,
                  wb,
                  wb, wb, pl.BlockSpec((1, tm, W_BRANCH), col(3)),
                  wb, wb, pl.BlockSpec((1, tm, W_BRANCH), col(4)),
                  pl.BlockSpec((1, W_BRANCH), const2), pl.BlockSpec((1, W_BRANCH), const2),
                  pl.BlockSpec((1, W_BRANCH), const2), pl.BlockSpec((1, W_BRANCH), const2),
                  pl.BlockSpec((W_BRANCH, W_BRANCH), const2),
                  pl.BlockSpec(w_out.shape, const2),
                  pl.BlockSpec((1, d), const2)],
        out_specs=pl.BlockSpec((1, tm, d), lambda b, i: (b, i, 0)),
        out_shape=jax.ShapeDtypeStruct((bsz, nrow * tm, d), F32),
        compiler_params=_cparams(("parallel", "parallel")),
        name="out_proj",
    )(xs, mo, ml, ml, mhf, mhb, dao, daz, syf, syb, ss, hof, hob, hg,
      mlg, dag, ssg, hgg, e64, w_out, final_g.reshape(1, d))


def _relayout_w_in(w_in):
    d = w_in.shape[0]
    o = {}
    off = 0
    for name, n in (("ml_q", 256), ("ml_k", 256), ("ml_v", 256), ("ml_o", 256), ("ml_i", 8), ("ml_f", 8),
                    ("ml_z", 256), ("da_q", 256), ("da_k", 256), ("da_v", 256), ("da_z", 256),
                    ("ss_xbc", 768), ("ss_dt", 8), ("ss_z", 256),
                    ("hg_q", 256), ("hg_f", 512), ("hg_i", 256), ("hg_z", 256)):
        o[name] = w_in[:, off:off + n]
        off += n

    def rot(w):
        g = w.reshape(d, W_BRANCH // DA_QK, 2, DA_QK // 2)
        return jnp.stack([-g[:, :, 1], g[:, :, 0]], axis=2).reshape(d, W_BRANCH)

    gates = jnp.concatenate([o["ml_i"], o["ml_f"], o["ss_dt"], jnp.zeros((d, LANES - 24), w_in.dtype)], axis=1)
    w2 = jnp.concatenate([o["ml_q"], o["ml_k"], o["ml_v"], o["ml_o"], o["ml_z"],
                          o["da_q"], o["da_k"], o["da_v"], o["da_z"], rot(o["da_q"]), rot(o["da_k"]),
                          o["ss_xbc"], o["ss_z"],
                          o["hg_q"], o["hg_f"], o["hg_i"], o["hg_z"], gates], axis=1)
    return w2.astype(BF16), gates.T.astype(BF16)


def _rope_tables(n_ctx, seq):
    pos = jnp.arange(seq)
    rows = (pos // GRID_W).astype(F32)
    cols = (pos % GRID_W).astype(F32)
    axis = DA_QK // 2
    inv = ROPE_BASE ** (-jnp.arange(0, axis, 2, dtype=F32) / axis)
    ang = jnp.concatenate([rows[:, None] * inv, cols[:, None] * inv], axis=-1)
    cos = jnp.concatenate([jnp.ones((n_ctx, axis), F32), jnp.cos(ang)], axis=0)
    sin = jnp.concatenate([jnp.zeros((n_ctx, axis), F32), jnp.sin(ang)], axis=0)
    reps = W_BRANCH // axis
    return jnp.tile(cos, (1, reps)), jnp.tile(sin, (1, reps))


def _gate_vec(vals):
    v = jnp.concatenate([vals.astype(F32), jnp.zeros((LANES - vals.shape[0],), F32)])
    return v.reshape(1, LANES), v.reshape(LANES, 1)


def kernel(x, c, ctx, c_ctx, w_mod, b_mod, norm_g, w_in, w_out, ml_gate_b, ml_norm_g, da_lambda, da_norm_g,
           ss_conv_w, ss_conv_b, ss_dt_bias, ss_a_log, ss_d, ss_norm_g, hg_lower, hg_norm_g, final_g):
    bsz, seq, d = x.shape
    n_ctx = ctx.shape[1]
    depth = w_mod.shape[0]
    assert n_ctx % ROW_TILE == 0 and seq % ROW_TILE == 0 and bsz < 8
    n_ctx_tiles = n_ctx // ROW_TILE
    n_ctx_chunks = n_ctx // CHUNK
    ctx_row = bsz

    xs = jnp.concatenate([ctx, x], axis=1)
    cc = jnp.concatenate([c, c_ctx[None, :], jnp.zeros((8 - bsz - 1, d), F32)], axis=0)
    cos, sin = _rope_tables(n_ctx, seq)
    lb_all = jnp.cumsum(jax.nn.softmax(hg_lower.astype(F32), axis=1), axis=1)
    lb_all = lb_all - lb_all[:, :1]
    hid = _iota2((W_BRANCH, W_BRANCH), 0) // HEAD_W
    e64 = (hid == hid.T).astype(BF16)

    out = None
    for l in range(depth):
        lam_init = 0.8 - 0.6 * math.exp(-0.3 * l)
        final = l == depth - 1
        w2, wgt = _relayout_w_in(w_in[l])
        mo = _modulation(cc, w_mod[l], b_mod[l])
        ml, daq, dak, dav, daz, ss, hg, gc, gt = _in_proj(xs, mo, norm_g[l], cos, sin, w2, wgt,
                                                          n_ctx_tiles, ctx_row)

        gb = ml_gate_b[l]
        zeros8 = jnp.zeros((8,), F32)
        brow, bcol = _gate_vec(jnp.concatenate([gb[:, 0].reshape(-1), gb[:, 1].reshape(-1),
                                                ss_dt_bias[l].reshape(-1)]))
        arow, acol = _gate_vec(jnp.concatenate([zeros8, zeros8, -jnp.exp(ss_a_log[l].astype(F32)).reshape(-1)]))
        mhf, mhb = _mlstm(ml, gc, gt, brow, bcol, n_ctx_chunks)
        dao = _attn(daq, dak, dav, da_lambda[l].astype(F32), lam_init, n_ctx_tiles, final)
        dskip = jnp.repeat(ss_d[l].astype(F32), HEAD_W).reshape(1, W_BRANCH)
        syf, syb = _ssd(ss, gc, gt, brow, bcol, arow, acol, ss_conv_w[l], ss_conv_b[l].reshape(1, -1),
                        dskip, n_ctx_chunks)
        lbh = lb_all[:, l]
        hof, hob = _hgrn2(hg, jnp.log(lbh), jnp.log1p(-lbh), 1.0 - lbh, e64, n_ctx_chunks)

        res = _out_proj(xs, mo, ml, mhf, mhb, dao, daz, ss, syf, syb, hg, hof, hob,
                        ml_norm_g[l].reshape(1, -1), jnp.tile(da_norm_g[l], N_HEADS).reshape(1, -1),
                        ss_norm_g[l].reshape(1, -1), hg_norm_g[l].reshape(1, -1), e64,
                        w_out[l].astype(BF16), final_g, lam_init, n_ctx_tiles, ctx_row, final)
        if final:
            out = res
        else:
            xs = res
    return out
```

```python
import functools
import math

import jax
import jax.numpy as jnp
from jax import lax
from jax.experimental import pallas as pl
from jax.experimental.pallas import tpu as pltpu

F32 = jnp.float32
BF16 = jnp.bfloat16

EPS = 1e-6
GRID_W = 64
ROPE_BASE = 10000.0
N_HEADS = 4
HEAD_W = 64
W_BRANCH = N_HEADS * HEAD_W
DA_QK = 32
SSM_N = 128
CHUNK = 128
ROW_TILE = 256
SUB = 16
LANES = 128
VMEM_LIMIT = 56 * 1024 * 1024

C_ML = 0
C_DAQ = 1280
C_DAQROT = 1536
C_DAV = 1792
C_DAZ = 2304
C_SS = 2560
C_HG = 3584
C_GATE = 4864
C_END = 4992
V_EXT = N_HEADS * LANES
KEY_TILE = 256
KEY_GROUP = 4

NT = (((1,), (1,)), ((), ()))
TN = (((0,), (0,)), ((), ()))


def _mm(a, b):
    return jnp.dot(a, b, preferred_element_type=F32)


def _mm_nt(a, b):
    return lax.dot_general(a, b, NT, preferred_element_type=F32)


def _mm_tn(a, b):
    return lax.dot_general(a, b, TN, preferred_element_type=F32)


def _split3(x):
    x0 = x.astype(BF16)
    r = x - x0.astype(F32)
    x1 = r.astype(BF16)
    x2 = (r - x1.astype(F32)).astype(BF16)
    return x0, x1, x2


def _mm_exact_l(t, x):
    x0, x1, x2 = _split3(x)
    return _mm(t, x0) + _mm(t, x1) + _mm(t, x2)


def _mm_exact_r(x, t):
    x0, x1, x2 = _split3(x)
    return _mm(x0, t) + _mm(x1, t) + _mm(x2, t)


def _sigmoid(x):
    return 1.0 / (1.0 + jnp.exp(-x))


def _silu(x):
    return x * _sigmoid(x)


def _log_sigmoid(x):
    return jnp.minimum(x, 0.0) - jnp.log1p(jnp.exp(-jnp.abs(x)))


def _softplus(x):
    return jnp.maximum(x, 0.0) + jnp.log1p(jnp.exp(-jnp.abs(x)))


def _iota2(shape, axis):
    return lax.broadcasted_iota(jnp.int32, shape, axis)


def _cparams(sem):
    return pltpu.CompilerParams(dimension_semantics=sem, vmem_limit_bytes=VMEM_LIMIT)


def _mod_kernel(c_ref, w_ref, b_ref, o_ref):
    c = c_ref[...]
    o_ref[...] = jnp.dot(_silu(c), w_ref[...], precision=lax.Precision.HIGHEST,
                         preferred_element_type=F32) + b_ref[...]


def _modulation(cc, w_mod, b_mod):
    d = cc.shape[1]
    n = w_mod.shape[1]
    tn = 512
    return pl.pallas_call(
        _mod_kernel,
        grid=(n // tn,),
        in_specs=[pl.BlockSpec((8, d), lambda j: (0, 0)),
                  pl.BlockSpec((d, tn), lambda j: (0, j)),
                  pl.BlockSpec((1, tn), lambda j: (0, j))],
        out_specs=pl.BlockSpec((8, tn), lambda j: (0, j)),
        out_shape=jax.ShapeDtypeStruct((8, n), F32),
        compiler_params=_cparams(("arbitrary",)),
        name="modulation",
    )(cc, w_mod, b_mod.reshape(1, n))


def _in_kernel(n_ctx_tiles, ctx_row, x_ref, mo_ref, g_ref, cos_ref, sin_ref, cost_ref, sint_ref,
               w_ref, wt_ref, ones_ref,
               ml_ref, daq_ref, kt_ref, vx_ref, daz_ref, ss_ref, hg_ref, gc_ref, gt_ref):
    b = pl.program_id(0)
    i = pl.program_id(1)
    d = x_ref.shape[2]
    r = jnp.where(i < n_ctx_tiles, ctx_row, b)
    mo = mo_ref[pl.ds(r, 1), :]
    shift = mo[:, 0:d]
    scale = mo[:, d:2 * d]
    x = x_ref[0]
    y = x * lax.rsqrt(jnp.mean(x * x, axis=-1, keepdims=True) + EPS) * g_ref[...]
    h = (y * (1.0 + scale) + shift).astype(BF16)

    ml_ref[0] = _mm(h, w_ref[:, C_ML:C_DAQ])
    q = _mm(h, w_ref[:, C_DAQ:C_DAQROT]) * cos_ref[...] + _mm(h, w_ref[:, C_DAQROT:C_DAV]) * sin_ref[...]
    daq_ref[0] = (q * (DA_QK ** -0.5)).astype(BF16)
    vx = (_mm(h, w_ref[:, C_DAV:C_DAZ]) + ones_ref[...]).astype(BF16)
    for hd in range(N_HEADS):
        vx_ref[0, 0, hd] = vx[:, LANES * hd:LANES * (hd + 1)]
    daz_ref[0] = _mm(h, w_ref[:, C_DAZ:C_SS])
    ss_ref[0] = _mm(h, w_ref[:, C_SS:C_HG])
    hg_ref[0] = _mm(h, w_ref[:, C_HG:C_GATE])
    gc_ref[0] = _mm(h, w_ref[:, C_GATE:C_END])
    tr = _mm_nt(wt_ref[...], h)
    kt_ref[0, 0] = (tr[0:W_BRANCH] * cost_ref[...] + tr[W_BRANCH:2 * W_BRANCH] * sint_ref[...]).astype(BF16)
    gt_ref[0] = tr[2 * W_BRANCH:2 * W_BRANCH + LANES]


def _in_proj(xs, mo, g, cos, sin, cos_t, sin_t, w2, wt, ones_row, n_ctx_tiles, ctx_row):
    bsz, t, d = xs.shape
    tm = ROW_TILE
    assert tm == KEY_TILE
    row = lambda b, i: (b, i, 0)
    const2 = lambda b, i: (0, 0)
    out_shape = [
        jax.ShapeDtypeStruct((bsz, t, 1280), F32),
        jax.ShapeDtypeStruct((bsz, t, W_BRANCH), BF16),
        jax.ShapeDtypeStruct((bsz, t // tm, W_BRANCH, tm), BF16),
        jax.ShapeDtypeStruct((bsz, t // tm, N_HEADS, tm, LANES), BF16),
        jax.ShapeDtypeStruct((bsz, t, W_BRANCH), F32),
        jax.ShapeDtypeStruct((bsz, t, 1024), F32),
        jax.ShapeDtypeStruct((bsz, t, 1280), F32),
        jax.ShapeDtypeStruct((bsz, t, LANES), F32),
        jax.ShapeDtypeStruct((bsz, LANES, t), F32),
    ]
    out_specs = [
        pl.BlockSpec((1, tm, 1280), row),
        pl.BlockSpec((1, tm, W_BRANCH), row),
        pl.BlockSpec((1, 1, W_BRANCH, tm), lambda b, i: (b, i, 0, 0)),
        pl.BlockSpec((1, 1, N_HEADS, tm, LANES), lambda b, i: (b, i, 0, 0, 0)),
        pl.BlockSpec((1, tm, W_BRANCH), row),
        pl.BlockSpec((1, tm, 1024), row),
        pl.BlockSpec((1, tm, 1280), row),
        pl.BlockSpec((1, tm, LANES), row),
        pl.BlockSpec((1, LANES, tm), lambda b, i: (b, 0, i)),
    ]
    return pl.pallas_call(
        functools.partial(_in_kernel, n_ctx_tiles, ctx_row),
        grid=(bsz, t // tm),
        in_specs=[pl.BlockSpec((1, tm, d), row),
                  pl.BlockSpec(mo.shape, const2),
                  pl.BlockSpec((1, d), const2),
                  pl.BlockSpec((tm, W_BRANCH), lambda b, i: (i, 0)),
                  pl.BlockSpec((tm, W_BRANCH), lambda b, i: (i, 0)),
                  pl.BlockSpec((W_BRANCH, tm), lambda b, i: (0, i)),
                  pl.BlockSpec((W_BRANCH, tm), lambda b, i: (0, i)),
                  pl.BlockSpec(w2.shape, const2),
                  pl.BlockSpec(wt.shape, const2),
                  pl.BlockSpec((1, V_EXT), const2)],
        out_specs=out_specs,
        out_shape=out_shape,
        compiler_params=_cparams(("parallel", "parallel")),
        name="in_proj",
    )(xs, mo, g.reshape(1, d), cos, sin, cos_t, sin_t, w2, wt, ones_row)


def _bwd_chunk(c, n_ctx_chunks, n_chunks):
    return jnp.where(c < n_ctx_chunks, n_ctx_chunks - 1 - c, n_chunks + n_ctx_chunks - 1 - c)


def _tri_masks(d):
    ri = _iota2((CHUNK, CHUNK), 0)
    ci = _iota2((CHUNK, CHUNK), 1)
    if d == 0:
        return ci <= ri, ri <= ci
    return ci >= ri, ri >= ci


def _mlstm_kernel(mlf_ref, mlb_ref, gcf_ref, gcb_ref, gtf_ref, gtb_ref, brow_ref, bcol_ref,
                  hf_ref, hb_ref, cn_ref, m_ref):
    @pl.when(pl.program_id(1) == 0)
    def _():
        cn_ref[...] = jnp.zeros_like(cn_ref)
        m_ref[...] = jnp.zeros_like(m_ref)

    lane = _iota2((CHUNK, LANES), 1)
    ninf = jnp.float32(-jnp.inf)
    for d in (0, 1):
        ml_ref = (mlf_ref, mlb_ref)[d]
        out_ref = (hf_ref, hb_ref)[d]
        vis, vis_t = _tri_masks(d)
        tc = jnp.where(vis, 1.0, 0.0).astype(BF16)
        tr = jnp.where(vis_t, 1.0, 0.0).astype(BF16)
        last = CHUNK - 1 if d == 0 else 0

        gcol = (gcf_ref, gcb_ref)[d][0] + brow_ref[...]
        fc_all = _mm_exact_l(tc, _log_sigmoid(gcol))
        grow = (gtf_ref, gtb_ref)[d][0] + bcol_ref[...]
        fr_all = _mm_exact_r(_log_sigmoid(grow), tr)

        for p in range(2):
            q_t = ml_ref[0, :, 128 * p:128 * p + 128]
            k_t = ml_ref[0, :, 256 + 128 * p:256 + 128 * p + 128] * (HEAD_W ** -0.5)
            v_t = ml_ref[0, :, 512 + 128 * p:512 + 128 * p + 128]
            k_b = k_t.astype(BF16)
            out_pair = jnp.zeros((CHUNK, LANES), F32)
            for half in range(2):
                h = 2 * p + half
                s_idx = 4 * d + h
                gi, gf = 4 * d + h, 8 + 4 * d + h
                hmask = (lane >= 64) if half else (lane < 64)
                den_lane = 0 if half else 64
                li_c = gcol[:, gi:gi + 1]
                f_c = fc_all[:, gf:gf + 1]
                li_r = grow[gi:gi + 1, :]
                f_r = fr_all[gf:gf + 1, :]
                f_last = f_c[last:last + 1, :]
                m_old = m_ref[s_idx:s_idx + 1, 0:1]

                qm = jnp.where(hmask, q_t, 0.0).astype(BF16)
                v_ext = jnp.where(hmask, v_t, jnp.where(lane == den_lane, 1.0, 0.0)).astype(BF16)
                cn = cn_ref[s_idx]

                dm = jnp.where(vis, f_c - f_r + li_r, ninf)
                a_int = f_c + m_old
                m_i = jnp.maximum(a_int, jnp.max(dm, axis=1, keepdims=True))
                s = _mm_nt(qm, k_b) * jnp.exp(dm - m_i)
                w_int = jnp.exp(a_int - m_i)
                tot = _mm(s.astype(BF16), v_ext) + w_int * _mm(qm, cn.astype(BF16))
                den = tot[:, den_lane:den_lane + 1]
                hh = tot / jnp.maximum(jnp.abs(den), jnp.exp(-m_i))
                out_pair = jnp.where(hmask, hh, out_pair)

                w_end = f_last - f_c + li_c
                b_end = jnp.max(w_end, axis=0, keepdims=True)
                e = jnp.exp(w_end - b_end)
                ke = (jnp.where(hmask, k_t, 0.0) * e).astype(BF16)
                dcn = _mm_tn(ke, v_ext)
                m_new = jnp.maximum(f_last + m_old, b_end)
                a = jnp.exp(f_last + m_old - m_new)
                g = jnp.exp(b_end - m_new)
                cn_ref[s_idx] = a * cn + g * dcn
                m_ref[s_idx:s_idx + 1, :] = jnp.broadcast_to(m_new, (1, LANES))
            out_ref[0, :, 128 * p:128 * p + 128] = out_pair


def _mlstm(ml, gc, gt, brow, bcol, n_ctx_chunks):
    bsz, t, _ = ml.shape
    nc = t // CHUNK
    fwd = lambda b, c: (b, c, 0)
    bwd = lambda b, c: (b, _bwd_chunk(c, n_ctx_chunks, nc), 0)
    fwd_t = lambda b, c: (b, 0, c)
    bwd_t = lambda b, c: (b, 0, _bwd_chunk(c, n_ctx_chunks, nc))
    const2 = lambda b, c: (0, 0)
    out = jax.ShapeDtypeStruct((bsz, t, W_BRANCH), F32)
    return pl.pallas_call(
        _mlstm_kernel,
        grid=(bsz, nc),
        in_specs=[pl.BlockSpec((1, CHUNK, 768), fwd), pl.BlockSpec((1, CHUNK, 768), bwd),
                  pl.BlockSpec((1, CHUNK, LANES), fwd), pl.BlockSpec((1, CHUNK, LANES), bwd),
                  pl.BlockSpec((1, LANES, CHUNK), fwd_t), pl.BlockSpec((1, LANES, CHUNK), bwd_t),
                  pl.BlockSpec((1, LANES), const2), pl.BlockSpec((LANES, 1), const2)],
        out_specs=[pl.BlockSpec((1, CHUNK, W_BRANCH), fwd), pl.BlockSpec((1, CHUNK, W_BRANCH), bwd)],
        out_shape=[out, out],
        scratch_shapes=[pltpu.VMEM((8, LANES, LANES), F32), pltpu.VMEM((8, LANES), F32)],
        compiler_params=_cparams(("parallel", "arbitrary")),
        name="mlstm",
    )(ml, ml, gc, gc, gt, gt, brow, bcol)


def _ssd_kernel(n_ctx_chunks, n_chunks,
                xf_ref, xfp_ref, xfn_ref, xb_ref, xbp_ref, xbn_ref,
                gcf_ref, gcb_ref, gtf_ref, gtb_ref, brow_ref, bcol_ref, arow_ref, acol_ref,
                cw_ref, cb_ref, dskip_ref, yf_ref, yb_ref, s_ref):
    c = pl.program_id(1)

    @pl.when(c == 0)
    def _():
        s_ref[...] = jnp.zeros_like(s_ref)

    lane = _iota2((CHUNK, LANES), 1)
    row768 = _iota2((CHUNK, 768), 0)
    ninf = jnp.float32(-jnp.inf)
    for d in (0, 1):
        x_ref, xp_ref, xn_ref = ((xf_ref, xfp_ref, xfn_ref), (xb_ref, xbp_ref, xbn_ref))[d]
        out_ref = (yf_ref, yb_ref)[d]
        j = c if d == 0 else _bwd_chunk(c, n_ctx_chunks, n_chunks)
        seg_first = jnp.logical_or(j == 0, j == n_ctx_chunks)
        seg_last = jnp.logical_or(j == n_ctx_chunks - 1, j == n_chunks - 1)
        vis, vis_t = _tri_masks(d)
        tc = jnp.where(vis, 1.0, 0.0).astype(BF16)
        tr = jnp.where(vis_t, 1.0, 0.0).astype(BF16)
        last = CHUNK - 1 if d == 0 else 0

        x = x_ref[0]
        prev = jnp.where(seg_first, 0.0, xp_ref[0, 7:8, :])
        nxt = jnp.where(seg_last, 0.0, xn_ref[0, 0:1, :])
        x_dn = jnp.where(row768 == 0, prev, pltpu.roll(x, 1, 0))
        x_up = jnp.where(row768 == CHUNK - 1, nxt, pltpu.roll(x, CHUNK - 1, 0))
        xc = x_dn * cw_ref[0:1, :] + x * cw_ref[1:2, :] + x_up * cw_ref[2:3, :] + cb_ref[...]
        xa = _silu(xc)

        gcol = (gcf_ref, gcb_ref)[d][0] + brow_ref[...]
        dtc_all = _softplus(gcol)
        ac_all = _mm_exact_l(tc, dtc_all * arow_ref[...])
        grow = (gtf_ref, gtb_ref)[d][0] + bcol_ref[...]
        dtr_all = _softplus(grow)
        ar_all = _mm_exact_r(dtr_all * acol_ref[...], tr)

        for g in range(2):
            x_pair = xa[:, 128 * g:128 * g + 128]
            b_g = xa[:, 256 + 128 * g:256 + 128 * g + 128]
            c_g = xa[:, 512 + 128 * g:512 + 128 * g + 128]
            c_gb = c_g.astype(BF16)
            cb = _mm_nt(c_gb, b_g.astype(BF16))
            y_pair = jnp.zeros((CHUNK, LANES), F32)
            for half in range(2):
                h = 2 * g + half
                s_idx = 4 * d + h
                gl = 16 + 4 * d + h
                hmask = (lane >= 64) if half else (lane < 64)
                a_c = ac_all[:, gl:gl + 1]
                dt_c = dtc_all[:, gl:gl + 1]
                a_r = ar_all[gl:gl + 1, :]
                dt_r = dtr_all[gl:gl + 1, :]
                a_last = a_c[last:last + 1, :]
                st = s_ref[s_idx]

                xm = jnp.where(hmask, x_pair, 0.0).astype(BF16)
                sc = cb * jnp.exp(jnp.where(vis, a_c - a_r, ninf)) * dt_r
                y_pair = y_pair + _mm(sc.astype(BF16), xm) + jnp.exp(a_c) * _mm(c_gb, st.astype(BF16))

                wb = jnp.exp(a_last - a_c) * dt_c
                ds = _mm_tn((b_g * wb).astype(BF16), xm)
                s_ref[s_idx] = jnp.exp(a_last) * st + ds
            if d == 0:
                y_pair = y_pair + dskip_ref[:, 128 * g:128 * g + 128] * x_pair
            out_ref[0, :, 128 * g:128 * g + 128] = y_pair


def _ssd(ss, gc, gt, brow, bcol, arow, acol, conv_w, conv_b, dskip, n_ctx_chunks):
    bsz, t, _ = ss.shape
    nc = t // CHUNK
    sub = CHUNK // 8
    nsub = t // 8
    bc = lambda c: _bwd_chunk(c, n_ctx_chunks, nc)
    fwd = lambda b, c: (b, c, 0)
    bwd = lambda b, c: (b, bc(c), 0)
    fwd_p = lambda b, c: (b, jnp.maximum(c * sub - 1, 0), 0)
    fwd_n = lambda b, c: (b, jnp.minimum((c + 1) * sub, nsub - 1), 0)
    bwd_p = lambda b, c: (b, jnp.maximum(bc(c) * sub - 1, 0), 0)
    bwd_n = lambda b, c: (b, jnp.minimum((bc(c) + 1) * sub, nsub - 1), 0)
    fwd_t = lambda b, c: (b, 0, c)
    bwd_t = lambda b, c: (b, 0, bc(c))
    const2 = lambda b, c: (0, 0)
    out = jax.ShapeDtypeStruct((bsz, t, W_BRANCH), F32)
    return pl.pallas_call(
        functools.partial(_ssd_kernel, n_ctx_chunks, nc),
        grid=(bsz, nc),
        in_specs=[pl.BlockSpec((1, CHUNK, 768), fwd), pl.BlockSpec((1, 8, 768), fwd_p),
                  pl.BlockSpec((1, 8, 768), fwd_n),
                  pl.BlockSpec((1, CHUNK, 768), bwd), pl.BlockSpec((1, 8, 768), bwd_p),
                  pl.BlockSpec((1, 8, 768), bwd_n),
                  pl.BlockSpec((1, CHUNK, LANES), fwd), pl.BlockSpec((1, CHUNK, LANES), bwd),
                  pl.BlockSpec((1, LANES, CHUNK), fwd_t), pl.BlockSpec((1, LANES, CHUNK), bwd_t),
                  pl.BlockSpec((1, LANES), const2), pl.BlockSpec((LANES, 1), const2),
                  pl.BlockSpec((1, LANES), const2), pl.BlockSpec((LANES, 1), const2),
                  pl.BlockSpec((3, 768), const2), pl.BlockSpec((1, 768), const2),
                  pl.BlockSpec((1, W_BRANCH), const2)],
        out_specs=[pl.BlockSpec((1, CHUNK, W_BRANCH), fwd), pl.BlockSpec((1, CHUNK, W_BRANCH), bwd)],
        out_shape=[out, out],
        scratch_shapes=[pltpu.VMEM((8, SSM_N, LANES), F32)],
        compiler_params=_cparams(("parallel", "arbitrary")),
        name="ssd",
    )(ss, ss, ss, ss, ss, ss, gc, gc, gt, gt, brow, bcol, arow, acol, conv_w, conv_b, dskip)


def _hgrn2_kernel(hf_ref, hb_ref, loglb_ref, log1m_ref, onem_ref, e_ref, of_ref, ob_ref,
                  st_ref, g_s, q_s, k_s, v_s, o_s):
    @pl.when(pl.program_id(1) == 0)
    def _():
        st_ref[...] = jnp.zeros_like(st_ref)

    ri = _iota2((CHUNK, CHUNK), 0)
    ci = _iota2((CHUNK, CHUNK), 1)
    lane = _iota2((CHUNK, LANES), 1)
    rw = _iota2((CHUNK, W_BRANCH), 0)
    blockdiag = (ri >> 6) == (ci >> 6)
    ninf = jnp.float32(-jnp.inf)
    for d in (0, 1):
        h_ref = (hf_ref, hb_ref)[d]
        out_ref = (of_ref, ob_ref)[d]
        vis, _ = _tri_masks(d)
        tc = jnp.where(vis, 1.0, 0.0).astype(BF16)
        last = CHUNK - 1 if d == 0 else 0

        q = _silu(h_ref[0, :, 0:W_BRANCH])
        z = h_ref[0, :, W_BRANCH * (1 + d):W_BRANCH * (2 + d)]
        v = h_ref[0, :, 3 * W_BRANCH:4 * W_BRANCH]
        la = loglb_ref[d:d + 1, :]
        lb_ = log1m_ref[d:d + 1, :] + _log_sigmoid(z)
        logf = jnp.maximum(la, lb_) + jnp.log1p(jnp.exp(-jnp.abs(la - lb_)))
        kk = onem_ref[d:d + 1, :] * _sigmoid(-z)
        gcum = _mm_exact_l(tc, logf)
        g_last = gcum[last:last + 1, :]

        qg = (q * jnp.exp(gcum)).astype(BF16)
        kg = (kk * jnp.exp(g_last - gcum)).astype(BF16)
        vb = v.astype(BF16)

        a_mats = [jnp.zeros((CHUNK, CHUNK), F32) for _ in range(N_HEADS)]
        blk = CHUNK // 2
        while blk >= SUB:
            first = (rw & (2 * blk - 1)) < blk
            brow_i = (blk - 1) if d == 0 else blk
            gb = gcum.reshape(CHUNK // (2 * blk), 2 * blk, W_BRANCH)[:, brow_i:brow_i + 1, :]
            gb = jnp.broadcast_to(gb, (CHUNK // (2 * blk), 2 * blk, W_BRANCH)).reshape(CHUNK, W_BRANCH)
            q_side = jnp.logical_not(first) if d == 0 else first
            qt = q * jnp.exp(jnp.where(q_side, gcum - gb, ninf))
            kt = (kk * jnp.exp(jnp.where(q_side, ninf, gb - gcum))).astype(BF16)
            same = (ri >> int(math.log2(2 * blk))) == (ci >> int(math.log2(2 * blk)))
            for h in range(N_HEADS):
                p, half = divmod(h, 2)
                hmask = (lane >= 64) if half else (lane < 64)
                qh = jnp.where(hmask, qt[:, 128 * p:128 * p + 128], 0.0).astype(BF16)
                a_mats[h] = a_mats[h] + jnp.where(same, _mm_nt(qh, kt[:, 128 * p:128 * p + 128]), 0.0)
            blk //= 2

        for p in range(2):
            sl = slice(128 * p, 128 * p + 128)
            st = st_ref[2 * d + p]
            o_inter = _mm_nt(qg[:, sl], st.astype(BF16))
            a_cat = jnp.concatenate([a_mats[2 * p], a_mats[2 * p + 1]], axis=1).astype(BF16)
            v_p = v[:, sl]
            v_cat = jnp.concatenate([jnp.where(lane < 64, v_p, 0.0), jnp.where(lane >= 64, v_p, 0.0)],
                                    axis=0).astype(BF16)
            o_s[:, sl] = o_inter + _mm(a_cat, v_cat)
            dst = _mm_tn(vb[:, sl], kg[:, sl])
            st_ref[2 * d + p] = st * jnp.exp(g_last[:, sl]) + jnp.where(blockdiag, dst, 0.0)

        g_s[...] = gcum
        q_s[...] = q
        k_s[...] = kk
        v_s[...] = v
        rs = _iota2((SUB, W_BRANCH), 0)

        def diag_block(r, carry):
            r0 = pl.multiple_of(r * SUB, SUB)
            g_blk = g_s[pl.ds(r0, SUB), :]
            q_blk = q_s[pl.ds(r0, SUB), :]
            parts = []
            for j in range(SUB):
                g_j = g_s[pl.ds(r0 + j, 1), :]
                k_j = k_s[pl.ds(r0 + j, 1), :]
                ok = (rs >= j) if d == 0 else (rs <= j)
                parts.append((q_blk * jnp.exp(jnp.where(ok, g_blk - g_j, ninf)) * k_j).astype(BF16))
            pst = jnp.concatenate(parts, axis=0)
            red = _mm(pst, e_ref[...])
            acc = o_s[pl.ds(r0, SUB), :]
            for j in range(SUB):
                acc = acc + red[SUB * j:SUB * j + SUB, :] * v_s[pl.ds(r0 + j, 1), :]
            o_s[pl.ds(r0, SUB), :] = acc
            return carry

        lax.fori_loop(0, CHUNK // SUB, diag_block, 0)
        out_ref[0] = o_s[...]


def _hgrn2(hg, loglb, log1m, onem, e64, n_ctx_chunks):
    bsz, t, _ = hg.shape
    nc = t // CHUNK
    fwd = lambda b, c: (b, c, 0)
    bwd = lambda b, c: (b, _bwd_chunk(c, n_ctx_chunks, nc), 0)
    const2 = lambda b, c: (0, 0)
    out = jax.ShapeDtypeStruct((bsz, t, W_BRANCH), F32)
    return pl.pallas_call(
        _hgrn2_kernel,
        grid=(bsz, nc),
        in_specs=[pl.BlockSpec((1, CHUNK, 1024), fwd), pl.BlockSpec((1, CHUNK, 1024), bwd),
                  pl.BlockSpec((2, W_BRANCH), const2), pl.BlockSpec((2, W_BRANCH), const2),
                  pl.BlockSpec((2, W_BRANCH), const2), pl.BlockSpec((W_BRANCH, W_BRANCH), const2)],
        out_specs=[pl.BlockSpec((1, CHUNK, W_BRANCH), fwd), pl.BlockSpec((1, CHUNK, W_BRANCH), bwd)],
        out_shape=[out, out],
        scratch_shapes=[pltpu.VMEM((4, LANES, LANES), F32)] +
                       [pltpu.VMEM((CHUNK, W_BRANCH), F32) for _ in range(5)],
        compiler_params=_cparams(("parallel", "arbitrary")),
        name="hgrn2",
    )(hg, hg, loglb, log1m, onem, e64)


def _attn(daq, kt, vx, lam_vecs, lam_init, n_ctx_tiles, skip_ctx):
    bsz, t, _ = daq.shape
    tq = ROW_TILE
    n_k_tiles = t // KEY_TILE
    n_lat_tiles = n_k_tiles - n_ctx_tiles
    group = math.gcd(KEY_GROUP, n_lat_tiles)
    q_off = n_ctx_tiles if skip_ctx else 0
    nq = t // tq - q_off
    n_maps = W_BRANCH // DA_QK
    rows = n_maps * tq

    def kern(q_ref, kt_ref, vx_ref, lv_ref, o_ref, q8_ref, m_ref, acc_ref):
        qi = pl.program_id(1) + q_off
        lane = _iota2((tq, W_BRANCH), 1)
        q = q_ref[0]
        for j in range(n_maps):
            q8_ref[j * tq:(j + 1) * tq, :] = jnp.where((lane >> 5) == j, q, jnp.zeros_like(q))
        m_ref[...] = jnp.full_like(m_ref, -jnp.inf)
        acc_ref[...] = jnp.zeros_like(acc_ref)

        def update(t0, n):
            q8 = q8_ref[...]
            s = [_mm(q8, kt_ref[0, t0 + i]) for i in range(n)]
            mx = s[0][:, 0:LANES]
            for i in range(n):
                for c in range(KEY_TILE // LANES):
                    if i or c:
                        mx = jnp.maximum(mx, s[i][:, LANES * c:LANES * (c + 1)])
            m_old = m_ref[...]
            m_new = jnp.maximum(m_old, jnp.max(mx, axis=1, keepdims=True))
            alpha = jnp.exp(m_old - m_new)
            m_ref[...] = m_new
            m2 = jnp.concatenate([m_new] * (KEY_TILE // LANES), axis=1)
            p = [jnp.exp(s[i] - m2).astype(BF16) for i in range(n)]
            for hd in range(N_HEADS):
                r0 = 2 * hd * tq
                pv = _mm(p[0][r0:r0 + 2 * tq], vx_ref[0, t0, hd])
                for i in range(1, n):
                    pv = pv + _mm(p[i][r0:r0 + 2 * tq], vx_ref[0, t0 + i, hd])
                acc_ref[r0:r0 + 2 * tq, :] = alpha[r0:r0 + 2 * tq] * acc_ref[r0:r0 + 2 * tq, :] + pv

        update(0, n_ctx_tiles)

        @pl.when(qi >= n_ctx_tiles)
        def _():
            def body(g, carry):
                update(n_ctx_tiles + g * group, group)
                return carry
            lax.fori_loop(0, n_lat_tiles // group, body, 0)

        lv = lv_ref[...]
        lam = (jnp.exp(jnp.sum(lv[0:1, :] * lv[1:2, :], axis=1, keepdims=True))
               - jnp.exp(jnp.sum(lv[2:3, :] * lv[3:4, :], axis=1, keepdims=True)) + lam_init)
        lane1 = _iota2((tq, LANES), 1)
        for pr in range(N_HEADS // 2):
            halves = []
            for half in range(2):
                r0 = 2 * (2 * pr + half) * tq
                one = 0 if half else HEAD_W
                a0 = acc_ref[r0:r0 + tq, :]
                a1 = acc_ref[r0 + tq:r0 + 2 * tq, :]
                halves.append(a0 / a0[:, one:one + 1] - lam * (a1 / a1[:, one:one + 1]))
            o_ref[0, :, LANES * pr:LANES * (pr + 1)] = jnp.where(lane1 < HEAD_W, halves[0], halves[1])

    return pl.pallas_call(
        kern,
        grid=(bsz, nq),
        in_specs=[pl.BlockSpec((1, tq, W_BRANCH), lambda b, i: (b, i + q_off, 0)),
                  pl.BlockSpec((1, n_k_tiles, W_BRANCH, KEY_TILE), lambda b, i: (b, 0, 0, 0)),
                  pl.BlockSpec((1, n_k_tiles, N_HEADS, KEY_TILE, LANES), lambda b, i: (b, 0, 0, 0, 0)),
                  pl.BlockSpec(lam_vecs.shape, lambda b, i: (0, 0))],
        out_specs=pl.BlockSpec((1, tq, W_BRANCH), lambda b, i: (b, i, 0)),
        out_shape=jax.ShapeDtypeStruct((bsz, nq * tq, W_BRANCH), F32),
        scratch_shapes=[pltpu.VMEM((rows, W_BRANCH), BF16),
                        pltpu.VMEM((rows, LANES), F32),
                        pltpu.VMEM((rows, LANES), F32)],
        compiler_params=_cparams(("parallel", "parallel")),
        name="diff_attn",
    )(daq, kt, vx, lam_vecs)


def _seg_mean(x, e_ref):
    x0 = x.astype(BF16)
    x1 = (x - x0.astype(F32)).astype(BF16)
    return (_mm(x0, e_ref[...]) + _mm(x1, e_ref[...])) * (1.0 / HEAD_W)


def _out_kernel(n_ctx_tiles, ctx_row, q_off, lam_init, final,
                x_ref, mo_ref, mlo_ref, mlz_ref, mhf_ref, mhb_ref, dao_ref, daz_ref,
                syf_ref, syb_ref, ssz_ref, hof_ref, hob_ref, hgz_ref,
                mlg_ref, dag_ref, ssg_ref, hgg_ref, e_ref, w_ref, fg_ref, o_ref):
    b = pl.program_id(0)
    i = pl.program_id(1) + q_off
    d = x_ref.shape[2]
    r = jnp.where(i < n_ctx_tiles, ctx_row, b)
    gate = mo_ref[pl.ds(r, 1), :][:, 2 * d:3 * d]

    u = _sigmoid(mlo_ref[0]) * (mhf_ref[0] + mhb_ref[0])
    dev = u - _seg_mean(u, e_ref)
    y_ml = dev * lax.rsqrt(_seg_mean(dev * dev, e_ref) + EPS) * mlg_ref[...] * _silu(mlz_ref[0])

    o = dao_ref[0]
    y_da = (o * lax.rsqrt(_seg_mean(o * o, e_ref) + EPS) * dag_ref[...]) * (1.0 - lam_init) * _silu(daz_ref[0])

    ys = (syf_ref[0] + syb_ref[0]) * _silu(ssz_ref[0])
    parts = []
    for g in range(2):
        yg = ys[:, 128 * g:128 * g + 128]
        parts.append(yg * lax.rsqrt(jnp.mean(yg * yg, axis=-1, keepdims=True) + EPS))
    y_ss = jnp.concatenate(parts, axis=1) * ssg_ref[...]

    oh = hof_ref[0] + hob_ref[0]
    y_hg = (oh * lax.rsqrt(_seg_mean(oh * oh, e_ref) + EPS) * hgg_ref[...]) * _silu(hgz_ref[0])

    acc = _mm(y_ml.astype(BF16), w_ref[0:W_BRANCH, :])
    acc = acc + _mm(y_da.astype(BF16), w_ref[W_BRANCH:2 * W_BRANCH, :])
    acc = acc + _mm(y_ss.astype(BF16), w_ref[2 * W_BRANCH:3 * W_BRANCH, :])
    acc = acc + _mm(y_hg.astype(BF16), w_ref[3 * W_BRANCH:4 * W_BRANCH, :])
    x_new = x_ref[0] + gate * acc
    if final:
        x_new = x_new * lax.rsqrt(jnp.mean(x_new * x_new, axis=-1, keepdims=True) + EPS) * fg_ref[...]
    o_ref[0] = x_new


def _out_proj(xs, mo, ml, mhf, mhb, dao, daz, ss, syf, syb, hg, hof, hob,
              mlg, dag, ssg, hgg, e64, w_out, final_g, lam_init, n_ctx_tiles, ctx_row, final):
    bsz, t, d = xs.shape
    tm = ROW_TILE
    q_off = n_ctx_tiles if final else 0
    nrow = t // tm - q_off
    row = lambda b, i: (b, i + q_off, 0)
    col = lambda k: (lambda b, i: (b, i + q_off, k))
    const2 = lambda b, i: (0, 0)
    wb = pl.BlockSpec((1, tm, W_BRANCH), row)
    return pl.pallas_call(
        functools.partial(_out_kernel, n_ctx_tiles, ctx_row, q_off, lam_init, final),
        grid=(bsz, nrow),
        in_specs=[pl.BlockSpec((1, tm, d), row),
                  pl.BlockSpec(mo.shape, const2),
                  pl.BlockSpec((1, tm, W_BRANCH), col(3)), pl.BlockSpec((1, tm, W_BRANCH), col(4)),
                  wb, wb,
                  pl.BlockSpec((1, tm, W_BRANCH), lambda b, i: (b, i, 0)) if final else wb,
                  wb,
                  wb, wb, pl.BlockSpec((1, tm, W_BRANCH), col(3)),
                  wb, wb, pl.BlockSpec((1, tm, W_BRANCH), col(4)),
                  pl.BlockSpec((1, W_BRANCH), const2), pl.BlockSpec((1, W_BRANCH), const2),
                  pl.BlockSpec((1, W_BRANCH), const2), pl.BlockSpec((1, W_BRANCH), const2),
                  pl.BlockSpec((W_BRANCH, W_BRANCH), const2),
                  pl.BlockSpec(w_out.shape, const2),
                  pl.BlockSpec((1, d), const2)],
        out_specs=pl.BlockSpec((1, tm, d), lambda b, i: (b, i, 0)),
        out_shape=jax.ShapeDtypeStruct((bsz, nrow * tm, d), F32),
        compiler_params=_cparams(("parallel", "parallel")),
        name="out_proj",
    )(xs, mo, ml, ml, mhf, mhb, dao, daz, syf, syb, ss, hof, hob, hg,
      mlg, dag, ssg, hgg, e64, w_out, final_g.reshape(1, d))


def _relayout_w_in(w_in):
    d = w_in.shape[0]
    o = {}
    off = 0
    for name, n in (("ml_q", 256), ("ml_k", 256), ("ml_v", 256), ("ml_o", 256), ("ml_i", 8), ("ml_f", 8),
                    ("ml_z", 256), ("da_q", 256), ("da_k", 256), ("da_v", 256), ("da_z", 256),
                    ("ss_xbc", 768), ("ss_dt", 8), ("ss_z", 256),
                    ("hg_q", 256), ("hg_f", 512), ("hg_i", 256), ("hg_z", 256)):
        o[name] = w_in[:, off:off + n]
        off += n

    def rot(w):
        g = w.reshape(d, W_BRANCH // DA_QK, 2, DA_QK // 2)
        return jnp.stack([-g[:, :, 1], g[:, :, 0]], axis=2).reshape(d, W_BRANCH)

    gates = jnp.concatenate([o["ml_i"], o["ml_f"], o["ss_dt"], jnp.zeros((d, LANES - 24), w_in.dtype)], axis=1)
    zpad = jnp.zeros((d, HEAD_W), w_in.dtype)
    v_ext = []
    for hd in range(N_HEADS):
        v_h = o["da_v"][:, HEAD_W * hd:HEAD_W * (hd + 1)]
        v_ext += [zpad, v_h] if hd % 2 else [v_h, zpad]
    w2 = jnp.concatenate([o["ml_q"], o["ml_k"], o["ml_v"], o["ml_o"], o["ml_z"],
                          o["da_q"], rot(o["da_q"])] + v_ext + [o["da_z"],
                          o["ss_xbc"], o["ss_z"],
                          o["hg_q"], o["hg_f"], o["hg_i"], o["hg_z"], gates], axis=1)
    wt = jnp.concatenate([o["da_k"], rot(o["da_k"]), gates], axis=1).T
    return w2.astype(BF16), wt.astype(BF16)


def _rope_tables(n_ctx, seq):
    pos = jnp.arange(seq)
    rows = (pos // GRID_W).astype(F32)
    cols = (pos % GRID_W).astype(F32)
    axis = DA_QK // 2
    inv = ROPE_BASE ** (-jnp.arange(0, axis, 2, dtype=F32) / axis)
    ang = jnp.concatenate([rows[:, None] * inv, cols[:, None] * inv], axis=-1)
    cos = jnp.concatenate([jnp.ones((n_ctx, axis), F32), jnp.cos(ang)], axis=0)
    sin = jnp.concatenate([jnp.zeros((n_ctx, axis), F32), jnp.sin(ang)], axis=0)
    reps = W_BRANCH // axis
    return jnp.tile(cos, (1, reps)), jnp.tile(sin, (1, reps))


def _gate_vec(vals):
    v = jnp.concatenate([vals.astype(F32), jnp.zeros((LANES - vals.shape[0],), F32)])
    return v.reshape(1, LANES), v.reshape(LANES, 1)


def kernel(x, c, ctx, c_ctx, w_mod, b_mod, norm_g, w_in, w_out, ml_gate_b, ml_norm_g, da_lambda, da_norm_g,
           ss_conv_w, ss_conv_b, ss_dt_bias, ss_a_log, ss_d, ss_norm_g, hg_lower, hg_norm_g, final_g):
    bsz, seq, d = x.shape
    n_ctx = ctx.shape[1]
    depth = w_mod.shape[0]
    assert n_ctx % ROW_TILE == 0 and seq % ROW_TILE == 0 and bsz < 8
    n_ctx_tiles = n_ctx // ROW_TILE
    n_ctx_chunks = n_ctx // CHUNK
    ctx_row = bsz

    xs = jnp.concatenate([ctx, x], axis=1)
    cc = jnp.concatenate([c, c_ctx[None, :], jnp.zeros((8 - bsz - 1, d), F32)], axis=0)
    cos, sin = _rope_tables(n_ctx, seq)
    cos_t, sin_t = cos.T, sin.T
    one_lanes = jnp.arange(V_EXT) % LANES == jnp.where((jnp.arange(V_EXT) // LANES) % 2 == 1, 0, HEAD_W)
    ones_row = one_lanes.astype(F32).reshape(1, V_EXT)
    lb_all = jnp.cumsum(jax.nn.softmax(hg_lower.astype(F32), axis=1), axis=1)
    lb_all = lb_all - lb_all[:, :1]
    hid = _iota2((W_BRANCH, W_BRANCH), 0) // HEAD_W
    e64 = (hid == hid.T).astype(BF16)

    out = None
    for l in range(depth):
        lam_init = 0.8 - 0.6 * math.exp(-0.3 * l)
        final = l == depth - 1
        w2, wt = _relayout_w_in(w_in[l])
        mo = _modulation(cc, w_mod[l], b_mod[l])
        ml, daq, dak, dav, daz, ss, hg, gc, gt = _in_proj(xs, mo, norm_g[l], cos, sin, cos_t, sin_t, w2, wt,
                                                          ones_row, n_ctx_tiles, ctx_row)

        gb = ml_gate_b[l]
        zeros8 = jnp.zeros((8,), F32)
        brow, bcol = _gate_vec(jnp.concatenate([gb[:, 0].reshape(-1), gb[:, 1].reshape(-1),
                                                ss_dt_bias[l].reshape(-1)]))
        arow, acol = _gate_vec(jnp.concatenate([zeros8, zeros8, -jnp.exp(ss_a_log[l].astype(F32)).reshape(-1)]))
        mhf, mhb = _mlstm(ml, gc, gt, brow, bcol, n_ctx_chunks)
        dao = _attn(daq, dak, dav, da_lambda[l].astype(F32), lam_init, n_ctx_tiles, final)
        dskip = jnp.repeat(ss_d[l].astype(F32), HEAD_W).reshape(1, W_BRANCH)
        syf, syb = _ssd(ss, gc, gt, brow, bcol, arow, acol, ss_conv_w[l], ss_conv_b[l].reshape(1, -1),
                        dskip, n_ctx_chunks)
        lbh = lb_all[:, l]
        hof, hob = _hgrn2(hg, jnp.log(lbh), jnp.log1p(-lbh), 1.0 - lbh, e64, n_ctx_chunks)

        res = _out_proj(xs, mo, ml, mhf, mhb, dao, daz, ss, syf, syb, hg, hof, hob,
                        ml_norm_g[l].reshape(1, -1), jnp.tile(da_norm_g[l], N_HEADS).reshape(1, -1),
                        ss_norm_g[l].reshape(1, -1), hg_norm_g[l].reshape(1, -1), e64,
                        w_out[l].astype(BF16), final_g, lam_init, n_ctx_tiles, ctx_row, final)
        if final:
            out = res
        else:
            xs = res
    return out
```

```python
import functools
import math

import jax
import jax.numpy as jnp
from jax import lax
from jax.experimental import pallas as pl
from jax.experimental.pallas import tpu as pltpu

F32 = jnp.float32
BF16 = jnp.bfloat16

EPS = 1e-6
GRID_W = 64
ROPE_BASE = 10000.0
N_HEADS = 4
HEAD_W = 64
W_BRANCH = N_HEADS * HEAD_W
DA_QK = 32
SSM_N = 128
CHUNK = 128
ROW_TILE = 256
SUB = 8
LANES = 128
VMEM_LIMIT = 56 * 1024 * 1024

C_ML = 0
C_DAQ = 1280
C_DAQROT = 1536
C_DAV = 1792
C_DAZ = 2304
C_SS = 2560
C_HG = 3584
C_GATE = 4864
C_END = 4992
V_EXT = N_HEADS * LANES
KEY_TILE = 256
KEY_GROUP = 4

NT = (((1,), (1,)), ((), ()))
TN = (((0,), (0,)), ((), ()))


def _mm(a, b):
    return jnp.dot(a, b, preferred_element_type=F32)


def _mm_nt(a, b):
    return lax.dot_general(a, b, NT, preferred_element_type=F32)


def _mm_tn(a, b):
    return lax.dot_general(a, b, TN, preferred_element_type=F32)


def _split3(x):
    x0 = x.astype(BF16)
    r = x - x0.astype(F32)
    x1 = r.astype(BF16)
    x2 = (r - x1.astype(F32)).astype(BF16)
    return x0, x1, x2


def _mm_exact_l(t, x):
    x0, x1, x2 = _split3(x)
    return _mm(t, x0) + _mm(t, x1) + _mm(t, x2)


def _mm_exact_r(x, t):
    x0, x1, x2 = _split3(x)
    return _mm(x0, t) + _mm(x1, t) + _mm(x2, t)


def _sigmoid(x):
    return 1.0 / (1.0 + jnp.exp(-x))


def _silu(x):
    return x * _sigmoid(x)


def _log_sigmoid(x):
    return jnp.minimum(x, 0.0) - jnp.log(1.0 + jnp.exp(-jnp.abs(x)))


def _softplus(x):
    return jnp.maximum(x, 0.0) + jnp.log(1.0 + jnp.exp(-jnp.abs(x)))


def _iota2(shape, axis):
    return lax.broadcasted_iota(jnp.int32, shape, axis)


def _cparams(sem):
    return pltpu.CompilerParams(dimension_semantics=sem, vmem_limit_bytes=VMEM_LIMIT)


def _mod_kernel(c_ref, w_ref, b_ref, o_ref):
    c = c_ref[...]
    o_ref[...] = jnp.dot(_silu(c), w_ref[...], precision=lax.Precision.HIGHEST,
                         preferred_element_type=F32) + b_ref[...]


def _modulation(cc, w_mod, b_mod):
    d = cc.shape[1]
    n = w_mod.shape[1]
    tn = 512
    return pl.pallas_call(
        _mod_kernel,
        grid=(n // tn,),
        in_specs=[pl.BlockSpec((8, d), lambda j: (0, 0)),
                  pl.BlockSpec((d, tn), lambda j: (0, j)),
                  pl.BlockSpec((1, tn), lambda j: (0, j))],
        out_specs=pl.BlockSpec((8, tn), lambda j: (0, j)),
        out_shape=jax.ShapeDtypeStruct((8, n), F32),
        compiler_params=_cparams(("arbitrary",)),
        name="modulation",
    )(cc, w_mod, b_mod.reshape(1, n))


def _in_kernel(n_ctx_tiles, ctx_row, x_ref, mo_ref, g_ref, cos_ref, sin_ref, cost_ref, sint_ref,
               w_ref, wt_ref, ones_ref,
               ml_ref, daq_ref, kt_ref, vx_ref, daz_ref, ss_ref, hg_ref, gc_ref, gt_ref):
    b = pl.program_id(0)
    i = pl.program_id(1)
    d = x_ref.shape[2]
    r = jnp.where(i < n_ctx_tiles, ctx_row, b)
    mo = mo_ref[pl.ds(r, 1), :]
    shift = mo[:, 0:d]
    scale = mo[:, d:2 * d]
    x = x_ref[0]
    y = x * lax.rsqrt(jnp.mean(x * x, axis=-1, keepdims=True) + EPS) * g_ref[...]
    h = (y * (1.0 + scale) + shift).astype(BF16)

    ml_ref[0] = _mm(h, w_ref[:, C_ML:C_DAQ])
    q = _mm(h, w_ref[:, C_DAQ:C_DAQROT]) * cos_ref[...] + _mm(h, w_ref[:, C_DAQROT:C_DAV]) * sin_ref[...]
    daq_ref[0] = (q * (DA_QK ** -0.5)).astype(BF16)
    vx = (_mm(h, w_ref[:, C_DAV:C_DAZ]) + ones_ref[...]).astype(BF16)
    for hd in range(N_HEADS):
        vx_ref[0, 0, hd] = vx[:, LANES * hd:LANES * (hd + 1)]
    daz_ref[0] = _mm(h, w_ref[:, C_DAZ:C_SS])
    ss_ref[0] = _mm(h, w_ref[:, C_SS:C_HG])
    hg_ref[0] = _mm(h, w_ref[:, C_HG:C_GATE])
    gc_ref[0] = _mm(h, w_ref[:, C_GATE:C_END])
    tr = _mm_nt(wt_ref[...], h)
    kt_ref[0, 0] = (tr[0:W_BRANCH] * cost_ref[...] + tr[W_BRANCH:2 * W_BRANCH] * sint_ref[...]).astype(BF16)
    gt_ref[0] = tr[2 * W_BRANCH:2 * W_BRANCH + LANES]


def _in_proj(xs, mo, g, cos, sin, cos_t, sin_t, w2, wt, ones_row, n_ctx_tiles, ctx_row):
    bsz, t, d = xs.shape
    tm = ROW_TILE
    assert tm == KEY_TILE
    row = lambda b, i: (b, i, 0)
    const2 = lambda b, i: (0, 0)
    out_shape = [
        jax.ShapeDtypeStruct((bsz, t, 1280), F32),
        jax.ShapeDtypeStruct((bsz, t, W_BRANCH), BF16),
        jax.ShapeDtypeStruct((bsz, t // tm, W_BRANCH, tm), BF16),
        jax.ShapeDtypeStruct((bsz, t // tm, N_HEADS, tm, LANES), BF16),
        jax.ShapeDtypeStruct((bsz, t, W_BRANCH), F32),
        jax.ShapeDtypeStruct((bsz, t, 1024), F32),
        jax.ShapeDtypeStruct((bsz, t, 1280), F32),
        jax.ShapeDtypeStruct((bsz, t, LANES), F32),
        jax.ShapeDtypeStruct((bsz, LANES, t), F32),
    ]
    out_specs = [
        pl.BlockSpec((1, tm, 1280), row),
        pl.BlockSpec((1, tm, W_BRANCH), row),
        pl.BlockSpec((1, 1, W_BRANCH, tm), lambda b, i: (b, i, 0, 0)),
        pl.BlockSpec((1, 1, N_HEADS, tm, LANES), lambda b, i: (b, i, 0, 0, 0)),
        pl.BlockSpec((1, tm, W_BRANCH), row),
        pl.BlockSpec((1, tm, 1024), row),
        pl.BlockSpec((1, tm, 1280), row),
        pl.BlockSpec((1, tm, LANES), row),
        pl.BlockSpec((1, LANES, tm), lambda b, i: (b, 0, i)),
    ]
    return pl.pallas_call(
        functools.partial(_in_kernel, n_ctx_tiles, ctx_row),
        grid=(bsz, t // tm),
        in_specs=[pl.BlockSpec((1, tm, d), row),
                  pl.BlockSpec(mo.shape, const2),
                  pl.BlockSpec((1, d), const2),
                  pl.BlockSpec((tm, W_BRANCH), lambda b, i: (i, 0)),
                  pl.BlockSpec((tm, W_BRANCH), lambda b, i: (i, 0)),
                  pl.BlockSpec((W_BRANCH, tm), lambda b, i: (0, i)),
                  pl.BlockSpec((W_BRANCH, tm), lambda b, i: (0, i)),
                  pl.BlockSpec(w2.shape, const2),
                  pl.BlockSpec(wt.shape, const2),
                  pl.BlockSpec((1, V_EXT), const2)],
        out_specs=out_specs,
        out_shape=out_shape,
        compiler_params=_cparams(("parallel", "parallel")),
        name="in_proj",
    )(xs, mo, g.reshape(1, d), cos, sin, cos_t, sin_t, w2, wt, ones_row)


def _bwd_chunk(c, n_ctx_chunks, n_chunks):
    return jnp.where(c < n_ctx_chunks, n_ctx_chunks - 1 - c, n_chunks + n_ctx_chunks - 1 - c)


def _tri_masks(d):
    ri = _iota2((CHUNK, CHUNK), 0)
    ci = _iota2((CHUNK, CHUNK), 1)
    if d == 0:
        return ci <= ri, ri <= ci
    return ci >= ri, ri >= ci


def _mlstm_kernel(mlf_ref, mlb_ref, gcf_ref, gcb_ref, gtf_ref, gtb_ref, brow_ref, bcol_ref,
                  hf_ref, hb_ref, cn_ref, m_ref):
    @pl.when(pl.program_id(1) == 0)
    def _():
        cn_ref[...] = jnp.zeros_like(cn_ref)
        m_ref[...] = jnp.zeros_like(m_ref)

    lane = _iota2((CHUNK, LANES), 1)
    ninf = jnp.float32(-jnp.inf)
    for d in (0, 1):
        ml_ref = (mlf_ref, mlb_ref)[d]
        out_ref = (hf_ref, hb_ref)[d]
        vis, vis_t = _tri_masks(d)
        tc = jnp.where(vis, 1.0, 0.0).astype(BF16)
        tr = jnp.where(vis_t, 1.0, 0.0).astype(BF16)
        last = CHUNK - 1 if d == 0 else 0

        gcol = (gcf_ref, gcb_ref)[d][0] + brow_ref[...]
        fc_all = _mm_exact_l(tc, _log_sigmoid(gcol))
        grow = (gtf_ref, gtb_ref)[d][0] + bcol_ref[...]
        fr_all = _mm_exact_r(_log_sigmoid(grow), tr)

        for p in range(2):
            q_t = ml_ref[0, :, 128 * p:128 * p + 128]
            k_t = ml_ref[0, :, 256 + 128 * p:256 + 128 * p + 128] * (HEAD_W ** -0.5)
            v_t = ml_ref[0, :, 512 + 128 * p:512 + 128 * p + 128]
            k_b = k_t.astype(BF16)
            out_pair = jnp.zeros((CHUNK, LANES), F32)
            for half in range(2):
                h = 2 * p + half
                s_idx = 4 * d + h
                gi, gf = 4 * d + h, 8 + 4 * d + h
                hmask = (lane >= 64) if half else (lane < 64)
                den_lane = 0 if half else 64
                li_c = gcol[:, gi:gi + 1]
                f_c = fc_all[:, gf:gf + 1]
                li_r = grow[gi:gi + 1, :]
                f_r = fr_all[gf:gf + 1, :]
                f_last = f_c[last:last + 1, :]
                m_old = m_ref[s_idx:s_idx + 1, 0:1]

                qm = jnp.where(hmask, q_t, 0.0).astype(BF16)
                v_ext = jnp.where(hmask, v_t, jnp.where(lane == den_lane, 1.0, 0.0)).astype(BF16)
                cn = cn_ref[s_idx]

                dm = jnp.where(vis, f_c - f_r + li_r, ninf)
                a_int = f_c + m_old
                m_i = jnp.maximum(a_int, jnp.max(dm, axis=1, keepdims=True))
                s = _mm_nt(qm, k_b) * jnp.exp(dm - m_i)
                w_int = jnp.exp(a_int - m_i)
                tot = _mm(s.astype(BF16), v_ext) + w_int * _mm(qm, cn.astype(BF16))
                den = tot[:, den_lane:den_lane + 1]
                hh = tot / jnp.maximum(jnp.abs(den), jnp.exp(-m_i))
                out_pair = jnp.where(hmask, hh, out_pair)

                w_end = f_last - f_c + li_c
                b_end = jnp.max(w_end, axis=0, keepdims=True)
                e = jnp.exp(w_end - b_end)
                ke = (jnp.where(hmask, k_t, 0.0) * e).astype(BF16)
                dcn = _mm_tn(ke, v_ext)
                m_new = jnp.maximum(f_last + m_old, b_end)
                a = jnp.exp(f_last + m_old - m_new)
                g = jnp.exp(b_end - m_new)
                cn_ref[s_idx] = a * cn + g * dcn
                m_ref[s_idx:s_idx + 1, :] = jnp.broadcast_to(m_new, (1, LANES))
            out_ref[0, :, 128 * p:128 * p + 128] = out_pair


def _mlstm(ml, gc, gt, brow, bcol, n_ctx_chunks):
    bsz, t, _ = ml.shape
    nc = t // CHUNK
    fwd = lambda b, c: (b, c, 0)
    bwd = lambda b, c: (b, _bwd_chunk(c, n_ctx_chunks, nc), 0)
    fwd_t = lambda b, c: (b, 0, c)
    bwd_t = lambda b, c: (b, 0, _bwd_chunk(c, n_ctx_chunks, nc))
    const2 = lambda b, c: (0, 0)
    out = jax.ShapeDtypeStruct((bsz, t, W_BRANCH), F32)
    return pl.pallas_call(
        _mlstm_kernel,
        grid=(bsz, nc),
        in_specs=[pl.BlockSpec((1, CHUNK, 768), fwd), pl.BlockSpec((1, CHUNK, 768), bwd),
                  pl.BlockSpec((1, CHUNK, LANES), fwd), pl.BlockSpec((1, CHUNK, LANES), bwd),
                  pl.BlockSpec((1, LANES, CHUNK), fwd_t), pl.BlockSpec((1, LANES, CHUNK), bwd_t),
                  pl.BlockSpec((1, LANES), const2), pl.BlockSpec((LANES, 1), const2)],
        out_specs=[pl.BlockSpec((1, CHUNK, W_BRANCH), fwd), pl.BlockSpec((1, CHUNK, W_BRANCH), bwd)],
        out_shape=[out, out],
        scratch_shapes=[pltpu.VMEM((8, LANES, LANES), F32), pltpu.VMEM((8, LANES), F32)],
        compiler_params=_cparams(("parallel", "arbitrary")),
        name="mlstm",
    )(ml, ml, gc, gc, gt, gt, brow, bcol)


def _ssd_kernel(n_ctx_chunks, n_chunks,
                xf_ref, xfp_ref, xfn_ref, xb_ref, xbp_ref, xbn_ref,
                gcf_ref, gcb_ref, gtf_ref, gtb_ref, brow_ref, bcol_ref, arow_ref, acol_ref,
                cw_ref, cb_ref, dskip_ref, yf_ref, yb_ref, s_ref):
    c = pl.program_id(1)

    @pl.when(c == 0)
    def _():
        s_ref[...] = jnp.zeros_like(s_ref)

    lane = _iota2((CHUNK, LANES), 1)
    row768 = _iota2((CHUNK, 768), 0)
    ninf = jnp.float32(-jnp.inf)
    for d in (0, 1):
        x_ref, xp_ref, xn_ref = ((xf_ref, xfp_ref, xfn_ref), (xb_ref, xbp_ref, xbn_ref))[d]
        out_ref = (yf_ref, yb_ref)[d]
        j = c if d == 0 else _bwd_chunk(c, n_ctx_chunks, n_chunks)
        seg_first = jnp.logical_or(j == 0, j == n_ctx_chunks)
        seg_last = jnp.logical_or(j == n_ctx_chunks - 1, j == n_chunks - 1)
        vis, vis_t = _tri_masks(d)
        tc = jnp.where(vis, 1.0, 0.0).astype(BF16)
        tr = jnp.where(vis_t, 1.0, 0.0).astype(BF16)
        last = CHUNK - 1 if d == 0 else 0

        x = x_ref[0]
        prev = jnp.where(seg_first, 0.0, xp_ref[0, 7:8, :])
        nxt = jnp.where(seg_last, 0.0, xn_ref[0, 0:1, :])
        x_dn = jnp.where(row768 == 0, prev, pltpu.roll(x, 1, 0))
        x_up = jnp.where(row768 == CHUNK - 1, nxt, pltpu.roll(x, CHUNK - 1, 0))
        xc = x_dn * cw_ref[0:1, :] + x * cw_ref[1:2, :] + x_up * cw_ref[2:3, :] + cb_ref[...]
        xa = _silu(xc)

        gcol = (gcf_ref, gcb_ref)[d][0] + brow_ref[...]
        dtc_all = _softplus(gcol)
        ac_all = _mm_exact_l(tc, dtc_all * arow_ref[...])
        grow = (gtf_ref, gtb_ref)[d][0] + bcol_ref[...]
        dtr_all = _softplus(grow)
        ar_all = _mm_exact_r(dtr_all * acol_ref[...], tr)

        for g in range(2):
            x_pair = xa[:, 128 * g:128 * g + 128]
            b_g = xa[:, 256 + 128 * g:256 + 128 * g + 128]
            c_g = xa[:, 512 + 128 * g:512 + 128 * g + 128]
            c_gb = c_g.astype(BF16)
            cb = _mm_nt(c_gb, b_g.astype(BF16))
            y_pair = jnp.zeros((CHUNK, LANES), F32)
            for half in range(2):
                h = 2 * g + half
                s_idx = 4 * d + h
                gl = 16 + 4 * d + h
                hmask = (lane >= 64) if half else (lane < 64)
                a_c = ac_all[:, gl:gl + 1]
                dt_c = dtc_all[:, gl:gl + 1]
                a_r = ar_all[gl:gl + 1, :]
                dt_r = dtr_all[gl:gl + 1, :]
                a_last = a_c[last:last + 1, :]
                st = s_ref[s_idx]

                xm = jnp.where(hmask, x_pair, 0.0).astype(BF16)
                sc = cb * jnp.exp(jnp.where(vis, a_c - a_r, ninf)) * dt_r
                y_pair = y_pair + _mm(sc.astype(BF16), xm) + jnp.exp(a_c) * _mm(c_gb, st.astype(BF16))

                wb = jnp.exp(a_last - a_c) * dt_c
                ds = _mm_tn((b_g * wb).astype(BF16), xm)
                s_ref[s_idx] = jnp.exp(a_last) * st + ds
            if d == 0:
                y_pair = y_pair + dskip_ref[:, 128 * g:128 * g + 128] * x_pair
            out_ref[0, :, 128 * g:128 * g + 128] = y_pair


def _ssd(ss, gc, gt, brow, bcol, arow, acol, conv_w, conv_b, dskip, n_ctx_chunks):
    bsz, t, _ = ss.shape
    nc = t // CHUNK
    sub = CHUNK // 8
    nsub = t // 8
    bc = lambda c: _bwd_chunk(c, n_ctx_chunks, nc)
    fwd = lambda b, c: (b, c, 0)
    bwd = lambda b, c: (b, bc(c), 0)
    fwd_p = lambda b, c: (b, jnp.maximum(c * sub - 1, 0), 0)
    fwd_n = lambda b, c: (b, jnp.minimum((c + 1) * sub, nsub - 1), 0)
    bwd_p = lambda b, c: (b, jnp.maximum(bc(c) * sub - 1, 0), 0)
    bwd_n = lambda b, c: (b, jnp.minimum((bc(c) + 1) * sub, nsub - 1), 0)
    fwd_t = lambda b, c: (b, 0, c)
    bwd_t = lambda b, c: (b, 0, bc(c))
    const2 = lambda b, c: (0, 0)
    out = jax.ShapeDtypeStruct((bsz, t, W_BRANCH), F32)
    return pl.pallas_call(
        functools.partial(_ssd_kernel, n_ctx_chunks, nc),
        grid=(bsz, nc),
        in_specs=[pl.BlockSpec((1, CHUNK, 768), fwd), pl.BlockSpec((1, 8, 768), fwd_p),
                  pl.BlockSpec((1, 8, 768), fwd_n),
                  pl.BlockSpec((1, CHUNK, 768), bwd), pl.BlockSpec((1, 8, 768), bwd_p),
                  pl.BlockSpec((1, 8, 768), bwd_n),
                  pl.BlockSpec((1, CHUNK, LANES), fwd), pl.BlockSpec((1, CHUNK, LANES), bwd),
                  pl.BlockSpec((1, LANES, CHUNK), fwd_t), pl.BlockSpec((1, LANES, CHUNK), bwd_t),
                  pl.BlockSpec((1, LANES), const2), pl.BlockSpec((LANES, 1), const2),
                  pl.BlockSpec((1, LANES), const2), pl.BlockSpec((LANES, 1), const2),
                  pl.BlockSpec((3, 768), const2), pl.BlockSpec((1, 768), const2),
                  pl.BlockSpec((1, W_BRANCH), const2)],
        out_specs=[pl.BlockSpec((1, CHUNK, W_BRANCH), fwd), pl.BlockSpec((1, CHUNK, W_BRANCH), bwd)],
        out_shape=[out, out],
        scratch_shapes=[pltpu.VMEM((8, SSM_N, LANES), F32)],
        compiler_params=_cparams(("parallel", "arbitrary")),
        name="ssd",
    )(ss, ss, ss, ss, ss, ss, gc, gc, gt, gt, brow, bcol, arow, acol, conv_w, conv_b, dskip)


def _hgrn2_kernel(hf_ref, hb_ref, loglb_ref, log1m_ref, onem_ref, e_ref, of_ref, ob_ref,
                  st_ref):
    @pl.when(pl.program_id(1) == 0)
    def _():
        st_ref[...] = jnp.zeros_like(st_ref)

    ri = _iota2((CHUNK, CHUNK), 0)
    ci = _iota2((CHUNK, CHUNK), 1)
    lane = _iota2((CHUNK, LANES), 1)
    rw = _iota2((CHUNK, W_BRANCH), 0)
    blockdiag = (ri >> 6) == (ci >> 6)
    ninf = jnp.float32(-jnp.inf)
    for d in (0, 1):
        h_ref = (hf_ref, hb_ref)[d]
        out_ref = (of_ref, ob_ref)[d]
        vis, _ = _tri_masks(d)
        tc = jnp.where(vis, 1.0, 0.0).astype(BF16)
        last = CHUNK - 1 if d == 0 else 0

        q = _silu(h_ref[0, :, 0:W_BRANCH])
        z = h_ref[0, :, W_BRANCH * (1 + d):W_BRANCH * (2 + d)]
        v = h_ref[0, :, 3 * W_BRANCH:4 * W_BRANCH]
        e = jnp.exp(-jnp.abs(z))
        ope = 1.0 + e
        la = loglb_ref[d:d + 1, :]
        lb_ = log1m_ref[d:d + 1, :] + (jnp.minimum(z, 0.0) - jnp.log(ope))
        logf = jnp.maximum(la, lb_) + jnp.log(1.0 + jnp.exp(-jnp.abs(la - lb_)))
        kk = onem_ref[d:d + 1, :] * (jnp.where(z >= 0.0, e, 1.0) / ope)
        gcum = _mm_exact_l(tc, logf)
        g_last = gcum[last:last + 1, :]

        qg = (q * jnp.exp(gcum)).astype(BF16)
        kg = (kk * jnp.exp(g_last - gcum)).astype(BF16)
        vb = v.astype(BF16)

        a_mats = [jnp.zeros((CHUNK, CHUNK), F32) for _ in range(N_HEADS)]
        blk = CHUNK // 2
        while blk >= SUB:
            first = (rw & (2 * blk - 1)) < blk
            brow_i = (blk - 1) if d == 0 else blk
            gb = gcum.reshape(CHUNK // (2 * blk), 2 * blk, W_BRANCH)[:, brow_i:brow_i + 1, :]
            gb = jnp.broadcast_to(gb, (CHUNK // (2 * blk), 2 * blk, W_BRANCH)).reshape(CHUNK, W_BRANCH)
            q_side = jnp.logical_not(first) if d == 0 else first
            qt = q * jnp.exp(jnp.where(q_side, gcum - gb, ninf))
            kt = (kk * jnp.exp(jnp.where(q_side, ninf, gb - gcum))).astype(BF16)
            same = (ri >> int(math.log2(2 * blk))) == (ci >> int(math.log2(2 * blk)))
            for h in range(N_HEADS):
                p, half = divmod(h, 2)
                hmask = (lane >= 64) if half else (lane < 64)
                qh = jnp.where(hmask, qt[:, 128 * p:128 * p + 128], 0.0).astype(BF16)
                a_mats[h] = a_mats[h] + jnp.where(same, _mm_nt(qh, kt[:, 128 * p:128 * p + 128]), 0.0)
            blk //= 2

        o_pairs = []
        for p in range(2):
            sl = slice(128 * p, 128 * p + 128)
            st = st_ref[2 * d + p]
            o_inter = _mm_nt(qg[:, sl], st.astype(BF16))
            a_cat = jnp.concatenate([a_mats[2 * p], a_mats[2 * p + 1]], axis=1).astype(BF16)
            v_p = v[:, sl]
            v_cat = jnp.concatenate([jnp.where(lane < 64, v_p, 0.0), jnp.where(lane >= 64, v_p, 0.0)],
                                    axis=0).astype(BF16)
            o_pairs.append(o_inter + _mm(a_cat, v_cat))
            dst = _mm_tn(vb[:, sl], kg[:, sl])
            st_ref[2 * d + p] = st * jnp.exp(g_last[:, sl]) + jnp.where(blockdiag, dst, 0.0)

        nb = CHUNK // SUB
        g3 = gcum.reshape(nb, SUB, W_BRANCH)
        q3 = q.reshape(nb, SUB, W_BRANCH)
        k3 = kk.reshape(nb, SUB, W_BRANCH)
        v3 = v.reshape(nb, SUB, W_BRANCH)
        rs = _iota2((nb, SUB, W_BRANCH), 1)
        o3 = jnp.concatenate(o_pairs, axis=1).reshape(nb, SUB, W_BRANCH)
        for j in range(SUB):
            ok = (rs >= j) if d == 0 else (rs <= j)
            pj = q3 * jnp.exp(jnp.where(ok, g3 - g3[:, j:j + 1, :], ninf)) * k3[:, j:j + 1, :]
            red = _mm(pj.reshape(CHUNK, W_BRANCH).astype(BF16), e_ref[...])
            o3 = o3 + red.reshape(nb, SUB, W_BRANCH) * v3[:, j:j + 1, :]
        out_ref[0] = o3.reshape(CHUNK, W_BRANCH)


def _hgrn2(hg, loglb, log1m, onem, e64, n_ctx_chunks):
    bsz, t, _ = hg.shape
    nc = t // CHUNK
    fwd = lambda b, c: (b, c, 0)
    bwd = lambda b, c: (b, _bwd_chunk(c, n_ctx_chunks, nc), 0)
    const2 = lambda b, c: (0, 0)
    out = jax.ShapeDtypeStruct((bsz, t, W_BRANCH), F32)
    return pl.pallas_call(
        _hgrn2_kernel,
        grid=(bsz, nc),
        in_specs=[pl.BlockSpec((1, CHUNK, 1024), fwd), pl.BlockSpec((1, CHUNK, 1024), bwd),
                  pl.BlockSpec((2, W_BRANCH), const2), pl.BlockSpec((2, W_BRANCH), const2),
                  pl.BlockSpec((2, W_BRANCH), const2), pl.BlockSpec((W_BRANCH, W_BRANCH), const2)],
        out_specs=[pl.BlockSpec((1, CHUNK, W_BRANCH), fwd), pl.BlockSpec((1, CHUNK, W_BRANCH), bwd)],
        out_shape=[out, out],
        scratch_shapes=[pltpu.VMEM((4, LANES, LANES), F32)],
        compiler_params=_cparams(("parallel", "arbitrary")),
        name="hgrn2",
    )(hg, hg, loglb, log1m, onem, e64)


def _attn(daq, kt, vx, lam_vecs, lam_init, n_ctx_tiles, skip_ctx):
    bsz, t, _ = daq.shape
    tq = ROW_TILE
    n_k_tiles = t // KEY_TILE
    n_lat_tiles = n_k_tiles - n_ctx_tiles
    group = math.gcd(KEY_GROUP, n_lat_tiles)
    q_off = n_ctx_tiles if skip_ctx else 0
    nq = t // tq - q_off
    n_maps = W_BRANCH // DA_QK
    rows = n_maps * tq

    def kern(q_ref, kt_ref, vx_ref, lv_ref, o_ref, q8_ref, m_ref, acc_ref):
        qi = pl.program_id(1) + q_off
        lane = _iota2((tq, W_BRANCH), 1)
        q = q_ref[0]
        for j in range(n_maps):
            q8_ref[j * tq:(j + 1) * tq, :] = jnp.where((lane >> 5) == j, q, jnp.zeros_like(q))
        m_ref[...] = jnp.full_like(m_ref, -jnp.inf)
        acc_ref[...] = jnp.zeros_like(acc_ref)

        def update(t0, n):
            q8 = q8_ref[...]
            s = [_mm(q8, kt_ref[0, t0 + i]) for i in range(n)]
            mx = s[0][:, 0:LANES]
            for i in range(n):
                for c in range(KEY_TILE // LANES):
                    if i or c:
                        mx = jnp.maximum(mx, s[i][:, LANES * c:LANES * (c + 1)])
            m_old = m_ref[...]
            m_new = jnp.maximum(m_old, jnp.max(mx, axis=1, keepdims=True))
            alpha = jnp.exp(m_old - m_new)
            m_ref[...] = m_new
            m2 = jnp.concatenate([m_new] * (KEY_TILE // LANES), axis=1)
            p = [jnp.exp(s[i] - m2).astype(BF16) for i in range(n)]
            for hd in range(N_HEADS):
                r0 = 2 * hd * tq
                pv = _mm(p[0][r0:r0 + 2 * tq], vx_ref[0, t0, hd])
                for i in range(1, n):
                    pv = pv + _mm(p[i][r0:r0 + 2 * tq], vx_ref[0, t0 + i, hd])
                acc_ref[r0:r0 + 2 * tq, :] = alpha[r0:r0 + 2 * tq] * acc_ref[r0:r0 + 2 * tq, :] + pv

        update(0, n_ctx_tiles)

        @pl.when(qi >= n_ctx_tiles)
        def _():
            def body(g, carry):
                update(n_ctx_tiles + g * group, group)
                return carry
            lax.fori_loop(0, n_lat_tiles // group, body, 0)

        lv = lv_ref[...]
        lam = (jnp.exp(jnp.sum(lv[0:1, :] * lv[1:2, :], axis=1, keepdims=True))
               - jnp.exp(jnp.sum(lv[2:3, :] * lv[3:4, :], axis=1, keepdims=True)) + lam_init)
        lane1 = _iota2((tq, LANES), 1)
        for pr in range(N_HEADS // 2):
            halves = []
            for half in range(2):
                r0 = 2 * (2 * pr + half) * tq
                one = 0 if half else HEAD_W
                a0 = acc_ref[r0:r0 + tq, :]
                a1 = acc_ref[r0 + tq:r0 + 2 * tq, :]
                halves.append(a0 / a0[:, one:one + 1] - lam * (a1 / a1[:, one:one + 1]))
            o_ref[0, :, LANES * pr:LANES * (pr + 1)] = jnp.where(lane1 < HEAD_W, halves[0], halves[1])

    return pl.pallas_call(
        kern,
        grid=(bsz, nq),
        in_specs=[pl.BlockSpec((1, tq, W_BRANCH), lambda b, i: (b, i + q_off, 0)),
                  pl.BlockSpec((1, n_k_tiles, W_BRANCH, KEY_TILE), lambda b, i: (b, 0, 0, 0)),
                  pl.BlockSpec((1, n_k_tiles, N_HEADS, KEY_TILE, LANES), lambda b, i: (b, 0, 0, 0, 0)),
                  pl.BlockSpec(lam_vecs.shape, lambda b, i: (0, 0))],
        out_specs=pl.BlockSpec((1, tq, W_BRANCH), lambda b, i: (b, i, 0)),
        out_shape=jax.ShapeDtypeStruct((bsz, nq * tq, W_BRANCH), F32),
        scratch_shapes=[pltpu.VMEM((rows, W_BRANCH), BF16),
                        pltpu.VMEM((rows, LANES), F32),
                        pltpu.VMEM((rows, LANES), F32)],
        compiler_params=_cparams(("parallel", "parallel")),
        name="diff_attn",
    )(daq, kt, vx, lam_vecs)


def _seg_mean(x, e_ref):
    x0 = x.astype(BF16)
    x1 = (x - x0.astype(F32)).astype(BF16)
    return (_mm(x0, e_ref[...]) + _mm(x1, e_ref[...])) * (1.0 / HEAD_W)


def _out_kernel(n_ctx_tiles, ctx_row, q_off, lam_init, final,
                x_ref, mo_ref, mlo_ref, mlz_ref, mhf_ref, mhb_ref, dao_ref, daz_ref,
                syf_ref, syb_ref, ssz_ref, hof_ref, hob_ref, hgz_ref,
                mlg_ref, dag_ref, ssg_ref, hgg_ref, e_ref, w_ref, fg_ref, o_ref):
    b = pl.program_id(0)
    i = pl.program_id(1) + q_off
    d = x_ref.shape[2]
    r = jnp.where(i < n_ctx_tiles, ctx_row, b)
    gate = mo_ref[pl.ds(r, 1), :][:, 2 * d:3 * d]

    u = _sigmoid(mlo_ref[0]) * (mhf_ref[0] + mhb_ref[0])
    dev = u - _seg_mean(u, e_ref)
    y_ml = dev * lax.rsqrt(_seg_mean(dev * dev, e_ref) + EPS) * mlg_ref[...] * _silu(mlz_ref[0])

    o = dao_ref[0]
    y_da = (o * lax.rsqrt(_seg_mean(o * o, e_ref) + EPS) * dag_ref[...]) * (1.0 - lam_init) * _silu(daz_ref[0])

    ys = (syf_ref[0] + syb_ref[0]) * _silu(ssz_ref[0])
    parts = []
    for g in range(2):
        yg = ys[:, 128 * g:128 * g + 128]
        parts.append(yg * lax.rsqrt(jnp.mean(yg * yg, axis=-1, keepdims=True) + EPS))
    y_ss = jnp.concatenate(parts, axis=1) * ssg_ref[...]

    oh = hof_ref[0] + hob_ref[0]
    y_hg = (oh * lax.rsqrt(_seg_mean(oh * oh, e_ref) + EPS) * hgg_ref[...]) * _silu(hgz_ref[0])

    acc = _mm(y_ml.astype(BF16), w_ref[0:W_BRANCH, :])
    acc = acc + _mm(y_da.astype(BF16), w_ref[W_BRANCH:2 * W_BRANCH, :])
    acc = acc + _mm(y_ss.astype(BF16), w_ref[2 * W_BRANCH:3 * W_BRANCH, :])
    acc = acc + _mm(y_hg.astype(BF16), w_ref[3 * W_BRANCH:4 * W_BRANCH, :])
    x_new = x_ref[0] + gate * acc
    if final:
        x_new = x_new * lax.rsqrt(jnp.mean(x_new * x_new, axis=-1, keepdims=True) + EPS) * fg_ref[...]
    o_ref[0] = x_new


def _out_proj(xs, mo, ml, mhf, mhb, dao, daz, ss, syf, syb, hg, hof, hob,
              mlg, dag, ssg, hgg, e64, w_out, final_g, lam_init, n_ctx_tiles, ctx_row, final):
    bsz, t, d = xs.shape
    tm = ROW_TILE
    q_off = n_ctx_tiles if final else 0
    nrow = t // tm - q_off
    row = lambda b, i: (b, i + q_off, 0)
    col = lambda k: (lambda b, i: (b, i + q_off, k))
    const2 = lambda b, i: (0, 0)
    wb = pl.BlockSpec((1, tm, W_BRANCH), row)
    return pl.pallas_call(
        functools.partial(_out_kernel, n_ctx_tiles, ctx_row, q_off, lam_init, final),
        grid=(bsz, nrow),
        in_specs=[pl.BlockSpec((1, tm, d), row),
                  pl.BlockSpec(mo.shape, const2),
                  pl.BlockSpec((1, tm, W_BRANCH), col(3)), pl.BlockSpec((1, tm, W_BRANCH), col(4)),
                  wb, wb,
                  pl.BlockSpec((1, tm, W_BRANCH), lambda b, i: (b, i, 0)) if final else wb,
                  wb,
                  wb, wb, pl.BlockSpec((1, tm, W_BRANCH), col(3)),
                  wb, wb, pl.BlockSpec((1, tm, W_BRANCH), col(4)),
                  pl.BlockSpec((1, W_BRANCH), const2), pl.BlockSpec((1, W_BRANCH), const2),
                  pl.BlockSpec((1, W_BRANCH), const2), pl.BlockSpec((1, W_BRANCH), const2),
                  pl.BlockSpec((W_BRANCH, W_BRANCH), const2),
                  pl.BlockSpec(w_out.shape, const2),
                  pl.BlockSpec((1, d), const2)],
        out_specs=pl.BlockSpec((1, tm, d), lambda b, i: (b, i, 0)),
        out_shape=jax.ShapeDtypeStruct((bsz, nrow * tm, d), F32),
        compiler_params=_cparams(("parallel", "parallel")),
        name="out_proj",
    )(xs, mo, ml, ml, mhf, mhb, dao, daz, syf, syb, ss, hof, hob, hg,
      mlg, dag, ssg, hgg, e64, w_out, final_g.reshape(1, d))


def _relayout_w_in(w_in):
    d = w_in.shape[0]
    o = {}
    off = 0
    for name, n in (("ml_q", 256), ("ml_k", 256), ("ml_v", 256), ("ml_o", 256), ("ml_i", 8), ("ml_f", 8),
                    ("ml_z", 256), ("da_q", 256), ("da_k", 256), ("da_v", 256), ("da_z", 256),
                    ("ss_xbc", 768), ("ss_dt", 8), ("ss_z", 256),
                    ("hg_q", 256), ("hg_f", 512), ("hg_i", 256), ("hg_z", 256)):
        o[name] = w_in[:, off:off + n]
        off += n

    def rot(w):
        g = w.reshape(d, W_BRANCH // DA_QK, 2, DA_QK // 2)
        return jnp.stack([-g[:, :, 1], g[:, :, 0]], axis=2).reshape(d, W_BRANCH)

    gates = jnp.concatenate([o["ml_i"], o["ml_f"], o["ss_dt"], jnp.zeros((d, LANES - 24), w_in.dtype)], axis=1)
    zpad = jnp.zeros((d, HEAD_W), w_in.dtype)
    v_ext = []
    for hd in range(N_HEADS):
        v_h = o["da_v"][:, HEAD_W * hd:HEAD_W * (hd + 1)]
        v_ext += [zpad, v_h] if hd % 2 else [v_h, zpad]
    w2 = jnp.concatenate([o["ml_q"], o["ml_k"], o["ml_v"], o["ml_o"], o["ml_z"],
                          o["da_q"], rot(o["da_q"])] + v_ext + [o["da_z"],
                          o["ss_xbc"], o["ss_z"],
                          o["hg_q"], o["hg_f"], o["hg_i"], o["hg_z"], gates], axis=1)
    wt = jnp.concatenate([o["da_k"], rot(o["da_k"]), gates], axis=1).T
    return w2.astype(BF16), wt.astype(BF16)


def _rope_tables(n_ctx, seq):
    pos = jnp.arange(seq)
    rows = (pos // GRID_W).astype(F32)
    cols = (pos % GRID_W).astype(F32)
    axis = DA_QK // 2
    inv = ROPE_BASE ** (-jnp.arange(0, axis, 2, dtype=F32) / axis)
    ang = jnp.concatenate([rows[:, None] * inv, cols[:, None] * inv], axis=-1)
    cos = jnp.concatenate([jnp.ones((n_ctx, axis), F32), jnp.cos(ang)], axis=0)
    sin = jnp.concatenate([jnp.zeros((n_ctx, axis), F32), jnp.sin(ang)], axis=0)
    reps = W_BRANCH // axis
    return jnp.tile(cos, (1, reps)), jnp.tile(sin, (1, reps))


def _gate_vec(vals):
    v = jnp.concatenate([vals.astype(F32), jnp.zeros((LANES - vals.shape[0],), F32)])
    return v.reshape(1, LANES), v.reshape(LANES, 1)


def kernel(x, c, ctx, c_ctx, w_mod, b_mod, norm_g, w_in, w_out, ml_gate_b, ml_norm_g, da_lambda, da_norm_g,
           ss_conv_w, ss_conv_b, ss_dt_bias, ss_a_log, ss_d, ss_norm_g, hg_lower, hg_norm_g, final_g):
    bsz, seq, d = x.shape
    n_ctx = ctx.shape[1]
    depth = w_mod.shape[0]
    assert n_ctx % ROW_TILE == 0 and seq % ROW_TILE == 0 and bsz < 8
    n_ctx_tiles = n_ctx // ROW_TILE
    n_ctx_chunks = n_ctx // CHUNK
    ctx_row = bsz

    xs = jnp.concatenate([ctx, x], axis=1)
    cc = jnp.concatenate([c, c_ctx[None, :], jnp.zeros((8 - bsz - 1, d), F32)], axis=0)
    cos, sin = _rope_tables(n_ctx, seq)
    cos_t, sin_t = cos.T, sin.T
    one_lanes = jnp.arange(V_EXT) % LANES == jnp.where((jnp.arange(V_EXT) // LANES) % 2 == 1, 0, HEAD_W)
    ones_row = one_lanes.astype(F32).reshape(1, V_EXT)
    lb_all = jnp.cumsum(jax.nn.softmax(hg_lower.astype(F32), axis=1), axis=1)
    lb_all = lb_all - lb_all[:, :1]
    hid = _iota2((W_BRANCH, W_BRANCH), 0) // HEAD_W
    e64 = (hid == hid.T).astype(BF16)

    out = None
    for l in range(depth):
        lam_init = 0.8 - 0.6 * math.exp(-0.3 * l)
        final = l == depth - 1
        w2, wt = _relayout_w_in(w_in[l])
        mo = _modulation(cc, w_mod[l], b_mod[l])
        ml, daq, dak, dav, daz, ss, hg, gc, gt = _in_proj(xs, mo, norm_g[l], cos, sin, cos_t, sin_t, w2, wt,
                                                          ones_row, n_ctx_tiles, ctx_row)

        gb = ml_gate_b[l]
        zeros8 = jnp.zeros((8,), F32)
        brow, bcol = _gate_vec(jnp.concatenate([gb[:, 0].reshape(-1), gb[:, 1].reshape(-1),
                                                ss_dt_bias[l].reshape(-1)]))
        arow, acol = _gate_vec(jnp.concatenate([zeros8, zeros8, -jnp.exp(ss_a_log[l].astype(F32)).reshape(-1)]))
        mhf, mhb = _mlstm(ml, gc, gt, brow, bcol, n_ctx_chunks)
        dao = _attn(daq, dak, dav, da_lambda[l].astype(F32), lam_init, n_ctx_tiles, final)
        dskip = jnp.repeat(ss_d[l].astype(F32), HEAD_W).reshape(1, W_BRANCH)
        syf, syb = _ssd(ss, gc, gt, brow, bcol, arow, acol, ss_conv_w[l], ss_conv_b[l].reshape(1, -1),
                        dskip, n_ctx_chunks)
        lbh = lb_all[:, l]
        hof, hob = _hgrn2(hg, jnp.log(lbh), jnp.log1p(-lbh), 1.0 - lbh, e64, n_ctx_chunks)

        res = _out_proj(xs, mo, ml, mhf, mhb, dao, daz, ss, syf, syb, hg, hof, hob,
                        ml_norm_g[l].reshape(1, -1), jnp.tile(da_norm_g[l], N_HEADS).reshape(1, -1),
                        ss_norm_g[l].reshape(1, -1), hg_norm_g[l].reshape(1, -1), e64,
                        w_out[l].astype(BF16), final_g, lam_init, n_ctx_tiles, ctx_row, final)
        if final:
            out = res
        else:
            xs = res
    return out
```

```python
import functools
import math

import jax
import jax.numpy as jnp
from jax import lax
from jax.experimental import pallas as pl
from jax.experimental.pallas import tpu as pltpu

F32 = jnp.float32
BF16 = jnp.bfloat16

EPS = 1e-6
GRID_W = 64
ROPE_BASE = 10000.0
N_HEADS = 4
HEAD_W = 64
W_BRANCH = N_HEADS * HEAD_W
DA_QK = 32
SSM_N = 128
CHUNK = 128
ROW_TILE = 256
SUB = 8
LANES = 128
VMEM_LIMIT = 56 * 1024 * 1024

C_ML = 0
C_DAQ = 1280
C_DAQROT = 1536
C_DAV = 1792
C_DAZ = 2304
C_SS = 2560
C_HG = 3584
C_END = 4864
V_EXT = N_HEADS * LANES
KEY_TILE = 256
KEY_GROUP = 4

NT = (((1,), (1,)), ((), ()))
TN = (((0,), (0,)), ((), ()))


def _mm(a, b):
    return jnp.dot(a, b, preferred_element_type=F32)


def _mm_nt(a, b):
    return lax.dot_general(a, b, NT, preferred_element_type=F32)


def _mm_tn(a, b):
    return lax.dot_general(a, b, TN, preferred_element_type=F32)


def _split3(x):
    x0 = x.astype(BF16)
    r = x - x0.astype(F32)
    x1 = r.astype(BF16)
    x2 = (r - x1.astype(F32)).astype(BF16)
    return x0, x1, x2


def _mm_exact_l(t, x):
    x0, x1, x2 = _split3(x)
    return _mm(t, x0) + _mm(t, x1) + _mm(t, x2)


def _mm_exact_r(x, t):
    x0, x1, x2 = _split3(x)
    return _mm(x0, t) + _mm(x1, t) + _mm(x2, t)


def _sigmoid(x):
    return 1.0 / (1.0 + jnp.exp(-x))


def _silu(x):
    return x * _sigmoid(x)


def _log_sigmoid(x):
    return jnp.minimum(x, 0.0) - jnp.log(1.0 + jnp.exp(-jnp.abs(x)))


def _softplus(x):
    return jnp.maximum(x, 0.0) + jnp.log(1.0 + jnp.exp(-jnp.abs(x)))


def _iota2(shape, axis):
    return lax.broadcasted_iota(jnp.int32, shape, axis)


def _cparams(sem):
    return pltpu.CompilerParams(dimension_semantics=sem, vmem_limit_bytes=VMEM_LIMIT)


def _mod_kernel(c_ref, w_ref, b_ref, o_ref):
    c = c_ref[...]
    o_ref[...] = jnp.dot(_silu(c), w_ref[...], precision=lax.Precision.HIGHEST,
                         preferred_element_type=F32) + b_ref[...]


def _modulation(cc, w_mod, b_mod):
    d = cc.shape[1]
    n = w_mod.shape[1]
    tn = 512
    return pl.pallas_call(
        _mod_kernel,
        grid=(n // tn,),
        in_specs=[pl.BlockSpec((8, d), lambda j: (0, 0)),
                  pl.BlockSpec((d, tn), lambda j: (0, j)),
                  pl.BlockSpec((1, tn), lambda j: (0, j))],
        out_specs=pl.BlockSpec((8, tn), lambda j: (0, j)),
        out_shape=jax.ShapeDtypeStruct((8, n), F32),
        compiler_params=_cparams(("arbitrary",)),
        name="modulation",
    )(cc, w_mod, b_mod.reshape(1, n))


def _in_kernel(n_ctx_tiles, ctx_row, x_ref, mo_ref, g_ref, cos_ref, sin_ref, cost_ref, sint_ref,
               w_ref, wt_ref, ones_ref,
               ml_ref, daq_ref, kt_ref, vx_ref, daz_ref, ss_ref, hg_ref, gt_ref, mkt_ref, hvt_ref):
    b = pl.program_id(0)
    i = pl.program_id(1)
    d = x_ref.shape[2]
    r = jnp.where(i < n_ctx_tiles, ctx_row, b)
    mo = mo_ref[pl.ds(r, 1), :]
    shift = mo[:, 0:d]
    scale = mo[:, d:2 * d]
    x = x_ref[0]
    y = x * lax.rsqrt(jnp.mean(x * x, axis=-1, keepdims=True) + EPS) * g_ref[...]
    h = (y * (1.0 + scale) + shift).astype(BF16)

    ml_ref[0] = _mm(h, w_ref[:, C_ML:C_DAQ])
    q = _mm(h, w_ref[:, C_DAQ:C_DAQROT]) * cos_ref[...] + _mm(h, w_ref[:, C_DAQROT:C_DAV]) * sin_ref[...]
    daq_ref[0] = (q * (DA_QK ** -0.5)).astype(BF16)
    vx = (_mm(h, w_ref[:, C_DAV:C_DAZ]) + ones_ref[...]).astype(BF16)
    for hd in range(N_HEADS):
        vx_ref[0, 0, hd] = vx[:, LANES * hd:LANES * (hd + 1)]
    daz_ref[0] = _mm(h, w_ref[:, C_DAZ:C_SS])
    ss_ref[0] = _mm(h, w_ref[:, C_SS:C_HG])
    hg_ref[0] = _mm(h, w_ref[:, C_HG:C_END])
    tr = _mm_nt(wt_ref[...], h)
    kt_ref[0, 0] = (tr[0:W_BRANCH] * cost_ref[...] + tr[W_BRANCH:2 * W_BRANCH] * sint_ref[...]).astype(BF16)
    gt_ref[0] = tr[2 * W_BRANCH:2 * W_BRANCH + LANES]
    mkt_ref[0] = tr[2 * W_BRANCH + LANES:3 * W_BRANCH + LANES]
    hvt_ref[0] = tr[3 * W_BRANCH + LANES:4 * W_BRANCH + LANES]


def _in_proj(xs, mo, g, cos, sin, cos_t, sin_t, w2, wt, ones_row, n_ctx_tiles, ctx_row):
    bsz, t, d = xs.shape
    tm = ROW_TILE
    assert tm == KEY_TILE
    row = lambda b, i: (b, i, 0)
    const2 = lambda b, i: (0, 0)
    out_shape = [
        jax.ShapeDtypeStruct((bsz, t, 1280), F32),
        jax.ShapeDtypeStruct((bsz, t, W_BRANCH), BF16),
        jax.ShapeDtypeStruct((bsz, t // tm, W_BRANCH, tm), BF16),
        jax.ShapeDtypeStruct((bsz, t // tm, N_HEADS, tm, LANES), BF16),
        jax.ShapeDtypeStruct((bsz, t, W_BRANCH), F32),
        jax.ShapeDtypeStruct((bsz, t, 1024), F32),
        jax.ShapeDtypeStruct((bsz, t, 1280), F32),
        jax.ShapeDtypeStruct((bsz, LANES, t), F32),
        jax.ShapeDtypeStruct((bsz, W_BRANCH, t), F32),
        jax.ShapeDtypeStruct((bsz, W_BRANCH, t), F32),
    ]
    out_specs = [
        pl.BlockSpec((1, tm, 1280), row),
        pl.BlockSpec((1, tm, W_BRANCH), row),
        pl.BlockSpec((1, 1, W_BRANCH, tm), lambda b, i: (b, i, 0, 0)),
        pl.BlockSpec((1, 1, N_HEADS, tm, LANES), lambda b, i: (b, i, 0, 0, 0)),
        pl.BlockSpec((1, tm, W_BRANCH), row),
        pl.BlockSpec((1, tm, 1024), row),
        pl.BlockSpec((1, tm, 1280), row),
        pl.BlockSpec((1, LANES, tm), lambda b, i: (b, 0, i)),
        pl.BlockSpec((1, W_BRANCH, tm), lambda b, i: (b, 0, i)),
        pl.BlockSpec((1, W_BRANCH, tm), lambda b, i: (b, 0, i)),
    ]
    return pl.pallas_call(
        functools.partial(_in_kernel, n_ctx_tiles, ctx_row),
        grid=(bsz, t // tm),
        in_specs=[pl.BlockSpec((1, tm, d), row),
                  pl.BlockSpec(mo.shape, const2),
                  pl.BlockSpec((1, d), const2),
                  pl.BlockSpec((tm, W_BRANCH), lambda b, i: (i, 0)),
                  pl.BlockSpec((tm, W_BRANCH), lambda b, i: (i, 0)),
                  pl.BlockSpec((W_BRANCH, tm), lambda b, i: (0, i)),
                  pl.BlockSpec((W_BRANCH, tm), lambda b, i: (0, i)),
                  pl.BlockSpec(w2.shape, const2),
                  pl.BlockSpec(wt.shape, const2),
                  pl.BlockSpec((1, V_EXT), const2)],
        out_specs=out_specs,
        out_shape=out_shape,
        compiler_params=_cparams(("parallel", "parallel")),
        name="in_proj",
    )(xs, mo, g.reshape(1, d), cos, sin, cos_t, sin_t, w2, wt, ones_row)


def _bwd_chunk(c, n_ctx_chunks, n_chunks):
    return jnp.where(c < n_ctx_chunks, n_ctx_chunks - 1 - c, n_chunks + n_ctx_chunks - 1 - c)


def _tri_masks(d):
    ri = _iota2((CHUNK, CHUNK), 0)
    ci = _iota2((CHUNK, CHUNK), 1)
    if d == 0:
        return ci <= ri, ri <= ci
    return ci >= ri, ri >= ci


def _split_f32(x, n):
    parts = []
    for _ in range(n):
        p = x.astype(BF16).astype(F32)
        parts.append(p)
        x = x - p
    return parts


def _cummax_lanes(u, d):
    lane = _iota2(u.shape, 1)
    ninf = jnp.float32(-jnp.inf)
    s = 1
    while s < CHUNK:
        if d == 0:
            sh = jnp.where(lane >= s, pltpu.roll(u, s, 1), ninf)
        else:
            sh = jnp.where(lane < CHUNK - s, pltpu.roll(u, CHUNK - s, 1), ninf)
        u = jnp.maximum(u, sh)
        s *= 2
    return u


def _state_row_selector(n_tiles):
    keep = (jnp.arange(CHUNK)[None, :] % 8) == jnp.arange(8)[:, None]
    return jnp.broadcast_to(keep[:, :, None], (8, CHUNK, n_tiles * LANES)).astype(BF16)


def _spread_selector(n_spread):
    t = jnp.arange(CHUNK)[:, None] // 16
    tiles = [jnp.broadcast_to(t == 3 + k, (CHUNK, LANES)) for k in range(n_spread)]
    return jnp.concatenate(tiles, axis=1).astype(BF16)


def _spread_operands(col_sum, row_sum, spread, sel_ref, ones_ref):
    z8 = jnp.zeros((8, CHUNK), F32)
    one8 = ones_ref[0:8, :].astype(F32)
    pieces = _split_f32(col_sum, 3) + [one8] * 3
    for x in spread:
        pieces += _split_f32(x, 2)
    assert len(pieces) <= CHUNK // 8
    lhs = jnp.concatenate(pieces + [z8] * (CHUNK // 8 - len(pieces)), axis=0).T.astype(BF16)
    rows = jnp.concatenate([one8] * 3 + _split_f32(row_sum, 3) + [z8] * (CHUNK // 8 - 6), axis=0)
    return lhs, jnp.concatenate([rows.astype(BF16), sel_ref[...]], axis=1)


def _mlstm_kernel(qf_ref, vf_ref, ktf_ref, gtf_ref, qb_ref, vb_ref, ktb_ref, gtb_ref, bcol_ref, ones_ref, sel_ref,
                  rsel_ref, hf_ref, hb_ref, cn_ref, m_ref):
    @pl.when(pl.program_id(1) == 0)
    def _():
        cn_ref[...] = jnp.zeros_like(cn_ref)
        m_ref[...] = jnp.zeros_like(m_ref)

    lane = _iota2((CHUNK, LANES), 1)
    feat = _iota2((LANES, CHUNK), 0)
    row8 = _iota2((8, LANES), 0)
    ones_t = ones_ref[...]
    ninf = jnp.float32(-jnp.inf)
    m_all = m_ref[...]
    heads = [None] * 8
    gates = []
    f_cum = []
    for d in (0, 1):
        gt_ref = (gtf_ref, gtb_ref)[d]
        tr = jnp.where(_tri_masks(d)[1], 1.0, 0.0).astype(BF16)
        f_cum.append(_mm_exact_r(_log_sigmoid(gt_ref[0, 8:16, :] + bcol_ref[8:16, :]), tr))

    for d in (0, 1):
        q_ref, v_ref, kt_ref = ((qf_ref, vf_ref, ktf_ref), (qb_ref, vb_ref, ktb_ref))[d]
        for p in range(2):
            q_t = q_ref[0, :, 128 * p:128 * p + 128]
            v_t = v_ref[0, :, 128 * p:128 * p + 128]
            kt_p = kt_ref[0, 128 * p:128 * p + 128, :] * (HEAD_W ** -0.5)
            kt_b = kt_p.astype(BF16)
            for half in range(2):
                r = 4 * d + 2 * p + half
                hmask = (lane >= 64) if half else (lane < 64)
                qm = jnp.where(hmask, q_t, 0.0).astype(BF16)
                cn = cn_ref[r]
                heads[r] = dict(
                    cn=cn, kt_p=kt_p,
                    vw=jnp.concatenate([jnp.where(hmask, v_t, 0.0).astype(BF16), ones_t], axis=1),
                    qk=_mm(qm, kt_b),
                    qc=_mm(qm, cn.astype(BF16)))

    for d in (0, 1):
        gt_ref = (gtf_ref, gtb_ref)[d]
        last = CHUNK - 1 if d == 0 else 0
        li = gt_ref[0, 0:8, :] + bcol_ref[0:8, :]
        f = f_cum[d]
        u = li - f
        m_old = m_all
        mx = jnp.maximum(m_old, _cummax_lanes(u, d))
        w_int = jnp.exp(m_old - mx)
        e_mi = jnp.exp(-(f + mx))
        u_max = jnp.max(u, axis=1, keepdims=True)
        e_end = jnp.exp(u - u_max)
        f_last = f[:, last:last + 1]
        b_end = f_last + u_max
        m_new = jnp.maximum(f_last + m_old, b_end)
        a_dec = jnp.exp(f_last + m_old - m_new)
        g_inc = jnp.exp(b_end - m_new)
        m_all = jnp.where((row8 >> 2) == d, m_new, m_all)
        gates.append((e_end, a_dec, g_inc) + _spread_operands(-mx, u, [w_int, e_mi], sel_ref, ones_ref))

    for r in range(8):
        hd = heads[r]
        e_end, a_dec, g_inc, lhs, rhs = gates[r // 4]
        fmask = (feat >= 64) if r % 2 else (feat < 64)
        ke = (jnp.where(fmask, hd["kt_p"], 0.0) * e_end[r:r + 1, :]).astype(BF16)
        a_r = a_dec[r:r + 1, :]
        g_r = g_inc[r:r + 1, :]
        hd["mt"] = _mm(lhs, rhs * rsel_ref[r])
        hd["cn_new"] = (jnp.concatenate([a_r, a_r], axis=1) * hd["cn"]
                        + jnp.concatenate([g_r, g_r], axis=1) * _mm(ke, hd["vw"]))

    for r in range(8):
        hd = heads[r]
        vis, _ = _tri_masks(r // 4)
        s = hd["qk"] * jnp.exp(jnp.where(vis, hd["mt"][:, 0:128], ninf))
        hd["sv"] = _mm(s.astype(BF16), hd["vw"])

    for d in (0, 1):
        out_ref = (hf_ref, hb_ref)[d]
        for p in range(2):
            pair = []
            for half in range(2):
                hd = heads[4 * d + 2 * p + half]
                w_b = hd["mt"][:, 128:256]
                tot = hd["sv"] + jnp.concatenate([w_b, w_b], axis=1) * hd["qc"]
                pair.append(tot[:, 0:128] / jnp.maximum(jnp.abs(tot[:, 128:256]), hd["mt"][:, 256:384]))
            out_ref[0, :, 128 * p:128 * p + 128] = jnp.where(lane < 64, pair[0], pair[1])
    m_ref[...] = m_all
    for r in range(8):
        cn_ref[r] = heads[r]["cn_new"]


def _mlstm(ml, mkt, gt, bcol, n_ctx_chunks):
    bsz, t, _ = ml.shape
    nc = t // CHUNK
    out = jax.ShapeDtypeStruct((bsz, t, W_BRANCH), F32)
    const2 = lambda b, c: (0, 0)

    def chunk_specs(chunk):
        return [pl.BlockSpec((1, CHUNK, W_BRANCH), lambda b, c: (b, chunk(c), 0)),
                pl.BlockSpec((1, CHUNK, W_BRANCH), lambda b, c: (b, chunk(c), 2)),
                pl.BlockSpec((1, W_BRANCH, CHUNK), lambda b, c: (b, 0, chunk(c))),
                pl.BlockSpec((1, LANES, CHUNK), lambda b, c: (b, 0, chunk(c)))]

    fwd_chunk = lambda c: c
    bwd_chunk = lambda c: _bwd_chunk(c, n_ctx_chunks, nc)
    return pl.pallas_call(
        _mlstm_kernel,
        grid=(bsz, nc),
        in_specs=chunk_specs(fwd_chunk) + chunk_specs(bwd_chunk) + [
            pl.BlockSpec((LANES, 1), const2), pl.BlockSpec((CHUNK, LANES), const2),
            pl.BlockSpec((CHUNK, 2 * LANES), const2),
            pl.BlockSpec((8, CHUNK, 3 * LANES), lambda b, c: (0, 0, 0))],
        out_specs=[pl.BlockSpec((1, CHUNK, W_BRANCH), lambda b, c: (b, c, 0)),
                   pl.BlockSpec((1, CHUNK, W_BRANCH), lambda b, c: (b, bwd_chunk(c), 0))],
        out_shape=[out, out],
        scratch_shapes=[pltpu.VMEM((8, LANES, 2 * LANES), F32), pltpu.VMEM((8, LANES), F32)],
        compiler_params=_cparams(("parallel", "arbitrary")),
        name="mlstm",
    )(ml, ml, mkt, gt, ml, ml, mkt, gt, bcol, jnp.ones((CHUNK, LANES), BF16), _spread_selector(2),
      _state_row_selector(3))


def _ssd_kernel(n_ctx_chunks, n_chunks,
                xf_ref, xfp_ref, xfn_ref, xb_ref, xbp_ref, xbn_ref,
                gtf_ref, gtb_ref, bcol_ref, acol_ref, cw_ref, cb_ref, dskip_ref, ones_ref, sel_ref, rsel_ref,
                yf_ref, yb_ref, s_ref):
    c = pl.program_id(1)

    @pl.when(c == 0)
    def _():
        s_ref[...] = jnp.zeros_like(s_ref)

    lane = _iota2((CHUNK, LANES), 1)
    row768 = _iota2((CHUNK, 768), 0)
    ninf = jnp.float32(-jnp.inf)

    dt, a_cum = [], []
    for d in (0, 1):
        gt_ref = (gtf_ref, gtb_ref)[d]
        tr = jnp.where(_tri_masks(d)[1], 1.0, 0.0).astype(BF16)
        dt.append(_softplus(gt_ref[0, 16:24, :] + bcol_ref[16:24, :]))
        a_cum.append(_mm_exact_r(dt[d] * acol_ref[16:24, :], tr))

    xa = []
    for d in (0, 1):
        x_ref, xp_ref, xn_ref = ((xf_ref, xfp_ref, xfn_ref), (xb_ref, xbp_ref, xbn_ref))[d]
        j = c if d == 0 else _bwd_chunk(c, n_ctx_chunks, n_chunks)
        seg_first = jnp.logical_or(j == 0, j == n_ctx_chunks)
        seg_last = jnp.logical_or(j == n_ctx_chunks - 1, j == n_chunks - 1)
        x = x_ref[0]
        prev = jnp.where(seg_first, 0.0, xp_ref[0, 7:8, :])
        nxt = jnp.where(seg_last, 0.0, xn_ref[0, 0:1, :])
        x_dn = jnp.where(row768 == 0, prev, pltpu.roll(x, 1, 0))
        x_up = jnp.where(row768 == CHUNK - 1, nxt, pltpu.roll(x, CHUNK - 1, 0))
        xa.append(_silu(x_dn * cw_ref[0:1, :] + x * cw_ref[1:2, :] + x_up * cw_ref[2:3, :] + cb_ref[...]))

    heads = [None] * 8
    groups = {}
    for d in (0, 1):
        for g in range(2):
            x_pair = xa[d][:, 128 * g:128 * g + 128]
            b_g = xa[d][:, 256 + 128 * g:256 + 128 * g + 128]
            c_gb = xa[d][:, 512 + 128 * g:512 + 128 * g + 128].astype(BF16)
            groups[d, g] = dict(x_pair=x_pair, bt=b_g.T.astype(BF16), cb=_mm_nt(c_gb, b_g.astype(BF16)))
            for half in range(2):
                r = 4 * d + 2 * g + half
                hmask = (lane >= 64) if half else (lane < 64)
                st = s_ref[r]
                xm = jnp.where(hmask, x_pair, 0.0)
                heads[r] = dict(st=st, xm=xm, xmb=xm.astype(BF16), cs=_mm(c_gb, st.astype(BF16)))

    for d in (0, 1):
        last = CHUNK - 1 if d == 0 else 0
        a_last = a_cum[d][:, last:last + 1]
        w_in = jnp.exp(a_last - a_cum[d]) * dt[d]
        e_last = jnp.exp(jnp.broadcast_to(a_last, (8, CHUNK)))
        lhs, rhs = _spread_operands(a_cum[d], -a_cum[d], [jnp.exp(a_cum[d]), w_in], sel_ref, ones_ref)
        for h in range(N_HEADS):
            r = 4 * d + h
            heads[r]["mt"] = _mm(lhs, rhs * rsel_ref[r])
            heads[r]["e_last"] = e_last[r:r + 1, :]

    for r in range(8):
        d, h = divmod(r, 4)
        hd = heads[r]
        ds = _mm(groups[d, h // 2]["bt"], (hd["xm"] * hd["mt"][:, 256:384]).astype(BF16))
        s_ref[r] = hd["e_last"] * hd["st"] + ds
    for r in range(8):
        d, h = divmod(r, 4)
        hd = heads[r]
        vis, _ = _tri_masks(d)
        sc = groups[d, h // 2]["cb"] * jnp.exp(jnp.where(vis, hd["mt"][:, 0:128], ninf)) * dt[d][r:r + 1, :]
        hd["y"] = _mm(sc.astype(BF16), hd["xmb"]) + hd["mt"][:, 128:256] * hd["cs"]

    for d in (0, 1):
        out_ref = (yf_ref, yb_ref)[d]
        for g in range(2):
            y_pair = heads[4 * d + 2 * g]["y"] + heads[4 * d + 2 * g + 1]["y"]
            if d == 0:
                y_pair = y_pair + dskip_ref[:, 128 * g:128 * g + 128] * groups[d, g]["x_pair"]
            out_ref[0, :, 128 * g:128 * g + 128] = y_pair


def _ssd(ss, gt, bcol, acol, conv_w, conv_b, dskip, n_ctx_chunks):
    bsz, t, _ = ss.shape
    nc = t // CHUNK
    sub = CHUNK // 8
    nsub = t // 8
    bc = lambda c: _bwd_chunk(c, n_ctx_chunks, nc)
    fwd = lambda b, c: (b, c, 0)
    bwd = lambda b, c: (b, bc(c), 0)
    fwd_p = lambda b, c: (b, jnp.maximum(c * sub - 1, 0), 0)
    fwd_n = lambda b, c: (b, jnp.minimum((c + 1) * sub, nsub - 1), 0)
    bwd_p = lambda b, c: (b, jnp.maximum(bc(c) * sub - 1, 0), 0)
    bwd_n = lambda b, c: (b, jnp.minimum((bc(c) + 1) * sub, nsub - 1), 0)
    fwd_t = lambda b, c: (b, 0, c)
    bwd_t = lambda b, c: (b, 0, bc(c))
    const2 = lambda b, c: (0, 0)
    out = jax.ShapeDtypeStruct((bsz, t, W_BRANCH), F32)
    return pl.pallas_call(
        functools.partial(_ssd_kernel, n_ctx_chunks, nc),
        grid=(bsz, nc),
        in_specs=[pl.BlockSpec((1, CHUNK, 768), fwd), pl.BlockSpec((1, 8, 768), fwd_p),
                  pl.BlockSpec((1, 8, 768), fwd_n),
                  pl.BlockSpec((1, CHUNK, 768), bwd), pl.BlockSpec((1, 8, 768), bwd_p),
                  pl.BlockSpec((1, 8, 768), bwd_n),
                  pl.BlockSpec((1, LANES, CHUNK), fwd_t), pl.BlockSpec((1, LANES, CHUNK), bwd_t),
                  pl.BlockSpec((LANES, 1), const2), pl.BlockSpec((LANES, 1), const2),
                  pl.BlockSpec((3, 768), const2), pl.BlockSpec((1, 768), const2),
                  pl.BlockSpec((1, W_BRANCH), const2),
                  pl.BlockSpec((CHUNK, LANES), const2), pl.BlockSpec((CHUNK, 2 * LANES), const2),
                  pl.BlockSpec((8, CHUNK, 3 * LANES), lambda b, c: (0, 0, 0))],
        out_specs=[pl.BlockSpec((1, CHUNK, W_BRANCH), fwd), pl.BlockSpec((1, CHUNK, W_BRANCH), bwd)],
        out_shape=[out, out],
        scratch_shapes=[pltpu.VMEM((8, SSM_N, LANES), F32)],
        compiler_params=_cparams(("parallel", "arbitrary")),
        name="ssd",
    )(ss, ss, ss, ss, ss, ss, gt, gt, bcol, acol, conv_w, conv_b, dskip,
      jnp.ones((CHUNK, LANES), BF16), _spread_selector(2), _state_row_selector(3))


def _hgrn2_kernel(hf_ref, vtf_ref, hb_ref, vtb_ref, loglb_ref, log1m_ref, onem_ref, e_ref, of_ref, ob_ref,
                  st_ref):
    @pl.when(pl.program_id(1) == 0)
    def _():
        st_ref[...] = jnp.zeros_like(st_ref)

    ri = _iota2((CHUNK, CHUNK), 0)
    ci = _iota2((CHUNK, CHUNK), 1)
    lane = _iota2((CHUNK, LANES), 1)
    rw = _iota2((CHUNK, W_BRANCH), 0)
    blockdiag = (ri >> 6) == (ci >> 6)
    ninf = jnp.float32(-jnp.inf)
    nb = CHUNK // SUB
    rs = _iota2((nb, SUB, W_BRANCH), 1)

    dirs = []
    for d in (0, 1):
        h_ref = (hf_ref, hb_ref)[d]
        tc = jnp.where(_tri_masks(d)[0], 1.0, 0.0).astype(BF16)
        z = h_ref[0, :, W_BRANCH * (1 + d):W_BRANCH * (2 + d)]
        e = jnp.exp(-jnp.abs(z))
        ope = 1.0 + e
        la = loglb_ref[d:d + 1, :]
        lb_ = log1m_ref[d:d + 1, :] + (jnp.minimum(z, 0.0) - jnp.log(ope))
        logf = jnp.maximum(la, lb_) + jnp.log(1.0 + jnp.exp(-jnp.abs(la - lb_)))
        dirs.append(dict(
            q=_silu(h_ref[0, :, 0:W_BRANCH]),
            v=h_ref[0, :, 3 * W_BRANCH:4 * W_BRANCH],
            kk=onem_ref[d:d + 1, :] * (jnp.where(z >= 0.0, e, 1.0) / ope),
            gcum=_mm_exact_l(tc, logf)))

    for d in (0, 1):
        dd = dirs[d]
        q, kk, gcum = dd["q"], dd["kk"], dd["gcum"]
        vt_ref = (vtf_ref, vtb_ref)[d]
        last = CHUNK - 1 if d == 0 else 0
        g_last = gcum[last:last + 1, :]
        qg = (q * jnp.exp(gcum)).astype(BF16)
        kg = (kk * jnp.exp(g_last - gcum)).astype(BF16)

        dd["o_inter"] = []
        for p in range(2):
            sl = slice(128 * p, 128 * p + 128)
            st = st_ref[2 * d + p]
            dd["o_inter"].append(_mm_nt(qg[:, sl], st.astype(BF16)))
            dst = _mm(vt_ref[0, sl, :].astype(BF16), kg[:, sl])
            st_ref[2 * d + p] = st * jnp.exp(g_last[:, sl]) + jnp.where(blockdiag, dst, 0.0)

        a_mats = [jnp.zeros((CHUNK, CHUNK), F32) for _ in range(N_HEADS)]
        blk = CHUNK // 2
        while blk >= SUB:
            first = (rw & (2 * blk - 1)) < blk
            edge = (blk - 1) if d == 0 else blk
            gb = gcum.reshape(CHUNK // (2 * blk), 2 * blk, W_BRANCH)[:, edge:edge + 1, :]
            gb = jnp.broadcast_to(gb, (CHUNK // (2 * blk), 2 * blk, W_BRANCH)).reshape(CHUNK, W_BRANCH)
            q_side = jnp.logical_not(first) if d == 0 else first
            qt = q * jnp.exp(jnp.where(q_side, gcum - gb, ninf))
            kt = (kk * jnp.exp(jnp.where(q_side, ninf, gb - gcum))).astype(BF16)
            same = (ri >> int(math.log2(2 * blk))) == (ci >> int(math.log2(2 * blk)))
            for h in range(N_HEADS):
                p, half = divmod(h, 2)
                hmask = (lane >= 64) if half else (lane < 64)
                qh = jnp.where(hmask, qt[:, 128 * p:128 * p + 128], 0.0).astype(BF16)
                a_mats[h] = a_mats[h] + jnp.where(same, _mm_nt(qh, kt[:, 128 * p:128 * p + 128]), 0.0)
            blk //= 2
        dd["a_mats"] = a_mats

        g3 = gcum.reshape(nb, SUB, W_BRANCH)
        q3 = q.reshape(nb, SUB, W_BRANCH)
        k3 = kk.reshape(nb, SUB, W_BRANCH)
        v3 = dd["v"].reshape(nb, SUB, W_BRANCH)
        o3 = None
        for j in range(SUB):
            ok = (rs >= j) if d == 0 else (rs <= j)
            pj = q3 * jnp.exp(jnp.where(ok, g3 - g3[:, j:j + 1, :], ninf)) * k3[:, j:j + 1, :]
            red = _mm(pj.reshape(CHUNK, W_BRANCH).astype(BF16), e_ref[...])
            term = red.reshape(nb, SUB, W_BRANCH) * v3[:, j:j + 1, :]
            o3 = term if o3 is None else o3 + term
        dd["o_diag"] = o3.reshape(CHUNK, W_BRANCH)

    for d in (0, 1):
        dd = dirs[d]
        out_ref = (of_ref, ob_ref)[d]
        for p in range(2):
            sl = slice(128 * p, 128 * p + 128)
            a_cat = jnp.concatenate([dd["a_mats"][2 * p], dd["a_mats"][2 * p + 1]], axis=1).astype(BF16)
            v_p = dd["v"][:, sl]
            v_cat = jnp.concatenate([jnp.where(lane < 64, v_p, 0.0), jnp.where(lane >= 64, v_p, 0.0)],
                                    axis=0).astype(BF16)
            out_ref[0, :, sl] = dd["o_inter"][p] + _mm(a_cat, v_cat) + dd["o_diag"][:, sl]


def _hgrn2(hg, hvt, loglb, log1m, onem, e64, n_ctx_chunks):
    bsz, t, _ = hg.shape
    nc = t // CHUNK
    fwd = lambda b, c: (b, c, 0)
    bwd = lambda b, c: (b, _bwd_chunk(c, n_ctx_chunks, nc), 0)
    fwd_t = lambda b, c: (b, 0, c)
    bwd_t = lambda b, c: (b, 0, _bwd_chunk(c, n_ctx_chunks, nc))
    const2 = lambda b, c: (0, 0)
    out = jax.ShapeDtypeStruct((bsz, t, W_BRANCH), F32)
    return pl.pallas_call(
        _hgrn2_kernel,
        grid=(bsz, nc),
        in_specs=[pl.BlockSpec((1, CHUNK, 1024), fwd), pl.BlockSpec((1, W_BRANCH, CHUNK), fwd_t),
                  pl.BlockSpec((1, CHUNK, 1024), bwd), pl.BlockSpec((1, W_BRANCH, CHUNK), bwd_t),
                  pl.BlockSpec((2, W_BRANCH), const2), pl.BlockSpec((2, W_BRANCH), const2),
                  pl.BlockSpec((2, W_BRANCH), const2), pl.BlockSpec((W_BRANCH, W_BRANCH), const2)],
        out_specs=[pl.BlockSpec((1, CHUNK, W_BRANCH), fwd), pl.BlockSpec((1, CHUNK, W_BRANCH), bwd)],
        out_shape=[out, out],
        scratch_shapes=[pltpu.VMEM((4, LANES, LANES), F32)],
        compiler_params=_cparams(("parallel", "arbitrary")),
        name="hgrn2",
    )(hg, hvt, hg, hvt, loglb, log1m, onem, e64)


def _attn(daq, kt, vx, lam_vecs, lam_init, n_ctx_tiles, skip_ctx):
    bsz, t, _ = daq.shape
    tq = ROW_TILE
    n_k_tiles = t // KEY_TILE
    n_lat_tiles = n_k_tiles - n_ctx_tiles
    group = math.gcd(KEY_GROUP, n_lat_tiles)
    q_off = n_ctx_tiles if skip_ctx else 0
    nq = t // tq - q_off
    n_maps = W_BRANCH // DA_QK
    rows = n_maps * tq

    def kern(q_ref, kt_ref, vx_ref, lv_ref, o_ref, q8_ref, m_ref, acc_ref):
        qi = pl.program_id(1) + q_off
        lane = _iota2((tq, W_BRANCH), 1)
        q = q_ref[0]
        for j in range(n_maps):
            q8_ref[j * tq:(j + 1) * tq, :] = jnp.where((lane >> 5) == j, q, jnp.zeros_like(q))
        m_ref[...] = jnp.full_like(m_ref, -jnp.inf)
        acc_ref[...] = jnp.zeros_like(acc_ref)

        def update(t0, n):
            q8 = q8_ref[...]
            s = [_mm(q8, kt_ref[0, t0 + i]) for i in range(n)]
            mx = s[0][:, 0:LANES]
            for i in range(n):
                for c in range(KEY_TILE // LANES):
                    if i or c:
                        mx = jnp.maximum(mx, s[i][:, LANES * c:LANES * (c + 1)])
            m_old = m_ref[...]
            m_new = jnp.maximum(m_old, jnp.max(mx, axis=1, keepdims=True))
            alpha = jnp.exp(m_old - m_new)
            m_ref[...] = m_new
            m2 = jnp.concatenate([m_new] * (KEY_TILE // LANES), axis=1)
            p = [jnp.exp(s[i] - m2).astype(BF16) for i in range(n)]
            for hd in range(N_HEADS):
                r0 = 2 * hd * tq
                pv = _mm(p[0][r0:r0 + 2 * tq], vx_ref[0, t0, hd])
                for i in range(1, n):
                    pv = pv + _mm(p[i][r0:r0 + 2 * tq], vx_ref[0, t0 + i, hd])
                acc_ref[r0:r0 + 2 * tq, :] = alpha[r0:r0 + 2 * tq] * acc_ref[r0:r0 + 2 * tq, :] + pv

        update(0, n_ctx_tiles)

        @pl.when(qi >= n_ctx_tiles)
        def _():
            def body(g, carry):
                update(n_ctx_tiles + g * group, group)
                return carry
            lax.fori_loop(0, n_lat_tiles // group, body, 0)

        lv = lv_ref[...]
        lam = (jnp.exp(jnp.sum(lv[0:1, :] * lv[1:2, :], axis=1, keepdims=True))
               - jnp.exp(jnp.sum(lv[2:3, :] * lv[3:4, :], axis=1, keepdims=True)) + lam_init)
        lane1 = _iota2((tq, LANES), 1)
        for pr in range(N_HEADS // 2):
            halves = []
            for half in range(2):
                r0 = 2 * (2 * pr + half) * tq
                one = 0 if half else HEAD_W
                a0 = acc_ref[r0:r0 + tq, :]
                a1 = acc_ref[r0 + tq:r0 + 2 * tq, :]
                halves.append(a0 / a0[:, one:one + 1] - lam * (a1 / a1[:, one:one + 1]))
            o_ref[0, :, LANES * pr:LANES * (pr + 1)] = jnp.where(lane1 < HEAD_W, halves[0], halves[1])

    return pl.pallas_call(
        kern,
        grid=(bsz, nq),
        in_specs=[pl.BlockSpec((1, tq, W_BRANCH), lambda b, i: (b, i + q_off, 0)),
                  pl.BlockSpec((1, n_k_tiles, W_BRANCH, KEY_TILE), lambda b, i: (b, 0, 0, 0)),
                  pl.BlockSpec((1, n_k_tiles, N_HEADS, KEY_TILE, LANES), lambda b, i: (b, 0, 0, 0, 0)),
                  pl.BlockSpec(lam_vecs.shape, lambda b, i: (0, 0))],
        out_specs=pl.BlockSpec((1, tq, W_BRANCH), lambda b, i: (b, i, 0)),
        out_shape=jax.ShapeDtypeStruct((bsz, nq * tq, W_BRANCH), F32),
        scratch_shapes=[pltpu.VMEM((rows, W_BRANCH), BF16),
                        pltpu.VMEM((rows, LANES), F32),
                        pltpu.VMEM((rows, LANES), F32)],
        compiler_params=_cparams(("parallel", "parallel")),
        name="diff_attn",
    )(daq, kt, vx, lam_vecs)


def _seg_mean(x, e_ref):
    x0 = x.astype(BF16)
    x1 = (x - x0.astype(F32)).astype(BF16)
    return (_mm(x0, e_ref[...]) + _mm(x1, e_ref[...])) * (1.0 / HEAD_W)


def _out_kernel(n_ctx_tiles, ctx_row, q_off, lam_init, final,
                x_ref, mo_ref, mlo_ref, mlz_ref, mhf_ref, mhb_ref, dao_ref, daz_ref,
                syf_ref, syb_ref, ssz_ref, hof_ref, hob_ref, hgz_ref,
                mlg_ref, dag_ref, ssg_ref, hgg_ref, e_ref, w_ref, fg_ref, o_ref):
    b = pl.program_id(0)
    i = pl.program_id(1) + q_off
    d = x_ref.shape[2]
    r = jnp.where(i < n_ctx_tiles, ctx_row, b)
    gate = mo_ref[pl.ds(r, 1), :][:, 2 * d:3 * d]

    u = _sigmoid(mlo_ref[0]) * (mhf_ref[0] + mhb_ref[0])
    dev = u - _seg_mean(u, e_ref)
    y_ml = dev * lax.rsqrt(_seg_mean(dev * dev, e_ref) + EPS) * mlg_ref[...] * _silu(mlz_ref[0])

    o = dao_ref[0]
    y_da = (o * lax.rsqrt(_seg_mean(o * o, e_ref) + EPS) * dag_ref[...]) * (1.0 - lam_init) * _silu(daz_ref[0])

    ys = (syf_ref[0] + syb_ref[0]) * _silu(ssz_ref[0])
    parts = []
    for g in range(2):
        yg = ys[:, 128 * g:128 * g + 128]
        parts.append(yg * lax.rsqrt(jnp.mean(yg * yg, axis=-1, keepdims=True) + EPS))
    y_ss = jnp.concatenate(parts, axis=1) * ssg_ref[...]

    oh = hof_ref[0] + hob_ref[0]
    y_hg = (oh * lax.rsqrt(_seg_mean(oh * oh, e_ref) + EPS) * hgg_ref[...]) * _silu(hgz_ref[0])

    acc = _mm(y_ml.astype(BF16), w_ref[0:W_BRANCH, :])
    acc = acc + _mm(y_da.astype(BF16), w_ref[W_BRANCH:2 * W_BRANCH, :])
    acc = acc + _mm(y_ss.astype(BF16), w_ref[2 * W_BRANCH:3 * W_BRANCH, :])
    acc = acc + _mm(y_hg.astype(BF16), w_ref[3 * W_BRANCH:4 * W_BRANCH, :])
    x_new = x_ref[0] + gate * acc
    if final:
        x_new = x_new * lax.rsqrt(jnp.mean(x_new * x_new, axis=-1, keepdims=True) + EPS) * fg_ref[...]
    o_ref[0] = x_new


def _out_proj(xs, mo, ml, mhf, mhb, dao, daz, ss, syf, syb, hg, hof, hob,
              mlg, dag, ssg, hgg, e64, w_out, final_g, lam_init, n_ctx_tiles, ctx_row, final):
    bsz, t, d = xs.shape
    tm = ROW_TILE
    q_off = n_ctx_tiles if final else 0
    nrow = t // tm - q_off
    row = lambda b, i: (b, i + q_off, 0)
    col = lambda k: (lambda b, i: (b, i + q_off, k))
    const2 = lambda b, i: (0, 0)
    wb = pl.BlockSpec((1, tm, W_BRANCH), row)
    return pl.pallas_call(
        functools.partial(_out_kernel, n_ctx_tiles, ctx_row, q_off, lam_init, final),
        grid=(bsz, nrow),
        in_specs=[pl.BlockSpec((1, tm, d), row),
                  pl.BlockSpec(mo.shape, const2),
                  pl.BlockSpec((1, tm, W_BRANCH), col(3)), pl.BlockSpec((1, tm, W_BRANCH), col(4)),
                  wb, wb,
                  pl.BlockSpec((1, tm, W_BRANCH), lambda b, i: (b, i, 0)) if final else wb,
                  wb,
                  wb, wb, pl.BlockSpec((1, tm, W_BRANCH), col(3)),
                  wb, wb, pl.BlockSpec((1, tm, W_BRANCH), col(4)),
                  pl.BlockSpec((1, W_BRANCH), const2), pl.BlockSpec((1, W_BRANCH), const2),
                  pl.BlockSpec((1, W_BRANCH), const2), pl.BlockSpec((1, W_BRANCH), const2),
                  pl.BlockSpec((W_BRANCH, W_BRANCH), const2),
                  pl.BlockSpec(w_out.shape, const2),
                  pl.BlockSpec((1, d), const2)],
        out_specs=pl.BlockSpec((1, tm, d), lambda b, i: (b, i, 0)),
        out_shape=jax.ShapeDtypeStruct((bsz, nrow * tm, d), F32),
        compiler_params=_cparams(("parallel", "parallel")),
        name="out_proj",
    )(xs, mo, ml, ml, mhf, mhb, dao, daz, syf, syb, ss, hof, hob, hg,
      mlg, dag, ssg, hgg, e64, w_out, final_g.reshape(1, d))


def _relayout_w_in(w_in):
    d = w_in.shape[0]
    o = {}
    off = 0
    for name, n in (("ml_q", 256), ("ml_k", 256), ("ml_v", 256), ("ml_o", 256), ("ml_i", 8), ("ml_f", 8),
                    ("ml_z", 256), ("da_q", 256), ("da_k", 256), ("da_v", 256), ("da_z", 256),
                    ("ss_xbc", 768), ("ss_dt", 8), ("ss_z", 256),
                    ("hg_q", 256), ("hg_f", 512), ("hg_i", 256), ("hg_z", 256)):
        o[name] = w_in[:, off:off + n]
        off += n

    def rot(w):
        g = w.reshape(d, W_BRANCH // DA_QK, 2, DA_QK // 2)
        return jnp.stack([-g[:, :, 1], g[:, :, 0]], axis=2).reshape(d, W_BRANCH)

    gates = jnp.concatenate([o["ml_i"], o["ml_f"], o["ss_dt"], jnp.zeros((d, LANES - 24), w_in.dtype)], axis=1)
    zpad = jnp.zeros((d, HEAD_W), w_in.dtype)
    v_ext = []
    for hd in range(N_HEADS):
        v_h = o["da_v"][:, HEAD_W * hd:HEAD_W * (hd + 1)]
        v_ext += [zpad, v_h] if hd % 2 else [v_h, zpad]
    w2 = jnp.concatenate([o["ml_q"], o["ml_k"], o["ml_v"], o["ml_o"], o["ml_z"],
                          o["da_q"], rot(o["da_q"])] + v_ext + [o["da_z"],
                          o["ss_xbc"], o["ss_z"],
                          o["hg_q"], o["hg_f"], o["hg_i"], o["hg_z"]], axis=1)
    wt = jnp.concatenate([o["da_k"], rot(o["da_k"]), gates, o["ml_k"], o["hg_i"]], axis=1).T
    return w2.astype(BF16), wt.astype(BF16)


def _rope_tables(n_ctx, seq):
    pos = jnp.arange(seq)
    rows = (pos // GRID_W).astype(F32)
    cols = (pos % GRID_W).astype(F32)
    axis = DA_QK // 2
    inv = ROPE_BASE ** (-jnp.arange(0, axis, 2, dtype=F32) / axis)
    ang = jnp.concatenate([rows[:, None] * inv, cols[:, None] * inv], axis=-1)
    cos = jnp.concatenate([jnp.ones((n_ctx, axis), F32), jnp.cos(ang)], axis=0)
    sin = jnp.concatenate([jnp.zeros((n_ctx, axis), F32), jnp.sin(ang)], axis=0)
    reps = W_BRANCH // axis
    return jnp.tile(cos, (1, reps)), jnp.tile(sin, (1, reps))


def _gate_col(vals):
    v = jnp.concatenate([vals.astype(F32), jnp.zeros((LANES - vals.shape[0],), F32)])
    return v.reshape(LANES, 1)


def kernel(x, c, ctx, c_ctx, w_mod, b_mod, norm_g, w_in, w_out, ml_gate_b, ml_norm_g, da_lambda, da_norm_g,
           ss_conv_w, ss_conv_b, ss_dt_bias, ss_a_log, ss_d, ss_norm_g, hg_lower, hg_norm_g, final_g):
    bsz, seq, d = x.shape
    n_ctx = ctx.shape[1]
    depth = w_mod.shape[0]
    assert n_ctx % ROW_TILE == 0 and seq % ROW_TILE == 0 and bsz < 8
    n_ctx_tiles = n_ctx // ROW_TILE
    n_ctx_chunks = n_ctx // CHUNK
    ctx_row = bsz

    xs = jnp.concatenate([ctx, x], axis=1)
    cc = jnp.concatenate([c, c_ctx[None, :], jnp.zeros((8 - bsz - 1, d), F32)], axis=0)
    cos, sin = _rope_tables(n_ctx, seq)
    cos_t, sin_t = cos.T, sin.T
    one_lanes = jnp.arange(V_EXT) % LANES == jnp.where((jnp.arange(V_EXT) // LANES) % 2 == 1, 0, HEAD_W)
    ones_row = one_lanes.astype(F32).reshape(1, V_EXT)
    lb_all = jnp.cumsum(jax.nn.softmax(hg_lower.astype(F32), axis=1), axis=1)
    lb_all = lb_all - lb_all[:, :1]
    hid = _iota2((W_BRANCH, W_BRANCH), 0) // HEAD_W
    e64 = (hid == hid.T).astype(BF16)

    out = None
    for l in range(depth):
        lam_init = 0.8 - 0.6 * math.exp(-0.3 * l)
        final = l == depth - 1
        w2, wt = _relayout_w_in(w_in[l])
        mo = _modulation(cc, w_mod[l], b_mod[l])
        ml, daq, dak, dav, daz, ss, hg, gt, mkt, hvt = _in_proj(xs, mo, norm_g[l], cos, sin, cos_t, sin_t, w2, wt,
                                                          ones_row, n_ctx_tiles, ctx_row)

        gb = ml_gate_b[l]
        zeros8 = jnp.zeros((8,), F32)
        bcol = _gate_col(jnp.concatenate([gb[:, 0].reshape(-1), gb[:, 1].reshape(-1),
                                                ss_dt_bias[l].reshape(-1)]))
        acol = _gate_col(jnp.concatenate([zeros8, zeros8, -jnp.exp(ss_a_log[l].astype(F32)).reshape(-1)]))
        mhf, mhb = _mlstm(ml, mkt, gt, bcol, n_ctx_chunks)
        dao = _attn(daq, dak, dav, da_lambda[l].astype(F32), lam_init, n_ctx_tiles, final)
        dskip = jnp.repeat(ss_d[l].astype(F32), HEAD_W).reshape(1, W_BRANCH)
        syf, syb = _ssd(ss, gt, bcol, acol, ss_conv_w[l], ss_conv_b[l].reshape(1, -1),
                        dskip, n_ctx_chunks)
        lbh = lb_all[:, l]
        hof, hob = _hgrn2(hg, hvt, jnp.log(lbh), jnp.log1p(-lbh), 1.0 - lbh, e64, n_ctx_chunks)

        res = _out_proj(xs, mo, ml, mhf, mhb, dao, daz, ss, syf, syb, hg, hof, hob,
                        ml_norm_g[l].reshape(1, -1), jnp.tile(da_norm_g[l], N_HEADS).reshape(1, -1),
                        ss_norm_g[l].reshape(1, -1), hg_norm_g[l].reshape(1, -1), e64,
                        w_out[l].astype(BF16), final_g, lam_init, n_ctx_tiles, ctx_row, final)
        if final:
            out = res
        else:
            xs = res
    return out
```

```python
import functools
import math

import jax
import jax.numpy as jnp
from jax import lax
from jax.experimental import pallas as pl
from jax.experimental.pallas import tpu as pltpu

F32 = jnp.float32
BF16 = jnp.bfloat16

EPS = 1e-6
GRID_W = 64
ROPE_BASE = 10000.0
N_HEADS = 4
HEAD_W = 64
W_BRANCH = N_HEADS * HEAD_W
DA_QK = 32
SSM_N = 128
CHUNK = 128
ROW_TILE = 256
SUB = 8
LANES = 128
VMEM_LIMIT = 56 * 1024 * 1024

C_ML = 0
C_DAQ = 1280
C_DAQROT = 1536
C_DAV = 1792
C_DAZ = 2304
C_SS = 2560
C_HG = 3584
C_END = 4864
V_EXT = N_HEADS * LANES
KEY_TILE = 256
KEY_GROUP = 4

NT = (((1,), (1,)), ((), ()))
TN = (((0,), (0,)), ((), ()))


def _mm(a, b):
    return jnp.dot(a, b, preferred_element_type=F32)


def _mm_nt(a, b):
    return lax.dot_general(a, b, NT, preferred_element_type=F32)


def _mm_tn(a, b):
    return lax.dot_general(a, b, TN, preferred_element_type=F32)


def _split3(x):
    x0 = x.astype(BF16)
    r = x - x0.astype(F32)
    x1 = r.astype(BF16)
    x2 = (r - x1.astype(F32)).astype(BF16)
    return x0, x1, x2


def _mm_exact_l(t, x):
    x0, x1, x2 = _split3(x)
    return _mm(t, x0) + _mm(t, x1) + _mm(t, x2)


def _mm_exact_r(x, t):
    x0, x1, x2 = _split3(x)
    return _mm(x0, t) + _mm(x1, t) + _mm(x2, t)


def _sigmoid(x):
    return 1.0 / (1.0 + jnp.exp(-x))


def _silu(x):
    return x * _sigmoid(x)


def _log_sigmoid(x):
    return jnp.minimum(x, 0.0) - jnp.log(1.0 + jnp.exp(-jnp.abs(x)))


def _softplus(x):
    return jnp.maximum(x, 0.0) + jnp.log(1.0 + jnp.exp(-jnp.abs(x)))


def _iota2(shape, axis):
    return lax.broadcasted_iota(jnp.int32, shape, axis)


def _cparams(sem):
    return pltpu.CompilerParams(dimension_semantics=sem, vmem_limit_bytes=VMEM_LIMIT)


def _mod_kernel(c_ref, w_ref, b_ref, o_ref):
    c = c_ref[...]
    o_ref[...] = jnp.dot(_silu(c), w_ref[...], precision=lax.Precision.HIGHEST,
                         preferred_element_type=F32) + b_ref[...]


def _modulation(cc, w_mod, b_mod):
    d = cc.shape[1]
    n = w_mod.shape[1]
    tn = 512
    return pl.pallas_call(
        _mod_kernel,
        grid=(n // tn,),
        in_specs=[pl.BlockSpec((8, d), lambda j: (0, 0)),
                  pl.BlockSpec((d, tn), lambda j: (0, j)),
                  pl.BlockSpec((1, tn), lambda j: (0, j))],
        out_specs=pl.BlockSpec((8, tn), lambda j: (0, j)),
        out_shape=jax.ShapeDtypeStruct((8, n), F32),
        compiler_params=_cparams(("arbitrary",)),
        name="modulation",
    )(cc, w_mod, b_mod.reshape(1, n))


def _in_kernel(n_ctx_tiles, ctx_row, x_ref, mo_ref, g_ref, cos_ref, sin_ref, cost_ref, sint_ref,
               w_ref, wt_ref, ones_ref,
               ml_ref, daq_ref, kt_ref, vx_ref, daz_ref, ss_ref, hg_ref, gt_ref, mkt_ref, hvt_ref):
    b = pl.program_id(0)
    i = pl.program_id(1)
    d = x_ref.shape[2]
    r = jnp.where(i < n_ctx_tiles, ctx_row, b)
    mo = mo_ref[pl.ds(r, 1), :]
    shift = mo[:, 0:d]
    scale = mo[:, d:2 * d]
    x = x_ref[0]
    y = x * lax.rsqrt(jnp.mean(x * x, axis=-1, keepdims=True) + EPS) * g_ref[...]
    h = (y * (1.0 + scale) + shift).astype(BF16)

    ml_ref[0] = _mm(h, w_ref[:, C_ML:C_DAQ])
    q = _mm(h, w_ref[:, C_DAQ:C_DAQROT]) * cos_ref[...] + _mm(h, w_ref[:, C_DAQROT:C_DAV]) * sin_ref[...]
    daq_ref[0] = (q * (DA_QK ** -0.5)).astype(BF16)
    vx = (_mm(h, w_ref[:, C_DAV:C_DAZ]) + ones_ref[...]).astype(BF16)
    for hd in range(N_HEADS):
        vx_ref[0, 0, hd] = vx[:, LANES * hd:LANES * (hd + 1)]
    daz_ref[0] = _mm(h, w_ref[:, C_DAZ:C_SS])
    ss_ref[0] = _mm(h, w_ref[:, C_SS:C_HG])
    hg_ref[0] = _mm(h, w_ref[:, C_HG:C_END])
    tr = _mm_nt(wt_ref[...], h)
    kt_ref[0, 0] = (tr[0:W_BRANCH] * cost_ref[...] + tr[W_BRANCH:2 * W_BRANCH] * sint_ref[...]).astype(BF16)
    gt_ref[0] = tr[2 * W_BRANCH:2 * W_BRANCH + LANES]
    mkt_ref[0] = tr[2 * W_BRANCH + LANES:3 * W_BRANCH + LANES]
    hvt_ref[0] = tr[3 * W_BRANCH + LANES:4 * W_BRANCH + LANES]


def _in_proj(xs, mo, g, cos, sin, cos_t, sin_t, w2, wt, ones_row, n_ctx_tiles, ctx_row):
    bsz, t, d = xs.shape
    tm = ROW_TILE
    assert tm == KEY_TILE
    row = lambda b, i: (b, i, 0)
    const2 = lambda b, i: (0, 0)
    out_shape = [
        jax.ShapeDtypeStruct((bsz, t, 1280), F32),
        jax.ShapeDtypeStruct((bsz, t, W_BRANCH), BF16),
        jax.ShapeDtypeStruct((bsz, t // tm, W_BRANCH, tm), BF16),
        jax.ShapeDtypeStruct((bsz, t // tm, N_HEADS, tm, LANES), BF16),
        jax.ShapeDtypeStruct((bsz, t, W_BRANCH), F32),
        jax.ShapeDtypeStruct((bsz, t, 1024), F32),
        jax.ShapeDtypeStruct((bsz, t, 1280), F32),
        jax.ShapeDtypeStruct((bsz, LANES, t), F32),
        jax.ShapeDtypeStruct((bsz, W_BRANCH, t), F32),
        jax.ShapeDtypeStruct((bsz, W_BRANCH, t), F32),
    ]
    out_specs = [
        pl.BlockSpec((1, tm, 1280), row),
        pl.BlockSpec((1, tm, W_BRANCH), row),
        pl.BlockSpec((1, 1, W_BRANCH, tm), lambda b, i: (b, i, 0, 0)),
        pl.BlockSpec((1, 1, N_HEADS, tm, LANES), lambda b, i: (b, i, 0, 0, 0)),
        pl.BlockSpec((1, tm, W_BRANCH), row),
        pl.BlockSpec((1, tm, 1024), row),
        pl.BlockSpec((1, tm, 1280), row),
        pl.BlockSpec((1, LANES, tm), lambda b, i: (b, 0, i)),
        pl.BlockSpec((1, W_BRANCH, tm), lambda b, i: (b, 0, i)),
        pl.BlockSpec((1, W_BRANCH, tm), lambda b, i: (b, 0, i)),
    ]
    return pl.pallas_call(
        functools.partial(_in_kernel, n_ctx_tiles, ctx_row),
        grid=(bsz, t // tm),
        in_specs=[pl.BlockSpec((1, tm, d), row),
                  pl.BlockSpec(mo.shape, const2),
                  pl.BlockSpec((1, d), const2),
                  pl.BlockSpec((tm, W_BRANCH), lambda b, i: (i, 0)),
                  pl.BlockSpec((tm, W_BRANCH), lambda b, i: (i, 0)),
                  pl.BlockSpec((W_BRANCH, tm), lambda b, i: (0, i)),
                  pl.BlockSpec((W_BRANCH, tm), lambda b, i: (0, i)),
                  pl.BlockSpec(w2.shape, const2),
                  pl.BlockSpec(wt.shape, const2),
                  pl.BlockSpec((1, V_EXT), const2)],
        out_specs=out_specs,
        out_shape=out_shape,
        compiler_params=_cparams(("parallel", "parallel")),
        name="in_proj",
    )(xs, mo, g.reshape(1, d), cos, sin, cos_t, sin_t, w2, wt, ones_row)


def _bwd_chunk(c, n_ctx_chunks, n_chunks):
    return jnp.where(c < n_ctx_chunks, n_ctx_chunks - 1 - c, n_chunks + n_ctx_chunks - 1 - c)


def _tri_masks(d):
    ri = _iota2((CHUNK, CHUNK), 0)
    ci = _iota2((CHUNK, CHUNK), 1)
    if d == 0:
        return ci <= ri, ri <= ci
    return ci >= ri, ri >= ci


def _split_f32(x, n):
    parts = []
    for _ in range(n):
        p = x.astype(BF16).astype(F32)
        parts.append(p)
        x = x - p
    return parts


def _cummax_lanes(u, d):
    lane = _iota2(u.shape, 1)
    ninf = jnp.float32(-jnp.inf)
    s = 1
    while s < CHUNK:
        if d == 0:
            sh = jnp.where(lane >= s, pltpu.roll(u, s, 1), ninf)
        else:
            sh = jnp.where(lane < CHUNK - s, pltpu.roll(u, CHUNK - s, 1), ninf)
        u = jnp.maximum(u, sh)
        s *= 2
    return u


def _state_row_selector(n_tiles):
    keep = (jnp.arange(CHUNK)[None, :] % 8) == jnp.arange(8)[:, None]
    return jnp.broadcast_to(keep[:, :, None], (8, CHUNK, n_tiles * LANES)).astype(BF16)


def _spread_selector(n_spread):
    t = jnp.arange(CHUNK)[:, None] // 16
    tiles = [jnp.broadcast_to(t == 3 + k, (CHUNK, LANES)) for k in range(n_spread)]
    return jnp.concatenate(tiles, axis=1).astype(BF16)


def _spread_operands(col_sum, row_sum, spread, sel_ref, ones_ref):
    z8 = jnp.zeros((8, CHUNK), F32)
    one8 = ones_ref[0:8, :].astype(F32)
    pieces = _split_f32(col_sum, 3) + [one8] * 3
    for x in spread:
        pieces += _split_f32(x, 2)
    assert len(pieces) <= CHUNK // 8
    lhs = jnp.concatenate(pieces + [z8] * (CHUNK // 8 - len(pieces)), axis=0).T.astype(BF16)
    rows = jnp.concatenate([one8] * 3 + _split_f32(row_sum, 3) + [z8] * (CHUNK // 8 - 6), axis=0)
    return lhs, jnp.concatenate([rows.astype(BF16), sel_ref[...]], axis=1)


def _mlstm_kernel(qf_ref, vf_ref, ktf_ref, gtf_ref, qb_ref, vb_ref, ktb_ref, gtb_ref, bcol_ref, ones_ref, sel_ref,
                  rsel_ref, hf_ref, hb_ref, cn_ref, m_ref):
    @pl.when(pl.program_id(1) == 0)
    def _():
        cn_ref[...] = jnp.zeros_like(cn_ref)
        m_ref[...] = jnp.zeros_like(m_ref)

    lane = _iota2((CHUNK, LANES), 1)
    feat = _iota2((LANES, CHUNK), 0)
    row8 = _iota2((8, LANES), 0)
    ones_t = ones_ref[...]
    ninf = jnp.float32(-jnp.inf)
    m_all = m_ref[...]
    heads = [None] * 8
    gates = []
    f_cum = []
    for d in (0, 1):
        gt_ref = (gtf_ref, gtb_ref)[d]
        tr = jnp.where(_tri_masks(d)[1], 1.0, 0.0).astype(BF16)
        f_cum.append(_mm_exact_r(_log_sigmoid(gt_ref[0, 8:16, :] + bcol_ref[8:16, :]), tr))

    for d in (0, 1):
        q_ref, v_ref, kt_ref = ((qf_ref, vf_ref, ktf_ref), (qb_ref, vb_ref, ktb_ref))[d]
        for p in range(2):
            q_t = q_ref[0, :, 128 * p:128 * p + 128]
            v_t = v_ref[0, :, 128 * p:128 * p + 128]
            kt_p = kt_ref[0, 128 * p:128 * p + 128, :] * (HEAD_W ** -0.5)
            kt_b = kt_p.astype(BF16)
            for half in range(2):
                r = 4 * d + 2 * p + half
                hmask = (lane >= 64) if half else (lane < 64)
                qm = jnp.where(hmask, q_t, 0.0).astype(BF16)
                cn = cn_ref[r]
                heads[r] = dict(
                    cn=cn, kt_p=kt_p,
                    vw=jnp.concatenate([jnp.where(hmask, v_t, 0.0).astype(BF16), ones_t], axis=1),
                    qk=_mm(qm, kt_b),
                    qc=_mm(qm, cn.astype(BF16)))

    for d in (0, 1):
        gt_ref = (gtf_ref, gtb_ref)[d]
        last = CHUNK - 1 if d == 0 else 0
        li = gt_ref[0, 0:8, :] + bcol_ref[0:8, :]
        f = f_cum[d]
        u = li - f
        m_old = m_all
        mx = jnp.maximum(m_old, _cummax_lanes(u, d))
        w_int = jnp.exp(m_old - mx)
        e_mi = jnp.exp(-(f + mx))
        u_max = jnp.max(u, axis=1, keepdims=True)
        e_end = jnp.exp(u - u_max)
        f_last = f[:, last:last + 1]
        b_end = f_last + u_max
        m_new = jnp.maximum(f_last + m_old, b_end)
        a_dec = jnp.exp(f_last + m_old - m_new)
        g_inc = jnp.exp(b_end - m_new)
        m_all = jnp.where((row8 >> 2) == d, m_new, m_all)
        gates.append((e_end, a_dec, g_inc) + _spread_operands(-mx, u, [w_int, e_mi], sel_ref, ones_ref))

    for r in range(8):
        hd = heads[r]
        e_end, a_dec, g_inc, lhs, rhs = gates[r // 4]
        fmask = (feat >= 64) if r % 2 else (feat < 64)
        ke = (jnp.where(fmask, hd["kt_p"], 0.0) * e_end[r:r + 1, :]).astype(BF16)
        a_r = a_dec[r:r + 1, :]
        g_r = g_inc[r:r + 1, :]
        hd["mt"] = _mm(lhs, rhs * rsel_ref[r])
        hd["cn_new"] = (jnp.concatenate([a_r, a_r], axis=1) * hd["cn"]
                        + jnp.concatenate([g_r, g_r], axis=1) * _mm(ke, hd["vw"]))

    for r in range(8):
        hd = heads[r]
        vis, _ = _tri_masks(r // 4)
        s = hd["qk"] * jnp.exp(jnp.where(vis, hd["mt"][:, 0:128], ninf))
        hd["sv"] = _mm(s.astype(BF16), hd["vw"])

    for d in (0, 1):
        out_ref = (hf_ref, hb_ref)[d]
        for p in range(2):
            pair = []
            for half in range(2):
                hd = heads[4 * d + 2 * p + half]
                w_b = hd["mt"][:, 128:256]
                tot = hd["sv"] + jnp.concatenate([w_b, w_b], axis=1) * hd["qc"]
                pair.append(tot[:, 0:128] / jnp.maximum(jnp.abs(tot[:, 128:256]), hd["mt"][:, 256:384]))
            out_ref[0, :, 128 * p:128 * p + 128] = jnp.where(lane < 64, pair[0], pair[1])
    m_ref[...] = m_all
    for r in range(8):
        cn_ref[r] = heads[r]["cn_new"]


def _mlstm(ml, mkt, gt, bcol, n_ctx_chunks):
    bsz, t, _ = ml.shape
    nc = t // CHUNK
    out = jax.ShapeDtypeStruct((bsz, t, W_BRANCH), F32)
    const2 = lambda b, c: (0, 0)

    def chunk_specs(chunk):
        return [pl.BlockSpec((1, CHUNK, W_BRANCH), lambda b, c: (b, chunk(c), 0)),
                pl.BlockSpec((1, CHUNK, W_BRANCH), lambda b, c: (b, chunk(c), 2)),
                pl.BlockSpec((1, W_BRANCH, CHUNK), lambda b, c: (b, 0, chunk(c))),
                pl.BlockSpec((1, LANES, CHUNK), lambda b, c: (b, 0, chunk(c)))]

    fwd_chunk = lambda c: c
    bwd_chunk = lambda c: _bwd_chunk(c, n_ctx_chunks, nc)
    return pl.pallas_call(
        _mlstm_kernel,
        grid=(bsz, nc),
        in_specs=chunk_specs(fwd_chunk) + chunk_specs(bwd_chunk) + [
            pl.BlockSpec((LANES, 1), const2), pl.BlockSpec((CHUNK, LANES), const2),
            pl.BlockSpec((CHUNK, 2 * LANES), const2),
            pl.BlockSpec((8, CHUNK, 3 * LANES), lambda b, c: (0, 0, 0))],
        out_specs=[pl.BlockSpec((1, CHUNK, W_BRANCH), lambda b, c: (b, c, 0)),
                   pl.BlockSpec((1, CHUNK, W_BRANCH), lambda b, c: (b, bwd_chunk(c), 0))],
        out_shape=[out, out],
        scratch_shapes=[pltpu.VMEM((8, LANES, 2 * LANES), F32), pltpu.VMEM((8, LANES), F32)],
        compiler_params=_cparams(("parallel", "arbitrary")),
        name="mlstm",
    )(ml, ml, mkt, gt, ml, ml, mkt, gt, bcol, jnp.ones((CHUNK, LANES), BF16), _spread_selector(2),
      _state_row_selector(3))


def _ssd_kernel(n_ctx_chunks, n_chunks,
                xf_ref, xfp_ref, xfn_ref, xb_ref, xbp_ref, xbn_ref,
                gtf_ref, gtb_ref, bcol_ref, acol_ref, cw_ref, cb_ref, dskip_ref, ones_ref, sel_ref, rsel_ref,
                yf_ref, yb_ref, s_ref):
    c = pl.program_id(1)

    @pl.when(c == 0)
    def _():
        s_ref[...] = jnp.zeros_like(s_ref)

    lane = _iota2((CHUNK, LANES), 1)
    row768 = _iota2((CHUNK, 768), 0)
    ninf = jnp.float32(-jnp.inf)

    dt, a_cum = [], []
    for d in (0, 1):
        gt_ref = (gtf_ref, gtb_ref)[d]
        tr = jnp.where(_tri_masks(d)[1], 1.0, 0.0).astype(BF16)
        dt.append(_softplus(gt_ref[0, 16:24, :] + bcol_ref[16:24, :]))
        a_cum.append(_mm_exact_r(dt[d] * acol_ref[16:24, :], tr))

    xa = []
    for d in (0, 1):
        x_ref, xp_ref, xn_ref = ((xf_ref, xfp_ref, xfn_ref), (xb_ref, xbp_ref, xbn_ref))[d]
        j = c if d == 0 else _bwd_chunk(c, n_ctx_chunks, n_chunks)
        seg_first = jnp.logical_or(j == 0, j == n_ctx_chunks)
        seg_last = jnp.logical_or(j == n_ctx_chunks - 1, j == n_chunks - 1)
        x = x_ref[0]
        prev = jnp.where(seg_first, 0.0, xp_ref[0, 7:8, :])
        nxt = jnp.where(seg_last, 0.0, xn_ref[0, 0:1, :])
        x_dn = jnp.where(row768 == 0, prev, pltpu.roll(x, 1, 0))
        x_up = jnp.where(row768 == CHUNK - 1, nxt, pltpu.roll(x, CHUNK - 1, 0))
        xa.append(_silu(x_dn * cw_ref[0:1, :] + x * cw_ref[1:2, :] + x_up * cw_ref[2:3, :] + cb_ref[...]))

    heads = [None] * 8
    groups = {}
    for d in (0, 1):
        for g in range(2):
            x_pair = xa[d][:, 128 * g:128 * g + 128]
            b_g = xa[d][:, 256 + 128 * g:256 + 128 * g + 128]
            c_gb = xa[d][:, 512 + 128 * g:512 + 128 * g + 128].astype(BF16)
            groups[d, g] = dict(x_pair=x_pair, bt=b_g.T.astype(BF16), cb=_mm_nt(c_gb, b_g.astype(BF16)))
            for half in range(2):
                r = 4 * d + 2 * g + half
                hmask = (lane >= 64) if half else (lane < 64)
                st = s_ref[r]
                xm = jnp.where(hmask, x_pair, 0.0)
                heads[r] = dict(st=st, xm=xm, xmb=xm.astype(BF16), cs=_mm(c_gb, st.astype(BF16)))

    for d in (0, 1):
        last = CHUNK - 1 if d == 0 else 0
        a_last = a_cum[d][:, last:last + 1]
        w_in = jnp.exp(a_last - a_cum[d]) * dt[d]
        e_last = jnp.exp(jnp.broadcast_to(a_last, (8, CHUNK)))
        lhs, rhs = _spread_operands(a_cum[d], -a_cum[d], [jnp.exp(a_cum[d]), w_in], sel_ref, ones_ref)
        for h in range(N_HEADS):
            r = 4 * d + h
            heads[r]["mt"] = _mm(lhs, rhs * rsel_ref[r])
            heads[r]["e_last"] = e_last[r:r + 1, :]

    for r in range(8):
        d, h = divmod(r, 4)
        hd = heads[r]
        ds = _mm(groups[d, h // 2]["bt"], (hd["xm"] * hd["mt"][:, 256:384]).astype(BF16))
        s_ref[r] = hd["e_last"] * hd["st"] + ds
    for r in range(8):
        d, h = divmod(r, 4)
        hd = heads[r]
        vis, _ = _tri_masks(d)
        sc = groups[d, h // 2]["cb"] * jnp.exp(jnp.where(vis, hd["mt"][:, 0:128], ninf)) * dt[d][r:r + 1, :]
        hd["y"] = _mm(sc.astype(BF16), hd["xmb"]) + hd["mt"][:, 128:256] * hd["cs"]

    for d in (0, 1):
        out_ref = (yf_ref, yb_ref)[d]
        for g in range(2):
            y_pair = heads[4 * d + 2 * g]["y"] + heads[4 * d + 2 * g + 1]["y"]
            if d == 0:
                y_pair = y_pair + dskip_ref[:, 128 * g:128 * g + 128] * groups[d, g]["x_pair"]
            out_ref[0, :, 128 * g:128 * g + 128] = y_pair


def _ssd(ss, gt, bcol, acol, conv_w, conv_b, dskip, n_ctx_chunks):
    bsz, t, _ = ss.shape
    nc = t // CHUNK
    sub = CHUNK // 8
    nsub = t // 8
    bc = lambda c: _bwd_chunk(c, n_ctx_chunks, nc)
    fwd = lambda b, c: (b, c, 0)
    bwd = lambda b, c: (b, bc(c), 0)
    fwd_p = lambda b, c: (b, jnp.maximum(c * sub - 1, 0), 0)
    fwd_n = lambda b, c: (b, jnp.minimum((c + 1) * sub, nsub - 1), 0)
    bwd_p = lambda b, c: (b, jnp.maximum(bc(c) * sub - 1, 0), 0)
    bwd_n = lambda b, c: (b, jnp.minimum((bc(c) + 1) * sub, nsub - 1), 0)
    fwd_t = lambda b, c: (b, 0, c)
    bwd_t = lambda b, c: (b, 0, bc(c))
    const2 = lambda b, c: (0, 0)
    out = jax.ShapeDtypeStruct((bsz, t, W_BRANCH), F32)
    return pl.pallas_call(
        functools.partial(_ssd_kernel, n_ctx_chunks, nc),
        grid=(bsz, nc),
        in_specs=[pl.BlockSpec((1, CHUNK, 768), fwd), pl.BlockSpec((1, 8, 768), fwd_p),
                  pl.BlockSpec((1, 8, 768), fwd_n),
                  pl.BlockSpec((1, CHUNK, 768), bwd), pl.BlockSpec((1, 8, 768), bwd_p),
                  pl.BlockSpec((1, 8, 768), bwd_n),
                  pl.BlockSpec((1, LANES, CHUNK), fwd_t), pl.BlockSpec((1, LANES, CHUNK), bwd_t),
                  pl.BlockSpec((LANES, 1), const2), pl.BlockSpec((LANES, 1), const2),
                  pl.BlockSpec((3, 768), const2), pl.BlockSpec((1, 768), const2),
                  pl.BlockSpec((1, W_BRANCH), const2),
                  pl.BlockSpec((CHUNK, LANES), const2), pl.BlockSpec((CHUNK, 2 * LANES), const2),
                  pl.BlockSpec((8, CHUNK, 3 * LANES), lambda b, c: (0, 0, 0))],
        out_specs=[pl.BlockSpec((1, CHUNK, W_BRANCH), fwd), pl.BlockSpec((1, CHUNK, W_BRANCH), bwd)],
        out_shape=[out, out],
        scratch_shapes=[pltpu.VMEM((8, SSM_N, LANES), F32)],
        compiler_params=_cparams(("parallel", "arbitrary")),
        name="ssd",
    )(ss, ss, ss, ss, ss, ss, gt, gt, bcol, acol, conv_w, conv_b, dskip,
      jnp.ones((CHUNK, LANES), BF16), _spread_selector(2), _state_row_selector(3))


def _hgrn2_kernel(hf_ref, vtf_ref, hb_ref, vtb_ref, loglb_ref, log1m_ref, onem_ref, e_ref, of_ref, ob_ref,
                  st_ref):
    @pl.when(pl.program_id(1) == 0)
    def _():
        st_ref[...] = jnp.zeros_like(st_ref)

    ri = _iota2((CHUNK, CHUNK), 0)
    ci = _iota2((CHUNK, CHUNK), 1)
    lane = _iota2((CHUNK, LANES), 1)
    rw = _iota2((CHUNK, W_BRANCH), 0)
    blockdiag = (ri >> 6) == (ci >> 6)
    ninf = jnp.float32(-jnp.inf)
    nb = CHUNK // SUB
    rs = _iota2((nb, SUB, W_BRANCH), 1)

    dirs = []
    for d in (0, 1):
        h_ref = (hf_ref, hb_ref)[d]
        tc = jnp.where(_tri_masks(d)[0], 1.0, 0.0).astype(BF16)
        z = h_ref[0, :, W_BRANCH * (1 + d):W_BRANCH * (2 + d)]
        e = jnp.exp(-jnp.abs(z))
        ope = 1.0 + e
        la = loglb_ref[d:d + 1, :]
        lb_ = log1m_ref[d:d + 1, :] + (jnp.minimum(z, 0.0) - jnp.log(ope))
        logf = jnp.maximum(la, lb_) + jnp.log(1.0 + jnp.exp(-jnp.abs(la - lb_)))
        dirs.append(dict(
            q=_silu(h_ref[0, :, 0:W_BRANCH]),
            v=h_ref[0, :, 3 * W_BRANCH:4 * W_BRANCH],
            kk=onem_ref[d:d + 1, :] * (jnp.where(z >= 0.0, e, 1.0) / ope),
            gcum=_mm_exact_l(tc, logf)))

    for d in (0, 1):
        dd = dirs[d]
        q, kk, gcum = dd["q"], dd["kk"], dd["gcum"]
        vt_ref = (vtf_ref, vtb_ref)[d]
        last = CHUNK - 1 if d == 0 else 0
        g_last = gcum[last:last + 1, :]
        qg = (q * jnp.exp(gcum)).astype(BF16)
        kg = (kk * jnp.exp(g_last - gcum)).astype(BF16)

        dd["o_inter"] = []
        for p in range(2):
            sl = slice(128 * p, 128 * p + 128)
            st = st_ref[2 * d + p]
            dd["o_inter"].append(_mm_nt(qg[:, sl], st.astype(BF16)))
            dst = _mm(vt_ref[0, sl, :].astype(BF16), kg[:, sl])
            st_ref[2 * d + p] = st * jnp.exp(g_last[:, sl]) + jnp.where(blockdiag, dst, 0.0)

        a_mats = [jnp.zeros((CHUNK, CHUNK), F32) for _ in range(N_HEADS)]
        blk = CHUNK // 2
        while blk >= SUB:
            first = (rw & (2 * blk - 1)) < blk
            edge = (blk - 1) if d == 0 else blk
            gb = gcum.reshape(CHUNK // (2 * blk), 2 * blk, W_BRANCH)[:, edge:edge + 1, :]
            gb = jnp.broadcast_to(gb, (CHUNK // (2 * blk), 2 * blk, W_BRANCH)).reshape(CHUNK, W_BRANCH)
            q_side = jnp.logical_not(first) if d == 0 else first
            qt = q * jnp.exp(jnp.where(q_side, gcum - gb, ninf))
            kt = (kk * jnp.exp(jnp.where(q_side, ninf, gb - gcum))).astype(BF16)
            same = (ri >> int(math.log2(2 * blk))) == (ci >> int(math.log2(2 * blk)))
            for h in range(N_HEADS):
                p, half = divmod(h, 2)
                hmask = (lane >= 64) if half else (lane < 64)
                qh = jnp.where(hmask, qt[:, 128 * p:128 * p + 128], 0.0).astype(BF16)
                a_mats[h] = a_mats[h] + jnp.where(same, _mm_nt(qh, kt[:, 128 * p:128 * p + 128]), 0.0)
            blk //= 2
        dd["a_mats"] = a_mats

        g3 = gcum.reshape(nb, SUB, W_BRANCH)
        q3 = q.reshape(nb, SUB, W_BRANCH)
        k3 = kk.reshape(nb, SUB, W_BRANCH)
        v3 = dd["v"].reshape(nb, SUB, W_BRANCH)
        o3 = None
        for j in range(SUB):
            ok = (rs >= j) if d == 0 else (rs <= j)
            pj = q3 * jnp.exp(jnp.where(ok, g3 - g3[:, j:j + 1, :], ninf)) * k3[:, j:j + 1, :]
            red = _mm(pj.reshape(CHUNK, W_BRANCH).astype(BF16), e_ref[...])
            term = red.reshape(nb, SUB, W_BRANCH) * v3[:, j:j + 1, :]
            o3 = term if o3 is None else o3 + term
        dd["o_diag"] = o3.reshape(CHUNK, W_BRANCH)

    for d in (0, 1):
        dd = dirs[d]
        out_ref = (of_ref, ob_ref)[d]
        for p in range(2):
            sl = slice(128 * p, 128 * p + 128)
            a_cat = jnp.concatenate([dd["a_mats"][2 * p], dd["a_mats"][2 * p + 1]], axis=1).astype(BF16)
            v_p = dd["v"][:, sl]
            v_cat = jnp.concatenate([jnp.where(lane < 64, v_p, 0.0), jnp.where(lane >= 64, v_p, 0.0)],
                                    axis=0).astype(BF16)
            out_ref[0, :, sl] = dd["o_inter"][p] + _mm(a_cat, v_cat) + dd["o_diag"][:, sl]


def _hgrn2(hg, hvt, loglb, log1m, onem, e64, n_ctx_chunks):
    bsz, t, _ = hg.shape
    nc = t // CHUNK
    fwd = lambda b, c: (b, c, 0)
    bwd = lambda b, c: (b, _bwd_chunk(c, n_ctx_chunks, nc), 0)
    fwd_t = lambda b, c: (b, 0, c)
    bwd_t = lambda b, c: (b, 0, _bwd_chunk(c, n_ctx_chunks, nc))
    const2 = lambda b, c: (0, 0)
    out = jax.ShapeDtypeStruct((bsz, t, W_BRANCH), F32)
    return pl.pallas_call(
        _hgrn2_kernel,
        grid=(bsz, nc),
        in_specs=[pl.BlockSpec((1, CHUNK, 1024), fwd), pl.BlockSpec((1, W_BRANCH, CHUNK), fwd_t),
                  pl.BlockSpec((1, CHUNK, 1024), bwd), pl.BlockSpec((1, W_BRANCH, CHUNK), bwd_t),
                  pl.BlockSpec((2, W_BRANCH), const2), pl.BlockSpec((2, W_BRANCH), const2),
                  pl.BlockSpec((2, W_BRANCH), const2), pl.BlockSpec((W_BRANCH, W_BRANCH), const2)],
        out_specs=[pl.BlockSpec((1, CHUNK, W_BRANCH), fwd), pl.BlockSpec((1, CHUNK, W_BRANCH), bwd)],
        out_shape=[out, out],
        scratch_shapes=[pltpu.VMEM((4, LANES, LANES), F32)],
        compiler_params=_cparams(("parallel", "arbitrary")),
        name="hgrn2",
    )(hg, hvt, hg, hvt, loglb, log1m, onem, e64)


def _attn(daq, kt, vx, lam_vecs, lam_init, n_ctx_tiles, skip_ctx):
    bsz, t, _ = daq.shape
    tq = ROW_TILE
    n_k_tiles = t // KEY_TILE
    n_lat_tiles = n_k_tiles - n_ctx_tiles
    group = math.gcd(KEY_GROUP, n_lat_tiles)
    q_off = n_ctx_tiles if skip_ctx else 0
    nq = t // tq - q_off
    n_maps = W_BRANCH // DA_QK
    rows = n_maps * tq

    def kern(q_ref, kt_ref, vx_ref, lv_ref, o_ref, q8_ref, m_ref, acc_ref, s_ref, mx_ref):
        qi = pl.program_id(1) + q_off
        lane = _iota2((tq, W_BRANCH), 1)
        q = q_ref[0]
        for j in range(n_maps):
            q8_ref[j * tq:(j + 1) * tq, :] = jnp.where((lane >> 5) == j, q, jnp.zeros_like(q))
        m_ref[...] = jnp.full_like(m_ref, -jnp.inf)
        acc_ref[...] = jnp.zeros_like(acc_ref)

        def update(t0, n):
            q8 = q8_ref[...]
            s = [_mm(q8, kt_ref[0, t0 + i]) for i in range(n)]
            mx = s[0][:, 0:LANES]
            for i in range(n):
                for c in range(KEY_TILE // LANES):
                    if i or c:
                        mx = jnp.maximum(mx, s[i][:, LANES * c:LANES * (c + 1)])
            m_old = m_ref[...]
            m_new = jnp.maximum(m_old, jnp.max(mx, axis=1, keepdims=True))
            alpha = jnp.exp(m_old - m_new)
            m_ref[...] = m_new
            m2 = jnp.concatenate([m_new] * (KEY_TILE // LANES), axis=1)
            p = [jnp.exp(s[i] - m2).astype(BF16) for i in range(n)]
            for hd in range(N_HEADS):
                r0 = 2 * hd * tq
                pv = _mm(p[0][r0:r0 + 2 * tq], vx_ref[0, t0, hd])
                for i in range(1, n):
                    pv = pv + _mm(p[i][r0:r0 + 2 * tq], vx_ref[0, t0 + i, hd])
                acc_ref[r0:r0 + 2 * tq, :] = alpha[r0:r0 + 2 * tq] * acc_ref[r0:r0 + 2 * tq, :] + pv

        def scores(g, slot):
            q8 = q8_ref[...]
            t0 = n_ctx_tiles + g * group
            mx = None
            for i in range(group):
                s = _mm(q8, kt_ref[0, t0 + i])
                s_ref[slot, :, KEY_TILE * i:KEY_TILE * (i + 1)] = s
                for c in range(KEY_TILE // LANES):
                    part = s[:, LANES * c:LANES * (c + 1)]
                    mx = part if mx is None else jnp.maximum(mx, part)
            mx_ref[slot] = mx

        def absorb(g, slot):
            t0 = n_ctx_tiles + g * group
            m_old = m_ref[...]
            m_new = jnp.maximum(m_old, jnp.max(mx_ref[slot], axis=1, keepdims=True))
            alpha = jnp.exp(m_old - m_new)
            m_ref[...] = m_new
            m2 = jnp.concatenate([m_new] * (KEY_TILE // LANES), axis=1)
            p = [jnp.exp(s_ref[slot, :, KEY_TILE * i:KEY_TILE * (i + 1)] - m2).astype(BF16) for i in range(group)]
            for hd in range(N_HEADS):
                r0 = 2 * hd * tq
                pv = _mm(p[0][r0:r0 + 2 * tq], vx_ref[0, t0, hd])
                for i in range(1, group):
                    pv = pv + _mm(p[i][r0:r0 + 2 * tq], vx_ref[0, t0 + i, hd])
                acc_ref[r0:r0 + 2 * tq, :] = alpha[r0:r0 + 2 * tq] * acc_ref[r0:r0 + 2 * tq, :] + pv

        update(0, n_ctx_tiles)

        @pl.when(qi >= n_ctx_tiles)
        def _():
            n_groups = n_lat_tiles // group
            n_pairs = (n_groups - 1) // 2
            scores(0, 0)

            def body(k, carry):
                g = 2 * k
                scores(g + 1, 1)
                absorb(g, 0)
                scores(g + 2, 0)
                absorb(g + 1, 1)
                return carry
            lax.fori_loop(0, n_pairs, body, 0)
            g0 = 2 * n_pairs
            if n_groups - g0 == 2:
                scores(g0 + 1, 1)
            absorb(g0, 0)
            if n_groups - g0 == 2:
                absorb(g0 + 1, 1)

        lv = lv_ref[...]
        lam = (jnp.exp(jnp.sum(lv[0:1, :] * lv[1:2, :], axis=1, keepdims=True))
               - jnp.exp(jnp.sum(lv[2:3, :] * lv[3:4, :], axis=1, keepdims=True)) + lam_init)
        lane1 = _iota2((tq, LANES), 1)
        for pr in range(N_HEADS // 2):
            halves = []
            for half in range(2):
                r0 = 2 * (2 * pr + half) * tq
                one = 0 if half else HEAD_W
                a0 = acc_ref[r0:r0 + tq, :]
                a1 = acc_ref[r0 + tq:r0 + 2 * tq, :]
                halves.append(a0 / a0[:, one:one + 1] - lam * (a1 / a1[:, one:one + 1]))
            o_ref[0, :, LANES * pr:LANES * (pr + 1)] = jnp.where(lane1 < HEAD_W, halves[0], halves[1])

    return pl.pallas_call(
        kern,
        grid=(bsz, nq),
        in_specs=[pl.BlockSpec((1, tq, W_BRANCH), lambda b, i: (b, i + q_off, 0)),
                  pl.BlockSpec((1, n_k_tiles, W_BRANCH, KEY_TILE), lambda b, i: (b, 0, 0, 0)),
                  pl.BlockSpec((1, n_k_tiles, N_HEADS, KEY_TILE, LANES), lambda b, i: (b, 0, 0, 0, 0)),
                  pl.BlockSpec(lam_vecs.shape, lambda b, i: (0, 0))],
        out_specs=pl.BlockSpec((1, tq, W_BRANCH), lambda b, i: (b, i, 0)),
        out_shape=jax.ShapeDtypeStruct((bsz, nq * tq, W_BRANCH), F32),
        scratch_shapes=[pltpu.VMEM((rows, W_BRANCH), BF16),
                        pltpu.VMEM((rows, LANES), F32),
                        pltpu.VMEM((rows, LANES), F32),
                        pltpu.VMEM((2, rows, group * KEY_TILE), F32),
                        pltpu.VMEM((2, rows, LANES), F32)],
        compiler_params=_cparams(("parallel", "parallel")),
        name="diff_attn",
    )(daq, kt, vx, lam_vecs)


def _seg_mean(x, e_ref):
    x0 = x.astype(BF16)
    x1 = (x - x0.astype(F32)).astype(BF16)
    return (_mm(x0, e_ref[...]) + _mm(x1, e_ref[...])) * (1.0 / HEAD_W)


def _out_kernel(n_ctx_tiles, ctx_row, q_off, lam_init, final,
                x_ref, mo_ref, mlo_ref, mlz_ref, mhf_ref, mhb_ref, dao_ref, daz_ref,
                syf_ref, syb_ref, ssz_ref, hof_ref, hob_ref, hgz_ref,
                mlg_ref, dag_ref, ssg_ref, hgg_ref, e_ref, w_ref, fg_ref, o_ref):
    b = pl.program_id(0)
    i = pl.program_id(1) + q_off
    d = x_ref.shape[2]
    r = jnp.where(i < n_ctx_tiles, ctx_row, b)
    gate = mo_ref[pl.ds(r, 1), :][:, 2 * d:3 * d]

    u = _sigmoid(mlo_ref[0]) * (mhf_ref[0] + mhb_ref[0])
    dev = u - _seg_mean(u, e_ref)
    y_ml = dev * lax.rsqrt(_seg_mean(dev * dev, e_ref) + EPS) * mlg_ref[...] * _silu(mlz_ref[0])

    o = dao_ref[0]
    y_da = (o * lax.rsqrt(_seg_mean(o * o, e_ref) + EPS) * dag_ref[...]) * (1.0 - lam_init) * _silu(daz_ref[0])

    ys = (syf_ref[0] + syb_ref[0]) * _silu(ssz_ref[0])
    parts = []
    for g in range(2):
        yg = ys[:, 128 * g:128 * g + 128]
        parts.append(yg * lax.rsqrt(jnp.mean(yg * yg, axis=-1, keepdims=True) + EPS))
    y_ss = jnp.concatenate(parts, axis=1) * ssg_ref[...]

    oh = hof_ref[0] + hob_ref[0]
    y_hg = (oh * lax.rsqrt(_seg_mean(oh * oh, e_ref) + EPS) * hgg_ref[...]) * _silu(hgz_ref[0])

    acc = _mm(y_ml.astype(BF16), w_ref[0:W_BRANCH, :])
    acc = acc + _mm(y_da.astype(BF16), w_ref[W_BRANCH:2 * W_BRANCH, :])
    acc = acc + _mm(y_ss.astype(BF16), w_ref[2 * W_BRANCH:3 * W_BRANCH, :])
    acc = acc + _mm(y_hg.astype(BF16), w_ref[3 * W_BRANCH:4 * W_BRANCH, :])
    x_new = x_ref[0] + gate * acc
    if final:
        x_new = x_new * lax.rsqrt(jnp.mean(x_new * x_new, axis=-1, keepdims=True) + EPS) * fg_ref[...]
    o_ref[0] = x_new


def _out_proj(xs, mo, ml, mhf, mhb, dao, daz, ss, syf, syb, hg, hof, hob,
              mlg, dag, ssg, hgg, e64, w_out, final_g, lam_init, n_ctx_tiles, ctx_row, final):
    bsz, t, d = xs.shape
    tm = ROW_TILE
    q_off = n_ctx_tiles if final else 0
    nrow = t // tm - q_off
    row = lambda b, i: (b, i + q_off, 0)
    col = lambda k: (lambda b, i: (b, i + q_off, k))
    const2 = lambda b, i: (0, 0)
    wb = pl.BlockSpec((1, tm, W_BRANCH), row)
    return pl.pallas_call(
        functools.partial(_out_kernel, n_ctx_tiles, ctx_row, q_off, lam_init, final),
        grid=(bsz, nrow),
        in_specs=[pl.BlockSpec((1, tm, d), row),
                  pl.BlockSpec(mo.shape, const2),
                  pl.BlockSpec((1, tm, W_BRANCH), col(3)), pl.BlockSpec((1, tm, W_BRANCH), col(4)),
                  wb, wb,
                  pl.BlockSpec((1, tm, W_BRANCH), lambda b, i: (b, i, 0)) if final else wb,
                  wb,
                  wb, wb, pl.BlockSpec((1, tm, W_BRANCH), col(3)),
                  wb, wb, pl.BlockSpec((1, tm, W_BRANCH), col(4)),
                  pl.BlockSpec((1, W_BRANCH), const2), pl.BlockSpec((1, W_BRANCH), const2),
                  pl.BlockSpec((1, W_BRANCH), const2), pl.BlockSpec((1, W_BRANCH), const2),
                  pl.BlockSpec((W_BRANCH, W_BRANCH), const2),
                  pl.BlockSpec(w_out.shape, const2),
                  pl.BlockSpec((1, d), const2)],
        out_specs=pl.BlockSpec((1, tm, d), lambda b, i: (b, i, 0)),
        out_shape=jax.ShapeDtypeStruct((bsz, nrow * tm, d), F32),
        compiler_params=_cparams(("parallel", "parallel")),
        name="out_proj",
    )(xs, mo, ml, ml, mhf, mhb, dao, daz, syf, syb, ss, hof, hob, hg,
      mlg, dag, ssg, hgg, e64, w_out, final_g.reshape(1, d))


def _relayout_w_in(w_in):
    d = w_in.shape[0]
    o = {}
    off = 0
    for name, n in (("ml_q", 256), ("ml_k", 256), ("ml_v", 256), ("ml_o", 256), ("ml_i", 8), ("ml_f", 8),
                    ("ml_z", 256), ("da_q", 256), ("da_k", 256), ("da_v", 256), ("da_z", 256),
                    ("ss_xbc", 768), ("ss_dt", 8), ("ss_z", 256),
                    ("hg_q", 256), ("hg_f", 512), ("hg_i", 256), ("hg_z", 256)):
        o[name] = w_in[:, off:off + n]
        off += n

    def rot(w):
        g = w.reshape(d, W_BRANCH // DA_QK, 2, DA_QK // 2)
        return jnp.stack([-g[:, :, 1], g[:, :, 0]], axis=2).reshape(d, W_BRANCH)

    gates = jnp.concatenate([o["ml_i"], o["ml_f"], o["ss_dt"], jnp.zeros((d, LANES - 24), w_in.dtype)], axis=1)
    zpad = jnp.zeros((d, HEAD_W), w_in.dtype)
    v_ext = []
    for hd in range(N_HEADS):
        v_h = o["da_v"][:, HEAD_W * hd:HEAD_W * (hd + 1)]
        v_ext += [zpad, v_h] if hd % 2 else [v_h, zpad]
    w2 = jnp.concatenate([o["ml_q"], o["ml_k"], o["ml_v"], o["ml_o"], o["ml_z"],
                          o["da_q"], rot(o["da_q"])] + v_ext + [o["da_z"],
                          o["ss_xbc"], o["ss_z"],
                          o["hg_q"], o["hg_f"], o["hg_i"], o["hg_z"]], axis=1)
    wt = jnp.concatenate([o["da_k"], rot(o["da_k"]), gates, o["ml_k"], o["hg_i"]], axis=1).T
    return w2.astype(BF16), wt.astype(BF16)


def _rope_tables(n_ctx, seq):
    pos = jnp.arange(seq)
    rows = (pos // GRID_W).astype(F32)
    cols = (pos % GRID_W).astype(F32)
    axis = DA_QK // 2
    inv = ROPE_BASE ** (-jnp.arange(0, axis, 2, dtype=F32) / axis)
    ang = jnp.concatenate([rows[:, None] * inv, cols[:, None] * inv], axis=-1)
    cos = jnp.concatenate([jnp.ones((n_ctx, axis), F32), jnp.cos(ang)], axis=0)
    sin = jnp.concatenate([jnp.zeros((n_ctx, axis), F32), jnp.sin(ang)], axis=0)
    reps = W_BRANCH // axis
    return jnp.tile(cos, (1, reps)), jnp.tile(sin, (1, reps))


def _gate_col(vals):
    v = jnp.concatenate([vals.astype(F32), jnp.zeros((LANES - vals.shape[0],), F32)])
    return v.reshape(LANES, 1)


def kernel(x, c, ctx, c_ctx, w_mod, b_mod, norm_g, w_in, w_out, ml_gate_b, ml_norm_g, da_lambda, da_norm_g,
           ss_conv_w, ss_conv_b, ss_dt_bias, ss_a_log, ss_d, ss_norm_g, hg_lower, hg_norm_g, final_g):
    bsz, seq, d = x.shape
    n_ctx = ctx.shape[1]
    depth = w_mod.shape[0]
    assert n_ctx % ROW_TILE == 0 and seq % ROW_TILE == 0 and bsz < 8
    n_ctx_tiles = n_ctx // ROW_TILE
    n_ctx_chunks = n_ctx // CHUNK
    ctx_row = bsz

    xs = jnp.concatenate([ctx, x], axis=1)
    cc = jnp.concatenate([c, c_ctx[None, :], jnp.zeros((8 - bsz - 1, d), F32)], axis=0)
    cos, sin = _rope_tables(n_ctx, seq)
    cos_t, sin_t = cos.T, sin.T
    one_lanes = jnp.arange(V_EXT) % LANES == jnp.where((jnp.arange(V_EXT) // LANES) % 2 == 1, 0, HEAD_W)
    ones_row = one_lanes.astype(F32).reshape(1, V_EXT)
    lb_all = jnp.cumsum(jax.nn.softmax(hg_lower.astype(F32), axis=1), axis=1)
    lb_all = lb_all - lb_all[:, :1]
    hid = _iota2((W_BRANCH, W_BRANCH), 0) // HEAD_W
    e64 = (hid == hid.T).astype(BF16)

    out = None
    for l in range(depth):
        lam_init = 0.8 - 0.6 * math.exp(-0.3 * l)
        final = l == depth - 1
        w2, wt = _relayout_w_in(w_in[l])
        mo = _modulation(cc, w_mod[l], b_mod[l])
        ml, daq, dak, dav, daz, ss, hg, gt, mkt, hvt = _in_proj(xs, mo, norm_g[l], cos, sin, cos_t, sin_t, w2, wt,
                                                          ones_row, n_ctx_tiles, ctx_row)

        gb = ml_gate_b[l]
        zeros8 = jnp.zeros((8,), F32)
        bcol = _gate_col(jnp.concatenate([gb[:, 0].reshape(-1), gb[:, 1].reshape(-1),
                                                ss_dt_bias[l].reshape(-1)]))
        acol = _gate_col(jnp.concatenate([zeros8, zeros8, -jnp.exp(ss_a_log[l].astype(F32)).reshape(-1)]))
        mhf, mhb = _mlstm(ml, mkt, gt, bcol, n_ctx_chunks)
        dao = _attn(daq, dak, dav, da_lambda[l].astype(F32), lam_init, n_ctx_tiles, final)
        dskip = jnp.repeat(ss_d[l].astype(F32), HEAD_W).reshape(1, W_BRANCH)
        syf, syb = _ssd(ss, gt, bcol, acol, ss_conv_w[l], ss_conv_b[l].reshape(1, -1),
                        dskip, n_ctx_chunks)
        lbh = lb_all[:, l]
        hof, hob = _hgrn2(hg, hvt, jnp.log(lbh), jnp.log1p(-lbh), 1.0 - lbh, e64, n_ctx_chunks)

        res = _out_proj(xs, mo, ml, mhf, mhb, dao, daz, ss, syf, syb, hg, hof, hob,
                        ml_norm_g[l].reshape(1, -1), jnp.tile(da_norm_g[l], N_HEADS).reshape(1, -1),
                        ss_norm_g[l].reshape(1, -1), hg_norm_g[l].reshape(1, -1), e64,
                        w_out[l].astype(BF16), final_g, lam_init, n_ctx_tiles, ctx_row, final)
        if final:
            out = res
        else:
            xs = res
    return out
```

```python
import functools
import math

import jax
import jax.numpy as jnp
from jax import lax
from jax.experimental import pallas as pl
from jax.experimental.pallas import tpu as pltpu

F32 = jnp.float32
BF16 = jnp.bfloat16

EPS = 1e-6
GRID_W = 64
ROPE_BASE = 10000.0
N_HEADS = 4
HEAD_W = 64
W_BRANCH = N_HEADS * HEAD_W
DA_QK = 32
SSM_N = 128
CHUNK = 128
ROW_TILE = 256
SUB = 8
LANES = 128
VMEM_LIMIT = 56 * 1024 * 1024

C_ML = 0
C_DAQ = 1024
C_DAV = 1280
C_DAZ = 1792
C_SS = 2048
C_HG = 3072
C_END = 4352
V_EXT = N_HEADS * LANES
KEY_TILE = 256
KEY_GROUP = 4

NT = (((1,), (1,)), ((), ()))
TN = (((0,), (0,)), ((), ()))


def _mm(a, b):
    return jnp.dot(a, b, preferred_element_type=F32)


def _mm_nt(a, b):
    return lax.dot_general(a, b, NT, preferred_element_type=F32)


def _mm_tn(a, b):
    return lax.dot_general(a, b, TN, preferred_element_type=F32)


def _split3(x):
    x0 = x.astype(BF16)
    r = x - x0.astype(F32)
    x1 = r.astype(BF16)
    x2 = (r - x1.astype(F32)).astype(BF16)
    return x0, x1, x2


def _mm_exact_l(t, x):
    x0, x1, x2 = _split3(x)
    return _mm(t, x0) + _mm(t, x1) + _mm(t, x2)


def _mm_exact_r(x, t):
    x0, x1, x2 = _split3(x)
    return _mm(x0, t) + _mm(x1, t) + _mm(x2, t)


def _sigmoid(x):
    return 1.0 / (1.0 + jnp.exp(-x))


def _silu(x):
    return x * _sigmoid(x)


def _log_sigmoid(x):
    return jnp.minimum(x, 0.0) - jnp.log(1.0 + jnp.exp(-jnp.abs(x)))


def _softplus(x):
    return jnp.maximum(x, 0.0) + jnp.log(1.0 + jnp.exp(-jnp.abs(x)))


def _iota2(shape, axis):
    return lax.broadcasted_iota(jnp.int32, shape, axis)


def _cparams(sem):
    return pltpu.CompilerParams(dimension_semantics=sem, vmem_limit_bytes=VMEM_LIMIT)


def _mod_kernel(c_ref, w_ref, b_ref, o_ref):
    c = c_ref[...]
    o_ref[...] = jnp.dot(_silu(c), w_ref[...], precision=lax.Precision.HIGHEST,
                         preferred_element_type=F32) + b_ref[...]


def _modulation(cc, w_mod, b_mod):
    d = cc.shape[1]
    n = w_mod.shape[1]
    tn = 512
    return pl.pallas_call(
        _mod_kernel,
        grid=(n // tn,),
        in_specs=[pl.BlockSpec((8, d), lambda j: (0, 0)),
                  pl.BlockSpec((d, tn), lambda j: (0, j)),
                  pl.BlockSpec((1, tn), lambda j: (0, j))],
        out_specs=pl.BlockSpec((8, tn), lambda j: (0, j)),
        out_shape=jax.ShapeDtypeStruct((8, n), F32),
        compiler_params=_cparams(("arbitrary",)),
        name="modulation",
    )(cc, w_mod, b_mod.reshape(1, n))


def _stream_specs(stream, n_ctx_tiles, first_tile):
    _, x_src, x_off, _ = stream
    d = x_src.shape[2]
    return [pl.BlockSpec((1, ROW_TILE, d), lambda b, i: (b, jnp.minimum(i + first_tile, n_ctx_tiles - 1), 0)),
            pl.BlockSpec((1, ROW_TILE, d), lambda b, i: (b, jnp.maximum(i + first_tile - n_ctx_tiles, 0) + x_off, 0))]


def _in_kernel(n_ctx_tiles, ctx_row, c_ref, x_ref, mo_ref, g_ref, cos_ref, sin_ref, cost_ref, sint_ref,
               w_ref, wt_ref, ones_ref,
               ml_ref, daq_ref, kt_ref, vx_ref, daz_ref, ss_ref, hg_ref, gt_ref, mkt_ref, hvt_ref):
    b = pl.program_id(0)
    i = pl.program_id(1)
    d = x_ref.shape[2]
    r = jnp.where(i < n_ctx_tiles, ctx_row, b)
    mo = mo_ref[pl.ds(r, 1), :]
    shift = mo[:, 0:d]
    scale = mo[:, d:2 * d]
    x = jnp.where(i < n_ctx_tiles, c_ref[0], x_ref[0])
    y = x * lax.rsqrt(jnp.mean(x * x, axis=-1, keepdims=True) + EPS) * g_ref[...]
    h = (y * (1.0 + scale) + shift).astype(BF16)

    ml_ref[0] = _mm(h, w_ref[:, C_ML:C_DAQ])
    q = _mm(h, w_ref[:, C_DAQ:C_DAV])
    half = DA_QK // 2
    lane = _iota2((x.shape[0], LANES), 1)
    q_rot = []
    for c in range(W_BRANCH // LANES):
        qc = q[:, LANES * c:LANES * (c + 1)]
        q_rot.append(jnp.where((lane & half) == 0, -pltpu.roll(qc, LANES - half, 1), pltpu.roll(qc, half, 1)))
    q = q * cos_ref[...] + jnp.concatenate(q_rot, axis=1) * sin_ref[...]
    daq_ref[0] = (q * (DA_QK ** -0.5)).astype(BF16)
    vx = (_mm(h, w_ref[:, C_DAV:C_DAZ]) + ones_ref[...]).astype(BF16)
    for hd in range(N_HEADS):
        vx_ref[0, 0, hd] = vx[:, LANES * hd:LANES * (hd + 1)]
    daz_ref[0] = _mm(h, w_ref[:, C_DAZ:C_SS])
    ss_ref[0] = _mm(h, w_ref[:, C_SS:C_HG])
    hg_ref[0] = _mm(h, w_ref[:, C_HG:C_END])
    tr = _mm_nt(wt_ref[...], h)
    kt = tr[0:W_BRANCH]
    kt_rot = []
    for j in range(W_BRANCH // DA_QK):
        kt_rot += [-kt[DA_QK * j + half:DA_QK * (j + 1)], kt[DA_QK * j:DA_QK * j + half]]
    kt_ref[0, 0] = (kt * cost_ref[...] + jnp.concatenate(kt_rot, axis=0) * sint_ref[...]).astype(BF16)
    gt_ref[0] = tr[W_BRANCH:W_BRANCH + LANES]
    mkt_ref[0] = tr[W_BRANCH + LANES:2 * W_BRANCH + LANES]
    hvt_ref[0] = tr[2 * W_BRANCH + LANES:3 * W_BRANCH + LANES]


def _in_proj(stream, mo, g, cos, sin, cos_t, sin_t, w2, wt, ones_row, n_ctx_tiles, ctx_row):
    c_src, x_src, x_off, t = stream
    bsz, _, d = x_src.shape
    tm = ROW_TILE
    assert tm == KEY_TILE
    row = lambda b, i: (b, i, 0)
    const2 = lambda b, i: (0, 0)
    out_shape = [
        jax.ShapeDtypeStruct((bsz, t, 1024), F32),
        jax.ShapeDtypeStruct((bsz, t, W_BRANCH), BF16),
        jax.ShapeDtypeStruct((bsz, t // tm, W_BRANCH, tm), BF16),
        jax.ShapeDtypeStruct((bsz, t // tm, N_HEADS, tm, LANES), BF16),
        jax.ShapeDtypeStruct((bsz, t, W_BRANCH), F32),
        jax.ShapeDtypeStruct((bsz, t, 1024), F32),
        jax.ShapeDtypeStruct((bsz, t, 1280), F32),
        jax.ShapeDtypeStruct((bsz, LANES, t), F32),
        jax.ShapeDtypeStruct((bsz, W_BRANCH, t), F32),
        jax.ShapeDtypeStruct((bsz, W_BRANCH, t), F32),
    ]
    out_specs = [
        pl.BlockSpec((1, tm, 1024), row),
        pl.BlockSpec((1, tm, W_BRANCH), row),
        pl.BlockSpec((1, 1, W_BRANCH, tm), lambda b, i: (b, i, 0, 0)),
        pl.BlockSpec((1, 1, N_HEADS, tm, LANES), lambda b, i: (b, i, 0, 0, 0)),
        pl.BlockSpec((1, tm, W_BRANCH), row),
        pl.BlockSpec((1, tm, 1024), row),
        pl.BlockSpec((1, tm, 1280), row),
        pl.BlockSpec((1, LANES, tm), lambda b, i: (b, 0, i)),
        pl.BlockSpec((1, W_BRANCH, tm), lambda b, i: (b, 0, i)),
        pl.BlockSpec((1, W_BRANCH, tm), lambda b, i: (b, 0, i)),
    ]
    return pl.pallas_call(
        functools.partial(_in_kernel, n_ctx_tiles, ctx_row),
        grid=(bsz, t // tm),
        in_specs=_stream_specs(stream, n_ctx_tiles, 0) + [
                  pl.BlockSpec(mo.shape, const2),
                  pl.BlockSpec((1, d), const2),
                  pl.BlockSpec((tm, W_BRANCH), lambda b, i: (i, 0)),
                  pl.BlockSpec((tm, W_BRANCH), lambda b, i: (i, 0)),
                  pl.BlockSpec((W_BRANCH, tm), lambda b, i: (0, i)),
                  pl.BlockSpec((W_BRANCH, tm), lambda b, i: (0, i)),
                  pl.BlockSpec(w2.shape, const2),
                  pl.BlockSpec(wt.shape, const2),
                  pl.BlockSpec((1, V_EXT), const2)],
        out_specs=out_specs,
        out_shape=out_shape,
        compiler_params=_cparams(("parallel", "parallel")),
        name="in_proj",
    )(c_src, x_src, mo, g.reshape(1, d), cos, sin, cos_t, sin_t, w2, wt, ones_row)


def _bwd_chunk(c, n_ctx_chunks, n_chunks):
    return jnp.where(c < n_ctx_chunks, n_ctx_chunks - 1 - c, n_chunks + n_ctx_chunks - 1 - c)


def _tri_masks(d):
    ri = _iota2((CHUNK, CHUNK), 0)
    ci = _iota2((CHUNK, CHUNK), 1)
    if d == 0:
        return ci <= ri, ri <= ci
    return ci >= ri, ri >= ci


def _split_f32(x, n):
    parts = []
    for _ in range(n):
        p = x.astype(BF16).astype(F32)
        parts.append(p)
        x = x - p
    return parts


def _cummax_lanes(u, d):
    lane = _iota2(u.shape, 1)
    ninf = jnp.float32(-jnp.inf)
    s = 1
    while s < CHUNK:
        if d == 0:
            sh = jnp.where(lane >= s, pltpu.roll(u, s, 1), ninf)
        else:
            sh = jnp.where(lane < CHUNK - s, pltpu.roll(u, CHUNK - s, 1), ninf)
        u = jnp.maximum(u, sh)
        s *= 2
    return u


def _state_row_selector(n_tiles):
    keep = (jnp.arange(CHUNK)[None, :] % 8) == jnp.arange(8)[:, None]
    return jnp.broadcast_to(keep[:, :, None], (8, CHUNK, n_tiles * LANES)).astype(BF16)


def _spread_selector(n_spread):
    t = jnp.arange(CHUNK)[:, None] // 16
    tiles = [jnp.broadcast_to(t == 3 + k, (CHUNK, LANES)) for k in range(n_spread)]
    return jnp.concatenate(tiles, axis=1).astype(BF16)


def _spread_operands(col_sum, row_sum, spread, sel_ref, ones_ref):
    z8 = jnp.zeros((8, CHUNK), F32)
    one8 = ones_ref[0:8, :].astype(F32)
    pieces = _split_f32(col_sum, 3) + [one8] * 3
    for x in spread:
        pieces += _split_f32(x, 2)
    assert len(pieces) <= CHUNK // 8
    lhs = jnp.concatenate(pieces + [z8] * (CHUNK // 8 - len(pieces)), axis=0).T.astype(BF16)
    rows = jnp.concatenate([one8] * 3 + _split_f32(row_sum, 3) + [z8] * (CHUNK // 8 - 6), axis=0)
    return lhs, jnp.concatenate([rows.astype(BF16), sel_ref[...]], axis=1)


def _mlstm_kernel(qf_ref, vf_ref, ktf_ref, gtf_ref, qb_ref, vb_ref, ktb_ref, gtb_ref, bcol_ref, ones_ref, sel_ref,
                  rsel_ref, hf_ref, hb_ref, cn_ref, m_ref):
    @pl.when(pl.program_id(1) == 0)
    def _():
        cn_ref[...] = jnp.zeros_like(cn_ref)
        m_ref[...] = jnp.zeros_like(m_ref)

    lane = _iota2((CHUNK, LANES), 1)
    feat = _iota2((LANES, CHUNK), 0)
    row8 = _iota2((8, LANES), 0)
    ones_t = ones_ref[...]
    ninf = jnp.float32(-jnp.inf)
    m_all = m_ref[...]
    heads = [None] * 8
    gates = []
    f_cum = []
    for d in (0, 1):
        gt_ref = (gtf_ref, gtb_ref)[d]
        tr = jnp.where(_tri_masks(d)[1], 1.0, 0.0).astype(BF16)
        f_cum.append(_mm_exact_r(_log_sigmoid(gt_ref[0, 8:16, :] + bcol_ref[8:16, :]), tr))

    for d in (0, 1):
        q_ref, v_ref, kt_ref = ((qf_ref, vf_ref, ktf_ref), (qb_ref, vb_ref, ktb_ref))[d]
        for p in range(2):
            q_t = q_ref[0, :, 128 * p:128 * p + 128]
            v_t = v_ref[0, :, 128 * p:128 * p + 128]
            kt_p = kt_ref[0, 128 * p:128 * p + 128, :] * (HEAD_W ** -0.5)
            kt_b = kt_p.astype(BF16)
            for half in range(2):
                r = 4 * d + 2 * p + half
                hmask = (lane >= 64) if half else (lane < 64)
                qm = jnp.where(hmask, q_t, 0.0).astype(BF16)
                cn = cn_ref[r]
                heads[r] = dict(
                    cn=cn, kt_p=kt_p,
                    vw=jnp.concatenate([jnp.where(hmask, v_t, 0.0).astype(BF16), ones_t], axis=1),
                    qk=_mm(qm, kt_b),
                    qc=_mm(qm, cn.astype(BF16)))

    for d in (0, 1):
        gt_ref = (gtf_ref, gtb_ref)[d]
        last = CHUNK - 1 if d == 0 else 0
        li = gt_ref[0, 0:8, :] + bcol_ref[0:8, :]
        f = f_cum[d]
        u = li - f
        m_old = m_all
        mx = jnp.maximum(m_old, _cummax_lanes(u, d))
        w_int = jnp.exp(m_old - mx)
        e_mi = jnp.exp(-(f + mx))
        u_max = jnp.max(u, axis=1, keepdims=True)
        e_end = jnp.exp(u - u_max)
        f_last = f[:, last:last + 1]
        b_end = f_last + u_max
        m_new = jnp.maximum(f_last + m_old, b_end)
        a_dec = jnp.exp(f_last + m_old - m_new)
        g_inc = jnp.exp(b_end - m_new)
        m_all = jnp.where((row8 >> 2) == d, m_new, m_all)
        gates.append((e_end, a_dec, g_inc) + _spread_operands(-mx, u, [w_int, e_mi], sel_ref, ones_ref))

    for r in range(8):
        hd = heads[r]
        e_end, a_dec, g_inc, lhs, rhs = gates[r // 4]
        fmask = (feat >= 64) if r % 2 else (feat < 64)
        ke = (jnp.where(fmask, hd["kt_p"], 0.0) * e_end[r:r + 1, :]).astype(BF16)
        a_r = a_dec[r:r + 1, :]
        g_r = g_inc[r:r + 1, :]
        hd["mt"] = _mm(lhs, rhs * rsel_ref[r])
        hd["cn_new"] = (jnp.concatenate([a_r, a_r], axis=1) * hd["cn"]
                        + jnp.concatenate([g_r, g_r], axis=1) * _mm(ke, hd["vw"]))

    for r in range(8):
        hd = heads[r]
        vis, _ = _tri_masks(r // 4)
        s = hd["qk"] * jnp.exp(jnp.where(vis, hd["mt"][:, 0:128], ninf))
        hd["sv"] = _mm(s.astype(BF16), hd["vw"])

    for d in (0, 1):
        out_ref = (hf_ref, hb_ref)[d]
        for p in range(2):
            pair = []
            for half in range(2):
                hd = heads[4 * d + 2 * p + half]
                w_b = hd["mt"][:, 128:256]
                tot = hd["sv"] + jnp.concatenate([w_b, w_b], axis=1) * hd["qc"]
                pair.append(tot[:, 0:128] / jnp.maximum(jnp.abs(tot[:, 128:256]), hd["mt"][:, 256:384]))
            out_ref[0, :, 128 * p:128 * p + 128] = jnp.where(lane < 64, pair[0], pair[1])
    m_ref[...] = m_all
    for r in range(8):
        cn_ref[r] = heads[r]["cn_new"]


def _mlstm(ml, mkt, gt, bcol, n_ctx_chunks):
    bsz, t, _ = ml.shape
    nc = t // CHUNK
    out = jax.ShapeDtypeStruct((bsz, t, W_BRANCH), F32)
    const2 = lambda b, c: (0, 0)

    def chunk_specs(chunk):
        return [pl.BlockSpec((1, CHUNK, W_BRANCH), lambda b, c: (b, chunk(c), 0)),
                pl.BlockSpec((1, CHUNK, W_BRANCH), lambda b, c: (b, chunk(c), 1)),
                pl.BlockSpec((1, W_BRANCH, CHUNK), lambda b, c: (b, 0, chunk(c))),
                pl.BlockSpec((1, LANES, CHUNK), lambda b, c: (b, 0, chunk(c)))]

    fwd_chunk = lambda c: c
    bwd_chunk = lambda c: _bwd_chunk(c, n_ctx_chunks, nc)
    return pl.pallas_call(
        _mlstm_kernel,
        grid=(bsz, nc),
        in_specs=chunk_specs(fwd_chunk) + chunk_specs(bwd_chunk) + [
            pl.BlockSpec((LANES, 1), const2), pl.BlockSpec((CHUNK, LANES), const2),
            pl.BlockSpec((CHUNK, 2 * LANES), const2),
            pl.BlockSpec((8, CHUNK, 3 * LANES), lambda b, c: (0, 0, 0))],
        out_specs=[pl.BlockSpec((1, CHUNK, W_BRANCH), lambda b, c: (b, c, 0)),
                   pl.BlockSpec((1, CHUNK, W_BRANCH), lambda b, c: (b, bwd_chunk(c), 0))],
        out_shape=[out, out],
        scratch_shapes=[pltpu.VMEM((8, LANES, 2 * LANES), F32), pltpu.VMEM((8, LANES), F32)],
        compiler_params=_cparams(("parallel", "arbitrary")),
        name="mlstm",
    )(ml, ml, mkt, gt, ml, ml, mkt, gt, bcol, jnp.ones((CHUNK, LANES), BF16), _spread_selector(2),
      _state_row_selector(3))


def _ssd_kernel(n_ctx_chunks, n_chunks,
                xf_ref, xfp_ref, xfn_ref, xb_ref, xbp_ref, xbn_ref,
                gtf_ref, gtb_ref, bcol_ref, acol_ref, cw_ref, cb_ref, dskip_ref, ones_ref, sel_ref, rsel_ref,
                yf_ref, yb_ref, s_ref):
    c = pl.program_id(1)

    @pl.when(c == 0)
    def _():
        s_ref[...] = jnp.zeros_like(s_ref)

    lane = _iota2((CHUNK, LANES), 1)
    row768 = _iota2((CHUNK, 768), 0)
    ninf = jnp.float32(-jnp.inf)

    dt, a_cum = [], []
    for d in (0, 1):
        gt_ref = (gtf_ref, gtb_ref)[d]
        tr = jnp.where(_tri_masks(d)[1], 1.0, 0.0).astype(BF16)
        dt.append(_softplus(gt_ref[0, 16:24, :] + bcol_ref[16:24, :]))
        a_cum.append(_mm_exact_r(dt[d] * acol_ref[16:24, :], tr))

    xa = []
    for d in (0, 1):
        x_ref, xp_ref, xn_ref = ((xf_ref, xfp_ref, xfn_ref), (xb_ref, xbp_ref, xbn_ref))[d]
        j = c if d == 0 else _bwd_chunk(c, n_ctx_chunks, n_chunks)
        seg_first = jnp.logical_or(j == 0, j == n_ctx_chunks)
        seg_last = jnp.logical_or(j == n_ctx_chunks - 1, j == n_chunks - 1)
        x = x_ref[0]
        prev = jnp.where(seg_first, 0.0, xp_ref[0, 7:8, :])
        nxt = jnp.where(seg_last, 0.0, xn_ref[0, 0:1, :])
        x_dn = jnp.where(row768 == 0, prev, pltpu.roll(x, 1, 0))
        x_up = jnp.where(row768 == CHUNK - 1, nxt, pltpu.roll(x, CHUNK - 1, 0))
        xa.append(_silu(x_dn * cw_ref[0:1, :] + x * cw_ref[1:2, :] + x_up * cw_ref[2:3, :] + cb_ref[...]))

    heads = [None] * 8
    groups = {}
    for d in (0, 1):
        for g in range(2):
            x_pair = xa[d][:, 128 * g:128 * g + 128]
            b_g = xa[d][:, 256 + 128 * g:256 + 128 * g + 128]
            c_gb = xa[d][:, 512 + 128 * g:512 + 128 * g + 128].astype(BF16)
            groups[d, g] = dict(x_pair=x_pair, bt=b_g.T.astype(BF16), cb=_mm_nt(c_gb, b_g.astype(BF16)))
            for half in range(2):
                r = 4 * d + 2 * g + half
                hmask = (lane >= 64) if half else (lane < 64)
                st = s_ref[r]
                xm = jnp.where(hmask, x_pair, 0.0)
                heads[r] = dict(st=st, xm=xm, xmb=xm.astype(BF16), cs=_mm(c_gb, st.astype(BF16)))

    for d in (0, 1):
        last = CHUNK - 1 if d == 0 else 0
        a_last = a_cum[d][:, last:last + 1]
        w_in = jnp.exp(a_last - a_cum[d]) * dt[d]
        e_last = jnp.exp(jnp.broadcast_to(a_last, (8, CHUNK)))
        lhs, rhs = _spread_operands(a_cum[d], -a_cum[d], [jnp.exp(a_cum[d]), w_in], sel_ref, ones_ref)
        for h in range(N_HEADS):
            r = 4 * d + h
            heads[r]["mt"] = _mm(lhs, rhs * rsel_ref[r])
            heads[r]["e_last"] = e_last[r:r + 1, :]

    for r in range(8):
        d, h = divmod(r, 4)
        hd = heads[r]
        ds = _mm(groups[d, h // 2]["bt"], (hd["xm"] * hd["mt"][:, 256:384]).astype(BF16))
        s_ref[r] = hd["e_last"] * hd["st"] + ds
    for r in range(8):
        d, h = divmod(r, 4)
        hd = heads[r]
        vis, _ = _tri_masks(d)
        sc = groups[d, h // 2]["cb"] * jnp.exp(jnp.where(vis, hd["mt"][:, 0:128], ninf)) * dt[d][r:r + 1, :]
        hd["y"] = _mm(sc.astype(BF16), hd["xmb"]) + hd["mt"][:, 128:256] * hd["cs"]

    for d in (0, 1):
        out_ref = (yf_ref, yb_ref)[d]
        for g in range(2):
            y_pair = heads[4 * d + 2 * g]["y"] + heads[4 * d + 2 * g + 1]["y"]
            if d == 0:
                y_pair = y_pair + dskip_ref[:, 128 * g:128 * g + 128] * groups[d, g]["x_pair"]
            out_ref[0, :, 128 * g:128 * g + 128] = y_pair


def _ssd(ss, gt, bcol, acol, conv_w, conv_b, dskip, n_ctx_chunks):
    bsz, t, _ = ss.shape
    nc = t // CHUNK
    sub = CHUNK // 8
    nsub = t // 8
    bc = lambda c: _bwd_chunk(c, n_ctx_chunks, nc)
    fwd = lambda b, c: (b, c, 0)
    bwd = lambda b, c: (b, bc(c), 0)
    fwd_p = lambda b, c: (b, jnp.maximum(c * sub - 1, 0), 0)
    fwd_n = lambda b, c: (b, jnp.minimum((c + 1) * sub, nsub - 1), 0)
    bwd_p = lambda b, c: (b, jnp.maximum(bc(c) * sub - 1, 0), 0)
    bwd_n = lambda b, c: (b, jnp.minimum((bc(c) + 1) * sub, nsub - 1), 0)
    fwd_t = lambda b, c: (b, 0, c)
    bwd_t = lambda b, c: (b, 0, bc(c))
    const2 = lambda b, c: (0, 0)
    out = jax.ShapeDtypeStruct((bsz, t, W_BRANCH), F32)
    return pl.pallas_call(
        functools.partial(_ssd_kernel, n_ctx_chunks, nc),
        grid=(bsz, nc),
        in_specs=[pl.BlockSpec((1, CHUNK, 768), fwd), pl.BlockSpec((1, 8, 768), fwd_p),
                  pl.BlockSpec((1, 8, 768), fwd_n),
                  pl.BlockSpec((1, CHUNK, 768), bwd), pl.BlockSpec((1, 8, 768), bwd_p),
                  pl.BlockSpec((1, 8, 768), bwd_n),
                  pl.BlockSpec((1, LANES, CHUNK), fwd_t), pl.BlockSpec((1, LANES, CHUNK), bwd_t),
                  pl.BlockSpec((LANES, 1), const2), pl.BlockSpec((LANES, 1), const2),
                  pl.BlockSpec((3, 768), const2), pl.BlockSpec((1, 768), const2),
                  pl.BlockSpec((1, W_BRANCH), const2),
                  pl.BlockSpec((CHUNK, LANES), const2), pl.BlockSpec((CHUNK, 2 * LANES), const2),
                  pl.BlockSpec((8, CHUNK, 3 * LANES), lambda b, c: (0, 0, 0))],
        out_specs=[pl.BlockSpec((1, CHUNK, W_BRANCH), fwd), pl.BlockSpec((1, CHUNK, W_BRANCH), bwd)],
        out_shape=[out, out],
        scratch_shapes=[pltpu.VMEM((8, SSM_N, LANES), F32)],
        compiler_params=_cparams(("parallel", "arbitrary")),
        name="ssd",
    )(ss, ss, ss, ss, ss, ss, gt, gt, bcol, acol, conv_w, conv_b, dskip,
      jnp.ones((CHUNK, LANES), BF16), _spread_selector(2), _state_row_selector(3))


def _hgrn2_kernel(hf_ref, vtf_ref, hb_ref, vtb_ref, loglb_ref, log1m_ref, onem_ref, e_ref, of_ref, ob_ref,
                  st_ref):
    @pl.when(pl.program_id(1) == 0)
    def _():
        st_ref[...] = jnp.zeros_like(st_ref)

    ri = _iota2((CHUNK, CHUNK), 0)
    ci = _iota2((CHUNK, CHUNK), 1)
    lane = _iota2((CHUNK, LANES), 1)
    rw = _iota2((CHUNK, W_BRANCH), 0)
    blockdiag = (ri >> 6) == (ci >> 6)
    ninf = jnp.float32(-jnp.inf)
    nb = CHUNK // SUB
    rs = _iota2((nb, SUB, W_BRANCH), 1)

    dirs = []
    for d in (0, 1):
        h_ref = (hf_ref, hb_ref)[d]
        tc = jnp.where(_tri_masks(d)[0], 1.0, 0.0).astype(BF16)
        z = h_ref[0, :, W_BRANCH * (1 + d):W_BRANCH * (2 + d)]
        e = jnp.exp(-jnp.abs(z))
        ope = 1.0 + e
        la = loglb_ref[d:d + 1, :]
        lb_ = log1m_ref[d:d + 1, :] + (jnp.minimum(z, 0.0) - jnp.log(ope))
        logf = jnp.maximum(la, lb_) + jnp.log(1.0 + jnp.exp(-jnp.abs(la - lb_)))
        dirs.append(dict(
            q=_silu(h_ref[0, :, 0:W_BRANCH]),
            v=h_ref[0, :, 3 * W_BRANCH:4 * W_BRANCH],
            kk=onem_ref[d:d + 1, :] * (jnp.where(z >= 0.0, e, 1.0) / ope),
            gcum=_mm_exact_l(tc, logf)))

    for d in (0, 1):
        dd = dirs[d]
        q, kk, gcum = dd["q"], dd["kk"], dd["gcum"]
        vt_ref = (vtf_ref, vtb_ref)[d]
        last = CHUNK - 1 if d == 0 else 0
        g_last = gcum[last:last + 1, :]
        qg = (q * jnp.exp(gcum)).astype(BF16)
        kg = (kk * jnp.exp(g_last - gcum)).astype(BF16)

        dd["o_inter"] = []
        for p in range(2):
            sl = slice(128 * p, 128 * p + 128)
            st = st_ref[2 * d + p]
            dd["o_inter"].append(_mm_nt(qg[:, sl], st.astype(BF16)))
            dst = _mm(vt_ref[0, sl, :].astype(BF16), kg[:, sl])
            st_ref[2 * d + p] = st * jnp.exp(g_last[:, sl]) + jnp.where(blockdiag, dst, 0.0)

        a_mats = [jnp.zeros((CHUNK, CHUNK), F32) for _ in range(N_HEADS)]
        blk = CHUNK // 2
        while blk >= SUB:
            first = (rw & (2 * blk - 1)) < blk
            edge = (blk - 1) if d == 0 else blk
            gb = gcum.reshape(CHUNK // (2 * blk), 2 * blk, W_BRANCH)[:, edge:edge + 1, :]
            gb = jnp.broadcast_to(gb, (CHUNK // (2 * blk), 2 * blk, W_BRANCH)).reshape(CHUNK, W_BRANCH)
            q_side = jnp.logical_not(first) if d == 0 else first
            qt = q * jnp.exp(jnp.where(q_side, gcum - gb, ninf))
            kt = (kk * jnp.exp(jnp.where(q_side, ninf, gb - gcum))).astype(BF16)
            same = (ri >> int(math.log2(2 * blk))) == (ci >> int(math.log2(2 * blk)))
            for h in range(N_HEADS):
                p, half = divmod(h, 2)
                hmask = (lane >= 64) if half else (lane < 64)
                qh = jnp.where(hmask, qt[:, 128 * p:128 * p + 128], 0.0).astype(BF16)
                a_mats[h] = a_mats[h] + jnp.where(same, _mm_nt(qh, kt[:, 128 * p:128 * p + 128]), 0.0)
            blk //= 2
        dd["a_mats"] = a_mats

        g3 = gcum.reshape(nb, SUB, W_BRANCH)
        q3 = q.reshape(nb, SUB, W_BRANCH)
        k3 = kk.reshape(nb, SUB, W_BRANCH)
        v3 = dd["v"].reshape(nb, SUB, W_BRANCH)
        o3 = None
        for j in range(SUB):
            ok = (rs >= j) if d == 0 else (rs <= j)
            pj = q3 * jnp.exp(jnp.where(ok, g3 - g3[:, j:j + 1, :], ninf)) * k3[:, j:j + 1, :]
            red = _mm(pj.reshape(CHUNK, W_BRANCH).astype(BF16), e_ref[...])
            term = red.reshape(nb, SUB, W_BRANCH) * v3[:, j:j + 1, :]
            o3 = term if o3 is None else o3 + term
        dd["o_diag"] = o3.reshape(CHUNK, W_BRANCH)

    for d in (0, 1):
        dd = dirs[d]
        out_ref = (of_ref, ob_ref)[d]
        for p in range(2):
            sl = slice(128 * p, 128 * p + 128)
            a_cat = jnp.concatenate([dd["a_mats"][2 * p], dd["a_mats"][2 * p + 1]], axis=1).astype(BF16)
            v_p = dd["v"][:, sl]
            v_cat = jnp.concatenate([jnp.where(lane < 64, v_p, 0.0), jnp.where(lane >= 64, v_p, 0.0)],
                                    axis=0).astype(BF16)
            out_ref[0, :, sl] = dd["o_inter"][p] + _mm(a_cat, v_cat) + dd["o_diag"][:, sl]


def _hgrn2(hg, hvt, loglb, log1m, onem, e64, n_ctx_chunks):
    bsz, t, _ = hg.shape
    nc = t // CHUNK
    fwd = lambda b, c: (b, c, 0)
    bwd = lambda b, c: (b, _bwd_chunk(c, n_ctx_chunks, nc), 0)
    fwd_t = lambda b, c: (b, 0, c)
    bwd_t = lambda b, c: (b, 0, _bwd_chunk(c, n_ctx_chunks, nc))
    const2 = lambda b, c: (0, 0)
    out = jax.ShapeDtypeStruct((bsz, t, W_BRANCH), F32)
    return pl.pallas_call(
        _hgrn2_kernel,
        grid=(bsz, nc),
        in_specs=[pl.BlockSpec((1, CHUNK, 1024), fwd), pl.BlockSpec((1, W_BRANCH, CHUNK), fwd_t),
                  pl.BlockSpec((1, CHUNK, 1024), bwd), pl.BlockSpec((1, W_BRANCH, CHUNK), bwd_t),
                  pl.BlockSpec((2, W_BRANCH), const2), pl.BlockSpec((2, W_BRANCH), const2),
                  pl.BlockSpec((2, W_BRANCH), const2), pl.BlockSpec((W_BRANCH, W_BRANCH), const2)],
        out_specs=[pl.BlockSpec((1, CHUNK, W_BRANCH), fwd), pl.BlockSpec((1, CHUNK, W_BRANCH), bwd)],
        out_shape=[out, out],
        scratch_shapes=[pltpu.VMEM((4, LANES, LANES), F32)],
        compiler_params=_cparams(("parallel", "arbitrary")),
        name="hgrn2",
    )(hg, hvt, hg, hvt, loglb, log1m, onem, e64)


def _attn(daq, kt, vx, lam_vecs, lam_init, n_ctx_tiles, skip_ctx):
    bsz, t, _ = daq.shape
    tq = ROW_TILE
    n_k_tiles = t // KEY_TILE
    n_lat_tiles = n_k_tiles - n_ctx_tiles
    group = math.gcd(KEY_GROUP, n_lat_tiles)
    q_off = n_ctx_tiles if skip_ctx else 0
    nq = t // tq - q_off
    n_maps = W_BRANCH // DA_QK
    rows = n_maps * tq

    def kern(q_ref, kt_ref, vx_ref, lv_ref, o_ref, q8_ref, m_ref, acc_ref, s_ref, mx_ref):
        qi = pl.program_id(1) + q_off
        lane = _iota2((tq, W_BRANCH), 1)
        q = q_ref[0]
        for j in range(n_maps):
            q8_ref[j * tq:(j + 1) * tq, :] = jnp.where((lane >> 5) == j, q, jnp.zeros_like(q))
        m_ref[...] = jnp.full_like(m_ref, -jnp.inf)
        acc_ref[...] = jnp.zeros_like(acc_ref)

        def update(t0, n, after_scores=None):
            q8 = q8_ref[...]
            s = [_mm(q8, kt_ref[0, t0 + i]) for i in range(n)]
            if after_scores is not None:
                after_scores()
            mx = s[0][:, 0:LANES]
            for i in range(n):
                for c in range(KEY_TILE // LANES):
                    if i or c:
                        mx = jnp.maximum(mx, s[i][:, LANES * c:LANES * (c + 1)])
            m_old = m_ref[...]
            m_new = jnp.maximum(m_old, jnp.max(mx, axis=1, keepdims=True))
            alpha = jnp.exp(m_old - m_new)
            m_ref[...] = m_new
            m2 = jnp.concatenate([m_new] * (KEY_TILE // LANES), axis=1)
            p = [jnp.exp(s[i] - m2).astype(BF16) for i in range(n)]
            for hd in range(N_HEADS):
                r0 = 2 * hd * tq
                pv = _mm(p[0][r0:r0 + 2 * tq], vx_ref[0, t0, hd])
                for i in range(1, n):
                    pv = pv + _mm(p[i][r0:r0 + 2 * tq], vx_ref[0, t0 + i, hd])
                acc_ref[r0:r0 + 2 * tq, :] = alpha[r0:r0 + 2 * tq] * acc_ref[r0:r0 + 2 * tq, :] + pv

        def scores(g, slot):
            q8 = q8_ref[...]
            t0 = n_ctx_tiles + g * group
            mx = None
            for i in range(group):
                s = _mm(q8, kt_ref[0, t0 + i])
                s_ref[slot, :, KEY_TILE * i:KEY_TILE * (i + 1)] = s
                for c in range(KEY_TILE // LANES):
                    part = s[:, LANES * c:LANES * (c + 1)]
                    mx = part if mx is None else jnp.maximum(mx, part)
            mx_ref[slot] = mx

        def absorb(g, slot):
            t0 = n_ctx_tiles + g * group
            m_old = m_ref[...]
            m_new = jnp.maximum(m_old, jnp.max(mx_ref[slot], axis=1, keepdims=True))
            alpha = jnp.exp(m_old - m_new)
            m_ref[...] = m_new
            m2 = jnp.concatenate([m_new] * (KEY_TILE // LANES), axis=1)
            p = [jnp.exp(s_ref[slot, :, KEY_TILE * i:KEY_TILE * (i + 1)] - m2).astype(BF16) for i in range(group)]
            for hd in range(N_HEADS):
                r0 = 2 * hd * tq
                pv = _mm(p[0][r0:r0 + 2 * tq], vx_ref[0, t0, hd])
                for i in range(1, group):
                    pv = pv + _mm(p[i][r0:r0 + 2 * tq], vx_ref[0, t0 + i, hd])
                acc_ref[r0:r0 + 2 * tq, :] = alpha[r0:r0 + 2 * tq] * acc_ref[r0:r0 + 2 * tq, :] + pv

        @pl.when(qi < n_ctx_tiles)
        def _():
            update(0, n_ctx_tiles)

        @pl.when(qi >= n_ctx_tiles)
        def _():
            n_groups = n_lat_tiles // group
            n_pairs = (n_groups - 1) // 2
            update(0, n_ctx_tiles, after_scores=lambda: scores(0, 0))

            def body(k, carry):
                g = 2 * k
                scores(g + 1, 1)
                absorb(g, 0)
                scores(g + 2, 0)
                absorb(g + 1, 1)
                return carry
            lax.fori_loop(0, n_pairs, body, 0)
            g0 = 2 * n_pairs
            if n_groups - g0 == 2:
                scores(g0 + 1, 1)
            absorb(g0, 0)
            if n_groups - g0 == 2:
                absorb(g0 + 1, 1)

        lv = lv_ref[...]
        lam = (jnp.exp(jnp.sum(lv[0:1, :] * lv[1:2, :], axis=1, keepdims=True))
               - jnp.exp(jnp.sum(lv[2:3, :] * lv[3:4, :], axis=1, keepdims=True)) + lam_init)
        lane1 = _iota2((tq, LANES), 1)
        for pr in range(N_HEADS // 2):
            halves = []
            for half in range(2):
                r0 = 2 * (2 * pr + half) * tq
                one = 0 if half else HEAD_W
                a0 = acc_ref[r0:r0 + tq, :]
                a1 = acc_ref[r0 + tq:r0 + 2 * tq, :]
                halves.append(a0 / a0[:, one:one + 1] - lam * (a1 / a1[:, one:one + 1]))
            o_ref[0, :, LANES * pr:LANES * (pr + 1)] = jnp.where(lane1 < HEAD_W, halves[0], halves[1])

    return pl.pallas_call(
        kern,
        grid=(bsz, nq),
        in_specs=[pl.BlockSpec((1, tq, W_BRANCH), lambda b, i: (b, i + q_off, 0)),
                  pl.BlockSpec((1, n_k_tiles, W_BRANCH, KEY_TILE), lambda b, i: (b, 0, 0, 0),
                               pipeline_mode=pl.Buffered(1)),
                  pl.BlockSpec((1, n_k_tiles, N_HEADS, KEY_TILE, LANES), lambda b, i: (b, 0, 0, 0, 0),
                               pipeline_mode=pl.Buffered(1)),
                  pl.BlockSpec(lam_vecs.shape, lambda b, i: (0, 0))],
        out_specs=pl.BlockSpec((1, tq, W_BRANCH), lambda b, i: (b, i, 0)),
        out_shape=jax.ShapeDtypeStruct((bsz, nq * tq, W_BRANCH), F32),
        scratch_shapes=[pltpu.VMEM((rows, W_BRANCH), BF16),
                        pltpu.VMEM((rows, LANES), F32),
                        pltpu.VMEM((rows, LANES), F32),
                        pltpu.VMEM((2, rows, group * KEY_TILE), F32),
                        pltpu.VMEM((2, rows, LANES), F32)],
        compiler_params=_cparams(("parallel", "parallel")),
        name="diff_attn",
    )(daq, kt, vx, lam_vecs)


def _seg_mean(x, e_ref):
    x0 = x.astype(BF16)
    x1 = (x - x0.astype(F32)).astype(BF16)
    return (_mm(x0, e_ref[...]) + _mm(x1, e_ref[...])) * (1.0 / HEAD_W)


def _out_kernel(n_ctx_tiles, ctx_row, q_off, lam_init, final,
                c_ref, x_ref, mo_ref, mlo_ref, mlz_ref, mhf_ref, mhb_ref, dao_ref, daz_ref,
                syf_ref, syb_ref, ssz_ref, hof_ref, hob_ref, hgz_ref,
                mlg_ref, dag_ref, ssg_ref, hgg_ref, e_ref, w_ref, fg_ref, o_ref):
    b = pl.program_id(0)
    i = pl.program_id(1) + q_off
    d = x_ref.shape[2]
    r = jnp.where(i < n_ctx_tiles, ctx_row, b)
    gate = mo_ref[pl.ds(r, 1), :][:, 2 * d:3 * d]

    u = _sigmoid(mlo_ref[0]) * (mhf_ref[0] + mhb_ref[0])
    dev = u - _seg_mean(u, e_ref)
    y_ml = dev * lax.rsqrt(_seg_mean(dev * dev, e_ref) + EPS) * mlg_ref[...] * _silu(mlz_ref[0])

    o = dao_ref[0]
    y_da = (o * lax.rsqrt(_seg_mean(o * o, e_ref) + EPS) * dag_ref[...]) * (1.0 - lam_init) * _silu(daz_ref[0])

    ys = (syf_ref[0] + syb_ref[0]) * _silu(ssz_ref[0])
    parts = []
    for g in range(2):
        yg = ys[:, 128 * g:128 * g + 128]
        parts.append(yg * lax.rsqrt(jnp.mean(yg * yg, axis=-1, keepdims=True) + EPS))
    y_ss = jnp.concatenate(parts, axis=1) * ssg_ref[...]

    oh = hof_ref[0] + hob_ref[0]
    y_hg = (oh * lax.rsqrt(_seg_mean(oh * oh, e_ref) + EPS) * hgg_ref[...]) * _silu(hgz_ref[0])

    acc = _mm(y_ml.astype(BF16), w_ref[0:W_BRANCH, :])
    acc = acc + _mm(y_da.astype(BF16), w_ref[W_BRANCH:2 * W_BRANCH, :])
    acc = acc + _mm(y_ss.astype(BF16), w_ref[2 * W_BRANCH:3 * W_BRANCH, :])
    acc = acc + _mm(y_hg.astype(BF16), w_ref[3 * W_BRANCH:4 * W_BRANCH, :])
    x_new = jnp.where(i < n_ctx_tiles, c_ref[0], x_ref[0]) + gate * acc
    if final:
        x_new = x_new * lax.rsqrt(jnp.mean(x_new * x_new, axis=-1, keepdims=True) + EPS) * fg_ref[...]
    o_ref[0] = x_new


def _out_proj(stream, mo, ml, mhf, mhb, dao, daz, ss, syf, syb, hg, hof, hob,
              mlg, dag, ssg, hgg, e64, w_out, final_g, lam_init, n_ctx_tiles, ctx_row, final):
    c_src, x_src, x_off, t = stream
    bsz, _, d = x_src.shape
    tm = ROW_TILE
    q_off = n_ctx_tiles if final else 0
    nrow = t // tm - q_off
    row = lambda b, i: (b, i + q_off, 0)
    col = lambda k: (lambda b, i: (b, i + q_off, k))
    const2 = lambda b, i: (0, 0)
    wb = pl.BlockSpec((1, tm, W_BRANCH), row)
    return pl.pallas_call(
        functools.partial(_out_kernel, n_ctx_tiles, ctx_row, q_off, lam_init, final),
        grid=(bsz, nrow),
        in_specs=_stream_specs(stream, n_ctx_tiles, q_off) + [
                  pl.BlockSpec(mo.shape, const2),
                  pl.BlockSpec((1, tm, W_BRANCH), col(2)), pl.BlockSpec((1, tm, W_BRANCH), col(3)),
                  wb, wb,
                  pl.BlockSpec((1, tm, W_BRANCH), lambda b, i: (b, i, 0)) if final else wb,
                  wb,
                  wb, wb, pl.BlockSpec((1, tm, W_BRANCH), col(3)),
                  wb, wb, pl.BlockSpec((1, tm, W_BRANCH), col(4)),
                  pl.BlockSpec((1, W_BRANCH), const2), pl.BlockSpec((1, W_BRANCH), const2),
                  pl.BlockSpec((1, W_BRANCH), const2), pl.BlockSpec((1, W_BRANCH), const2),
                  pl.BlockSpec((W_BRANCH, W_BRANCH), const2),
                  pl.BlockSpec(w_out.shape, const2),
                  pl.BlockSpec((1, d), const2)],
        out_specs=pl.BlockSpec((1, tm, d), lambda b, i: (b, i, 0)),
        out_shape=jax.ShapeDtypeStruct((bsz, nrow * tm, d), F32),
        compiler_params=_cparams(("parallel", "parallel")),
        name="out_proj",
    )(c_src, x_src, mo, ml, ml, mhf, mhb, dao, daz, syf, syb, ss, hof, hob, hg,
      mlg, dag, ssg, hgg, e64, w_out, final_g.reshape(1, d))


def _relayout_w_in(w_in):
    d = w_in.shape[0]
    o = {}
    off = 0
    for name, n in (("ml_q", 256), ("ml_k", 256), ("ml_v", 256), ("ml_o", 256), ("ml_i", 8), ("ml_f", 8),
                    ("ml_z", 256), ("da_q", 256), ("da_k", 256), ("da_v", 256), ("da_z", 256),
                    ("ss_xbc", 768), ("ss_dt", 8), ("ss_z", 256),
                    ("hg_q", 256), ("hg_f", 512), ("hg_i", 256), ("hg_z", 256)):
        o[name] = w_in[:, off:off + n]
        off += n

    gates = jnp.concatenate([o["ml_i"], o["ml_f"], o["ss_dt"], jnp.zeros((d, LANES - 24), w_in.dtype)], axis=1)
    zpad = jnp.zeros((d, HEAD_W), w_in.dtype)
    v_ext = []
    for hd in range(N_HEADS):
        v_h = o["da_v"][:, HEAD_W * hd:HEAD_W * (hd + 1)]
        v_ext += [zpad, v_h] if hd % 2 else [v_h, zpad]
    w2 = jnp.concatenate([o["ml_q"], o["ml_v"], o["ml_o"], o["ml_z"],
                          o["da_q"]] + v_ext + [o["da_z"],
                          o["ss_xbc"], o["ss_z"],
                          o["hg_q"], o["hg_f"], o["hg_i"], o["hg_z"]], axis=1)
    wt = jnp.concatenate([o["da_k"], gates, o["ml_k"], o["hg_i"]], axis=1).T
    return w2.astype(BF16), wt.astype(BF16)


def _rope_tables(n_ctx, seq):
    pos = jnp.arange(seq)
    rows = (pos // GRID_W).astype(F32)
    cols = (pos % GRID_W).astype(F32)
    axis = DA_QK // 2
    inv = ROPE_BASE ** (-jnp.arange(0, axis, 2, dtype=F32) / axis)
    ang = jnp.concatenate([rows[:, None] * inv, cols[:, None] * inv], axis=-1)
    cos = jnp.concatenate([jnp.ones((n_ctx, axis), F32), jnp.cos(ang)], axis=0)
    sin = jnp.concatenate([jnp.zeros((n_ctx, axis), F32), jnp.sin(ang)], axis=0)
    reps = W_BRANCH // axis
    return jnp.tile(cos, (1, reps)), jnp.tile(sin, (1, reps))


def _gate_col(vals):
    v = jnp.concatenate([vals.astype(F32), jnp.zeros((LANES - vals.shape[0],), F32)])
    return v.reshape(LANES, 1)


def kernel(x, c, ctx, c_ctx, w_mod, b_mod, norm_g, w_in, w_out, ml_gate_b, ml_norm_g, da_lambda, da_norm_g,
           ss_conv_w, ss_conv_b, ss_dt_bias, ss_a_log, ss_d, ss_norm_g, hg_lower, hg_norm_g, final_g):
    bsz, seq, d = x.shape
    n_ctx = ctx.shape[1]
    depth = w_mod.shape[0]
    assert n_ctx % ROW_TILE == 0 and seq % ROW_TILE == 0 and bsz < 8
    n_ctx_tiles = n_ctx // ROW_TILE
    n_ctx_chunks = n_ctx // CHUNK
    ctx_row = bsz

    stream = (ctx, x, 0, n_ctx + seq)
    cc = jnp.concatenate([c, c_ctx[None, :], jnp.zeros((8 - bsz - 1, d), F32)], axis=0)
    cos, sin = _rope_tables(n_ctx, seq)
    cos_t, sin_t = cos.T, sin.T
    one_lanes = jnp.arange(V_EXT) % LANES == jnp.where((jnp.arange(V_EXT) // LANES) % 2 == 1, 0, HEAD_W)
    ones_row = one_lanes.astype(F32).reshape(1, V_EXT)
    lb_all = jnp.cumsum(jax.nn.softmax(hg_lower.astype(F32), axis=1), axis=1)
    lb_all = lb_all - lb_all[:, :1]
    hid = _iota2((W_BRANCH, W_BRANCH), 0) // HEAD_W
    e64 = (hid == hid.T).astype(BF16)

    out = None
    for l in range(depth):
        lam_init = 0.8 - 0.6 * math.exp(-0.3 * l)
        final = l == depth - 1
        w2, wt = _relayout_w_in(w_in[l])
        mo = _modulation(cc, w_mod[l], b_mod[l])
        ml, daq, dak, dav, daz, ss, hg, gt, mkt, hvt = _in_proj(stream, mo, norm_g[l], cos, sin, cos_t, sin_t, w2, wt,
                                                          ones_row, n_ctx_tiles, ctx_row)

        gb = ml_gate_b[l]
        zeros8 = jnp.zeros((8,), F32)
        bcol = _gate_col(jnp.concatenate([gb[:, 0].reshape(-1), gb[:, 1].reshape(-1),
                                                ss_dt_bias[l].reshape(-1)]))
        acol = _gate_col(jnp.concatenate([zeros8, zeros8, -jnp.exp(ss_a_log[l].astype(F32)).reshape(-1)]))
        mhf, mhb = _mlstm(ml, mkt, gt, bcol, n_ctx_chunks)
        dao = _attn(daq, dak, dav, da_lambda[l].astype(F32), lam_init, n_ctx_tiles, final)
        dskip = jnp.repeat(ss_d[l].astype(F32), HEAD_W).reshape(1, W_BRANCH)
        syf, syb = _ssd(ss, gt, bcol, acol, ss_conv_w[l], ss_conv_b[l].reshape(1, -1),
                        dskip, n_ctx_chunks)
        lbh = lb_all[:, l]
        hof, hob = _hgrn2(hg, hvt, jnp.log(lbh), jnp.log1p(-lbh), 1.0 - lbh, e64, n_ctx_chunks)

        res = _out_proj(stream, mo, ml, mhf, mhb, dao, daz, ss, syf, syb, hg, hof, hob,
                        ml_norm_g[l].reshape(1, -1), jnp.tile(da_norm_g[l], N_HEADS).reshape(1, -1),
                        ss_norm_g[l].reshape(1, -1), hg_norm_g[l].reshape(1, -1), e64,
                        w_out[l].astype(BF16), final_g, lam_init, n_ctx_tiles, ctx_row, final)
        if final:
            out = res
        else:
            stream = (res, res, n_ctx_tiles, n_ctx + seq)
    return out
```

```python
import functools
import math

import jax
import jax.numpy as jnp
from jax import lax
from jax.experimental import pallas as pl
from jax.experimental.pallas import tpu as pltpu

F32 = jnp.float32
BF16 = jnp.bfloat16

EPS = 1e-6
GRID_W = 64
ROPE_BASE = 10000.0
N_HEADS = 4
HEAD_W = 64
W_BRANCH = N_HEADS * HEAD_W
DA_QK = 32
SSM_N = 128
CHUNK = 128
ROW_TILE = 256
SUB = 8
LANES = 128
VMEM_LIMIT = 56 * 1024 * 1024

C_ML = 0
C_DAQ = 1024
C_DAV = 1280
C_DAZ = 1792
C_SS = 2048
C_HG = 3072
C_END = 4352
V_EXT = N_HEADS * LANES
KEY_TILE = 256
KEY_GROUP = 4

NT = (((1,), (1,)), ((), ()))
TN = (((0,), (0,)), ((), ()))


def _mm(a, b):
    return jnp.dot(a, b, preferred_element_type=F32)


def _mm_nt(a, b):
    return lax.dot_general(a, b, NT, preferred_element_type=F32)


def _mm_tn(a, b):
    return lax.dot_general(a, b, TN, preferred_element_type=F32)


def _split3(x):
    x0 = x.astype(BF16)
    r = x - x0.astype(F32)
    x1 = r.astype(BF16)
    x2 = (r - x1.astype(F32)).astype(BF16)
    return x0, x1, x2


def _mm_exact_l(t, x):
    x0, x1, x2 = _split3(x)
    return _mm(t, x0) + _mm(t, x1) + _mm(t, x2)


def _mm_exact_r(x, t):
    x0, x1, x2 = _split3(x)
    return _mm(x0, t) + _mm(x1, t) + _mm(x2, t)


def _sigmoid(x):
    return 1.0 / (1.0 + jnp.exp(-x))


def _silu(x):
    return x * _sigmoid(x)


def _log_sigmoid(x):
    return jnp.minimum(x, 0.0) - jnp.log(1.0 + jnp.exp(-jnp.abs(x)))


def _softplus(x):
    return jnp.maximum(x, 0.0) + jnp.log(1.0 + jnp.exp(-jnp.abs(x)))


def _iota2(shape, axis):
    return lax.broadcasted_iota(jnp.int32, shape, axis)


def _cparams(sem):
    return pltpu.CompilerParams(dimension_semantics=sem, vmem_limit_bytes=VMEM_LIMIT)


def _mod_kernel(c_ref, w_ref, b_ref, o_ref):
    c = c_ref[...]
    o_ref[...] = jnp.dot(_silu(c), w_ref[...], precision=lax.Precision.HIGHEST,
                         preferred_element_type=F32) + b_ref[...]


def _modulation(cc, w_mod, b_mod):
    d = cc.shape[1]
    n = w_mod.shape[1]
    tn = 512
    return pl.pallas_call(
        _mod_kernel,
        grid=(n // tn,),
        in_specs=[pl.BlockSpec((8, d), lambda j: (0, 0)),
                  pl.BlockSpec((d, tn), lambda j: (0, j)),
                  pl.BlockSpec((1, tn), lambda j: (0, j))],
        out_specs=pl.BlockSpec((8, tn), lambda j: (0, j)),
        out_shape=jax.ShapeDtypeStruct((8, n), F32),
        compiler_params=_cparams(("arbitrary",)),
        name="modulation",
    )(cc, w_mod, b_mod.reshape(1, n))


def _stream_specs(stream, n_ctx_tiles, first_tile):
    _, x_src, x_off, _ = stream
    d = x_src.shape[2]
    return [pl.BlockSpec((1, ROW_TILE, d), lambda b, i: (b, jnp.minimum(i + first_tile, n_ctx_tiles - 1), 0)),
            pl.BlockSpec((1, ROW_TILE, d), lambda b, i: (b, jnp.maximum(i + first_tile - n_ctx_tiles, 0) + x_off, 0))]


def _in_kernel(n_ctx_tiles, ctx_row, c_ref, x_ref, mo_ref, g_ref, cos_ref, sin_ref, cost_ref, sint_ref,
               w_ref, wt_ref, ones_ref,
               ml_ref, daq_ref, kt_ref, vx_ref, daz_ref, ss_ref, hg_ref, gt_ref, mkt_ref, hvt_ref):
    b = pl.program_id(0)
    i = pl.program_id(1)
    d = x_ref.shape[2]
    r = jnp.where(i < n_ctx_tiles, ctx_row, b)
    mo = mo_ref[pl.ds(r, 1), :]
    shift = mo[:, 0:d]
    scale = mo[:, d:2 * d]
    x = jnp.where(i < n_ctx_tiles, c_ref[0], x_ref[0])
    y = x * lax.rsqrt(jnp.mean(x * x, axis=-1, keepdims=True) + EPS) * g_ref[...]
    h = (y * (1.0 + scale) + shift).astype(BF16)

    ml_ref[0] = _mm(h, w_ref[:, C_ML:C_DAQ])
    q = _mm(h, w_ref[:, C_DAQ:C_DAV])
    half = DA_QK // 2
    lane = _iota2((x.shape[0], LANES), 1)
    q_rot = []
    for c in range(W_BRANCH // LANES):
        qc = q[:, LANES * c:LANES * (c + 1)]
        q_rot.append(jnp.where((lane & half) == 0, -pltpu.roll(qc, LANES - half, 1), pltpu.roll(qc, half, 1)))
    q = q * cos_ref[...] + jnp.concatenate(q_rot, axis=1) * sin_ref[...]
    daq_ref[0] = (q * (DA_QK ** -0.5)).astype(BF16)
    vx = (_mm(h, w_ref[:, C_DAV:C_DAZ]) + ones_ref[...]).astype(BF16)
    for hd in range(N_HEADS):
        vx_ref[0, 0, hd] = vx[:, LANES * hd:LANES * (hd + 1)]
    daz_ref[0] = _mm(h, w_ref[:, C_DAZ:C_SS])
    ss_ref[0] = _mm(h, w_ref[:, C_SS:C_HG])
    hg_ref[0] = _mm(h, w_ref[:, C_HG:C_END])
    tr = _mm_nt(wt_ref[...], h)
    kt = tr[0:W_BRANCH]
    kt_rot = []
    for j in range(W_BRANCH // DA_QK):
        kt_rot += [-kt[DA_QK * j + half:DA_QK * (j + 1)], kt[DA_QK * j:DA_QK * j + half]]
    kt_ref[0, 0] = (kt * cost_ref[...] + jnp.concatenate(kt_rot, axis=0) * sint_ref[...]).astype(BF16)
    gt_ref[0] = tr[W_BRANCH:W_BRANCH + LANES]
    mkt_ref[0] = tr[W_BRANCH + LANES:2 * W_BRANCH + LANES]
    hvt_ref[0] = tr[2 * W_BRANCH + LANES:3 * W_BRANCH + LANES]


def _in_proj(stream, mo, g, cos, sin, cos_t, sin_t, w2, wt, ones_row, n_ctx_tiles, ctx_row):
    c_src, x_src, x_off, t = stream
    bsz, _, d = x_src.shape
    tm = ROW_TILE
    assert tm == KEY_TILE
    row = lambda b, i: (b, i, 0)
    const2 = lambda b, i: (0, 0)
    out_shape = [
        jax.ShapeDtypeStruct((bsz, t, 1024), F32),
        jax.ShapeDtypeStruct((bsz, t, W_BRANCH), BF16),
        jax.ShapeDtypeStruct((bsz, t // tm, W_BRANCH, tm), BF16),
        jax.ShapeDtypeStruct((bsz, t // tm, N_HEADS, tm, LANES), BF16),
        jax.ShapeDtypeStruct((bsz, t, W_BRANCH), F32),
        jax.ShapeDtypeStruct((bsz, t, 1024), F32),
        jax.ShapeDtypeStruct((bsz, t, 1280), F32),
        jax.ShapeDtypeStruct((bsz, LANES, t), F32),
        jax.ShapeDtypeStruct((bsz, W_BRANCH, t), F32),
        jax.ShapeDtypeStruct((bsz, W_BRANCH, t), F32),
    ]
    out_specs = [
        pl.BlockSpec((1, tm, 1024), row),
        pl.BlockSpec((1, tm, W_BRANCH), row),
        pl.BlockSpec((1, 1, W_BRANCH, tm), lambda b, i: (b, i, 0, 0)),
        pl.BlockSpec((1, 1, N_HEADS, tm, LANES), lambda b, i: (b, i, 0, 0, 0)),
        pl.BlockSpec((1, tm, W_BRANCH), row),
        pl.BlockSpec((1, tm, 1024), row),
        pl.BlockSpec((1, tm, 1280), row),
        pl.BlockSpec((1, LANES, tm), lambda b, i: (b, 0, i)),
        pl.BlockSpec((1, W_BRANCH, tm), lambda b, i: (b, 0, i)),
        pl.BlockSpec((1, W_BRANCH, tm), lambda b, i: (b, 0, i)),
    ]
    return pl.pallas_call(
        functools.partial(_in_kernel, n_ctx_tiles, ctx_row),
        grid=(bsz, t // tm),
        in_specs=_stream_specs(stream, n_ctx_tiles, 0) + [
                  pl.BlockSpec(mo.shape, const2),
                  pl.BlockSpec((1, d), const2),
                  pl.BlockSpec((tm, W_BRANCH), lambda b, i: (i, 0)),
                  pl.BlockSpec((tm, W_BRANCH), lambda b, i: (i, 0)),
                  pl.BlockSpec((W_BRANCH, tm), lambda b, i: (0, i)),
                  pl.BlockSpec((W_BRANCH, tm), lambda b, i: (0, i)),
                  pl.BlockSpec(w2.shape, const2),
                  pl.BlockSpec(wt.shape, const2),
                  pl.BlockSpec((1, V_EXT), const2)],
        out_specs=out_specs,
        out_shape=out_shape,
        compiler_params=_cparams(("parallel", "parallel")),
        name="in_proj",
    )(c_src, x_src, mo, g.reshape(1, d), cos, sin, cos_t, sin_t, w2, wt, ones_row)


def _bwd_chunk(c, n_ctx_chunks, n_chunks):
    return jnp.where(c < n_ctx_chunks, n_ctx_chunks - 1 - c, n_chunks + n_ctx_chunks - 1 - c)


def _tri_masks(d):
    ri = _iota2((CHUNK, CHUNK), 0)
    ci = _iota2((CHUNK, CHUNK), 1)
    if d == 0:
        return ci <= ri, ri <= ci
    return ci >= ri, ri >= ci


def _split_f32(x, n):
    parts = []
    for _ in range(n):
        p = x.astype(BF16).astype(F32)
        parts.append(p)
        x = x - p
    return parts


def _cummax_lanes(u, d):
    lane = _iota2(u.shape, 1)
    ninf = jnp.float32(-jnp.inf)
    s = 1
    while s < CHUNK:
        if d == 0:
            sh = jnp.where(lane >= s, pltpu.roll(u, s, 1), ninf)
        else:
            sh = jnp.where(lane < CHUNK - s, pltpu.roll(u, CHUNK - s, 1), ninf)
        u = jnp.maximum(u, sh)
        s *= 2
    return u


def _state_row_selector(n_tiles):
    keep = (jnp.arange(CHUNK)[None, :] % 8) == jnp.arange(8)[:, None]
    return jnp.broadcast_to(keep[:, :, None], (8, CHUNK, n_tiles * LANES)).astype(BF16)


def _spread_selector(n_spread):
    t = jnp.arange(CHUNK)[:, None] // 16
    tiles = [jnp.broadcast_to(t == 3 + k, (CHUNK, LANES)) for k in range(n_spread)]
    return jnp.concatenate(tiles, axis=1).astype(BF16)


def _spread_operands(col_sum, row_sum, spread, sel_ref, ones_ref):
    z8 = jnp.zeros((8, CHUNK), F32)
    one8 = ones_ref[0:8, :].astype(F32)
    pieces = _split_f32(col_sum, 3) + [one8] * 3
    for x in spread:
        pieces += _split_f32(x, 2)
    assert len(pieces) <= CHUNK // 8
    lhs = jnp.concatenate(pieces + [z8] * (CHUNK // 8 - len(pieces)), axis=0).T.astype(BF16)
    rows = jnp.concatenate([one8] * 3 + _split_f32(row_sum, 3) + [z8] * (CHUNK // 8 - 6), axis=0)
    return lhs, jnp.concatenate([rows.astype(BF16), sel_ref[...]], axis=1)


def _mlstm_steps(qf_ref, vf_ref, ktf_ref, gtf_ref, qb_ref, vb_ref, ktb_ref, gtb_ref, bcol_ref, ones_ref, sel_ref,
                  rsel_ref, hf_ref, hb_ref, cn_ref, m_ref):
    @pl.when(pl.program_id(1) == 0)
    def _():
        cn_ref[...] = jnp.zeros_like(cn_ref)
        m_ref[...] = jnp.zeros_like(m_ref)

    lane = _iota2((CHUNK, LANES), 1)
    feat = _iota2((LANES, CHUNK), 0)
    row8 = _iota2((8, LANES), 0)
    ones_t = ones_ref[...]
    ninf = jnp.float32(-jnp.inf)
    m_all = m_ref[...]
    heads = [None] * 8
    gates = []
    f_cum = []
    for d in (0, 1):
        gt_ref = (gtf_ref, gtb_ref)[d]
        tr = jnp.where(_tri_masks(d)[1], 1.0, 0.0).astype(BF16)
        f_cum.append(_mm_exact_r(_log_sigmoid(gt_ref[0, 8:16, :] + bcol_ref[8:16, :]), tr))

    yield
    for d in (0, 1):
        q_ref, v_ref, kt_ref = ((qf_ref, vf_ref, ktf_ref), (qb_ref, vb_ref, ktb_ref))[d]
        for p in range(2):
            q_t = q_ref[0, :, 128 * p:128 * p + 128]
            v_t = v_ref[0, :, 128 * p:128 * p + 128]
            kt_p = kt_ref[0, 128 * p:128 * p + 128, :] * (HEAD_W ** -0.5)
            kt_b = kt_p.astype(BF16)
            for half in range(2):
                r = 4 * d + 2 * p + half
                hmask = (lane >= 64) if half else (lane < 64)
                qm = jnp.where(hmask, q_t, 0.0).astype(BF16)
                cn = cn_ref[r]
                heads[r] = dict(
                    cn=cn, kt_p=kt_p,
                    vw=jnp.concatenate([jnp.where(hmask, v_t, 0.0).astype(BF16), ones_t], axis=1),
                    qk=_mm(qm, kt_b),
                    qc=_mm(qm, cn.astype(BF16)))

    yield
    for d in (0, 1):
        gt_ref = (gtf_ref, gtb_ref)[d]
        last = CHUNK - 1 if d == 0 else 0
        li = gt_ref[0, 0:8, :] + bcol_ref[0:8, :]
        f = f_cum[d]
        u = li - f
        m_old = m_all
        mx = jnp.maximum(m_old, _cummax_lanes(u, d))
        w_int = jnp.exp(m_old - mx)
        e_mi = jnp.exp(-(f + mx))
        u_max = jnp.max(u, axis=1, keepdims=True)
        e_end = jnp.exp(u - u_max)
        f_last = f[:, last:last + 1]
        b_end = f_last + u_max
        m_new = jnp.maximum(f_last + m_old, b_end)
        a_dec = jnp.exp(f_last + m_old - m_new)
        g_inc = jnp.exp(b_end - m_new)
        m_all = jnp.where((row8 >> 2) == d, m_new, m_all)
        gates.append((e_end, a_dec, g_inc) + _spread_operands(-mx, u, [w_int, e_mi], sel_ref, ones_ref))

    for r in range(8):
        hd = heads[r]
        e_end, a_dec, g_inc, lhs, rhs = gates[r // 4]
        fmask = (feat >= 64) if r % 2 else (feat < 64)
        ke = (jnp.where(fmask, hd["kt_p"], 0.0) * e_end[r:r + 1, :]).astype(BF16)
        a_r = a_dec[r:r + 1, :]
        g_r = g_inc[r:r + 1, :]
        hd["mt"] = _mm(lhs, rhs * rsel_ref[r])
        hd["cn_new"] = (jnp.concatenate([a_r, a_r], axis=1) * hd["cn"]
                        + jnp.concatenate([g_r, g_r], axis=1) * _mm(ke, hd["vw"]))

    yield
    for r in range(8):
        hd = heads[r]
        vis, _ = _tri_masks(r // 4)
        s = hd["qk"] * jnp.exp(jnp.where(vis, hd["mt"][:, 0:128], ninf))
        hd["sv"] = _mm(s.astype(BF16), hd["vw"])

    yield
    for d in (0, 1):
        out_ref = (hf_ref, hb_ref)[d]
        for p in range(2):
            pair = []
            for half in range(2):
                hd = heads[4 * d + 2 * p + half]
                w_b = hd["mt"][:, 128:256]
                tot = hd["sv"] + jnp.concatenate([w_b, w_b], axis=1) * hd["qc"]
                pair.append(tot[:, 0:128] / jnp.maximum(jnp.abs(tot[:, 128:256]), hd["mt"][:, 256:384]))
            out_ref[0, :, 128 * p:128 * p + 128] = jnp.where(lane < 64, pair[0], pair[1])
    m_ref[...] = m_all
    for r in range(8):
        cn_ref[r] = heads[r]["cn_new"]


def _mlstm(ml, mkt, gt, bcol, n_ctx_chunks):
    bsz, t, _ = ml.shape
    nc = t // CHUNK
    out = jax.ShapeDtypeStruct((bsz, t, W_BRANCH), F32)
    const2 = lambda b, c: (0, 0)

    def chunk_specs(chunk):
        return [pl.BlockSpec((1, CHUNK, W_BRANCH), lambda b, c: (b, chunk(c), 0)),
                pl.BlockSpec((1, CHUNK, W_BRANCH), lambda b, c: (b, chunk(c), 1)),
                pl.BlockSpec((1, W_BRANCH, CHUNK), lambda b, c: (b, 0, chunk(c))),
                pl.BlockSpec((1, LANES, CHUNK), lambda b, c: (b, 0, chunk(c)))]

    fwd_chunk = lambda c: c
    bwd_chunk = lambda c: _bwd_chunk(c, n_ctx_chunks, nc)
    return dict(
        steps=_mlstm_steps,
        in_specs=chunk_specs(fwd_chunk) + chunk_specs(bwd_chunk) + [
            pl.BlockSpec((LANES, 1), const2), pl.BlockSpec((CHUNK, LANES), const2),
            pl.BlockSpec((CHUNK, 2 * LANES), const2),
            pl.BlockSpec((8, CHUNK, 3 * LANES), lambda b, c: (0, 0, 0))],
        out_specs=[pl.BlockSpec((1, CHUNK, W_BRANCH), lambda b, c: (b, c, 0)),
                   pl.BlockSpec((1, CHUNK, W_BRANCH), lambda b, c: (b, bwd_chunk(c), 0))],
        out_shape=[out, out],
        scratch_shapes=[pltpu.VMEM((8, LANES, 2 * LANES), F32), pltpu.VMEM((8, LANES), F32)],
        args=(ml, ml, mkt, gt, ml, ml, mkt, gt, bcol, jnp.ones((CHUNK, LANES), BF16), _spread_selector(2),
              _state_row_selector(3)))


def _ssd_steps(n_ctx_chunks, n_chunks,
                xf_ref, xfp_ref, xfn_ref, xb_ref, xbp_ref, xbn_ref,
                gtf_ref, gtb_ref, bcol_ref, acol_ref, cw_ref, cb_ref, dskip_ref, ones_ref, sel_ref, rsel_ref,
                yf_ref, yb_ref, s_ref):
    c = pl.program_id(1)

    @pl.when(c == 0)
    def _():
        s_ref[...] = jnp.zeros_like(s_ref)

    lane = _iota2((CHUNK, LANES), 1)
    row768 = _iota2((CHUNK, 768), 0)
    ninf = jnp.float32(-jnp.inf)

    dt, a_cum = [], []
    for d in (0, 1):
        gt_ref = (gtf_ref, gtb_ref)[d]
        tr = jnp.where(_tri_masks(d)[1], 1.0, 0.0).astype(BF16)
        dt.append(_softplus(gt_ref[0, 16:24, :] + bcol_ref[16:24, :]))
        a_cum.append(_mm_exact_r(dt[d] * acol_ref[16:24, :], tr))

    yield
    xa = []
    for d in (0, 1):
        x_ref, xp_ref, xn_ref = ((xf_ref, xfp_ref, xfn_ref), (xb_ref, xbp_ref, xbn_ref))[d]
        j = c if d == 0 else _bwd_chunk(c, n_ctx_chunks, n_chunks)
        seg_first = jnp.logical_or(j == 0, j == n_ctx_chunks)
        seg_last = jnp.logical_or(j == n_ctx_chunks - 1, j == n_chunks - 1)
        x = x_ref[0]
        prev = jnp.where(seg_first, 0.0, xp_ref[0, 7:8, :])
        nxt = jnp.where(seg_last, 0.0, xn_ref[0, 0:1, :])
        x_dn = jnp.where(row768 == 0, prev, pltpu.roll(x, 1, 0))
        x_up = jnp.where(row768 == CHUNK - 1, nxt, pltpu.roll(x, CHUNK - 1, 0))
        xa.append(_silu(x_dn * cw_ref[0:1, :] + x * cw_ref[1:2, :] + x_up * cw_ref[2:3, :] + cb_ref[...]))

    heads = [None] * 8
    groups = {}
    for d in (0, 1):
        for g in range(2):
            x_pair = xa[d][:, 128 * g:128 * g + 128]
            b_g = xa[d][:, 256 + 128 * g:256 + 128 * g + 128]
            c_gb = xa[d][:, 512 + 128 * g:512 + 128 * g + 128].astype(BF16)
            groups[d, g] = dict(x_pair=x_pair, bt=b_g.T.astype(BF16), cb=_mm_nt(c_gb, b_g.astype(BF16)))
            for half in range(2):
                r = 4 * d + 2 * g + half
                hmask = (lane >= 64) if half else (lane < 64)
                st = s_ref[r]
                xm = jnp.where(hmask, x_pair, 0.0)
                heads[r] = dict(st=st, xm=xm, xmb=xm.astype(BF16), cs=_mm(c_gb, st.astype(BF16)))

    yield
    for d in (0, 1):
        last = CHUNK - 1 if d == 0 else 0
        a_last = a_cum[d][:, last:last + 1]
        w_in = jnp.exp(a_last - a_cum[d]) * dt[d]
        e_last = jnp.exp(jnp.broadcast_to(a_last, (8, CHUNK)))
        lhs, rhs = _spread_operands(a_cum[d], -a_cum[d], [jnp.exp(a_cum[d]), w_in], sel_ref, ones_ref)
        for h in range(N_HEADS):
            r = 4 * d + h
            heads[r]["mt"] = _mm(lhs, rhs * rsel_ref[r])
            heads[r]["e_last"] = e_last[r:r + 1, :]

    yield
    for r in range(8):
        d, h = divmod(r, 4)
        hd = heads[r]
        ds = _mm(groups[d, h // 2]["bt"], (hd["xm"] * hd["mt"][:, 256:384]).astype(BF16))
        s_ref[r] = hd["e_last"] * hd["st"] + ds
    for r in range(8):
        d, h = divmod(r, 4)
        hd = heads[r]
        vis, _ = _tri_masks(d)
        sc = groups[d, h // 2]["cb"] * jnp.exp(jnp.where(vis, hd["mt"][:, 0:128], ninf)) * dt[d][r:r + 1, :]
        hd["y"] = _mm(sc.astype(BF16), hd["xmb"]) + hd["mt"][:, 128:256] * hd["cs"]

    for d in (0, 1):
        out_ref = (yf_ref, yb_ref)[d]
        for g in range(2):
            y_pair = heads[4 * d + 2 * g]["y"] + heads[4 * d + 2 * g + 1]["y"]
            if d == 0:
                y_pair = y_pair + dskip_ref[:, 128 * g:128 * g + 128] * groups[d, g]["x_pair"]
            out_ref[0, :, 128 * g:128 * g + 128] = y_pair


def _ssd(ss, gt, bcol, acol, conv_w, conv_b, dskip, n_ctx_chunks):
    bsz, t, _ = ss.shape
    nc = t // CHUNK
    sub = CHUNK // 8
    nsub = t // 8
    bc = lambda c: _bwd_chunk(c, n_ctx_chunks, nc)
    fwd = lambda b, c: (b, c, 0)
    bwd = lambda b, c: (b, bc(c), 0)
    fwd_p = lambda b, c: (b, jnp.maximum(c * sub - 1, 0), 0)
    fwd_n = lambda b, c: (b, jnp.minimum((c + 1) * sub, nsub - 1), 0)
    bwd_p = lambda b, c: (b, jnp.maximum(bc(c) * sub - 1, 0), 0)
    bwd_n = lambda b, c: (b, jnp.minimum((bc(c) + 1) * sub, nsub - 1), 0)
    fwd_t = lambda b, c: (b, 0, c)
    bwd_t = lambda b, c: (b, 0, bc(c))
    const2 = lambda b, c: (0, 0)
    out = jax.ShapeDtypeStruct((bsz, t, W_BRANCH), F32)
    return dict(
        steps=functools.partial(_ssd_steps, n_ctx_chunks, nc),
        in_specs=[pl.BlockSpec((1, CHUNK, 768), fwd), pl.BlockSpec((1, 8, 768), fwd_p),
                  pl.BlockSpec((1, 8, 768), fwd_n),
                  pl.BlockSpec((1, CHUNK, 768), bwd), pl.BlockSpec((1, 8, 768), bwd_p),
                  pl.BlockSpec((1, 8, 768), bwd_n),
                  pl.BlockSpec((1, LANES, CHUNK), fwd_t), pl.BlockSpec((1, LANES, CHUNK), bwd_t),
                  pl.BlockSpec((LANES, 1), const2), pl.BlockSpec((LANES, 1), const2),
                  pl.BlockSpec((3, 768), const2), pl.BlockSpec((1, 768), const2),
                  pl.BlockSpec((1, W_BRANCH), const2),
                  pl.BlockSpec((CHUNK, LANES), const2), pl.BlockSpec((CHUNK, 2 * LANES), const2),
                  pl.BlockSpec((8, CHUNK, 3 * LANES), lambda b, c: (0, 0, 0))],
        out_specs=[pl.BlockSpec((1, CHUNK, W_BRANCH), fwd), pl.BlockSpec((1, CHUNK, W_BRANCH), bwd)],
        out_shape=[out, out],
        scratch_shapes=[pltpu.VMEM((8, SSM_N, LANES), F32)],
        args=(ss, ss, ss, ss, ss, ss, gt, gt, bcol, acol, conv_w, conv_b, dskip,
              jnp.ones((CHUNK, LANES), BF16), _spread_selector(2), _state_row_selector(3)))


def _hgrn2_steps(hf_ref, vtf_ref, hb_ref, vtb_ref, loglb_ref, log1m_ref, onem_ref, e_ref, of_ref, ob_ref,
                  st_ref):
    @pl.when(pl.program_id(1) == 0)
    def _():
        st_ref[...] = jnp.zeros_like(st_ref)

    ri = _iota2((CHUNK, CHUNK), 0)
    ci = _iota2((CHUNK, CHUNK), 1)
    lane = _iota2((CHUNK, LANES), 1)
    rw = _iota2((CHUNK, W_BRANCH), 0)
    blockdiag = (ri >> 6) == (ci >> 6)
    ninf = jnp.float32(-jnp.inf)
    nb = CHUNK // SUB
    rs = _iota2((nb, SUB, W_BRANCH), 1)

    dirs = []
    for d in (0, 1):
        h_ref = (hf_ref, hb_ref)[d]
        tc = jnp.where(_tri_masks(d)[0], 1.0, 0.0).astype(BF16)
        z = h_ref[0, :, W_BRANCH * (1 + d):W_BRANCH * (2 + d)]
        e = jnp.exp(-jnp.abs(z))
        ope = 1.0 + e
        la = loglb_ref[d:d + 1, :]
        lb_ = log1m_ref[d:d + 1, :] + (jnp.minimum(z, 0.0) - jnp.log(ope))
        logf = jnp.maximum(la, lb_) + jnp.log(1.0 + jnp.exp(-jnp.abs(la - lb_)))
        dirs.append(dict(
            q=_silu(h_ref[0, :, 0:W_BRANCH]),
            v=h_ref[0, :, 3 * W_BRANCH:4 * W_BRANCH],
            kk=onem_ref[d:d + 1, :] * (jnp.where(z >= 0.0, e, 1.0) / ope),
            gcum=_mm_exact_l(tc, logf)))

    yield
    for d in (0, 1):
        dd = dirs[d]
        q, kk, gcum = dd["q"], dd["kk"], dd["gcum"]
        vt_ref = (vtf_ref, vtb_ref)[d]
        last = CHUNK - 1 if d == 0 else 0
        g_last = gcum[last:last + 1, :]
        qg = (q * jnp.exp(gcum)).astype(BF16)
        kg = (kk * jnp.exp(g_last - gcum)).astype(BF16)

        dd["o_inter"] = []
        for p in range(2):
            sl = slice(128 * p, 128 * p + 128)
            st = st_ref[2 * d + p]
            dd["o_inter"].append(_mm_nt(qg[:, sl], st.astype(BF16)))
            dst = _mm(vt_ref[0, sl, :].astype(BF16), kg[:, sl])
            st_ref[2 * d + p] = st * jnp.exp(g_last[:, sl]) + jnp.where(blockdiag, dst, 0.0)

        a_mats = [jnp.zeros((CHUNK, CHUNK), F32) for _ in range(N_HEADS)]
        blk = CHUNK // 2
        while blk >= SUB:
            first = (rw & (2 * blk - 1)) < blk
            edge = (blk - 1) if d == 0 else blk
            gb = gcum.reshape(CHUNK // (2 * blk), 2 * blk, W_BRANCH)[:, edge:edge + 1, :]
            gb = jnp.broadcast_to(gb, (CHUNK // (2 * blk), 2 * blk, W_BRANCH)).reshape(CHUNK, W_BRANCH)
            q_side = jnp.logical_not(first) if d == 0 else first
            qt = q * jnp.exp(jnp.where(q_side, gcum - gb, ninf))
            kt = (kk * jnp.exp(jnp.where(q_side, ninf, gb - gcum))).astype(BF16)
            same = (ri >> int(math.log2(2 * blk))) == (ci >> int(math.log2(2 * blk)))
            for h in range(N_HEADS):
                p, half = divmod(h, 2)
                hmask = (lane >= 64) if half else (lane < 64)
                qh = jnp.where(hmask, qt[:, 128 * p:128 * p + 128], 0.0).astype(BF16)
                a_mats[h] = a_mats[h] + jnp.where(same, _mm_nt(qh, kt[:, 128 * p:128 * p + 128]), 0.0)
            blk //= 2
        dd["a_mats"] = a_mats

        g3 = gcum.reshape(nb, SUB, W_BRANCH)
        q3 = q.reshape(nb, SUB, W_BRANCH)
        k3 = kk.reshape(nb, SUB, W_BRANCH)
        v3 = dd["v"].reshape(nb, SUB, W_BRANCH)
        o3 = None
        for j in range(SUB):
            ok = (rs >= j) if d == 0 else (rs <= j)
            pj = q3 * jnp.exp(jnp.where(ok, g3 - g3[:, j:j + 1, :], ninf)) * k3[:, j:j + 1, :]
            red = _mm(pj.reshape(CHUNK, W_BRANCH).astype(BF16), e_ref[...])
            term = red.reshape(nb, SUB, W_BRANCH) * v3[:, j:j + 1, :]
            o3 = term if o3 is None else o3 + term
        dd["o_diag"] = o3.reshape(CHUNK, W_BRANCH)

    yield
    for d in (0, 1):
        dd = dirs[d]
        out_ref = (of_ref, ob_ref)[d]
        for p in range(2):
            sl = slice(128 * p, 128 * p + 128)
            a_cat = jnp.concatenate([dd["a_mats"][2 * p], dd["a_mats"][2 * p + 1]], axis=1).astype(BF16)
            v_p = dd["v"][:, sl]
            v_cat = jnp.concatenate([jnp.where(lane < 64, v_p, 0.0), jnp.where(lane >= 64, v_p, 0.0)],
                                    axis=0).astype(BF16)
            out_ref[0, :, sl] = dd["o_inter"][p] + _mm(a_cat, v_cat) + dd["o_diag"][:, sl]


def _hgrn2(hg, hvt, loglb, log1m, onem, e64, n_ctx_chunks):
    bsz, t, _ = hg.shape
    nc = t // CHUNK
    fwd = lambda b, c: (b, c, 0)
    bwd = lambda b, c: (b, _bwd_chunk(c, n_ctx_chunks, nc), 0)
    fwd_t = lambda b, c: (b, 0, c)
    bwd_t = lambda b, c: (b, 0, _bwd_chunk(c, n_ctx_chunks, nc))
    const2 = lambda b, c: (0, 0)
    out = jax.ShapeDtypeStruct((bsz, t, W_BRANCH), F32)
    return dict(
        steps=_hgrn2_steps,
        in_specs=[pl.BlockSpec((1, CHUNK, 1024), fwd), pl.BlockSpec((1, W_BRANCH, CHUNK), fwd_t),
                  pl.BlockSpec((1, CHUNK, 1024), bwd), pl.BlockSpec((1, W_BRANCH, CHUNK), bwd_t),
                  pl.BlockSpec((2, W_BRANCH), const2), pl.BlockSpec((2, W_BRANCH), const2),
                  pl.BlockSpec((2, W_BRANCH), const2), pl.BlockSpec((W_BRANCH, W_BRANCH), const2)],
        out_specs=[pl.BlockSpec((1, CHUNK, W_BRANCH), fwd), pl.BlockSpec((1, CHUNK, W_BRANCH), bwd)],
        out_shape=[out, out],
        scratch_shapes=[pltpu.VMEM((4, LANES, LANES), F32)],
        args=(hg, hvt, hg, hvt, loglb, log1m, onem, e64))


def _scan_kernel(parts, *refs):
    n_in = [len(p["in_specs"]) for p in parts]
    n_out = [len(p["out_specs"]) for p in parts]
    n_scr = [len(p["scratch_shapes"]) for p in parts]
    ins, outs, scr = refs[:sum(n_in)], refs[sum(n_in):sum(n_in) + sum(n_out)], refs[sum(n_in) + sum(n_out):]
    gens = []
    for k, p in enumerate(parts):
        mine = (ins[sum(n_in[:k]):sum(n_in[:k + 1])] + outs[sum(n_out[:k]):sum(n_out[:k + 1])]
                + scr[sum(n_scr[:k]):sum(n_scr[:k + 1])])
        gens.append(p["steps"](*mine))
    while gens:
        for g in list(gens):
            if next(g, gens) is gens:
                gens.remove(g)


def _chunk_scans(parts, bsz, n_chunks):
    return pl.pallas_call(
        functools.partial(_scan_kernel, parts),
        grid=(bsz, n_chunks),
        in_specs=[s for p in parts for s in p["in_specs"]],
        out_specs=[s for p in parts for s in p["out_specs"]],
        out_shape=[s for p in parts for s in p["out_shape"]],
        scratch_shapes=[s for p in parts for s in p["scratch_shapes"]],
        compiler_params=_cparams(("parallel", "arbitrary")),
        name="chunk_scans",
    )(*[a for p in parts for a in p["args"]])


def _attn(daq, kt, vx, lam_vecs, lam_init, n_ctx_tiles, skip_ctx):
    bsz, t, _ = daq.shape
    tq = ROW_TILE
    n_k_tiles = t // KEY_TILE
    n_lat_tiles = n_k_tiles - n_ctx_tiles
    group = math.gcd(KEY_GROUP, n_lat_tiles)
    q_off = n_ctx_tiles if skip_ctx else 0
    nq = t // tq - q_off
    n_maps = W_BRANCH // DA_QK
    rows = n_maps * tq

    def kern(q_ref, kt_ref, vx_ref, lv_ref, o_ref, q8_ref, m_ref, acc_ref, s_ref, mx_ref):
        qi = pl.program_id(1) + q_off
        lane = _iota2((tq, W_BRANCH), 1)
        q = q_ref[0]
        for j in range(n_maps):
            q8_ref[j * tq:(j + 1) * tq, :] = jnp.where((lane >> 5) == j, q, jnp.zeros_like(q))
        m_ref[...] = jnp.full_like(m_ref, -jnp.inf)
        acc_ref[...] = jnp.zeros_like(acc_ref)

        def update(t0, n, after_scores=None):
            q8 = q8_ref[...]
            s = [_mm(q8, kt_ref[0, t0 + i]) for i in range(n)]
            if after_scores is not None:
                after_scores()
            mx = s[0][:, 0:LANES]
            for i in range(n):
                for c in range(KEY_TILE // LANES):
                    if i or c:
                        mx = jnp.maximum(mx, s[i][:, LANES * c:LANES * (c + 1)])
            m_old = m_ref[...]
            m_new = jnp.maximum(m_old, jnp.max(mx, axis=1, keepdims=True))
            alpha = jnp.exp(m_old - m_new)
            m_ref[...] = m_new
            m2 = jnp.concatenate([m_new] * (KEY_TILE // LANES), axis=1)
            p = [jnp.exp(s[i] - m2).astype(BF16) for i in range(n)]
            for hd in range(N_HEADS):
                r0 = 2 * hd * tq
                pv = _mm(p[0][r0:r0 + 2 * tq], vx_ref[0, t0, hd])
                for i in range(1, n):
                    pv = pv + _mm(p[i][r0:r0 + 2 * tq], vx_ref[0, t0 + i, hd])
                acc_ref[r0:r0 + 2 * tq, :] = alpha[r0:r0 + 2 * tq] * acc_ref[r0:r0 + 2 * tq, :] + pv

        def scores(g, slot):
            q8 = q8_ref[...]
            t0 = n_ctx_tiles + g * group
            mx = None
            for i in range(group):
                s = _mm(q8, kt_ref[0, t0 + i])
                s_ref[slot, :, KEY_TILE * i:KEY_TILE * (i + 1)] = s
                for c in range(KEY_TILE // LANES):
                    part = s[:, LANES * c:LANES * (c + 1)]
                    mx = part if mx is None else jnp.maximum(mx, part)
            mx_ref[slot] = mx

        def absorb(g, slot):
            t0 = n_ctx_tiles + g * group
            m_old = m_ref[...]
            m_new = jnp.maximum(m_old, jnp.max(mx_ref[slot], axis=1, keepdims=True))
            alpha = jnp.exp(m_old - m_new)
            m_ref[...] = m_new
            m2 = jnp.concatenate([m_new] * (KEY_TILE // LANES), axis=1)
            p = [jnp.exp(s_ref[slot, :, KEY_TILE * i:KEY_TILE * (i + 1)] - m2).astype(BF16) for i in range(group)]
            for hd in range(N_HEADS):
                r0 = 2 * hd * tq
                pv = _mm(p[0][r0:r0 + 2 * tq], vx_ref[0, t0, hd])
                for i in range(1, group):
                    pv = pv + _mm(p[i][r0:r0 + 2 * tq], vx_ref[0, t0 + i, hd])
                acc_ref[r0:r0 + 2 * tq, :] = alpha[r0:r0 + 2 * tq] * acc_ref[r0:r0 + 2 * tq, :] + pv

        @pl.when(qi < n_ctx_tiles)
        def _():
            update(0, n_ctx_tiles)

        @pl.when(qi >= n_ctx_tiles)
        def _():
            n_groups = n_lat_tiles // group
            n_pairs = (n_groups - 1) // 2
            update(0, n_ctx_tiles, after_scores=lambda: scores(0, 0))

            def body(k, carry):
                g = 2 * k
                scores(g + 1, 1)
                absorb(g, 0)
                scores(g + 2, 0)
                absorb(g + 1, 1)
                return carry
            lax.fori_loop(0, n_pairs, body, 0)
            g0 = 2 * n_pairs
            if n_groups - g0 == 2:
                scores(g0 + 1, 1)
            absorb(g0, 0)
            if n_groups - g0 == 2:
                absorb(g0 + 1, 1)

        lv = lv_ref[...]
        lam = (jnp.exp(jnp.sum(lv[0:1, :] * lv[1:2, :], axis=1, keepdims=True))
               - jnp.exp(jnp.sum(lv[2:3, :] * lv[3:4, :], axis=1, keepdims=True)) + lam_init)
        lane1 = _iota2((tq, LANES), 1)
        for pr in range(N_HEADS // 2):
            halves = []
            for half in range(2):
                r0 = 2 * (2 * pr + half) * tq
                one = 0 if half else HEAD_W
                a0 = acc_ref[r0:r0 + tq, :]
                a1 = acc_ref[r0 + tq:r0 + 2 * tq, :]
                halves.append(a0 / a0[:, one:one + 1] - lam * (a1 / a1[:, one:one + 1]))
            o_ref[0, :, LANES * pr:LANES * (pr + 1)] = jnp.where(lane1 < HEAD_W, halves[0], halves[1])

    return pl.pallas_call(
        kern,
        grid=(bsz, nq),
        in_specs=[pl.BlockSpec((1, tq, W_BRANCH), lambda b, i: (b, i + q_off, 0)),
                  pl.BlockSpec((1, n_k_tiles, W_BRANCH, KEY_TILE), lambda b, i: (b, 0, 0, 0),
                               pipeline_mode=pl.Buffered(1)),
                  pl.BlockSpec((1, n_k_tiles, N_HEADS, KEY_TILE, LANES), lambda b, i: (b, 0, 0, 0, 0),
                               pipeline_mode=pl.Buffered(1)),
                  pl.BlockSpec(lam_vecs.shape, lambda b, i: (0, 0))],
        out_specs=pl.BlockSpec((1, tq, W_BRANCH), lambda b, i: (b, i, 0)),
        out_shape=jax.ShapeDtypeStruct((bsz, nq * tq, W_BRANCH), F32),
        scratch_shapes=[pltpu.VMEM((rows, W_BRANCH), BF16),
                        pltpu.VMEM((rows, LANES), F32),
                        pltpu.VMEM((rows, LANES), F32),
                        pltpu.VMEM((2, rows, group * KEY_TILE), F32),
                        pltpu.VMEM((2, rows, LANES), F32)],
        compiler_params=_cparams(("parallel", "parallel")),
        name="diff_attn",
    )(daq, kt, vx, lam_vecs)


def _seg_mean(x, e_ref):
    x0 = x.astype(BF16)
    x1 = (x - x0.astype(F32)).astype(BF16)
    return (_mm(x0, e_ref[...]) + _mm(x1, e_ref[...])) * (1.0 / HEAD_W)


def _out_kernel(n_ctx_tiles, ctx_row, q_off, lam_init, final,
                c_ref, x_ref, mo_ref, mlo_ref, mlz_ref, mhf_ref, mhb_ref, dao_ref, daz_ref,
                syf_ref, syb_ref, ssz_ref, hof_ref, hob_ref, hgz_ref,
                mlg_ref, dag_ref, ssg_ref, hgg_ref, e_ref, w_ref, fg_ref, o_ref):
    b = pl.program_id(0)
    i = pl.program_id(1) + q_off
    d = x_ref.shape[2]
    r = jnp.where(i < n_ctx_tiles, ctx_row, b)
    gate = mo_ref[pl.ds(r, 1), :][:, 2 * d:3 * d]

    u = _sigmoid(mlo_ref[0]) * (mhf_ref[0] + mhb_ref[0])
    dev = u - _seg_mean(u, e_ref)
    y_ml = dev * lax.rsqrt(_seg_mean(dev * dev, e_ref) + EPS) * mlg_ref[...] * _silu(mlz_ref[0])

    o = dao_ref[0]
    y_da = (o * lax.rsqrt(_seg_mean(o * o, e_ref) + EPS) * dag_ref[...]) * (1.0 - lam_init) * _silu(daz_ref[0])

    ys = (syf_ref[0] + syb_ref[0]) * _silu(ssz_ref[0])
    parts = []
    for g in range(2):
        yg = ys[:, 128 * g:128 * g + 128]
        parts.append(yg * lax.rsqrt(jnp.mean(yg * yg, axis=-1, keepdims=True) + EPS))
    y_ss = jnp.concatenate(parts, axis=1) * ssg_ref[...]

    oh = hof_ref[0] + hob_ref[0]
    y_hg = (oh * lax.rsqrt(_seg_mean(oh * oh, e_ref) + EPS) * hgg_ref[...]) * _silu(hgz_ref[0])

    acc = _mm(y_ml.astype(BF16), w_ref[0:W_BRANCH, :])
    acc = acc + _mm(y_da.astype(BF16), w_ref[W_BRANCH:2 * W_BRANCH, :])
    acc = acc + _mm(y_ss.astype(BF16), w_ref[2 * W_BRANCH:3 * W_BRANCH, :])
    acc = acc + _mm(y_hg.astype(BF16), w_ref[3 * W_BRANCH:4 * W_BRANCH, :])
    x_new = jnp.where(i < n_ctx_tiles, c_ref[0], x_ref[0]) + gate * acc
    if final:
        x_new = x_new * lax.rsqrt(jnp.mean(x_new * x_new, axis=-1, keepdims=True) + EPS) * fg_ref[...]
    o_ref[0] = x_new


def _out_proj(stream, mo, ml, mhf, mhb, dao, daz, ss, syf, syb, hg, hof, hob,
              mlg, dag, ssg, hgg, e64, w_out, final_g, lam_init, n_ctx_tiles, ctx_row, final):
    c_src, x_src, x_off, t = stream
    bsz, _, d = x_src.shape
    tm = ROW_TILE
    q_off = n_ctx_tiles if final else 0
    nrow = t // tm - q_off
    row = lambda b, i: (b, i + q_off, 0)
    col = lambda k: (lambda b, i: (b, i + q_off, k))
    const2 = lambda b, i: (0, 0)
    wb = pl.BlockSpec((1, tm, W_BRANCH), row)
    return pl.pallas_call(
        functools.partial(_out_kernel, n_ctx_tiles, ctx_row, q_off, lam_init, final),
        grid=(bsz, nrow),
        in_specs=_stream_specs(stream, n_ctx_tiles, q_off) + [
                  pl.BlockSpec(mo.shape, const2),
                  pl.BlockSpec((1, tm, W_BRANCH), col(2)), pl.BlockSpec((1, tm, W_BRANCH), col(3)),
                  wb, wb,
                  pl.BlockSpec((1, tm, W_BRANCH), lambda b, i: (b, i, 0)) if final else wb,
                  wb,
                  wb, wb, pl.BlockSpec((1, tm, W_BRANCH), col(3)),
                  wb, wb, pl.BlockSpec((1, tm, W_BRANCH), col(4)),
                  pl.BlockSpec((1, W_BRANCH), const2), pl.BlockSpec((1, W_BRANCH), const2),
                  pl.BlockSpec((1, W_BRANCH), const2), pl.BlockSpec((1, W_BRANCH), const2),
                  pl.BlockSpec((W_BRANCH, W_BRANCH), const2),
                  pl.BlockSpec(w_out.shape, const2),
                  pl.BlockSpec((1, d), const2)],
        out_specs=pl.BlockSpec((1, tm, d), lambda b, i: (b, i, 0)),
        out_shape=jax.ShapeDtypeStruct((bsz, nrow * tm, d), F32),
        compiler_params=_cparams(("parallel", "parallel")),
        name="out_proj",
    )(c_src, x_src, mo, ml, ml, mhf, mhb, dao, daz, syf, syb, ss, hof, hob, hg,
      mlg, dag, ssg, hgg, e64, w_out, final_g.reshape(1, d))


def _relayout_w_in(w_in):
    d = w_in.shape[0]
    o = {}
    off = 0
    for name, n in (("ml_q", 256), ("ml_k", 256), ("ml_v", 256), ("ml_o", 256), ("ml_i", 8), ("ml_f", 8),
                    ("ml_z", 256), ("da_q", 256), ("da_k", 256), ("da_v", 256), ("da_z", 256),
                    ("ss_xbc", 768), ("ss_dt", 8), ("ss_z", 256),
                    ("hg_q", 256), ("hg_f", 512), ("hg_i", 256), ("hg_z", 256)):
        o[name] = w_in[:, off:off + n]
        off += n

    gates = jnp.concatenate([o["ml_i"], o["ml_f"], o["ss_dt"], jnp.zeros((d, LANES - 24), w_in.dtype)], axis=1)
    zpad = jnp.zeros((d, HEAD_W), w_in.dtype)
    v_ext = []
    for hd in range(N_HEADS):
        v_h = o["da_v"][:, HEAD_W * hd:HEAD_W * (hd + 1)]
        v_ext += [zpad, v_h] if hd % 2 else [v_h, zpad]
    w2 = jnp.concatenate([o["ml_q"], o["ml_v"], o["ml_o"], o["ml_z"],
                          o["da_q"]] + v_ext + [o["da_z"],
                          o["ss_xbc"], o["ss_z"],
                          o["hg_q"], o["hg_f"], o["hg_i"], o["hg_z"]], axis=1)
    wt = jnp.concatenate([o["da_k"], gates, o["ml_k"], o["hg_i"]], axis=1).T
    return w2.astype(BF16), wt.astype(BF16)


def _rope_tables(n_ctx, seq):
    pos = jnp.arange(seq)
    rows = (pos // GRID_W).astype(F32)
    cols = (pos % GRID_W).astype(F32)
    axis = DA_QK // 2
    inv = ROPE_BASE ** (-jnp.arange(0, axis, 2, dtype=F32) / axis)
    ang = jnp.concatenate([rows[:, None] * inv, cols[:, None] * inv], axis=-1)
    cos = jnp.concatenate([jnp.ones((n_ctx, axis), F32), jnp.cos(ang)], axis=0)
    sin = jnp.concatenate([jnp.zeros((n_ctx, axis), F32), jnp.sin(ang)], axis=0)
    reps = W_BRANCH // axis
    return jnp.tile(cos, (1, reps)), jnp.tile(sin, (1, reps))


def _gate_col(vals):
    v = jnp.concatenate([vals.astype(F32), jnp.zeros((LANES - vals.shape[0],), F32)])
    return v.reshape(LANES, 1)


def kernel(x, c, ctx, c_ctx, w_mod, b_mod, norm_g, w_in, w_out, ml_gate_b, ml_norm_g, da_lambda, da_norm_g,
           ss_conv_w, ss_conv_b, ss_dt_bias, ss_a_log, ss_d, ss_norm_g, hg_lower, hg_norm_g, final_g):
    bsz, seq, d = x.shape
    n_ctx = ctx.shape[1]
    depth = w_mod.shape[0]
    assert n_ctx % ROW_TILE == 0 and seq % ROW_TILE == 0 and bsz < 8
    n_ctx_tiles = n_ctx // ROW_TILE
    n_ctx_chunks = n_ctx // CHUNK
    ctx_row = bsz

    stream = (ctx, x, 0, n_ctx + seq)
    cc = jnp.concatenate([c, c_ctx[None, :], jnp.zeros((8 - bsz - 1, d), F32)], axis=0)
    cos, sin = _rope_tables(n_ctx, seq)
    cos_t, sin_t = cos.T, sin.T
    one_lanes = jnp.arange(V_EXT) % LANES == jnp.where((jnp.arange(V_EXT) // LANES) % 2 == 1, 0, HEAD_W)
    ones_row = one_lanes.astype(F32).reshape(1, V_EXT)
    lb_all = jnp.cumsum(jax.nn.softmax(hg_lower.astype(F32), axis=1), axis=1)
    lb_all = lb_all - lb_all[:, :1]
    hid = _iota2((W_BRANCH, W_BRANCH), 0) // HEAD_W
    e64 = (hid == hid.T).astype(BF16)

    out = None
    for l in range(depth):
        lam_init = 0.8 - 0.6 * math.exp(-0.3 * l)
        final = l == depth - 1
        w2, wt = _relayout_w_in(w_in[l])
        mo = _modulation(cc, w_mod[l], b_mod[l])
        ml, daq, dak, dav, daz, ss, hg, gt, mkt, hvt = _in_proj(stream, mo, norm_g[l], cos, sin, cos_t, sin_t, w2, wt,
                                                          ones_row, n_ctx_tiles, ctx_row)

        gb = ml_gate_b[l]
        zeros8 = jnp.zeros((8,), F32)
        bcol = _gate_col(jnp.concatenate([gb[:, 0].reshape(-1), gb[:, 1].reshape(-1),
                                                ss_dt_bias[l].reshape(-1)]))
        acol = _gate_col(jnp.concatenate([zeros8, zeros8, -jnp.exp(ss_a_log[l].astype(F32)).reshape(-1)]))
        dao = _attn(daq, dak, dav, da_lambda[l].astype(F32), lam_init, n_ctx_tiles, final)
        dskip = jnp.repeat(ss_d[l].astype(F32), HEAD_W).reshape(1, W_BRANCH)
        lbh = lb_all[:, l]
        scans = [_mlstm(ml, mkt, gt, bcol, n_ctx_chunks),
                 _ssd(ss, gt, bcol, acol, ss_conv_w[l], ss_conv_b[l].reshape(1, -1), dskip, n_ctx_chunks),
                 _hgrn2(hg, hvt, jnp.log(lbh), jnp.log1p(-lbh), 1.0 - lbh, e64, n_ctx_chunks)]
        mhf, mhb, syf, syb, hof, hob = _chunk_scans(scans, bsz, (n_ctx + seq) // CHUNK)

        res = _out_proj(stream, mo, ml, mhf, mhb, dao, daz, ss, syf, syb, hg, hof, hob,
                        ml_norm_g[l].reshape(1, -1), jnp.tile(da_norm_g[l], N_HEADS).reshape(1, -1),
                        ss_norm_g[l].reshape(1, -1), hg_norm_g[l].reshape(1, -1), e64,
                        w_out[l].astype(BF16), final_g, lam_init, n_ctx_tiles, ctx_row, final)
        if final:
            out = res
        else:
            stream = (res, res, n_ctx_tiles, n_ctx + seq)
    return out
```

```python
import functools
import math

import jax
import jax.numpy as jnp
from jax import lax
from jax.experimental import pallas as pl
from jax.experimental.pallas import tpu as pltpu

F32 = jnp.float32
BF16 = jnp.bfloat16

EPS = 1e-6
GRID_W = 64
ROPE_BASE = 10000.0
N_HEADS = 4
HEAD_W = 64
W_BRANCH = N_HEADS * HEAD_W
DA_QK = 32
SSM_N = 128
CHUNK = 128
ROW_TILE = 256
SUB = 8
LANES = 128
VMEM_LIMIT = 56 * 1024 * 1024

C_ML = 0
C_DAQ = 512
C_DAV = 768
C_SS = 1280
C_HG = 2048
C_ZG = 3072
C_END = 4352
V_EXT = N_HEADS * LANES
KEY_TILE = 256
KEY_GROUP = 4

NT = (((1,), (1,)), ((), ()))
TN = (((0,), (0,)), ((), ()))


def _mm(a, b):
    return jnp.dot(a, b, preferred_element_type=F32)


def _mm_nt(a, b):
    return lax.dot_general(a, b, NT, preferred_element_type=F32)


def _mm_tn(a, b):
    return lax.dot_general(a, b, TN, preferred_element_type=F32)


def _split3(x):
    x0 = x.astype(BF16)
    r = x - x0.astype(F32)
    x1 = r.astype(BF16)
    x2 = (r - x1.astype(F32)).astype(BF16)
    return x0, x1, x2


def _mm_exact_l(t, x):
    x0, x1, x2 = _split3(x)
    return _mm(t, x0) + _mm(t, x1) + _mm(t, x2)


def _mm_exact_r(x, t):
    x0, x1, x2 = _split3(x)
    return _mm(x0, t) + _mm(x1, t) + _mm(x2, t)


def _sigmoid(x):
    return 1.0 / (1.0 + jnp.exp(-x))


def _silu(x):
    return x * _sigmoid(x)


def _log_sigmoid(x):
    return jnp.minimum(x, 0.0) - jnp.log(1.0 + jnp.exp(-jnp.abs(x)))


def _softplus(x):
    return jnp.maximum(x, 0.0) + jnp.log(1.0 + jnp.exp(-jnp.abs(x)))


def _iota2(shape, axis):
    return lax.broadcasted_iota(jnp.int32, shape, axis)


def _cparams(sem):
    return pltpu.CompilerParams(dimension_semantics=sem, vmem_limit_bytes=VMEM_LIMIT)


def _mod_kernel(c_ref, w_ref, b_ref, o_ref):
    c = c_ref[...]
    o_ref[...] = jnp.dot(_silu(c), w_ref[...], precision=lax.Precision.HIGHEST,
                         preferred_element_type=F32) + b_ref[...]


def _modulation(cc, w_mod, b_mod):
    d = cc.shape[1]
    n = w_mod.shape[1]
    tn = 512
    return pl.pallas_call(
        _mod_kernel,
        grid=(n // tn,),
        in_specs=[pl.BlockSpec((8, d), lambda j: (0, 0)),
                  pl.BlockSpec((d, tn), lambda j: (0, j)),
                  pl.BlockSpec((1, tn), lambda j: (0, j))],
        out_specs=pl.BlockSpec((8, tn), lambda j: (0, j)),
        out_shape=jax.ShapeDtypeStruct((8, n), F32),
        compiler_params=_cparams(("arbitrary",)),
        name="modulation",
    )(cc, w_mod, b_mod.reshape(1, n))


def _stream_specs(stream, n_ctx_tiles, first_tile):
    _, x_src, x_off, _ = stream
    d = x_src.shape[2]
    return [pl.BlockSpec((1, ROW_TILE, d), lambda b, i: (b, jnp.minimum(i + first_tile, n_ctx_tiles - 1), 0)),
            pl.BlockSpec((1, ROW_TILE, d), lambda b, i: (b, jnp.maximum(i + first_tile - n_ctx_tiles, 0) + x_off, 0))]


def _in_kernel(n_ctx_tiles, ctx_row, c_ref, x_ref, mo_ref, g_ref, cos_ref, sin_ref, cost_ref, sint_ref,
               w_ref, wt_ref, ones_ref,
               ml_ref, daq_ref, kt_ref, vx_ref, ss_ref, hg_ref, zg_ref, gt_ref, mkt_ref, hvt_ref):
    b = pl.program_id(0)
    i = pl.program_id(1)
    d = x_ref.shape[2]
    r = jnp.where(i < n_ctx_tiles, ctx_row, b)
    mo = mo_ref[pl.ds(r, 1), :]
    shift = mo[:, 0:d]
    scale = mo[:, d:2 * d]
    x = jnp.where(i < n_ctx_tiles, c_ref[0], x_ref[0])
    y = x * lax.rsqrt(jnp.mean(x * x, axis=-1, keepdims=True) + EPS) * g_ref[...]
    h = (y * (1.0 + scale) + shift).astype(BF16)

    ml_ref[0] = _mm(h, w_ref[:, C_ML:C_DAQ])
    q = _mm(h, w_ref[:, C_DAQ:C_DAV])
    half = DA_QK // 2
    lane = _iota2((x.shape[0], LANES), 1)
    q_rot = []
    for c in range(W_BRANCH // LANES):
        qc = q[:, LANES * c:LANES * (c + 1)]
        q_rot.append(jnp.where((lane & half) == 0, -pltpu.roll(qc, LANES - half, 1), pltpu.roll(qc, half, 1)))
    q = q * cos_ref[...] + jnp.concatenate(q_rot, axis=1) * sin_ref[...]
    daq_ref[0] = (q * (DA_QK ** -0.5)).astype(BF16)
    vx = (_mm(h, w_ref[:, C_DAV:C_SS]) + ones_ref[...]).astype(BF16)
    for hd in range(N_HEADS):
        vx_ref[0, 0, hd] = vx[:, LANES * hd:LANES * (hd + 1)]
    ss_ref[0] = _mm(h, w_ref[:, C_SS:C_HG])
    hg_ref[0] = _mm(h, w_ref[:, C_HG:C_ZG])
    zg_ref[0] = _mm(h, w_ref[:, C_ZG:C_END]).astype(BF16)
    tr = _mm_nt(wt_ref[...], h)
    kt = tr[0:W_BRANCH]
    kt_rot = []
    for j in range(W_BRANCH // DA_QK):
        kt_rot += [-kt[DA_QK * j + half:DA_QK * (j + 1)], kt[DA_QK * j:DA_QK * j + half]]
    kt_ref[0, 0] = (kt * cost_ref[...] + jnp.concatenate(kt_rot, axis=0) * sint_ref[...]).astype(BF16)
    gt_ref[0] = tr[W_BRANCH:W_BRANCH + LANES]
    mkt_ref[0] = tr[W_BRANCH + LANES:2 * W_BRANCH + LANES]
    hvt_ref[0] = tr[2 * W_BRANCH + LANES:3 * W_BRANCH + LANES]


def _in_proj(stream, mo, g, cos, sin, cos_t, sin_t, w2, wt, ones_row, n_ctx_tiles, ctx_row):
    c_src, x_src, x_off, t = stream
    bsz, _, d = x_src.shape
    tm = ROW_TILE
    assert tm == KEY_TILE
    row = lambda b, i: (b, i, 0)
    const2 = lambda b, i: (0, 0)
    out_shape = [
        jax.ShapeDtypeStruct((bsz, t, 2 * W_BRANCH), F32),
        jax.ShapeDtypeStruct((bsz, t, W_BRANCH), BF16),
        jax.ShapeDtypeStruct((bsz, t // tm, W_BRANCH, tm), BF16),
        jax.ShapeDtypeStruct((bsz, t // tm, N_HEADS, tm, LANES), BF16),
        jax.ShapeDtypeStruct((bsz, t, 768), F32),
        jax.ShapeDtypeStruct((bsz, t, 4 * W_BRANCH), F32),
        jax.ShapeDtypeStruct((bsz, t, 5 * W_BRANCH), BF16),
        jax.ShapeDtypeStruct((bsz, LANES, t), F32),
        jax.ShapeDtypeStruct((bsz, W_BRANCH, t), F32),
        jax.ShapeDtypeStruct((bsz, W_BRANCH, t), F32),
    ]
    out_specs = [
        pl.BlockSpec((1, tm, 2 * W_BRANCH), row),
        pl.BlockSpec((1, tm, W_BRANCH), row),
        pl.BlockSpec((1, 1, W_BRANCH, tm), lambda b, i: (b, i, 0, 0)),
        pl.BlockSpec((1, 1, N_HEADS, tm, LANES), lambda b, i: (b, i, 0, 0, 0)),
        pl.BlockSpec((1, tm, 768), row),
        pl.BlockSpec((1, tm, 4 * W_BRANCH), row),
        pl.BlockSpec((1, tm, 5 * W_BRANCH), row),
        pl.BlockSpec((1, LANES, tm), lambda b, i: (b, 0, i)),
        pl.BlockSpec((1, W_BRANCH, tm), lambda b, i: (b, 0, i)),
        pl.BlockSpec((1, W_BRANCH, tm), lambda b, i: (b, 0, i)),
    ]
    return pl.pallas_call(
        functools.partial(_in_kernel, n_ctx_tiles, ctx_row),
        grid=(bsz, t // tm),
        in_specs=_stream_specs(stream, n_ctx_tiles, 0) + [
                  pl.BlockSpec(mo.shape, const2),
                  pl.BlockSpec((1, d), const2),
                  pl.BlockSpec((tm, W_BRANCH), lambda b, i: (i, 0)),
                  pl.BlockSpec((tm, W_BRANCH), lambda b, i: (i, 0)),
                  pl.BlockSpec((W_BRANCH, tm), lambda b, i: (0, i)),
                  pl.BlockSpec((W_BRANCH, tm), lambda b, i: (0, i)),
                  pl.BlockSpec(w2.shape, const2),
                  pl.BlockSpec(wt.shape, const2),
                  pl.BlockSpec((1, V_EXT), const2)],
        out_specs=out_specs,
        out_shape=out_shape,
        compiler_params=_cparams(("parallel", "parallel")),
        name="in_proj",
    )(c_src, x_src, mo, g.reshape(1, d), cos, sin, cos_t, sin_t, w2, wt, ones_row)


def _bwd_chunk(c, n_ctx_chunks, n_chunks):
    return jnp.where(c < n_ctx_chunks, n_ctx_chunks - 1 - c, n_chunks + n_ctx_chunks - 1 - c)


def _tri_masks(d):
    ri = _iota2((CHUNK, CHUNK), 0)
    ci = _iota2((CHUNK, CHUNK), 1)
    if d == 0:
        return ci <= ri, ri <= ci
    return ci >= ri, ri >= ci


def _split_f32(x, n):
    parts = []
    for _ in range(n):
        p = x.astype(BF16).astype(F32)
        parts.append(p)
        x = x - p
    return parts


def _cummax_lanes(u, d):
    lane = _iota2(u.shape, 1)
    ninf = jnp.float32(-jnp.inf)
    s = 1
    while s < CHUNK:
        if d == 0:
            sh = jnp.where(lane >= s, pltpu.roll(u, s, 1), ninf)
        else:
            sh = jnp.where(lane < CHUNK - s, pltpu.roll(u, CHUNK - s, 1), ninf)
        u = jnp.maximum(u, sh)
        s *= 2
    return u


def _state_row_selector(n_tiles):
    keep = (jnp.arange(CHUNK)[None, :] % 8) == jnp.arange(8)[:, None]
    return jnp.broadcast_to(keep[:, :, None], (8, CHUNK, n_tiles * LANES)).astype(BF16)


def _spread_selector(n_spread):
    t = jnp.arange(CHUNK)[:, None] // 16
    tiles = [jnp.broadcast_to(t == 3 + k, (CHUNK, LANES)) for k in range(n_spread)]
    return jnp.concatenate(tiles, axis=1).astype(BF16)


def _spread_operands(col_sum, row_sum, spread, sel_ref, ones_ref):
    z8 = jnp.zeros((8, CHUNK), F32)
    one8 = ones_ref[0:8, :].astype(F32)
    pieces = _split_f32(col_sum, 3) + [one8] * 3
    for x in spread:
        pieces += _split_f32(x, 2)
    assert len(pieces) <= CHUNK // 8
    lhs = jnp.concatenate(pieces + [z8] * (CHUNK // 8 - len(pieces)), axis=0).T.astype(BF16)
    rows = jnp.concatenate([one8] * 3 + _split_f32(row_sum, 3) + [z8] * (CHUNK // 8 - 6), axis=0)
    return lhs, jnp.concatenate([rows.astype(BF16), sel_ref[...]], axis=1)


def _mlstm_steps(qf_ref, vf_ref, ktf_ref, gtf_ref, qb_ref, vb_ref, ktb_ref, gtb_ref, bcol_ref, ones_ref, sel_ref,
                  rsel_ref, hf_ref, hb_ref, cn_ref, m_ref):
    @pl.when(pl.program_id(1) == 0)
    def _():
        cn_ref[...] = jnp.zeros_like(cn_ref)
        m_ref[...] = jnp.zeros_like(m_ref)

    lane = _iota2((CHUNK, LANES), 1)
    feat = _iota2((LANES, CHUNK), 0)
    row8 = _iota2((8, LANES), 0)
    ones_t = ones_ref[...]
    ninf = jnp.float32(-jnp.inf)
    m_all = m_ref[...]
    heads = [None] * 8
    gates = []
    f_cum = []
    for d in (0, 1):
        gt_ref = (gtf_ref, gtb_ref)[d]
        tr = jnp.where(_tri_masks(d)[1], 1.0, 0.0).astype(BF16)
        f_cum.append(_mm_exact_r(_log_sigmoid(gt_ref[0, 8:16, :] + bcol_ref[8:16, :]), tr))

    yield
    for d in (0, 1):
        q_ref, v_ref, kt_ref = ((qf_ref, vf_ref, ktf_ref), (qb_ref, vb_ref, ktb_ref))[d]
        for p in range(2):
            q_t = q_ref[0, :, 128 * p:128 * p + 128]
            v_t = v_ref[0, :, 128 * p:128 * p + 128]
            kt_p = kt_ref[0, 128 * p:128 * p + 128, :] * (HEAD_W ** -0.5)
            kt_b = kt_p.astype(BF16)
            for half in range(2):
                r = 4 * d + 2 * p + half
                hmask = (lane >= 64) if half else (lane < 64)
                qm = jnp.where(hmask, q_t, 0.0).astype(BF16)
                cn = cn_ref[r]
                heads[r] = dict(
                    cn=cn, kt_p=kt_p,
                    vw=jnp.concatenate([jnp.where(hmask, v_t, 0.0).astype(BF16), ones_t], axis=1),
                    qk=_mm(qm, kt_b),
                    qc=_mm(qm, cn.astype(BF16)))

    yield
    for d in (0, 1):
        gt_ref = (gtf_ref, gtb_ref)[d]
        last = CHUNK - 1 if d == 0 else 0
        li = gt_ref[0, 0:8, :] + bcol_ref[0:8, :]
        f = f_cum[d]
        u = li - f
        m_old = m_all
        mx = jnp.maximum(m_old, _cummax_lanes(u, d))
        w_int = jnp.exp(m_old - mx)
        e_mi = jnp.exp(-(f + mx))
        u_max = jnp.max(u, axis=1, keepdims=True)
        e_end = jnp.exp(u - u_max)
        f_last = f[:, last:last + 1]
        b_end = f_last + u_max
        m_new = jnp.maximum(f_last + m_old, b_end)
        a_dec = jnp.exp(f_last + m_old - m_new)
        g_inc = jnp.exp(b_end - m_new)
        m_all = jnp.where((row8 >> 2) == d, m_new, m_all)
        gates.append((e_end, a_dec, g_inc) + _spread_operands(-mx, u, [w_int, e_mi], sel_ref, ones_ref))

    for r in range(8):
        hd = heads[r]
        e_end, a_dec, g_inc, lhs, rhs = gates[r // 4]
        fmask = (feat >= 64) if r % 2 else (feat < 64)
        ke = (jnp.where(fmask, hd["kt_p"], 0.0) * e_end[r:r + 1, :]).astype(BF16)
        a_r = a_dec[r:r + 1, :]
        g_r = g_inc[r:r + 1, :]
        hd["mt"] = _mm(lhs, rhs * rsel_ref[r])
        hd["cn_new"] = (jnp.concatenate([a_r, a_r], axis=1) * hd["cn"]
                        + jnp.concatenate([g_r, g_r], axis=1) * _mm(ke, hd["vw"]))

    yield
    for r in range(8):
        hd = heads[r]
        vis, _ = _tri_masks(r // 4)
        s = hd["qk"] * jnp.exp(jnp.where(vis, hd["mt"][:, 0:128], ninf))
        hd["sv"] = _mm(s.astype(BF16), hd["vw"])

    yield
    for d in (0, 1):
        out_ref = (hf_ref, hb_ref)[d]
        for p in range(2):
            pair = []
            for half in range(2):
                hd = heads[4 * d + 2 * p + half]
                w_b = hd["mt"][:, 128:256]
                tot = hd["sv"] + jnp.concatenate([w_b, w_b], axis=1) * hd["qc"]
                pair.append(tot[:, 0:128] / jnp.maximum(jnp.abs(tot[:, 128:256]), hd["mt"][:, 256:384]))
            out_ref[0, :, 128 * p:128 * p + 128] = jnp.where(lane < 64, pair[0], pair[1]).astype(BF16)
    m_ref[...] = m_all
    for r in range(8):
        cn_ref[r] = heads[r]["cn_new"]


def _mlstm(ml, mkt, gt, bcol, n_ctx_chunks):
    bsz, t, _ = ml.shape
    nc = t // CHUNK
    out = jax.ShapeDtypeStruct((bsz, t, W_BRANCH), BF16)
    const2 = lambda b, c: (0, 0)

    def chunk_specs(chunk):
        return [pl.BlockSpec((1, CHUNK, W_BRANCH), lambda b, c: (b, chunk(c), 0)),
                pl.BlockSpec((1, CHUNK, W_BRANCH), lambda b, c: (b, chunk(c), 1)),
                pl.BlockSpec((1, W_BRANCH, CHUNK), lambda b, c: (b, 0, chunk(c))),
                pl.BlockSpec((1, LANES, CHUNK), lambda b, c: (b, 0, chunk(c)))]

    fwd_chunk = lambda c: c
    bwd_chunk = lambda c: _bwd_chunk(c, n_ctx_chunks, nc)
    return dict(
        steps=_mlstm_steps,
        in_specs=chunk_specs(fwd_chunk) + chunk_specs(bwd_chunk) + [
            pl.BlockSpec((LANES, 1), const2), pl.BlockSpec((CHUNK, LANES), const2),
            pl.BlockSpec((CHUNK, 2 * LANES), const2),
            pl.BlockSpec((8, CHUNK, 3 * LANES), lambda b, c: (0, 0, 0))],
        out_specs=[pl.BlockSpec((1, CHUNK, W_BRANCH), lambda b, c: (b, c, 0)),
                   pl.BlockSpec((1, CHUNK, W_BRANCH), lambda b, c: (b, bwd_chunk(c), 0))],
        out_shape=[out, out],
        scratch_shapes=[pltpu.VMEM((8, LANES, 2 * LANES), F32), pltpu.VMEM((8, LANES), F32)],
        args=(ml, ml, mkt, gt, ml, ml, mkt, gt, bcol, jnp.ones((CHUNK, LANES), BF16), _spread_selector(2),
              _state_row_selector(3)))


def _ssd_steps(n_ctx_chunks, n_chunks,
                xf_ref, xfp_ref, xfn_ref, xb_ref, xbp_ref, xbn_ref,
                gtf_ref, gtb_ref, bcol_ref, acol_ref, cw_ref, cb_ref, dskip_ref, ones_ref, sel_ref, rsel_ref,
                yf_ref, yb_ref, s_ref):
    c = pl.program_id(1)

    @pl.when(c == 0)
    def _():
        s_ref[...] = jnp.zeros_like(s_ref)

    lane = _iota2((CHUNK, LANES), 1)
    row768 = _iota2((CHUNK, 768), 0)
    ninf = jnp.float32(-jnp.inf)

    dt, a_cum = [], []
    for d in (0, 1):
        gt_ref = (gtf_ref, gtb_ref)[d]
        tr = jnp.where(_tri_masks(d)[1], 1.0, 0.0).astype(BF16)
        dt.append(_softplus(gt_ref[0, 16:24, :] + bcol_ref[16:24, :]))
        a_cum.append(_mm_exact_r(dt[d] * acol_ref[16:24, :], tr))

    yield
    xa = []
    for d in (0, 1):
        x_ref, xp_ref, xn_ref = ((xf_ref, xfp_ref, xfn_ref), (xb_ref, xbp_ref, xbn_ref))[d]
        j = c if d == 0 else _bwd_chunk(c, n_ctx_chunks, n_chunks)
        seg_first = jnp.logical_or(j == 0, j == n_ctx_chunks)
        seg_last = jnp.logical_or(j == n_ctx_chunks - 1, j == n_chunks - 1)
        x = x_ref[0]
        prev = jnp.where(seg_first, 0.0, xp_ref[0, 7:8, :])
        nxt = jnp.where(seg_last, 0.0, xn_ref[0, 0:1, :])
        x_dn = jnp.where(row768 == 0, prev, pltpu.roll(x, 1, 0))
        x_up = jnp.where(row768 == CHUNK - 1, nxt, pltpu.roll(x, CHUNK - 1, 0))
        xa.append(_silu(x_dn * cw_ref[0:1, :] + x * cw_ref[1:2, :] + x_up * cw_ref[2:3, :] + cb_ref[...]))

    heads = [None] * 8
    groups = {}
    for d in (0, 1):
        for g in range(2):
            x_pair = xa[d][:, 128 * g:128 * g + 128]
            b_g = xa[d][:, 256 + 128 * g:256 + 128 * g + 128]
            c_gb = xa[d][:, 512 + 128 * g:512 + 128 * g + 128].astype(BF16)
            groups[d, g] = dict(x_pair=x_pair, bt=b_g.T.astype(BF16), cb=_mm_nt(c_gb, b_g.astype(BF16)))
            for half in range(2):
                r = 4 * d + 2 * g + half
                hmask = (lane >= 64) if half else (lane < 64)
                st = s_ref[r]
                xm = jnp.where(hmask, x_pair, 0.0)
                heads[r] = dict(st=st, xm=xm, xmb=xm.astype(BF16), cs=_mm(c_gb, st.astype(BF16)))

    yield
    for d in (0, 1):
        last = CHUNK - 1 if d == 0 else 0
        a_last = a_cum[d][:, last:last + 1]
        w_in = jnp.exp(a_last - a_cum[d]) * dt[d]
        e_last = jnp.exp(jnp.broadcast_to(a_last, (8, CHUNK)))
        lhs, rhs = _spread_operands(a_cum[d], -a_cum[d], [jnp.exp(a_cum[d]), w_in], sel_ref, ones_ref)
        for h in range(N_HEADS):
            r = 4 * d + h
            heads[r]["mt"] = _mm(lhs, rhs * rsel_ref[r])
            heads[r]["e_last"] = e_last[r:r + 1, :]

    yield
    for r in range(8):
        d, h = divmod(r, 4)
        hd = heads[r]
        ds = _mm(groups[d, h // 2]["bt"], (hd["xm"] * hd["mt"][:, 256:384]).astype(BF16))
        s_ref[r] = hd["e_last"] * hd["st"] + ds
    for r in range(8):
        d, h = divmod(r, 4)
        hd = heads[r]
        vis, _ = _tri_masks(d)
        sc = groups[d, h // 2]["cb"] * jnp.exp(jnp.where(vis, hd["mt"][:, 0:128], ninf)) * dt[d][r:r + 1, :]
        hd["y"] = _mm(sc.astype(BF16), hd["xmb"]) + hd["mt"][:, 128:256] * hd["cs"]

    for d in (0, 1):
        out_ref = (yf_ref, yb_ref)[d]
        for g in range(2):
            y_pair = heads[4 * d + 2 * g]["y"] + heads[4 * d + 2 * g + 1]["y"]
            if d == 0:
                y_pair = y_pair + dskip_ref[:, 128 * g:128 * g + 128] * groups[d, g]["x_pair"]
            out_ref[0, :, 128 * g:128 * g + 128] = y_pair.astype(BF16)


def _ssd(ss, gt, bcol, acol, conv_w, conv_b, dskip, n_ctx_chunks):
    bsz, t, _ = ss.shape
    nc = t // CHUNK
    sub = CHUNK // 8
    nsub = t // 8
    bc = lambda c: _bwd_chunk(c, n_ctx_chunks, nc)
    fwd = lambda b, c: (b, c, 0)
    bwd = lambda b, c: (b, bc(c), 0)
    fwd_p = lambda b, c: (b, jnp.maximum(c * sub - 1, 0), 0)
    fwd_n = lambda b, c: (b, jnp.minimum((c + 1) * sub, nsub - 1), 0)
    bwd_p = lambda b, c: (b, jnp.maximum(bc(c) * sub - 1, 0), 0)
    bwd_n = lambda b, c: (b, jnp.minimum((bc(c) + 1) * sub, nsub - 1), 0)
    fwd_t = lambda b, c: (b, 0, c)
    bwd_t = lambda b, c: (b, 0, bc(c))
    const2 = lambda b, c: (0, 0)
    out = jax.ShapeDtypeStruct((bsz, t, W_BRANCH), BF16)
    return dict(
        steps=functools.partial(_ssd_steps, n_ctx_chunks, nc),
        in_specs=[pl.BlockSpec((1, CHUNK, 768), fwd), pl.BlockSpec((1, 8, 768), fwd_p),
                  pl.BlockSpec((1, 8, 768), fwd_n),
                  pl.BlockSpec((1, CHUNK, 768), bwd), pl.BlockSpec((1, 8, 768), bwd_p),
                  pl.BlockSpec((1, 8, 768), bwd_n),
                  pl.BlockSpec((1, LANES, CHUNK), fwd_t), pl.BlockSpec((1, LANES, CHUNK), bwd_t),
                  pl.BlockSpec((LANES, 1), const2), pl.BlockSpec((LANES, 1), const2),
                  pl.BlockSpec((3, 768), const2), pl.BlockSpec((1, 768), const2),
                  pl.BlockSpec((1, W_BRANCH), const2),
                  pl.BlockSpec((CHUNK, LANES), const2), pl.BlockSpec((CHUNK, 2 * LANES), const2),
                  pl.BlockSpec((8, CHUNK, 3 * LANES), lambda b, c: (0, 0, 0))],
        out_specs=[pl.BlockSpec((1, CHUNK, W_BRANCH), fwd), pl.BlockSpec((1, CHUNK, W_BRANCH), bwd)],
        out_shape=[out, out],
        scratch_shapes=[pltpu.VMEM((8, SSM_N, LANES), F32)],
        args=(ss, ss, ss, ss, ss, ss, gt, gt, bcol, acol, conv_w, conv_b, dskip,
              jnp.ones((CHUNK, LANES), BF16), _spread_selector(2), _state_row_selector(3)))


def _hgrn2_steps(hf_ref, vtf_ref, hb_ref, vtb_ref, loglb_ref, log1m_ref, onem_ref, e_ref, of_ref, ob_ref,
                  st_ref):
    @pl.when(pl.program_id(1) == 0)
    def _():
        st_ref[...] = jnp.zeros_like(st_ref)

    ri = _iota2((CHUNK, CHUNK), 0)
    ci = _iota2((CHUNK, CHUNK), 1)
    lane = _iota2((CHUNK, LANES), 1)
    rw = _iota2((CHUNK, W_BRANCH), 0)
    blockdiag = (ri >> 6) == (ci >> 6)
    ninf = jnp.float32(-jnp.inf)
    nb = CHUNK // SUB
    rs = _iota2((nb, SUB, W_BRANCH), 1)

    dirs = []
    for d in (0, 1):
        h_ref = (hf_ref, hb_ref)[d]
        tc = jnp.where(_tri_masks(d)[0], 1.0, 0.0).astype(BF16)
        z = h_ref[0, :, W_BRANCH * (1 + d):W_BRANCH * (2 + d)]
        e = jnp.exp(-jnp.abs(z))
        ope = 1.0 + e
        la = loglb_ref[d:d + 1, :]
        lb_ = log1m_ref[d:d + 1, :] + (jnp.minimum(z, 0.0) - jnp.log(ope))
        logf = jnp.maximum(la, lb_) + jnp.log(1.0 + jnp.exp(-jnp.abs(la - lb_)))
        dirs.append(dict(
            q=_silu(h_ref[0, :, 0:W_BRANCH]),
            v=h_ref[0, :, 3 * W_BRANCH:4 * W_BRANCH],
            kk=onem_ref[d:d + 1, :] * (jnp.where(z >= 0.0, e, 1.0) / ope),
            gcum=_mm_exact_l(tc, logf)))

    yield
    for d in (0, 1):
        dd = dirs[d]
        q, kk, gcum = dd["q"], dd["kk"], dd["gcum"]
        vt_ref = (vtf_ref, vtb_ref)[d]
        last = CHUNK - 1 if d == 0 else 0
        g_last = gcum[last:last + 1, :]
        qg = (q * jnp.exp(gcum)).astype(BF16)
        kg = (kk * jnp.exp(g_last - gcum)).astype(BF16)

        dd["o_inter"] = []
        for p in range(2):
            sl = slice(128 * p, 128 * p + 128)
            st = st_ref[2 * d + p]
            dd["o_inter"].append(_mm_nt(qg[:, sl], st.astype(BF16)))
            dst = _mm(vt_ref[0, sl, :].astype(BF16), kg[:, sl])
            st_ref[2 * d + p] = st * jnp.exp(g_last[:, sl]) + jnp.where(blockdiag, dst, 0.0)

        a_mats = [jnp.zeros((CHUNK, CHUNK), F32) for _ in range(N_HEADS)]
        blk = CHUNK // 2
        while blk >= SUB:
            first = (rw & (2 * blk - 1)) < blk
            edge = (blk - 1) if d == 0 else blk
            gb = gcum.reshape(CHUNK // (2 * blk), 2 * blk, W_BRANCH)[:, edge:edge + 1, :]
            gb = jnp.broadcast_to(gb, (CHUNK // (2 * blk), 2 * blk, W_BRANCH)).reshape(CHUNK, W_BRANCH)
            q_side = jnp.logical_not(first) if d == 0 else first
            qt = q * jnp.exp(jnp.where(q_side, gcum - gb, ninf))
            kt = (kk * jnp.exp(jnp.where(q_side, ninf, gb - gcum))).astype(BF16)
            same = (ri >> int(math.log2(2 * blk))) == (ci >> int(math.log2(2 * blk)))
            for h in range(N_HEADS):
                p, half = divmod(h, 2)
                hmask = (lane >= 64) if half else (lane < 64)
                qh = jnp.where(hmask, qt[:, 128 * p:128 * p + 128], 0.0).astype(BF16)
                a_mats[h] = a_mats[h] + jnp.where(same, _mm_nt(qh, kt[:, 128 * p:128 * p + 128]), 0.0)
            blk //= 2
        dd["a_mats"] = a_mats

        g3 = gcum.reshape(nb, SUB, W_BRANCH)
        q3 = q.reshape(nb, SUB, W_BRANCH)
        k3 = kk.reshape(nb, SUB, W_BRANCH)
        v3 = dd["v"].reshape(nb, SUB, W_BRANCH)
        o3 = None
        for j in range(SUB):
            ok = (rs >= j) if d == 0 else (rs <= j)
            pj = q3 * jnp.exp(jnp.where(ok, g3 - g3[:, j:j + 1, :], ninf)) * k3[:, j:j + 1, :]
            red = _mm(pj.reshape(CHUNK, W_BRANCH).astype(BF16), e_ref[...])
            term = red.reshape(nb, SUB, W_BRANCH) * v3[:, j:j + 1, :]
            o3 = term if o3 is None else o3 + term
        dd["o_diag"] = o3.reshape(CHUNK, W_BRANCH)

    yield
    for d in (0, 1):
        dd = dirs[d]
        out_ref = (of_ref, ob_ref)[d]
        for p in range(2):
            sl = slice(128 * p, 128 * p + 128)
            a_cat = jnp.concatenate([dd["a_mats"][2 * p], dd["a_mats"][2 * p + 1]], axis=1).astype(BF16)
            v_p = dd["v"][:, sl]
            v_cat = jnp.concatenate([jnp.where(lane < 64, v_p, 0.0), jnp.where(lane >= 64, v_p, 0.0)],
                                    axis=0).astype(BF16)
            out_ref[0, :, sl] = (dd["o_inter"][p] + _mm(a_cat, v_cat) + dd["o_diag"][:, sl]).astype(BF16)


def _hgrn2(hg, hvt, loglb, log1m, onem, e64, n_ctx_chunks):
    bsz, t, _ = hg.shape
    nc = t // CHUNK
    fwd = lambda b, c: (b, c, 0)
    bwd = lambda b, c: (b, _bwd_chunk(c, n_ctx_chunks, nc), 0)
    fwd_t = lambda b, c: (b, 0, c)
    bwd_t = lambda b, c: (b, 0, _bwd_chunk(c, n_ctx_chunks, nc))
    const2 = lambda b, c: (0, 0)
    out = jax.ShapeDtypeStruct((bsz, t, W_BRANCH), BF16)
    return dict(
        steps=_hgrn2_steps,
        in_specs=[pl.BlockSpec((1, CHUNK, 1024), fwd), pl.BlockSpec((1, W_BRANCH, CHUNK), fwd_t),
                  pl.BlockSpec((1, CHUNK, 1024), bwd), pl.BlockSpec((1, W_BRANCH, CHUNK), bwd_t),
                  pl.BlockSpec((2, W_BRANCH), const2), pl.BlockSpec((2, W_BRANCH), const2),
                  pl.BlockSpec((2, W_BRANCH), const2), pl.BlockSpec((W_BRANCH, W_BRANCH), const2)],
        out_specs=[pl.BlockSpec((1, CHUNK, W_BRANCH), fwd), pl.BlockSpec((1, CHUNK, W_BRANCH), bwd)],
        out_shape=[out, out],
        scratch_shapes=[pltpu.VMEM((4, LANES, LANES), F32)],
        args=(hg, hvt, hg, hvt, loglb, log1m, onem, e64))


def _scan_kernel(parts, *refs):
    n_in = [len(p["in_specs"]) for p in parts]
    n_out = [len(p["out_specs"]) for p in parts]
    n_scr = [len(p["scratch_shapes"]) for p in parts]
    ins, outs, scr = refs[:sum(n_in)], refs[sum(n_in):sum(n_in) + sum(n_out)], refs[sum(n_in) + sum(n_out):]
    gens = []
    for k, p in enumerate(parts):
        mine = (ins[sum(n_in[:k]):sum(n_in[:k + 1])] + outs[sum(n_out[:k]):sum(n_out[:k + 1])]
                + scr[sum(n_scr[:k]):sum(n_scr[:k + 1])])
        gens.append(p["steps"](*mine))
    while gens:
        for g in list(gens):
            if next(g, gens) is gens:
                gens.remove(g)


def _chunk_scans(parts, bsz, n_chunks):
    return pl.pallas_call(
        functools.partial(_scan_kernel, parts),
        grid=(bsz, n_chunks),
        in_specs=[s for p in parts for s in p["in_specs"]],
        out_specs=[s for p in parts for s in p["out_specs"]],
        out_shape=[s for p in parts for s in p["out_shape"]],
        scratch_shapes=[s for p in parts for s in p["scratch_shapes"]],
        compiler_params=_cparams(("parallel", "arbitrary")),
        name="chunk_scans",
    )(*[a for p in parts for a in p["args"]])


def _attn(daq, kt, vx, lam_vecs, lam_init, n_ctx_tiles, skip_ctx):
    bsz, t, _ = daq.shape
    tq = ROW_TILE
    n_k_tiles = t // KEY_TILE
    n_lat_tiles = n_k_tiles - n_ctx_tiles
    group = math.gcd(KEY_GROUP, n_lat_tiles)
    q_off = n_ctx_tiles if skip_ctx else 0
    nq = t // tq - q_off
    n_maps = W_BRANCH // DA_QK
    rows = n_maps * tq

    def kern(q_ref, kt_ref, vx_ref, lv_ref, o_ref, q8_ref, m_ref, acc_ref, s_ref, mx_ref):
        qi = pl.program_id(1) + q_off
        lane = _iota2((tq, W_BRANCH), 1)
        q = q_ref[0]
        for j in range(n_maps):
            q8_ref[j * tq:(j + 1) * tq, :] = jnp.where((lane >> 5) == j, q, jnp.zeros_like(q))
        m_ref[...] = jnp.full_like(m_ref, -jnp.inf)
        acc_ref[...] = jnp.zeros_like(acc_ref)

        def update(t0, n, after_scores=None):
            q8 = q8_ref[...]
            s = [_mm(q8, kt_ref[0, t0 + i]) for i in range(n)]
            if after_scores is not None:
                after_scores()
            mx = s[0][:, 0:LANES]
            for i in range(n):
                for c in range(KEY_TILE // LANES):
                    if i or c:
                        mx = jnp.maximum(mx, s[i][:, LANES * c:LANES * (c + 1)])
            m_old = m_ref[...]
            m_new = jnp.maximum(m_old, jnp.max(mx, axis=1, keepdims=True))
            alpha = jnp.exp(m_old - m_new)
            m_ref[...] = m_new
            m2 = jnp.concatenate([m_new] * (KEY_TILE // LANES), axis=1)
            p = [jnp.exp(s[i] - m2).astype(BF16) for i in range(n)]
            for hd in range(N_HEADS):
                r0 = 2 * hd * tq
                pv = _mm(p[0][r0:r0 + 2 * tq], vx_ref[0, t0, hd])
                for i in range(1, n):
                    pv = pv + _mm(p[i][r0:r0 + 2 * tq], vx_ref[0, t0 + i, hd])
                acc_ref[r0:r0 + 2 * tq, :] = alpha[r0:r0 + 2 * tq] * acc_ref[r0:r0 + 2 * tq, :] + pv

        def scores(g, slot):
            q8 = q8_ref[...]
            t0 = n_ctx_tiles + g * group
            mx = None
            for i in range(group):
                s = _mm(q8, kt_ref[0, t0 + i])
                s_ref[slot, :, KEY_TILE * i:KEY_TILE * (i + 1)] = s
                for c in range(KEY_TILE // LANES):
                    part = s[:, LANES * c:LANES * (c + 1)]
                    mx = part if mx is None else jnp.maximum(mx, part)
            mx_ref[slot] = mx

        def absorb(g, slot):
            t0 = n_ctx_tiles + g * group
            m_old = m_ref[...]
            m_new = jnp.maximum(m_old, jnp.max(mx_ref[slot], axis=1, keepdims=True))
            alpha = jnp.exp(m_old - m_new)
            m_ref[...] = m_new
            m2 = jnp.concatenate([m_new] * (KEY_TILE // LANES), axis=1)
            p = [jnp.exp(s_ref[slot, :, KEY_TILE * i:KEY_TILE * (i + 1)] - m2).astype(BF16) for i in range(group)]
            for hd in range(N_HEADS):
                r0 = 2 * hd * tq
                pv = _mm(p[0][r0:r0 + 2 * tq], vx_ref[0, t0, hd])
                for i in range(1, group):
                    pv = pv + _mm(p[i][r0:r0 + 2 * tq], vx_ref[0, t0 + i, hd])
                acc_ref[r0:r0 + 2 * tq, :] = alpha[r0:r0 + 2 * tq] * acc_ref[r0:r0 + 2 * tq, :] + pv

        @pl.when(qi < n_ctx_tiles)
        def _():
            update(0, n_ctx_tiles)

        @pl.when(qi >= n_ctx_tiles)
        def _():
            n_groups = n_lat_tiles // group
            n_pairs = (n_groups - 1) // 2
            update(0, n_ctx_tiles, after_scores=lambda: scores(0, 0))

            def body(k, carry):
                g = 2 * k
                scores(g + 1, 1)
                absorb(g, 0)
                scores(g + 2, 0)
                absorb(g + 1, 1)
                return carry
            lax.fori_loop(0, n_pairs, body, 0)
            g0 = 2 * n_pairs
            if n_groups - g0 == 2:
                scores(g0 + 1, 1)
            absorb(g0, 0)
            if n_groups - g0 == 2:
                absorb(g0 + 1, 1)

        lv = lv_ref[...]
        lam = (jnp.exp(jnp.sum(lv[0:1, :] * lv[1:2, :], axis=1, keepdims=True))
               - jnp.exp(jnp.sum(lv[2:3, :] * lv[3:4, :], axis=1, keepdims=True)) + lam_init)
        lane1 = _iota2((tq, LANES), 1)
        for pr in range(N_HEADS // 2):
            halves = []
            for half in range(2):
                r0 = 2 * (2 * pr + half) * tq
                one = 0 if half else HEAD_W
                a0 = acc_ref[r0:r0 + tq, :]
                a1 = acc_ref[r0 + tq:r0 + 2 * tq, :]
                halves.append(a0 / a0[:, one:one + 1] - lam * (a1 / a1[:, one:one + 1]))
            o_ref[0, :, LANES * pr:LANES * (pr + 1)] = jnp.where(lane1 < HEAD_W, halves[0], halves[1]).astype(BF16)

    return pl.pallas_call(
        kern,
        grid=(bsz, nq),
        in_specs=[pl.BlockSpec((1, tq, W_BRANCH), lambda b, i: (b, i + q_off, 0)),
                  pl.BlockSpec((1, n_k_tiles, W_BRANCH, KEY_TILE), lambda b, i: (b, 0, 0, 0),
                               pipeline_mode=pl.Buffered(1)),
                  pl.BlockSpec((1, n_k_tiles, N_HEADS, KEY_TILE, LANES), lambda b, i: (b, 0, 0, 0, 0),
                               pipeline_mode=pl.Buffered(1)),
                  pl.BlockSpec(lam_vecs.shape, lambda b, i: (0, 0))],
        out_specs=pl.BlockSpec((1, tq, W_BRANCH), lambda b, i: (b, i, 0)),
        out_shape=jax.ShapeDtypeStruct((bsz, nq * tq, W_BRANCH), BF16),
        scratch_shapes=[pltpu.VMEM((rows, W_BRANCH), BF16),
                        pltpu.VMEM((rows, LANES), F32),
                        pltpu.VMEM((rows, LANES), F32),
                        pltpu.VMEM((2, rows, group * KEY_TILE), F32),
                        pltpu.VMEM((2, rows, LANES), F32)],
        compiler_params=_cparams(("parallel", "parallel")),
        name="diff_attn",
    )(daq, kt, vx, lam_vecs)


def _seg_mean(x, e_ref):
    x0 = x.astype(BF16)
    x1 = (x - x0.astype(F32)).astype(BF16)
    return (_mm(x0, e_ref[...]) + _mm(x1, e_ref[...])) * (1.0 / HEAD_W)


def _out_kernel(n_ctx_tiles, ctx_row, q_off, lam_init, final,
                c_ref, x_ref, mo_ref, mlo_ref, mlz_ref, mhf_ref, mhb_ref, dao_ref, daz_ref,
                syf_ref, syb_ref, ssz_ref, hof_ref, hob_ref, hgz_ref,
                mlg_ref, dag_ref, ssg_ref, hgg_ref, e_ref, w_ref, fg_ref, o_ref):
    b = pl.program_id(0)
    i = pl.program_id(1) + q_off
    d = x_ref.shape[2]
    r = jnp.where(i < n_ctx_tiles, ctx_row, b)
    gate = mo_ref[pl.ds(r, 1), :][:, 2 * d:3 * d]

    u = _sigmoid(mlo_ref[0].astype(F32)) * (mhf_ref[0].astype(F32) + mhb_ref[0].astype(F32))
    dev = u - _seg_mean(u, e_ref)
    y_ml = dev * lax.rsqrt(_seg_mean(dev * dev, e_ref) + EPS) * mlg_ref[...] * _silu(mlz_ref[0].astype(F32))

    o = dao_ref[0].astype(F32)
    y_da = (o * lax.rsqrt(_seg_mean(o * o, e_ref) + EPS) * dag_ref[...]) * (1.0 - lam_init) * _silu(daz_ref[0].astype(F32))

    ys = (syf_ref[0].astype(F32) + syb_ref[0].astype(F32)) * _silu(ssz_ref[0].astype(F32))
    parts = []
    for g in range(2):
        yg = ys[:, 128 * g:128 * g + 128]
        parts.append(yg * lax.rsqrt(jnp.mean(yg * yg, axis=-1, keepdims=True) + EPS))
    y_ss = jnp.concatenate(parts, axis=1) * ssg_ref[...]

    oh = hof_ref[0].astype(F32) + hob_ref[0].astype(F32)
    y_hg = (oh * lax.rsqrt(_seg_mean(oh * oh, e_ref) + EPS) * hgg_ref[...]) * _silu(hgz_ref[0].astype(F32))

    acc = _mm(y_ml.astype(BF16), w_ref[0:W_BRANCH, :])
    acc = acc + _mm(y_da.astype(BF16), w_ref[W_BRANCH:2 * W_BRANCH, :])
    acc = acc + _mm(y_ss.astype(BF16), w_ref[2 * W_BRANCH:3 * W_BRANCH, :])
    acc = acc + _mm(y_hg.astype(BF16), w_ref[3 * W_BRANCH:4 * W_BRANCH, :])
    x_new = jnp.where(i < n_ctx_tiles, c_ref[0], x_ref[0]) + gate * acc
    if final:
        x_new = x_new * lax.rsqrt(jnp.mean(x_new * x_new, axis=-1, keepdims=True) + EPS) * fg_ref[...]
    o_ref[0] = x_new


def _out_proj(stream, mo, zg, mhf, mhb, dao, syf, syb, hof, hob,
              mlg, dag, ssg, hgg, e64, w_out, final_g, lam_init, n_ctx_tiles, ctx_row, final):
    c_src, x_src, x_off, t = stream
    bsz, _, d = x_src.shape
    tm = ROW_TILE
    q_off = n_ctx_tiles if final else 0
    nrow = t // tm - q_off
    row = lambda b, i: (b, i + q_off, 0)
    col = lambda k: (lambda b, i: (b, i + q_off, k))
    const2 = lambda b, i: (0, 0)
    wb = pl.BlockSpec((1, tm, W_BRANCH), row)
    return pl.pallas_call(
        functools.partial(_out_kernel, n_ctx_tiles, ctx_row, q_off, lam_init, final),
        grid=(bsz, nrow),
        in_specs=_stream_specs(stream, n_ctx_tiles, q_off) + [
                  pl.BlockSpec(mo.shape, const2),
                  pl.BlockSpec((1, tm, W_BRANCH), col(0)), pl.BlockSpec((1, tm, W_BRANCH), col(1)),
                  wb, wb,
                  pl.BlockSpec((1, tm, W_BRANCH), lambda b, i: (b, i, 0)) if final else wb,
                  pl.BlockSpec((1, tm, W_BRANCH), col(2)),
                  wb, wb, pl.BlockSpec((1, tm, W_BRANCH), col(3)),
                  wb, wb, pl.BlockSpec((1, tm, W_BRANCH), col(4)),
                  pl.BlockSpec((1, W_BRANCH), const2), pl.BlockSpec((1, W_BRANCH), const2),
                  pl.BlockSpec((1, W_BRANCH), const2), pl.BlockSpec((1, W_BRANCH), const2),
                  pl.BlockSpec((W_BRANCH, W_BRANCH), const2),
                  pl.BlockSpec(w_out.shape, const2),
                  pl.BlockSpec((1, d), const2)],
        out_specs=pl.BlockSpec((1, tm, d), lambda b, i: (b, i, 0)),
        out_shape=jax.ShapeDtypeStruct((bsz, nrow * tm, d), F32),
        compiler_params=_cparams(("parallel", "parallel")),
        name="out_proj",
    )(c_src, x_src, mo, zg, zg, mhf, mhb, dao, zg, syf, syb, zg, hof, hob, zg,
      mlg, dag, ssg, hgg, e64, w_out, final_g.reshape(1, d))


def _relayout_w_in(w_in):
    d = w_in.shape[0]
    o = {}
    off = 0
    for name, n in (("ml_q", 256), ("ml_k", 256), ("ml_v", 256), ("ml_o", 256), ("ml_i", 8), ("ml_f", 8),
                    ("ml_z", 256), ("da_q", 256), ("da_k", 256), ("da_v", 256), ("da_z", 256),
                    ("ss_xbc", 768), ("ss_dt", 8), ("ss_z", 256),
                    ("hg_q", 256), ("hg_f", 512), ("hg_i", 256), ("hg_z", 256)):
        o[name] = w_in[:, off:off + n]
        off += n

    gates = jnp.concatenate([o["ml_i"], o["ml_f"], o["ss_dt"], jnp.zeros((d, LANES - 24), w_in.dtype)], axis=1)
    zpad = jnp.zeros((d, HEAD_W), w_in.dtype)
    v_ext = []
    for hd in range(N_HEADS):
        v_h = o["da_v"][:, HEAD_W * hd:HEAD_W * (hd + 1)]
        v_ext += [zpad, v_h] if hd % 2 else [v_h, zpad]
    w2 = jnp.concatenate([o["ml_q"], o["ml_v"], o["da_q"]] + v_ext + [
                          o["ss_xbc"], o["hg_q"], o["hg_f"], o["hg_i"],
                          o["ml_o"], o["ml_z"], o["da_z"], o["ss_z"], o["hg_z"]], axis=1)
    wt = jnp.concatenate([o["da_k"], gates, o["ml_k"], o["hg_i"]], axis=1).T
    return w2.astype(BF16), wt.astype(BF16)


def _rope_tables(n_ctx, seq):
    pos = jnp.arange(seq)
    rows = (pos // GRID_W).astype(F32)
    cols = (pos % GRID_W).astype(F32)
    axis = DA_QK // 2
    inv = ROPE_BASE ** (-jnp.arange(0, axis, 2, dtype=F32) / axis)
    ang = jnp.concatenate([rows[:, None] * inv, cols[:, None] * inv], axis=-1)
    cos = jnp.concatenate([jnp.ones((n_ctx, axis), F32), jnp.cos(ang)], axis=0)
    sin = jnp.concatenate([jnp.zeros((n_ctx, axis), F32), jnp.sin(ang)], axis=0)
    reps = W_BRANCH // axis
    return jnp.tile(cos, (1, reps)), jnp.tile(sin, (1, reps))


def _gate_col(vals):
    v = jnp.concatenate([vals.astype(F32), jnp.zeros((LANES - vals.shape[0],), F32)])
    return v.reshape(LANES, 1)


def kernel(x, c, ctx, c_ctx, w_mod, b_mod, norm_g, w_in, w_out, ml_gate_b, ml_norm_g, da_lambda, da_norm_g,
           ss_conv_w, ss_conv_b, ss_dt_bias, ss_a_log, ss_d, ss_norm_g, hg_lower, hg_norm_g, final_g):
    bsz, seq, d = x.shape
    n_ctx = ctx.shape[1]
    depth = w_mod.shape[0]
    assert n_ctx % ROW_TILE == 0 and seq % ROW_TILE == 0 and bsz < 8
    n_ctx_tiles = n_ctx // ROW_TILE
    n_ctx_chunks = n_ctx // CHUNK
    ctx_row = bsz

    stream = (ctx, x, 0, n_ctx + seq)
    cc = jnp.concatenate([c, c_ctx[None, :], jnp.zeros((8 - bsz - 1, d), F32)], axis=0)
    cos, sin = _rope_tables(n_ctx, seq)
    cos_t, sin_t = cos.T, sin.T
    one_lanes = jnp.arange(V_EXT) % LANES == jnp.where((jnp.arange(V_EXT) // LANES) % 2 == 1, 0, HEAD_W)
    ones_row = one_lanes.astype(F32).reshape(1, V_EXT)
    lb_all = jnp.cumsum(jax.nn.softmax(hg_lower.astype(F32), axis=1), axis=1)
    lb_all = lb_all - lb_all[:, :1]
    hid = _iota2((W_BRANCH, W_BRANCH), 0) // HEAD_W
    e64 = (hid == hid.T).astype(BF16)

    out = None
    for l in range(depth):
        lam_init = 0.8 - 0.6 * math.exp(-0.3 * l)
        final = l == depth - 1
        w2, wt = _relayout_w_in(w_in[l])
        mo = _modulation(cc, w_mod[l], b_mod[l])
        ml, daq, dak, dav, ss, hg, zg, gt, mkt, hvt = _in_proj(stream, mo, norm_g[l], cos, sin, cos_t, sin_t, w2, wt,
                                                          ones_row, n_ctx_tiles, ctx_row)

        gb = ml_gate_b[l]
        zeros8 = jnp.zeros((8,), F32)
        bcol = _gate_col(jnp.concatenate([gb[:, 0].reshape(-1), gb[:, 1].reshape(-1),
                                                ss_dt_bias[l].reshape(-1)]))
        acol = _gate_col(jnp.concatenate([zeros8, zeros8, -jnp.exp(ss_a_log[l].astype(F32)).reshape(-1)]))
        dao = _attn(daq, dak, dav, da_lambda[l].astype(F32), lam_init, n_ctx_tiles, final)
        dskip = jnp.repeat(ss_d[l].astype(F32), HEAD_W).reshape(1, W_BRANCH)
        lbh = lb_all[:, l]
        scans = [_mlstm(ml, mkt, gt, bcol, n_ctx_chunks),
                 _ssd(ss, gt, bcol, acol, ss_conv_w[l], ss_conv_b[l].reshape(1, -1), dskip, n_ctx_chunks),
                 _hgrn2(hg, hvt, jnp.log(lbh), jnp.log1p(-lbh), 1.0 - lbh, e64, n_ctx_chunks)]
        mhf, mhb, syf, syb, hof, hob = _chunk_scans(scans, bsz, (n_ctx + seq) // CHUNK)

        res = _out_proj(stream, mo, zg, mhf, mhb, dao, syf, syb, hof, hob,
                        ml_norm_g[l].reshape(1, -1), jnp.tile(da_norm_g[l], N_HEADS).reshape(1, -1),
                        ss_norm_g[l].reshape(1, -1), hg_norm_g[l].reshape(1, -1), e64,
                        w_out[l].astype(BF16), final_g, lam_init, n_ctx_tiles, ctx_row, final)
        if final:
            out = res
        else:
            stream = (res, res, n_ctx_tiles, n_ctx + seq)
    return out
```

```python
import functools
import math

import jax
import jax.numpy as jnp
from jax import lax
from jax.experimental import pallas as pl
from jax.experimental.pallas import tpu as pltpu

F32 = jnp.float32
BF16 = jnp.bfloat16

EPS = 1e-6
GRID_W = 64
ROPE_BASE = 10000.0
N_HEADS = 4
HEAD_W = 64
W_BRANCH = N_HEADS * HEAD_W
DA_QK = 32
SSM_N = 128
CHUNK = 128
ROW_TILE = 256
SUB = 8
LANES = 128
VMEM_LIMIT = 56 * 1024 * 1024

C_ML = 0
C_DAQ = 512
C_DAV = 768
C_SS = 1024
C_HG = 1792
C_ZG = 2816
C_END = 4096
GATE_ROWS = 32
KEY_TILE = 256
KEY_GROUP = 4

NT = (((1,), (1,)), ((), ()))
TN = (((0,), (0,)), ((), ()))


def _mm(a, b):
    return jnp.dot(a, b, preferred_element_type=F32)


def _mm_nt(a, b):
    return lax.dot_general(a, b, NT, preferred_element_type=F32)


def _mm_tn(a, b):
    return lax.dot_general(a, b, TN, preferred_element_type=F32)


def _split3(x):
    x0 = x.astype(BF16)
    r = x - x0.astype(F32)
    x1 = r.astype(BF16)
    x2 = (r - x1.astype(F32)).astype(BF16)
    return x0, x1, x2


def _mm_exact_l(t, x):
    x0, x1, x2 = _split3(x)
    return _mm(t, x0) + _mm(t, x1) + _mm(t, x2)


def _mm_exact_r(x, t):
    x0, x1, x2 = _split3(x)
    return _mm(x0, t) + _mm(x1, t) + _mm(x2, t)


def _sigmoid(x):
    return 1.0 / (1.0 + jnp.exp(-x))


def _silu(x):
    return x * _sigmoid(x)


def _log_sigmoid(x):
    return jnp.minimum(x, 0.0) - jnp.log(1.0 + jnp.exp(-jnp.abs(x)))


def _softplus(x):
    return jnp.maximum(x, 0.0) + jnp.log(1.0 + jnp.exp(-jnp.abs(x)))


def _iota2(shape, axis):
    return lax.broadcasted_iota(jnp.int32, shape, axis)


def _cparams(sem):
    return pltpu.CompilerParams(dimension_semantics=sem, vmem_limit_bytes=VMEM_LIMIT)


def _mod_kernel(c_ref, w_ref, b_ref, o_ref):
    c = c_ref[...]
    o_ref[...] = jnp.dot(_silu(c), w_ref[...], precision=lax.Precision.HIGHEST,
                         preferred_element_type=F32) + b_ref[...]


def _modulation(cc, w_mod, b_mod):
    d = cc.shape[1]
    n = w_mod.shape[1]
    tn = 512
    return pl.pallas_call(
        _mod_kernel,
        grid=(n // tn,),
        in_specs=[pl.BlockSpec((8, d), lambda j: (0, 0)),
                  pl.BlockSpec((d, tn), lambda j: (0, j)),
                  pl.BlockSpec((1, tn), lambda j: (0, j))],
        out_specs=pl.BlockSpec((8, tn), lambda j: (0, j)),
        out_shape=jax.ShapeDtypeStruct((8, n), F32),
        compiler_params=_cparams(("arbitrary",)),
        name="modulation",
    )(cc, w_mod, b_mod.reshape(1, n))


def _stream_specs(stream, n_ctx_tiles, first_tile):
    _, x_src, x_off, _ = stream
    d = x_src.shape[2]
    return [pl.BlockSpec((1, ROW_TILE, d), lambda b, i: (b, jnp.minimum(i + first_tile, n_ctx_tiles - 1), 0)),
            pl.BlockSpec((1, ROW_TILE, d), lambda b, i: (b, jnp.maximum(i + first_tile - n_ctx_tiles, 0) + x_off, 0))]


def _in_kernel(n_ctx_tiles, ctx_row, c_ref, x_ref, mo_ref, g_ref, cos_ref, sin_ref, cost_ref, sint_ref,
               w_ref, wt_ref,
               ml_ref, daq_ref, kt_ref, vx_ref, ss_ref, hg_ref, zg_ref, gt_ref, mkt_ref):
    b = pl.program_id(0)
    i = pl.program_id(1)
    d = x_ref.shape[2]
    r = jnp.where(i < n_ctx_tiles, ctx_row, b)
    mo = mo_ref[pl.ds(r, 1), :]
    shift = mo[:, 0:d]
    scale = mo[:, d:2 * d]
    x = jnp.where(i < n_ctx_tiles, c_ref[0], x_ref[0])
    y = x * lax.rsqrt(jnp.mean(x * x, axis=-1, keepdims=True) + EPS) * g_ref[...]
    h = (y * (1.0 + scale) + shift).astype(BF16)

    ml_ref[0] = _mm(h, w_ref[:, C_ML:C_DAQ])
    q = _mm(h, w_ref[:, C_DAQ:C_DAV])
    half = DA_QK // 2
    lane = _iota2((x.shape[0], LANES), 1)
    q_rot = []
    for c in range(W_BRANCH // LANES):
        qc = q[:, LANES * c:LANES * (c + 1)]
        q_rot.append(jnp.where((lane & half) == 0, -pltpu.roll(qc, LANES - half, 1), pltpu.roll(qc, half, 1)))
    q = q * cos_ref[...] + jnp.concatenate(q_rot, axis=1) * sin_ref[...]
    daq_ref[0] = (q * (DA_QK ** -0.5)).astype(BF16)
    v = _mm(h, w_ref[:, C_DAV:C_SS])
    for hd in range(N_HEADS):
        pair = v[:, LANES * (hd // 2):LANES * (hd // 2 + 1)]
        own = (lane >= HEAD_W) if hd % 2 else (lane < HEAD_W)
        vx_ref[0, 0, hd] = jnp.where(own, pair, jnp.where(lane == (0 if hd % 2 else HEAD_W), 1.0, 0.0)).astype(BF16)
    ss_ref[0] = _mm(h, w_ref[:, C_SS:C_HG])
    hg_ref[0] = _mm(h, w_ref[:, C_HG:C_ZG])
    zg_ref[0] = _mm(h, w_ref[:, C_ZG:C_END]).astype(BF16)
    tr = _mm_nt(wt_ref[...], h)
    kt = tr[0:W_BRANCH]
    kt_rot = []
    for j in range(W_BRANCH // DA_QK):
        kt_rot += [-kt[DA_QK * j + half:DA_QK * (j + 1)], kt[DA_QK * j:DA_QK * j + half]]
    kt_ref[0, 0] = (kt * cost_ref[...] + jnp.concatenate(kt_rot, axis=0) * sint_ref[...]).astype(BF16)
    gt_ref[0] = tr[W_BRANCH:W_BRANCH + GATE_ROWS]
    mkt_ref[0] = tr[W_BRANCH + GATE_ROWS:2 * W_BRANCH + GATE_ROWS]


def _in_proj(stream, mo, g, cos, sin, cos_t, sin_t, w2, wt, n_ctx_tiles, ctx_row):
    c_src, x_src, x_off, t = stream
    bsz, _, d = x_src.shape
    tm = ROW_TILE
    assert tm == KEY_TILE
    row = lambda b, i: (b, i, 0)
    const2 = lambda b, i: (0, 0)
    out_shape = [
        jax.ShapeDtypeStruct((bsz, t, 2 * W_BRANCH), F32),
        jax.ShapeDtypeStruct((bsz, t, W_BRANCH), BF16),
        jax.ShapeDtypeStruct((bsz, t // tm, W_BRANCH, tm), BF16),
        jax.ShapeDtypeStruct((bsz, t // tm, N_HEADS, tm, LANES), BF16),
        jax.ShapeDtypeStruct((bsz, t, 768), F32),
        jax.ShapeDtypeStruct((bsz, t, 4 * W_BRANCH), F32),
        jax.ShapeDtypeStruct((bsz, t, 5 * W_BRANCH), BF16),
        jax.ShapeDtypeStruct((bsz, GATE_ROWS, t), F32),
        jax.ShapeDtypeStruct((bsz, W_BRANCH, t), F32),
    ]
    out_specs = [
        pl.BlockSpec((1, tm, 2 * W_BRANCH), row),
        pl.BlockSpec((1, tm, W_BRANCH), row),
        pl.BlockSpec((1, 1, W_BRANCH, tm), lambda b, i: (b, i, 0, 0)),
        pl.BlockSpec((1, 1, N_HEADS, tm, LANES), lambda b, i: (b, i, 0, 0, 0)),
        pl.BlockSpec((1, tm, 768), row),
        pl.BlockSpec((1, tm, 4 * W_BRANCH), row),
        pl.BlockSpec((1, tm, 5 * W_BRANCH), row),
        pl.BlockSpec((1, GATE_ROWS, tm), lambda b, i: (b, 0, i)),
        pl.BlockSpec((1, W_BRANCH, tm), lambda b, i: (b, 0, i)),
    ]
    return pl.pallas_call(
        functools.partial(_in_kernel, n_ctx_tiles, ctx_row),
        grid=(bsz, t // tm),
        in_specs=_stream_specs(stream, n_ctx_tiles, 0) + [
                  pl.BlockSpec(mo.shape, const2),
                  pl.BlockSpec((1, d), const2),
                  pl.BlockSpec((tm, W_BRANCH), lambda b, i: (i, 0)),
                  pl.BlockSpec((tm, W_BRANCH), lambda b, i: (i, 0)),
                  pl.BlockSpec((W_BRANCH, tm), lambda b, i: (0, i)),
                  pl.BlockSpec((W_BRANCH, tm), lambda b, i: (0, i)),
                  pl.BlockSpec(w2.shape, const2),
                  pl.BlockSpec(wt.shape, const2)],
        out_specs=out_specs,
        out_shape=out_shape,
        compiler_params=_cparams(("parallel", "parallel")),
        name="in_proj",
    )(c_src, x_src, mo, g.reshape(1, d), cos, sin, cos_t, sin_t, w2, wt)


def _bwd_chunk(c, n_ctx_chunks, n_chunks):
    return jnp.where(c < n_ctx_chunks, n_ctx_chunks - 1 - c, n_chunks + n_ctx_chunks - 1 - c)


def _tri_masks(d):
    ri = _iota2((CHUNK, CHUNK), 0)
    ci = _iota2((CHUNK, CHUNK), 1)
    if d == 0:
        return ci <= ri, ri <= ci
    return ci >= ri, ri >= ci


def _split_f32(x, n):
    parts = []
    for _ in range(n):
        p = x.astype(BF16).astype(F32)
        parts.append(p)
        x = x - p
    return parts


def _cummax_lanes(u, d):
    lane = _iota2(u.shape, 1)
    ninf = jnp.float32(-jnp.inf)
    s = 1
    while s < CHUNK:
        if d == 0:
            sh = jnp.where(lane >= s, pltpu.roll(u, s, 1), ninf)
        else:
            sh = jnp.where(lane < CHUNK - s, pltpu.roll(u, CHUNK - s, 1), ninf)
        u = jnp.maximum(u, sh)
        s *= 2
    return u


def _state_row_selector(n_tiles):
    keep = (jnp.arange(CHUNK)[None, :] % 8) == jnp.arange(8)[:, None]
    return jnp.broadcast_to(keep[:, :, None], (8, CHUNK, n_tiles * LANES)).astype(BF16)


def _spread_selector(n_spread):
    t = jnp.arange(CHUNK)[:, None] // 16
    tiles = [jnp.broadcast_to(t == 3 + k, (CHUNK, LANES)) for k in range(n_spread)]
    return jnp.concatenate(tiles, axis=1).astype(BF16)


def _spread_operands(col_sum, row_sum, spread, sel_ref, ones_ref):
    z8 = jnp.zeros((8, CHUNK), F32)
    one8 = ones_ref[0:8, :].astype(F32)
    pieces = _split_f32(col_sum, 3) + [one8] * 3
    for x in spread:
        pieces += _split_f32(x, 2)
    assert len(pieces) <= CHUNK // 8
    lhs = jnp.concatenate(pieces + [z8] * (CHUNK // 8 - len(pieces)), axis=0).T.astype(BF16)
    rows = jnp.concatenate([one8] * 3 + _split_f32(row_sum, 3) + [z8] * (CHUNK // 8 - 6), axis=0)
    return lhs, jnp.concatenate([rows.astype(BF16), sel_ref[...]], axis=1)


def _mlstm_steps(qf_ref, vf_ref, ktf_ref, gtf_ref, qb_ref, vb_ref, ktb_ref, gtb_ref, bcol_ref, ones_ref, sel_ref,
                  rsel_ref, hf_ref, hb_ref, cn_ref, m_ref):
    @pl.when(pl.program_id(1) == 0)
    def _():
        cn_ref[...] = jnp.zeros_like(cn_ref)
        m_ref[...] = jnp.zeros_like(m_ref)

    lane = _iota2((CHUNK, LANES), 1)
    feat = _iota2((LANES, CHUNK), 0)
    row8 = _iota2((8, LANES), 0)
    ones_t = ones_ref[...]
    ninf = jnp.float32(-jnp.inf)
    m_all = m_ref[...]
    heads = [None] * 8
    gates = []
    for d in (0, 1):
        q_ref, v_ref, kt_ref = ((qf_ref, vf_ref, ktf_ref), (qb_ref, vb_ref, ktb_ref))[d]
        for p in range(2):
            q_t = q_ref[0, :, 128 * p:128 * p + 128]
            v_t = v_ref[0, :, 128 * p:128 * p + 128]
            kt_p = kt_ref[0, 128 * p:128 * p + 128, :] * (HEAD_W ** -0.5)
            kt_b = kt_p.astype(BF16)
            for half in range(2):
                r = 4 * d + 2 * p + half
                hmask = (lane >= 64) if half else (lane < 64)
                qm = jnp.where(hmask, q_t, 0.0).astype(BF16)
                cn = cn_ref[r]
                heads[r] = dict(
                    cn=cn, kt_p=kt_p,
                    vw=jnp.concatenate([jnp.where(hmask, v_t, 0.0).astype(BF16), ones_t], axis=1),
                    qk=_mm(qm, kt_b),
                    qc=_mm(qm, cn.astype(BF16)))

    yield
    f_cum = []
    for d in (0, 1):
        gt_ref = (gtf_ref, gtb_ref)[d]
        tr = jnp.where(_tri_masks(d)[1], 1.0, 0.0).astype(BF16)
        f_cum.append(_mm_exact_r(_log_sigmoid(gt_ref[0, 8:16, :] + bcol_ref[8:16, :]), tr))

    yield
    for d in (0, 1):
        gt_ref = (gtf_ref, gtb_ref)[d]
        last = CHUNK - 1 if d == 0 else 0
        li = gt_ref[0, 0:8, :] + bcol_ref[0:8, :]
        f = f_cum[d]
        u = li - f
        m_old = m_all
        mx = jnp.maximum(m_old, _cummax_lanes(u, d))
        w_int = jnp.exp(m_old - mx)
        e_mi = jnp.exp(-(f + mx))
        u_max = jnp.max(u, axis=1, keepdims=True)
        e_end = jnp.exp(u - u_max)
        f_last = f[:, last:last + 1]
        b_end = f_last + u_max
        m_new = jnp.maximum(f_last + m_old, b_end)
        a_dec = jnp.exp(f_last + m_old - m_new)
        g_inc = jnp.exp(b_end - m_new)
        m_all = jnp.where((row8 >> 2) == d, m_new, m_all)
        gates.append((e_end, a_dec, g_inc) + _spread_operands(-mx, u, [w_int, e_mi], sel_ref, ones_ref))

    for r in range(8):
        hd = heads[r]
        e_end, a_dec, g_inc, lhs, rhs = gates[r // 4]
        fmask = (feat >= 64) if r % 2 else (feat < 64)
        ke = (jnp.where(fmask, hd["kt_p"], 0.0) * e_end[r:r + 1, :]).astype(BF16)
        a_r = a_dec[r:r + 1, :]
        g_r = g_inc[r:r + 1, :]
        hd["mt"] = _mm(lhs, rhs * rsel_ref[r])
        hd["cn_new"] = (jnp.concatenate([a_r, a_r], axis=1) * hd["cn"]
                        + jnp.concatenate([g_r, g_r], axis=1) * _mm(ke, hd["vw"]))

    yield
    for r in range(8):
        hd = heads[r]
        vis, _ = _tri_masks(r // 4)
        s = hd["qk"] * jnp.exp(jnp.where(vis, hd["mt"][:, 0:128], ninf))
        hd["sv"] = _mm(s.astype(BF16), hd["vw"])

    yield
    for d in (0, 1):
        out_ref = (hf_ref, hb_ref)[d]
        for p in range(2):
            pair = []
            for half in range(2):
                hd = heads[4 * d + 2 * p + half]
                w_b = hd["mt"][:, 128:256]
                tot = hd["sv"] + jnp.concatenate([w_b, w_b], axis=1) * hd["qc"]
                pair.append(tot[:, 0:128] / jnp.maximum(jnp.abs(tot[:, 128:256]), hd["mt"][:, 256:384]))
            out_ref[0, :, 128 * p:128 * p + 128] = jnp.where(lane < 64, pair[0], pair[1]).astype(BF16)
    m_ref[...] = m_all
    for r in range(8):
        cn_ref[r] = heads[r]["cn_new"]


def _mlstm(ml, mkt, gt, bcol, n_ctx_chunks):
    bsz, t, _ = ml.shape
    nc = t // CHUNK
    out = jax.ShapeDtypeStruct((bsz, t, W_BRANCH), BF16)
    const2 = lambda b, c: (0, 0)

    def chunk_specs(chunk):
        return [pl.BlockSpec((1, CHUNK, W_BRANCH), lambda b, c: (b, chunk(c), 0)),
                pl.BlockSpec((1, CHUNK, W_BRANCH), lambda b, c: (b, chunk(c), 1)),
                pl.BlockSpec((1, W_BRANCH, CHUNK), lambda b, c: (b, 0, chunk(c))),
                pl.BlockSpec((1, GATE_ROWS, CHUNK), lambda b, c: (b, 0, chunk(c)))]

    fwd_chunk = lambda c: c
    bwd_chunk = lambda c: _bwd_chunk(c, n_ctx_chunks, nc)
    return dict(
        steps=_mlstm_steps,
        in_specs=chunk_specs(fwd_chunk) + chunk_specs(bwd_chunk) + [
            pl.BlockSpec((GATE_ROWS, 1), const2), pl.BlockSpec((CHUNK, LANES), const2),
            pl.BlockSpec((CHUNK, 2 * LANES), const2),
            pl.BlockSpec((8, CHUNK, 3 * LANES), lambda b, c: (0, 0, 0))],
        out_specs=[pl.BlockSpec((1, CHUNK, W_BRANCH), lambda b, c: (b, c, 0)),
                   pl.BlockSpec((1, CHUNK, W_BRANCH), lambda b, c: (b, bwd_chunk(c), 0))],
        out_shape=[out, out],
        scratch_shapes=[pltpu.VMEM((8, LANES, 2 * LANES), F32), pltpu.VMEM((8, LANES), F32)],
        args=(ml, ml, mkt, gt, ml, ml, mkt, gt, bcol, jnp.ones((CHUNK, LANES), BF16), _spread_selector(2),
              _state_row_selector(3)))


def _ssd_steps(n_ctx_chunks, n_chunks,
                xf_ref, xfp_ref, xfn_ref, xb_ref, xbp_ref, xbn_ref,
                gtf_ref, gtb_ref, bcol_ref, acol_ref, cw_ref, cb_ref, dskip_ref, ones_ref, sel_ref, rsel_ref,
                yf_ref, yb_ref, s_ref):
    c = pl.program_id(1)

    @pl.when(c == 0)
    def _():
        s_ref[...] = jnp.zeros_like(s_ref)

    lane = _iota2((CHUNK, LANES), 1)
    row768 = _iota2((CHUNK, 768), 0)
    ninf = jnp.float32(-jnp.inf)

    dt, a_cum = [], []
    for d in (0, 1):
        gt_ref = (gtf_ref, gtb_ref)[d]
        tr = jnp.where(_tri_masks(d)[1], 1.0, 0.0).astype(BF16)
        dt.append(_softplus(gt_ref[0, 16:24, :] + bcol_ref[16:24, :]))
        a_cum.append(_mm_exact_r(dt[d] * acol_ref[16:24, :], tr))

    yield
    xa = []
    for d in (0, 1):
        x_ref, xp_ref, xn_ref = ((xf_ref, xfp_ref, xfn_ref), (xb_ref, xbp_ref, xbn_ref))[d]
        j = c if d == 0 else _bwd_chunk(c, n_ctx_chunks, n_chunks)
        seg_first = jnp.logical_or(j == 0, j == n_ctx_chunks)
        seg_last = jnp.logical_or(j == n_ctx_chunks - 1, j == n_chunks - 1)
        x = x_ref[0]
        prev = jnp.where(seg_first, 0.0, xp_ref[0, 7:8, :])
        nxt = jnp.where(seg_last, 0.0, xn_ref[0, 0:1, :])
        x_dn = jnp.where(row768 == 0, prev, pltpu.roll(x, 1, 0))
        x_up = jnp.where(row768 == CHUNK - 1, nxt, pltpu.roll(x, CHUNK - 1, 0))
        xa.append(_silu(x_dn * cw_ref[0:1, :] + x * cw_ref[1:2, :] + x_up * cw_ref[2:3, :] + cb_ref[...]))

    heads = [None] * 8
    groups = {}
    for d in (0, 1):
        for g in range(2):
            x_pair = xa[d][:, 128 * g:128 * g + 128]
            b_g = xa[d][:, 256 + 128 * g:256 + 128 * g + 128]
            c_gb = xa[d][:, 512 + 128 * g:512 + 128 * g + 128].astype(BF16)
            groups[d, g] = dict(x_pair=x_pair, bt=b_g.T.astype(BF16), cb=_mm_nt(c_gb, b_g.astype(BF16)))
            for half in range(2):
                r = 4 * d + 2 * g + half
                hmask = (lane >= 64) if half else (lane < 64)
                st = s_ref[r]
                xm = jnp.where(hmask, x_pair, 0.0)
                heads[r] = dict(st=st, xm=xm, xmb=xm.astype(BF16), cs=_mm(c_gb, st.astype(BF16)))

    yield
    for d in (0, 1):
        last = CHUNK - 1 if d == 0 else 0
        a_last = a_cum[d][:, last:last + 1]
        w_in = jnp.exp(a_last - a_cum[d]) * dt[d]
        e_last = jnp.exp(jnp.broadcast_to(a_last, (8, CHUNK)))
        lhs, rhs = _spread_operands(a_cum[d], -a_cum[d], [jnp.exp(a_cum[d]), w_in], sel_ref, ones_ref)
        for h in range(N_HEADS):
            r = 4 * d + h
            heads[r]["mt"] = _mm(lhs, rhs * rsel_ref[r])
            heads[r]["e_last"] = e_last[r:r + 1, :]

    yield
    for r in range(8):
        d, h = divmod(r, 4)
        hd = heads[r]
        ds = _mm(groups[d, h // 2]["bt"], (hd["xm"] * hd["mt"][:, 256:384]).astype(BF16))
        s_ref[r] = hd["e_last"] * hd["st"] + ds
    for r in range(8):
        d, h = divmod(r, 4)
        hd = heads[r]
        vis, _ = _tri_masks(d)
        sc = groups[d, h // 2]["cb"] * jnp.exp(jnp.where(vis, hd["mt"][:, 0:128], ninf)) * dt[d][r:r + 1, :]
        hd["y"] = _mm(sc.astype(BF16), hd["xmb"]) + hd["mt"][:, 128:256] * hd["cs"]

    for d in (0, 1):
        out_ref = (yf_ref, yb_ref)[d]
        for g in range(2):
            y_pair = heads[4 * d + 2 * g]["y"] + heads[4 * d + 2 * g + 1]["y"]
            if d == 0:
                y_pair = y_pair + dskip_ref[:, 128 * g:128 * g + 128] * groups[d, g]["x_pair"]
            out_ref[0, :, 128 * g:128 * g + 128] = y_pair.astype(BF16)


def _ssd(ss, gt, bcol, acol, conv_w, conv_b, dskip, n_ctx_chunks):
    bsz, t, _ = ss.shape
    nc = t // CHUNK
    sub = CHUNK // 8
    nsub = t // 8
    bc = lambda c: _bwd_chunk(c, n_ctx_chunks, nc)
    fwd = lambda b, c: (b, c, 0)
    bwd = lambda b, c: (b, bc(c), 0)
    fwd_p = lambda b, c: (b, jnp.maximum(c * sub - 1, 0), 0)
    fwd_n = lambda b, c: (b, jnp.minimum((c + 1) * sub, nsub - 1), 0)
    bwd_p = lambda b, c: (b, jnp.maximum(bc(c) * sub - 1, 0), 0)
    bwd_n = lambda b, c: (b, jnp.minimum((bc(c) + 1) * sub, nsub - 1), 0)
    fwd_t = lambda b, c: (b, 0, c)
    bwd_t = lambda b, c: (b, 0, bc(c))
    const2 = lambda b, c: (0, 0)
    out = jax.ShapeDtypeStruct((bsz, t, W_BRANCH), BF16)
    return dict(
        steps=functools.partial(_ssd_steps, n_ctx_chunks, nc),
        in_specs=[pl.BlockSpec((1, CHUNK, 768), fwd), pl.BlockSpec((1, 8, 768), fwd_p),
                  pl.BlockSpec((1, 8, 768), fwd_n),
                  pl.BlockSpec((1, CHUNK, 768), bwd), pl.BlockSpec((1, 8, 768), bwd_p),
                  pl.BlockSpec((1, 8, 768), bwd_n),
                  pl.BlockSpec((1, GATE_ROWS, CHUNK), fwd_t), pl.BlockSpec((1, GATE_ROWS, CHUNK), bwd_t),
                  pl.BlockSpec((GATE_ROWS, 1), const2), pl.BlockSpec((GATE_ROWS, 1), const2),
                  pl.BlockSpec((3, 768), const2), pl.BlockSpec((1, 768), const2),
                  pl.BlockSpec((1, W_BRANCH), const2),
                  pl.BlockSpec((CHUNK, LANES), const2), pl.BlockSpec((CHUNK, 2 * LANES), const2),
                  pl.BlockSpec((8, CHUNK, 3 * LANES), lambda b, c: (0, 0, 0))],
        out_specs=[pl.BlockSpec((1, CHUNK, W_BRANCH), fwd), pl.BlockSpec((1, CHUNK, W_BRANCH), bwd)],
        out_shape=[out, out],
        scratch_shapes=[pltpu.VMEM((8, SSM_N, LANES), F32)],
        args=(ss, ss, ss, ss, ss, ss, gt, gt, bcol, acol, conv_w, conv_b, dskip,
              jnp.ones((CHUNK, LANES), BF16), _spread_selector(2), _state_row_selector(3)))


def _hgrn2_steps(hf_ref, hb_ref, loglb_ref, log1m_ref, onem_ref, e_ref, of_ref, ob_ref,
                  st_ref):
    @pl.when(pl.program_id(1) == 0)
    def _():
        st_ref[...] = jnp.zeros_like(st_ref)

    ri = _iota2((CHUNK, CHUNK), 0)
    ci = _iota2((CHUNK, CHUNK), 1)
    lane = _iota2((CHUNK, LANES), 1)
    rw = _iota2((CHUNK, W_BRANCH), 0)
    blockdiag = (ri >> 6) == (ci >> 6)
    ninf = jnp.float32(-jnp.inf)
    nb = CHUNK // SUB
    rs = _iota2((nb, SUB, W_BRANCH), 1)

    dirs = []
    for d in (0, 1):
        h_ref = (hf_ref, hb_ref)[d]
        tc = jnp.where(_tri_masks(d)[0], 1.0, 0.0).astype(BF16)
        z = h_ref[0, :, W_BRANCH * (1 + d):W_BRANCH * (2 + d)]
        e = jnp.exp(-jnp.abs(z))
        ope = 1.0 + e
        la = loglb_ref[d:d + 1, :]
        lb_ = log1m_ref[d:d + 1, :] + (jnp.minimum(z, 0.0) - jnp.log(ope))
        logf = jnp.maximum(la, lb_) + jnp.log(1.0 + jnp.exp(-jnp.abs(la - lb_)))
        dirs.append(dict(
            q=_silu(h_ref[0, :, 0:W_BRANCH]),
            v=h_ref[0, :, 3 * W_BRANCH:4 * W_BRANCH],
            kk=onem_ref[d:d + 1, :] * (jnp.where(z >= 0.0, e, 1.0) / ope),
            gcum=_mm_exact_l(tc, logf)))

    yield
    for d in (0, 1):
        dd = dirs[d]
        q, kk, gcum = dd["q"], dd["kk"], dd["gcum"]
        last = CHUNK - 1 if d == 0 else 0
        g_last = gcum[last:last + 1, :]
        qg = (q * jnp.exp(gcum)).astype(BF16)
        kg = (kk * jnp.exp(g_last - gcum)).astype(BF16)
        vb = dd["v"].astype(BF16)

        dd["o_inter"] = []
        for p in range(2):
            sl = slice(128 * p, 128 * p + 128)
            st = st_ref[2 * d + p]
            dd["o_inter"].append(_mm_nt(qg[:, sl], st.astype(BF16)))
            dst = _mm_tn(vb[:, sl], kg[:, sl])
            st_ref[2 * d + p] = st * jnp.exp(g_last[:, sl]) + jnp.where(blockdiag, dst, 0.0)

        a_mats = [jnp.zeros((CHUNK, CHUNK), F32) for _ in range(N_HEADS)]
        blk = CHUNK // 2
        while blk >= SUB:
            first = (rw & (2 * blk - 1)) < blk
            edge = (blk - 1) if d == 0 else blk
            gb = gcum.reshape(CHUNK // (2 * blk), 2 * blk, W_BRANCH)[:, edge:edge + 1, :]
            gb = jnp.broadcast_to(gb, (CHUNK // (2 * blk), 2 * blk, W_BRANCH)).reshape(CHUNK, W_BRANCH)
            q_side = jnp.logical_not(first) if d == 0 else first
            qt = q * jnp.exp(jnp.where(q_side, gcum - gb, ninf))
            kt = (kk * jnp.exp(jnp.where(q_side, ninf, gb - gcum))).astype(BF16)
            same = (ri >> int(math.log2(2 * blk))) == (ci >> int(math.log2(2 * blk)))
            for h in range(N_HEADS):
                p, half = divmod(h, 2)
                hmask = (lane >= 64) if half else (lane < 64)
                qh = jnp.where(hmask, qt[:, 128 * p:128 * p + 128], 0.0).astype(BF16)
                a_mats[h] = a_mats[h] + jnp.where(same, _mm_nt(qh, kt[:, 128 * p:128 * p + 128]), 0.0)
            blk //= 2
        dd["a_mats"] = a_mats

        g3 = gcum.reshape(nb, SUB, W_BRANCH)
        q3 = q.reshape(nb, SUB, W_BRANCH)
        k3 = kk.reshape(nb, SUB, W_BRANCH)
        v3 = dd["v"].reshape(nb, SUB, W_BRANCH)
        o3 = None
        for j in range(SUB):
            ok = (rs >= j) if d == 0 else (rs <= j)
            pj = q3 * jnp.exp(jnp.where(ok, g3 - g3[:, j:j + 1, :], ninf)) * k3[:, j:j + 1, :]
            red = _mm(pj.reshape(CHUNK, W_BRANCH).astype(BF16), e_ref[...])
            term = red.reshape(nb, SUB, W_BRANCH) * v3[:, j:j + 1, :]
            o3 = term if o3 is None else o3 + term
        dd["o_diag"] = o3.reshape(CHUNK, W_BRANCH)

    yield
    for d in (0, 1):
        dd = dirs[d]
        out_ref = (of_ref, ob_ref)[d]
        for p in range(2):
            sl = slice(128 * p, 128 * p + 128)
            a_cat = jnp.concatenate([dd["a_mats"][2 * p], dd["a_mats"][2 * p + 1]], axis=1).astype(BF16)
            v_p = dd["v"][:, sl]
            v_cat = jnp.concatenate([jnp.where(lane < 64, v_p, 0.0), jnp.where(lane >= 64, v_p, 0.0)],
                                    axis=0).astype(BF16)
            out_ref[0, :, sl] = (dd["o_inter"][p] + _mm(a_cat, v_cat) + dd["o_diag"][:, sl]).astype(BF16)


def _hgrn2(hg, loglb, log1m, onem, e64, n_ctx_chunks):
    bsz, t, _ = hg.shape
    nc = t // CHUNK
    fwd = lambda b, c: (b, c, 0)
    bwd = lambda b, c: (b, _bwd_chunk(c, n_ctx_chunks, nc), 0)
    const2 = lambda b, c: (0, 0)
    out = jax.ShapeDtypeStruct((bsz, t, W_BRANCH), BF16)
    return dict(
        steps=_hgrn2_steps,
        in_specs=[pl.BlockSpec((1, CHUNK, 1024), fwd), pl.BlockSpec((1, CHUNK, 1024), bwd),
                  pl.BlockSpec((2, W_BRANCH), const2), pl.BlockSpec((2, W_BRANCH), const2),
                  pl.BlockSpec((2, W_BRANCH), const2), pl.BlockSpec((W_BRANCH, W_BRANCH), const2)],
        out_specs=[pl.BlockSpec((1, CHUNK, W_BRANCH), fwd), pl.BlockSpec((1, CHUNK, W_BRANCH), bwd)],
        out_shape=[out, out],
        scratch_shapes=[pltpu.VMEM((4, LANES, LANES), F32)],
        args=(hg, hg, loglb, log1m, onem, e64))


def _scan_kernel(parts, *refs):
    n_in = [len(p["in_specs"]) for p in parts]
    n_out = [len(p["out_specs"]) for p in parts]
    n_scr = [len(p["scratch_shapes"]) for p in parts]
    ins, outs, scr = refs[:sum(n_in)], refs[sum(n_in):sum(n_in) + sum(n_out)], refs[sum(n_in) + sum(n_out):]
    gens = []
    for k, p in enumerate(parts):
        mine = (ins[sum(n_in[:k]):sum(n_in[:k + 1])] + outs[sum(n_out[:k]):sum(n_out[:k + 1])]
                + scr[sum(n_scr[:k]):sum(n_scr[:k + 1])])
        gens.append(p["steps"](*mine))
    while gens:
        for g in list(gens):
            if next(g, gens) is gens:
                gens.remove(g)


def _chunk_scans(parts, bsz, n_chunks):
    return pl.pallas_call(
        functools.partial(_scan_kernel, parts),
        grid=(bsz, n_chunks),
        in_specs=[s for p in parts for s in p["in_specs"]],
        out_specs=[s for p in parts for s in p["out_specs"]],
        out_shape=[s for p in parts for s in p["out_shape"]],
        scratch_shapes=[s for p in parts for s in p["scratch_shapes"]],
        compiler_params=_cparams(("parallel", "arbitrary")),
        name="chunk_scans",
    )(*[a for p in parts for a in p["args"]])


def _attn(daq, kt, vx, lam_vecs, lam_init, n_ctx_tiles, skip_ctx):
    bsz, t, _ = daq.shape
    tq = ROW_TILE
    n_k_tiles = t // KEY_TILE
    n_lat_tiles = n_k_tiles - n_ctx_tiles
    group = math.gcd(KEY_GROUP, n_lat_tiles)
    q_off = n_ctx_tiles if skip_ctx else 0
    nq = t // tq - q_off
    n_maps = W_BRANCH // DA_QK
    rows = n_maps * tq

    def kern(q_ref, kt_ref, vx_ref, lv_ref, o_ref, q8_ref, m_ref, acc_ref, s_ref, mx_ref):
        qi = pl.program_id(1) + q_off
        lane = _iota2((tq, W_BRANCH), 1)
        q = q_ref[0]
        for j in range(n_maps):
            q8_ref[j * tq:(j + 1) * tq, :] = jnp.where((lane >> 5) == j, q, jnp.zeros_like(q))
        m_ref[...] = jnp.full_like(m_ref, -jnp.inf)
        acc_ref[...] = jnp.zeros_like(acc_ref)

        def update(t0, n, after_scores=None):
            q8 = q8_ref[...]
            s = [_mm(q8, kt_ref[0, t0 + i]) for i in range(n)]
            if after_scores is not None:
                after_scores()
            mx = s[0][:, 0:LANES]
            for i in range(n):
                for c in range(KEY_TILE // LANES):
                    if i or c:
                        mx = jnp.maximum(mx, s[i][:, LANES * c:LANES * (c + 1)])
            m_old = m_ref[...]
            m_new = jnp.maximum(m_old, jnp.max(mx, axis=1, keepdims=True))
            alpha = jnp.exp(m_old - m_new)
            m_ref[...] = m_new
            m2 = jnp.concatenate([m_new] * (KEY_TILE // LANES), axis=1)
            p = [jnp.exp(s[i] - m2).astype(BF16) for i in range(n)]
            for hd in range(N_HEADS):
                r0 = 2 * hd * tq
                pv = _mm(p[0][r0:r0 + 2 * tq], vx_ref[0, t0, hd])
                for i in range(1, n):
                    pv = pv + _mm(p[i][r0:r0 + 2 * tq], vx_ref[0, t0 + i, hd])
                acc_ref[r0:r0 + 2 * tq, :] = alpha[r0:r0 + 2 * tq] * acc_ref[r0:r0 + 2 * tq, :] + pv

        def scores(g, slot):
            q8 = q8_ref[...]
            t0 = n_ctx_tiles + g * group
            mx = None
            for i in range(group):
                s = _mm(q8, kt_ref[0, t0 + i])
                s_ref[slot, :, KEY_TILE * i:KEY_TILE * (i + 1)] = s
                for c in range(KEY_TILE // LANES):
                    part = s[:, LANES * c:LANES * (c + 1)]
                    mx = part if mx is None else jnp.maximum(mx, part)
            mx_ref[slot] = mx

        def absorb(g, slot):
            t0 = n_ctx_tiles + g * group
            m_old = m_ref[...]
            m_new = jnp.maximum(m_old, jnp.max(mx_ref[slot], axis=1, keepdims=True))
            alpha = jnp.exp(m_old - m_new)
            m_ref[...] = m_new
            m2 = jnp.concatenate([m_new] * (KEY_TILE // LANES), axis=1)
            p = [jnp.exp(s_ref[slot, :, KEY_TILE * i:KEY_TILE * (i + 1)] - m2).astype(BF16) for i in range(group)]
            for hd in range(N_HEADS):
                r0 = 2 * hd * tq
                pv = _mm(p[0][r0:r0 + 2 * tq], vx_ref[0, t0, hd])
                for i in range(1, group):
                    pv = pv + _mm(p[i][r0:r0 + 2 * tq], vx_ref[0, t0 + i, hd])
                acc_ref[r0:r0 + 2 * tq, :] = alpha[r0:r0 + 2 * tq] * acc_ref[r0:r0 + 2 * tq, :] + pv

        @pl.when(qi < n_ctx_tiles)
        def _():
            update(0, n_ctx_tiles)

        @pl.when(qi >= n_ctx_tiles)
        def _():
            n_groups = n_lat_tiles // group
            n_pairs = (n_groups - 1) // 2
            update(0, n_ctx_tiles, after_scores=lambda: scores(0, 0))

            def body(k, carry):
                g = 2 * k
                scores(g + 1, 1)
                absorb(g, 0)
                scores(g + 2, 0)
                absorb(g + 1, 1)
                return carry
            lax.fori_loop(0, n_pairs, body, 0)
            g0 = 2 * n_pairs
            if n_groups - g0 == 2:
                scores(g0 + 1, 1)
            absorb(g0, 0)
            if n_groups - g0 == 2:
                absorb(g0 + 1, 1)

        lv = lv_ref[...]
        lam = (jnp.exp(jnp.sum(lv[0:1, :] * lv[1:2, :], axis=1, keepdims=True))
               - jnp.exp(jnp.sum(lv[2:3, :] * lv[3:4, :], axis=1, keepdims=True)) + lam_init)
        lane1 = _iota2((tq, LANES), 1)
        for pr in range(N_HEADS // 2):
            halves = []
            for half in range(2):
                r0 = 2 * (2 * pr + half) * tq
                one = 0 if half else HEAD_W
                a0 = acc_ref[r0:r0 + tq, :]
                a1 = acc_ref[r0 + tq:r0 + 2 * tq, :]
                halves.append(a0 / a0[:, one:one + 1] - lam * (a1 / a1[:, one:one + 1]))
            o_ref[0, :, LANES * pr:LANES * (pr + 1)] = jnp.where(lane1 < HEAD_W, halves[0], halves[1]).astype(BF16)

    return pl.pallas_call(
        kern,
        grid=(bsz, nq),
        in_specs=[pl.BlockSpec((1, tq, W_BRANCH), lambda b, i: (b, i + q_off, 0)),
                  pl.BlockSpec((1, n_k_tiles, W_BRANCH, KEY_TILE), lambda b, i: (b, 0, 0, 0),
                               pipeline_mode=pl.Buffered(1)),
                  pl.BlockSpec((1, n_k_tiles, N_HEADS, KEY_TILE, LANES), lambda b, i: (b, 0, 0, 0, 0),
                               pipeline_mode=pl.Buffered(1)),
                  pl.BlockSpec(lam_vecs.shape, lambda b, i: (0, 0))],
        out_specs=pl.BlockSpec((1, tq, W_BRANCH), lambda b, i: (b, i, 0)),
        out_shape=jax.ShapeDtypeStruct((bsz, nq * tq, W_BRANCH), BF16),
        scratch_shapes=[pltpu.VMEM((rows, W_BRANCH), BF16),
                        pltpu.VMEM((rows, LANES), F32),
                        pltpu.VMEM((rows, LANES), F32),
                        pltpu.VMEM((2, rows, group * KEY_TILE), F32),
                        pltpu.VMEM((2, rows, LANES), F32)],
        compiler_params=_cparams(("parallel", "parallel")),
        name="diff_attn",
    )(daq, kt, vx, lam_vecs)


def _seg_mean(x, e_ref):
    x0 = x.astype(BF16)
    x1 = (x - x0.astype(F32)).astype(BF16)
    return (_mm(x0, e_ref[...]) + _mm(x1, e_ref[...])) * (1.0 / HEAD_W)


def _out_kernel(n_ctx_tiles, ctx_row, q_off, lam_init, final,
                c_ref, x_ref, mo_ref, mlo_ref, mlz_ref, mhf_ref, mhb_ref, dao_ref, daz_ref,
                syf_ref, syb_ref, ssz_ref, hof_ref, hob_ref, hgz_ref,
                mlg_ref, dag_ref, ssg_ref, hgg_ref, e_ref, w_ref, fg_ref, o_ref):
    b = pl.program_id(0)
    i = pl.program_id(1) + q_off
    d = x_ref.shape[2]
    r = jnp.where(i < n_ctx_tiles, ctx_row, b)
    gate = mo_ref[pl.ds(r, 1), :][:, 2 * d:3 * d]

    u = _sigmoid(mlo_ref[0].astype(F32)) * (mhf_ref[0].astype(F32) + mhb_ref[0].astype(F32))
    dev = u - _seg_mean(u, e_ref)
    y_ml = dev * lax.rsqrt(_seg_mean(dev * dev, e_ref) + EPS) * mlg_ref[...] * _silu(mlz_ref[0].astype(F32))

    o = dao_ref[0].astype(F32)
    y_da = (o * lax.rsqrt(_seg_mean(o * o, e_ref) + EPS) * dag_ref[...]) * (1.0 - lam_init) * _silu(daz_ref[0].astype(F32))

    ys = (syf_ref[0].astype(F32) + syb_ref[0].astype(F32)) * _silu(ssz_ref[0].astype(F32))
    parts = []
    for g in range(2):
        yg = ys[:, 128 * g:128 * g + 128]
        parts.append(yg * lax.rsqrt(jnp.mean(yg * yg, axis=-1, keepdims=True) + EPS))
    y_ss = jnp.concatenate(parts, axis=1) * ssg_ref[...]

    oh = hof_ref[0].astype(F32) + hob_ref[0].astype(F32)
    y_hg = (oh * lax.rsqrt(_seg_mean(oh * oh, e_ref) + EPS) * hgg_ref[...]) * _silu(hgz_ref[0].astype(F32))

    acc = _mm(y_ml.astype(BF16), w_ref[0:W_BRANCH, :])
    acc = acc + _mm(y_da.astype(BF16), w_ref[W_BRANCH:2 * W_BRANCH, :])
    acc = acc + _mm(y_ss.astype(BF16), w_ref[2 * W_BRANCH:3 * W_BRANCH, :])
    acc = acc + _mm(y_hg.astype(BF16), w_ref[3 * W_BRANCH:4 * W_BRANCH, :])
    x_new = jnp.where(i < n_ctx_tiles, c_ref[0], x_ref[0]) + gate * acc
    if final:
        x_new = x_new * lax.rsqrt(jnp.mean(x_new * x_new, axis=-1, keepdims=True) + EPS) * fg_ref[...]
    o_ref[0] = x_new


def _out_proj(stream, mo, zg, mhf, mhb, dao, syf, syb, hof, hob,
              mlg, dag, ssg, hgg, e64, w_out, final_g, lam_init, n_ctx_tiles, ctx_row, final):
    c_src, x_src, x_off, t = stream
    bsz, _, d = x_src.shape
    tm = ROW_TILE
    q_off = n_ctx_tiles if final else 0
    nrow = t // tm - q_off
    row = lambda b, i: (b, i + q_off, 0)
    col = lambda k: (lambda b, i: (b, i + q_off, k))
    const2 = lambda b, i: (0, 0)
    wb = pl.BlockSpec((1, tm, W_BRANCH), row)
    return pl.pallas_call(
        functools.partial(_out_kernel, n_ctx_tiles, ctx_row, q_off, lam_init, final),
        grid=(bsz, nrow),
        in_specs=_stream_specs(stream, n_ctx_tiles, q_off) + [
                  pl.BlockSpec(mo.shape, const2),
                  pl.BlockSpec((1, tm, W_BRANCH), col(0)), pl.BlockSpec((1, tm, W_BRANCH), col(1)),
                  wb, wb,
                  pl.BlockSpec((1, tm, W_BRANCH), lambda b, i: (b, i, 0)) if final else wb,
                  pl.BlockSpec((1, tm, W_BRANCH), col(2)),
                  wb, wb, pl.BlockSpec((1, tm, W_BRANCH), col(3)),
                  wb, wb, pl.BlockSpec((1, tm, W_BRANCH), col(4)),
                  pl.BlockSpec((1, W_BRANCH), const2), pl.BlockSpec((1, W_BRANCH), const2),
                  pl.BlockSpec((1, W_BRANCH), const2), pl.BlockSpec((1, W_BRANCH), const2),
                  pl.BlockSpec((W_BRANCH, W_BRANCH), const2),
                  pl.BlockSpec(w_out.shape, const2),
                  pl.BlockSpec((1, d), const2)],
        out_specs=pl.BlockSpec((1, tm, d), lambda b, i: (b, i, 0)),
        out_shape=jax.ShapeDtypeStruct((bsz, nrow * tm, d), F32),
        compiler_params=_cparams(("parallel", "parallel")),
        name="out_proj",
    )(c_src, x_src, mo, zg, zg, mhf, mhb, dao, zg, syf, syb, zg, hof, hob, zg,
      mlg, dag, ssg, hgg, e64, w_out, final_g.reshape(1, d))


def _relayout_w_in(w_in):
    d = w_in.shape[0]
    o = {}
    off = 0
    for name, n in (("ml_q", 256), ("ml_k", 256), ("ml_v", 256), ("ml_o", 256), ("ml_i", 8), ("ml_f", 8),
                    ("ml_z", 256), ("da_q", 256), ("da_k", 256), ("da_v", 256), ("da_z", 256),
                    ("ss_xbc", 768), ("ss_dt", 8), ("ss_z", 256),
                    ("hg_q", 256), ("hg_f", 512), ("hg_i", 256), ("hg_z", 256)):
        o[name] = w_in[:, off:off + n]
        off += n

    gates = jnp.concatenate([o["ml_i"], o["ml_f"], o["ss_dt"], jnp.zeros((d, GATE_ROWS - 24), w_in.dtype)], axis=1)
    w2 = jnp.concatenate([o["ml_q"], o["ml_v"], o["da_q"], o["da_v"],
                          o["ss_xbc"], o["hg_q"], o["hg_f"], o["hg_i"],
                          o["ml_o"], o["ml_z"], o["da_z"], o["ss_z"], o["hg_z"]], axis=1)
    wt = jnp.concatenate([o["da_k"], gates, o["ml_k"]], axis=1).T
    return w2.astype(BF16), wt.astype(BF16)


def _rope_tables(n_ctx, seq):
    pos = jnp.arange(seq)
    rows = (pos // GRID_W).astype(F32)
    cols = (pos % GRID_W).astype(F32)
    axis = DA_QK // 2
    inv = ROPE_BASE ** (-jnp.arange(0, axis, 2, dtype=F32) / axis)
    ang = jnp.concatenate([rows[:, None] * inv, cols[:, None] * inv], axis=-1)
    cos = jnp.concatenate([jnp.ones((n_ctx, axis), F32), jnp.cos(ang)], axis=0)
    sin = jnp.concatenate([jnp.zeros((n_ctx, axis), F32), jnp.sin(ang)], axis=0)
    reps = W_BRANCH // axis
    return jnp.tile(cos, (1, reps)), jnp.tile(sin, (1, reps))


def _gate_col(vals):
    v = jnp.concatenate([vals.astype(F32), jnp.zeros((GATE_ROWS - vals.shape[0],), F32)])
    return v.reshape(GATE_ROWS, 1)


def kernel(x, c, ctx, c_ctx, w_mod, b_mod, norm_g, w_in, w_out, ml_gate_b, ml_norm_g, da_lambda, da_norm_g,
           ss_conv_w, ss_conv_b, ss_dt_bias, ss_a_log, ss_d, ss_norm_g, hg_lower, hg_norm_g, final_g):
    bsz, seq, d = x.shape
    n_ctx = ctx.shape[1]
    depth = w_mod.shape[0]
    assert n_ctx % ROW_TILE == 0 and seq % ROW_TILE == 0 and bsz < 8
    n_ctx_tiles = n_ctx // ROW_TILE
    n_ctx_chunks = n_ctx // CHUNK
    ctx_row = bsz

    stream = (ctx, x, 0, n_ctx + seq)
    cc = jnp.concatenate([c, c_ctx[None, :], jnp.zeros((8 - bsz - 1, d), F32)], axis=0)
    cos, sin = _rope_tables(n_ctx, seq)
    cos_t, sin_t = cos.T, sin.T
    lb_all = jnp.cumsum(jax.nn.softmax(hg_lower.astype(F32), axis=1), axis=1)
    lb_all = lb_all - lb_all[:, :1]
    hid = _iota2((W_BRANCH, W_BRANCH), 0) // HEAD_W
    e64 = (hid == hid.T).astype(BF16)

    out = None
    for l in range(depth):
        lam_init = 0.8 - 0.6 * math.exp(-0.3 * l)
        final = l == depth - 1
        w2, wt = _relayout_w_in(w_in[l])
        mo = _modulation(cc, w_mod[l], b_mod[l])
        ml, daq, dak, dav, ss, hg, zg, gt, mkt = _in_proj(stream, mo, norm_g[l], cos, sin, cos_t, sin_t, w2, wt,
                                                     n_ctx_tiles, ctx_row)

        gb = ml_gate_b[l]
        zeros8 = jnp.zeros((8,), F32)
        bcol = _gate_col(jnp.concatenate([gb[:, 0].reshape(-1), gb[:, 1].reshape(-1),
                                                ss_dt_bias[l].reshape(-1)]))
        acol = _gate_col(jnp.concatenate([zeros8, zeros8, -jnp.exp(ss_a_log[l].astype(F32)).reshape(-1)]))
        dao = _attn(daq, dak, dav, da_lambda[l].astype(F32), lam_init, n_ctx_tiles, final)
        dskip = jnp.repeat(ss_d[l].astype(F32), HEAD_W).reshape(1, W_BRANCH)
        lbh = lb_all[:, l]
        scans = [_mlstm(ml, mkt, gt, bcol, n_ctx_chunks),
                 _ssd(ss, gt, bcol, acol, ss_conv_w[l], ss_conv_b[l].reshape(1, -1), dskip, n_ctx_chunks),
                 _hgrn2(hg, jnp.log(lbh), jnp.log1p(-lbh), 1.0 - lbh, e64, n_ctx_chunks)]
        mhf, mhb, syf, syb, hof, hob = _chunk_scans(scans, bsz, (n_ctx + seq) // CHUNK)

        res = _out_proj(stream, mo, zg, mhf, mhb, dao, syf, syb, hof, hob,
                        ml_norm_g[l].reshape(1, -1), jnp.tile(da_norm_g[l], N_HEADS).reshape(1, -1),
                        ss_norm_g[l].reshape(1, -1), hg_norm_g[l].reshape(1, -1), e64,
                        w_out[l].astype(BF16), final_g, lam_init, n_ctx_tiles, ctx_row, final)
        if final:
            out = res
        else:
            stream = (res, res, n_ctx_tiles, n_ctx + seq)
    return out
```

```python
import functools
import math

import jax
import jax.numpy as jnp
from jax import lax
from jax.experimental import pallas as pl
from jax.experimental.pallas import tpu as pltpu

F32 = jnp.float32
BF16 = jnp.bfloat16

EPS = 1e-6
GRID_W = 64
ROPE_BASE = 10000.0
N_HEADS = 4
HEAD_W = 64
W_BRANCH = N_HEADS * HEAD_W
DA_QK = 32
SSM_N = 128
CHUNK = 128
ROW_TILE = 256
SUB = 8
LANES = 128
VMEM_LIMIT = 56 * 1024 * 1024

C_ML = 0
C_DAQ = 512
C_DAV = 768
C_SS = 1024
C_HG = 1792
C_ZG = 2816
C_END = 4096
GATE_ROWS = 32
KEY_TILE = 256
KEY_GROUP = 4

NT = (((1,), (1,)), ((), ()))
TN = (((0,), (0,)), ((), ()))


def _mm(a, b):
    return jnp.dot(a, b, preferred_element_type=F32)


def _mm_nt(a, b):
    return lax.dot_general(a, b, NT, preferred_element_type=F32)


def _mm_tn(a, b):
    return lax.dot_general(a, b, TN, preferred_element_type=F32)


def _split3(x):
    x0 = x.astype(BF16)
    r = x - x0.astype(F32)
    x1 = r.astype(BF16)
    x2 = (r - x1.astype(F32)).astype(BF16)
    return x0, x1, x2


def _mm_exact_l(t, x):
    x0, x1, x2 = _split3(x)
    return _mm(t, x0) + _mm(t, x1) + _mm(t, x2)


def _mm_exact_r(x, t):
    x0, x1, x2 = _split3(x)
    return _mm(x0, t) + _mm(x1, t) + _mm(x2, t)


def _sigmoid(x):
    return 1.0 / (1.0 + jnp.exp(-x))


def _silu(x):
    return x * _sigmoid(x)


def _log_sigmoid(x):
    return jnp.minimum(x, 0.0) - jnp.log(1.0 + jnp.exp(-jnp.abs(x)))


def _softplus(x):
    return jnp.maximum(x, 0.0) + jnp.log(1.0 + jnp.exp(-jnp.abs(x)))


def _iota2(shape, axis):
    return lax.broadcasted_iota(jnp.int32, shape, axis)


def _cparams(sem):
    return pltpu.CompilerParams(dimension_semantics=sem, vmem_limit_bytes=VMEM_LIMIT)


def _mod_kernel(c_ref, w_ref, b_ref, o_ref):
    c = c_ref[...]
    o_ref[...] = jnp.dot(_silu(c), w_ref[...], precision=lax.Precision.HIGHEST,
                         preferred_element_type=F32) + b_ref[...]


def _modulation(cc, w_mod, b_mod):
    d = cc.shape[1]
    n = w_mod.shape[1]
    tn = 512
    return pl.pallas_call(
        _mod_kernel,
        grid=(n // tn,),
        in_specs=[pl.BlockSpec((8, d), lambda j: (0, 0)),
                  pl.BlockSpec((d, tn), lambda j: (0, j)),
                  pl.BlockSpec((1, tn), lambda j: (0, j))],
        out_specs=pl.BlockSpec((8, tn), lambda j: (0, j)),
        out_shape=jax.ShapeDtypeStruct((8, n), F32),
        compiler_params=_cparams(("arbitrary",)),
        name="modulation",
    )(cc, w_mod, b_mod.reshape(1, n))


def _stream_specs(stream, n_ctx_tiles, first_tile):
    _, x_src, x_off, _ = stream
    d = x_src.shape[2]
    return [pl.BlockSpec((1, ROW_TILE, d), lambda b, i: (b, jnp.minimum(i + first_tile, n_ctx_tiles - 1), 0)),
            pl.BlockSpec((1, ROW_TILE, d), lambda b, i: (b, jnp.maximum(i + first_tile - n_ctx_tiles, 0) + x_off, 0))]


def _in_kernel(n_ctx_tiles, ctx_row, c_ref, x_ref, mo_ref, g_ref, cos_ref, sin_ref, cost_ref, sint_ref,
               w_ref, wt_ref,
               ml_ref, daq_ref, kt_ref, vx_ref, ss_ref, hg_ref, zg_ref, gt_ref, mkt_ref):
    b = pl.program_id(0)
    i = pl.program_id(1)
    d = x_ref.shape[2]
    r = jnp.where(i < n_ctx_tiles, ctx_row, b)
    mo = mo_ref[pl.ds(r, 1), :]
    shift = mo[:, 0:d]
    scale = mo[:, d:2 * d]
    x = jnp.where(i < n_ctx_tiles, c_ref[0], x_ref[0])
    y = x * lax.rsqrt(jnp.mean(x * x, axis=-1, keepdims=True) + EPS) * g_ref[...]
    h = (y * (1.0 + scale) + shift).astype(BF16)

    ml_ref[0] = _mm(h, w_ref[:, C_ML:C_DAQ])
    q = _mm(h, w_ref[:, C_DAQ:C_DAV])
    half = DA_QK // 2
    lane = _iota2((x.shape[0], LANES), 1)
    q_rot = []
    for c in range(W_BRANCH // LANES):
        qc = q[:, LANES * c:LANES * (c + 1)]
        q_rot.append(jnp.where((lane & half) == 0, -pltpu.roll(qc, LANES - half, 1), pltpu.roll(qc, half, 1)))
    q = q * cos_ref[...] + jnp.concatenate(q_rot, axis=1) * sin_ref[...]
    daq_ref[0] = (q * (DA_QK ** -0.5)).astype(BF16)
    v = _mm(h, w_ref[:, C_DAV:C_SS])
    for hd in range(N_HEADS):
        pair = v[:, LANES * (hd // 2):LANES * (hd // 2 + 1)]
        own = (lane >= HEAD_W) if hd % 2 else (lane < HEAD_W)
        vx_ref[0, 0, hd] = jnp.where(own, pair, jnp.where(lane == (0 if hd % 2 else HEAD_W), 1.0, 0.0)).astype(BF16)
    ss_ref[0] = _mm(h, w_ref[:, C_SS:C_HG])
    hg_ref[0] = _mm(h, w_ref[:, C_HG:C_ZG])
    zg_ref[0] = _mm(h, w_ref[:, C_ZG:C_END]).astype(BF16)
    tr = _mm_nt(wt_ref[...], h)
    kt = tr[0:W_BRANCH]
    kt_rot = []
    for j in range(W_BRANCH // DA_QK):
        kt_rot += [-kt[DA_QK * j + half:DA_QK * (j + 1)], kt[DA_QK * j:DA_QK * j + half]]
    kt_ref[0, 0] = (kt * cost_ref[...] + jnp.concatenate(kt_rot, axis=0) * sint_ref[...]).astype(BF16)
    gt_ref[0] = tr[W_BRANCH:W_BRANCH + GATE_ROWS]
    mkt_ref[0] = tr[W_BRANCH + GATE_ROWS:2 * W_BRANCH + GATE_ROWS]


def _in_proj(stream, mo, g, cos, sin, cos_t, sin_t, w2, wt, n_ctx_tiles, ctx_row):
    c_src, x_src, x_off, t = stream
    bsz, _, d = x_src.shape
    tm = ROW_TILE
    assert tm == KEY_TILE
    row = lambda b, i: (b, i, 0)
    const2 = lambda b, i: (0, 0)
    out_shape = [
        jax.ShapeDtypeStruct((bsz, t, 2 * W_BRANCH), F32),
        jax.ShapeDtypeStruct((bsz, t, W_BRANCH), BF16),
        jax.ShapeDtypeStruct((bsz, t // tm, W_BRANCH, tm), BF16),
        jax.ShapeDtypeStruct((bsz, t // tm, N_HEADS, tm, LANES), BF16),
        jax.ShapeDtypeStruct((bsz, t, 768), F32),
        jax.ShapeDtypeStruct((bsz, t, 4 * W_BRANCH), F32),
        jax.ShapeDtypeStruct((bsz, t, 5 * W_BRANCH), BF16),
        jax.ShapeDtypeStruct((bsz, GATE_ROWS, t), F32),
        jax.ShapeDtypeStruct((bsz, W_BRANCH, t), F32),
    ]
    out_specs = [
        pl.BlockSpec((1, tm, 2 * W_BRANCH), row),
        pl.BlockSpec((1, tm, W_BRANCH), row),
        pl.BlockSpec((1, 1, W_BRANCH, tm), lambda b, i: (b, i, 0, 0)),
        pl.BlockSpec((1, 1, N_HEADS, tm, LANES), lambda b, i: (b, i, 0, 0, 0)),
        pl.BlockSpec((1, tm, 768), row),
        pl.BlockSpec((1, tm, 4 * W_BRANCH), row),
        pl.BlockSpec((1, tm, 5 * W_BRANCH), row),
        pl.BlockSpec((1, GATE_ROWS, tm), lambda b, i: (b, 0, i)),
        pl.BlockSpec((1, W_BRANCH, tm), lambda b, i: (b, 0, i)),
    ]
    return pl.pallas_call(
        functools.partial(_in_kernel, n_ctx_tiles, ctx_row),
        grid=(bsz, t // tm),
        in_specs=_stream_specs(stream, n_ctx_tiles, 0) + [
                  pl.BlockSpec(mo.shape, const2),
                  pl.BlockSpec((1, d), const2),
                  pl.BlockSpec((tm, W_BRANCH), lambda b, i: (i, 0)),
                  pl.BlockSpec((tm, W_BRANCH), lambda b, i: (i, 0)),
                  pl.BlockSpec((W_BRANCH, tm), lambda b, i: (0, i)),
                  pl.BlockSpec((W_BRANCH, tm), lambda b, i: (0, i)),
                  pl.BlockSpec(w2.shape, const2),
                  pl.BlockSpec(wt.shape, const2)],
        out_specs=out_specs,
        out_shape=out_shape,
        compiler_params=_cparams(("parallel", "parallel")),
        name="in_proj",
    )(c_src, x_src, mo, g.reshape(1, d), cos, sin, cos_t, sin_t, w2, wt)


def _bwd_chunk(c, n_ctx_chunks, n_chunks):
    return jnp.where(c < n_ctx_chunks, n_ctx_chunks - 1 - c, n_chunks + n_ctx_chunks - 1 - c)


def _tri_masks(d):
    ri = _iota2((CHUNK, CHUNK), 0)
    ci = _iota2((CHUNK, CHUNK), 1)
    if d == 0:
        return ci <= ri, ri <= ci
    return ci >= ri, ri >= ci


def _split_f32(x, n):
    parts = []
    for _ in range(n):
        p = x.astype(BF16).astype(F32)
        parts.append(p)
        x = x - p
    return parts


def _cummax_lanes(u, d):
    lane = _iota2(u.shape, 1)
    ninf = jnp.float32(-jnp.inf)
    s = 1
    while s < CHUNK:
        if d == 0:
            sh = jnp.where(lane >= s, pltpu.roll(u, s, 1), ninf)
        else:
            sh = jnp.where(lane < CHUNK - s, pltpu.roll(u, CHUNK - s, 1), ninf)
        u = jnp.maximum(u, sh)
        s *= 2
    return u


def _state_row_selector(n_tiles):
    keep = (jnp.arange(CHUNK)[None, :] % 8) == jnp.arange(8)[:, None]
    return jnp.broadcast_to(keep[:, :, None], (8, CHUNK, n_tiles * LANES)).astype(BF16)


def _spread_selector(n_spread):
    t = jnp.arange(CHUNK)[:, None] // 16
    tiles = [jnp.broadcast_to(t == 3 + k, (CHUNK, LANES)) for k in range(n_spread)]
    return jnp.concatenate(tiles, axis=1).astype(BF16)


def _spread_operands(col_sum, row_sum, spread, sel_ref, ones_ref):
    z8 = jnp.zeros((8, CHUNK), F32)
    one8 = ones_ref[0:8, :].astype(F32)
    pieces = _split_f32(col_sum, 3) + [one8] * 3
    for x in spread:
        pieces += _split_f32(x, 2)
    assert len(pieces) <= CHUNK // 8
    lhs = jnp.concatenate(pieces + [z8] * (CHUNK // 8 - len(pieces)), axis=0).T.astype(BF16)
    rows = jnp.concatenate([one8] * 3 + _split_f32(row_sum, 3) + [z8] * (CHUNK // 8 - 6), axis=0)
    return lhs, jnp.concatenate([rows.astype(BF16), sel_ref[...]], axis=1)


def _mlstm_steps(qf_ref, vf_ref, ktf_ref, gtf_ref, qb_ref, vb_ref, ktb_ref, gtb_ref, bcol_ref, ones_ref, sel_ref,
                  rsel_ref, hf_ref, hb_ref, cn_ref, m_ref):
    @pl.when(pl.program_id(1) == 0)
    def _():
        cn_ref[...] = jnp.zeros_like(cn_ref)
        m_ref[...] = jnp.zeros_like(m_ref)

    lane = _iota2((CHUNK, LANES), 1)
    feat = _iota2((LANES, CHUNK), 0)
    row8 = _iota2((8, LANES), 0)
    ones_t = ones_ref[...]
    ninf = jnp.float32(-jnp.inf)
    m_all = m_ref[...]
    heads = [None] * 8
    gates = []
    for d in (0, 1):
        q_ref, v_ref, kt_ref = ((qf_ref, vf_ref, ktf_ref), (qb_ref, vb_ref, ktb_ref))[d]
        for p in range(2):
            q_t = q_ref[0, :, 128 * p:128 * p + 128]
            v_t = v_ref[0, :, 128 * p:128 * p + 128]
            kt_p = kt_ref[0, 128 * p:128 * p + 128, :] * (HEAD_W ** -0.5)
            kt_b = kt_p.astype(BF16)
            for half in range(2):
                r = 4 * d + 2 * p + half
                hmask = (lane >= 64) if half else (lane < 64)
                qm = jnp.where(hmask, q_t, 0.0).astype(BF16)
                cn = cn_ref[r]
                heads[r] = dict(
                    cn=cn, kt_p=kt_p,
                    vw=jnp.concatenate([jnp.where(hmask, v_t, 0.0).astype(BF16), ones_t], axis=1),
                    qk=_mm(qm, kt_b),
                    qc=_mm(qm, cn.astype(BF16)))

    yield
    f_cum = []
    for d in (0, 1):
        gt_ref = (gtf_ref, gtb_ref)[d]
        tr = jnp.where(_tri_masks(d)[1], 1.0, 0.0).astype(BF16)
        f_cum.append(_mm_exact_r(_log_sigmoid(gt_ref[0, 8:16, :] + bcol_ref[8:16, :]), tr))

    yield
    for d in (0, 1):
        gt_ref = (gtf_ref, gtb_ref)[d]
        last = CHUNK - 1 if d == 0 else 0
        li = gt_ref[0, 0:8, :] + bcol_ref[0:8, :]
        f = f_cum[d]
        u = li - f
        m_old = m_all
        mx = jnp.maximum(m_old, _cummax_lanes(u, d))
        w_int = jnp.exp(m_old - mx)
        e_mi = jnp.exp(-(f + mx))
        u_max = jnp.max(u, axis=1, keepdims=True)
        e_end = jnp.exp(u - u_max)
        f_last = f[:, last:last + 1]
        b_end = f_last + u_max
        m_new = jnp.maximum(f_last + m_old, b_end)
        a_dec = jnp.exp(f_last + m_old - m_new)
        g_inc = jnp.exp(b_end - m_new)
        m_all = jnp.where((row8 >> 2) == d, m_new, m_all)
        gates.append((e_end, a_dec, g_inc) + _spread_operands(-mx, u, [w_int, e_mi], sel_ref, ones_ref))

    for r in range(8):
        hd = heads[r]
        e_end, a_dec, g_inc, lhs, rhs = gates[r // 4]
        fmask = (feat >= 64) if r % 2 else (feat < 64)
        ke = (jnp.where(fmask, hd["kt_p"], 0.0) * e_end[r:r + 1, :]).astype(BF16)
        a_r = a_dec[r:r + 1, :]
        g_r = g_inc[r:r + 1, :]
        hd["mt"] = _mm(lhs, rhs * rsel_ref[r])
        hd["cn_new"] = (jnp.concatenate([a_r, a_r], axis=1) * hd["cn"]
                        + jnp.concatenate([g_r, g_r], axis=1) * _mm(ke, hd["vw"]))

    yield
    for r in range(8):
        hd = heads[r]
        vis, _ = _tri_masks(r // 4)
        s = hd["qk"] * jnp.exp(jnp.where(vis, hd["mt"][:, 0:128], ninf))
        hd["sv"] = _mm(s.astype(BF16), hd["vw"])

    yield
    for d in (0, 1):
        out_ref = (hf_ref, hb_ref)[d]
        for p in range(2):
            pair = []
            for half in range(2):
                hd = heads[4 * d + 2 * p + half]
                w_b = hd["mt"][:, 128:256]
                tot = hd["sv"] + jnp.concatenate([w_b, w_b], axis=1) * hd["qc"]
                pair.append(tot[:, 0:128] / jnp.maximum(jnp.abs(tot[:, 128:256]), hd["mt"][:, 256:384]))
            out_ref[0, :, 128 * p:128 * p + 128] = jnp.where(lane < 64, pair[0], pair[1]).astype(BF16)
    m_ref[...] = m_all
    for r in range(8):
        cn_ref[r] = heads[r]["cn_new"]


def _mlstm(ml, mkt, gt, bcol, n_ctx_chunks):
    bsz, t, _ = ml.shape
    nc = t // CHUNK
    out = jax.ShapeDtypeStruct((bsz, t, W_BRANCH), BF16)
    const2 = lambda b, c: (0, 0)

    def chunk_specs(chunk):
        return [pl.BlockSpec((1, CHUNK, W_BRANCH), lambda b, c: (b, chunk(c), 0)),
                pl.BlockSpec((1, CHUNK, W_BRANCH), lambda b, c: (b, chunk(c), 1)),
                pl.BlockSpec((1, W_BRANCH, CHUNK), lambda b, c: (b, 0, chunk(c))),
                pl.BlockSpec((1, GATE_ROWS, CHUNK), lambda b, c: (b, 0, chunk(c)))]

    fwd_chunk = lambda c: c
    bwd_chunk = lambda c: _bwd_chunk(c, n_ctx_chunks, nc)
    return dict(
        steps=_mlstm_steps,
        in_specs=chunk_specs(fwd_chunk) + chunk_specs(bwd_chunk) + [
            pl.BlockSpec((GATE_ROWS, 1), const2), pl.BlockSpec((CHUNK, LANES), const2),
            pl.BlockSpec((CHUNK, 2 * LANES), const2),
            pl.BlockSpec((8, CHUNK, 3 * LANES), lambda b, c: (0, 0, 0))],
        out_specs=[pl.BlockSpec((1, CHUNK, W_BRANCH), lambda b, c: (b, c, 0)),
                   pl.BlockSpec((1, CHUNK, W_BRANCH), lambda b, c: (b, bwd_chunk(c), 0))],
        out_shape=[out, out],
        scratch_shapes=[pltpu.VMEM((8, LANES, 2 * LANES), F32), pltpu.VMEM((8, LANES), F32)],
        args=(ml, ml, mkt, gt, ml, ml, mkt, gt, bcol, jnp.ones((CHUNK, LANES), BF16), _spread_selector(2),
              _state_row_selector(3)))


def _ssd_steps(n_ctx_chunks, n_chunks,
                xf_ref, xfp_ref, xfn_ref, xb_ref, xbp_ref, xbn_ref,
                gtf_ref, gtb_ref, bcol_ref, acol_ref, cw_ref, cb_ref, dskip_ref, ones_ref, sel_ref, rsel_ref,
                yf_ref, yb_ref, s_ref):
    c = pl.program_id(1)

    @pl.when(c == 0)
    def _():
        s_ref[...] = jnp.zeros_like(s_ref)

    lane = _iota2((CHUNK, LANES), 1)
    row768 = _iota2((CHUNK, 768), 0)
    ninf = jnp.float32(-jnp.inf)

    dt, a_cum = [], []
    for d in (0, 1):
        gt_ref = (gtf_ref, gtb_ref)[d]
        tr = jnp.where(_tri_masks(d)[1], 1.0, 0.0).astype(BF16)
        dt.append(_softplus(gt_ref[0, 16:24, :] + bcol_ref[16:24, :]))
        a_cum.append(_mm_exact_r(dt[d] * acol_ref[16:24, :], tr))

    yield
    xa = []
    for d in (0, 1):
        x_ref, xp_ref, xn_ref = ((xf_ref, xfp_ref, xfn_ref), (xb_ref, xbp_ref, xbn_ref))[d]
        j = c if d == 0 else _bwd_chunk(c, n_ctx_chunks, n_chunks)
        seg_first = jnp.logical_or(j == 0, j == n_ctx_chunks)
        seg_last = jnp.logical_or(j == n_ctx_chunks - 1, j == n_chunks - 1)
        x = x_ref[0]
        prev = jnp.where(seg_first, 0.0, xp_ref[0, 7:8, :])
        nxt = jnp.where(seg_last, 0.0, xn_ref[0, 0:1, :])
        x_dn = jnp.where(row768 == 0, prev, pltpu.roll(x, 1, 0))
        x_up = jnp.where(row768 == CHUNK - 1, nxt, pltpu.roll(x, CHUNK - 1, 0))
        xa.append(_silu(x_dn * cw_ref[0:1, :] + x * cw_ref[1:2, :] + x_up * cw_ref[2:3, :] + cb_ref[...]))

    heads = [None] * 8
    groups = {}
    for d in (0, 1):
        for g in range(2):
            x_pair = xa[d][:, 128 * g:128 * g + 128]
            b_g = xa[d][:, 256 + 128 * g:256 + 128 * g + 128]
            c_gb = xa[d][:, 512 + 128 * g:512 + 128 * g + 128].astype(BF16)
            groups[d, g] = dict(x_pair=x_pair, bt=b_g.T.astype(BF16), cb=_mm_nt(c_gb, b_g.astype(BF16)))
            for half in range(2):
                r = 4 * d + 2 * g + half
                hmask = (lane >= 64) if half else (lane < 64)
                st = s_ref[r]
                xm = jnp.where(hmask, x_pair, 0.0)
                heads[r] = dict(st=st, xm=xm, xmb=xm.astype(BF16), cs=_mm(c_gb, st.astype(BF16)))

    yield
    for d in (0, 1):
        last = CHUNK - 1 if d == 0 else 0
        a_last = a_cum[d][:, last:last + 1]
        w_in = jnp.exp(a_last - a_cum[d]) * dt[d]
        e_last = jnp.exp(jnp.broadcast_to(a_last, (8, CHUNK)))
        lhs, rhs = _spread_operands(a_cum[d], -a_cum[d], [jnp.exp(a_cum[d]), w_in], sel_ref, ones_ref)
        for h in range(N_HEADS):
            r = 4 * d + h
            heads[r]["mt"] = _mm(lhs, rhs * rsel_ref[r])
            heads[r]["e_last"] = e_last[r:r + 1, :]

    yield
    for r in range(8):
        d, h = divmod(r, 4)
        hd = heads[r]
        ds = _mm(groups[d, h // 2]["bt"], (hd["xm"] * hd["mt"][:, 256:384]).astype(BF16))
        s_ref[r] = hd["e_last"] * hd["st"] + ds
    for r in range(8):
        d, h = divmod(r, 4)
        hd = heads[r]
        vis, _ = _tri_masks(d)
        sc = groups[d, h // 2]["cb"] * jnp.exp(jnp.where(vis, hd["mt"][:, 0:128], ninf)) * dt[d][r:r + 1, :]
        hd["y"] = _mm(sc.astype(BF16), hd["xmb"]) + hd["mt"][:, 128:256] * hd["cs"]

    for d in (0, 1):
        out_ref = (yf_ref, yb_ref)[d]
        for g in range(2):
            y_pair = heads[4 * d + 2 * g]["y"] + heads[4 * d + 2 * g + 1]["y"]
            if d == 0:
                y_pair = y_pair + dskip_ref[:, 128 * g:128 * g + 128] * groups[d, g]["x_pair"]
            out_ref[0, :, 128 * g:128 * g + 128] = y_pair.astype(BF16)


def _ssd(ss, gt, bcol, acol, conv_w, conv_b, dskip, n_ctx_chunks):
    bsz, t, _ = ss.shape
    nc = t // CHUNK
    sub = CHUNK // 8
    nsub = t // 8
    bc = lambda c: _bwd_chunk(c, n_ctx_chunks, nc)
    fwd = lambda b, c: (b, c, 0)
    bwd = lambda b, c: (b, bc(c), 0)
    fwd_p = lambda b, c: (b, jnp.maximum(c * sub - 1, 0), 0)
    fwd_n = lambda b, c: (b, jnp.minimum((c + 1) * sub, nsub - 1), 0)
    bwd_p = lambda b, c: (b, jnp.maximum(bc(c) * sub - 1, 0), 0)
    bwd_n = lambda b, c: (b, jnp.minimum((bc(c) + 1) * sub, nsub - 1), 0)
    fwd_t = lambda b, c: (b, 0, c)
    bwd_t = lambda b, c: (b, 0, bc(c))
    const2 = lambda b, c: (0, 0)
    out = jax.ShapeDtypeStruct((bsz, t, W_BRANCH), BF16)
    return dict(
        steps=functools.partial(_ssd_steps, n_ctx_chunks, nc),
        in_specs=[pl.BlockSpec((1, CHUNK, 768), fwd), pl.BlockSpec((1, 8, 768), fwd_p),
                  pl.BlockSpec((1, 8, 768), fwd_n),
                  pl.BlockSpec((1, CHUNK, 768), bwd), pl.BlockSpec((1, 8, 768), bwd_p),
                  pl.BlockSpec((1, 8, 768), bwd_n),
                  pl.BlockSpec((1, GATE_ROWS, CHUNK), fwd_t), pl.BlockSpec((1, GATE_ROWS, CHUNK), bwd_t),
                  pl.BlockSpec((GATE_ROWS, 1), const2), pl.BlockSpec((GATE_ROWS, 1), const2),
                  pl.BlockSpec((3, 768), const2), pl.BlockSpec((1, 768), const2),
                  pl.BlockSpec((1, W_BRANCH), const2),
                  pl.BlockSpec((CHUNK, LANES), const2), pl.BlockSpec((CHUNK, 2 * LANES), const2),
                  pl.BlockSpec((8, CHUNK, 3 * LANES), lambda b, c: (0, 0, 0))],
        out_specs=[pl.BlockSpec((1, CHUNK, W_BRANCH), fwd), pl.BlockSpec((1, CHUNK, W_BRANCH), bwd)],
        out_shape=[out, out],
        scratch_shapes=[pltpu.VMEM((8, SSM_N, LANES), F32)],
        args=(ss, ss, ss, ss, ss, ss, gt, gt, bcol, acol, conv_w, conv_b, dskip,
              jnp.ones((CHUNK, LANES), BF16), _spread_selector(2), _state_row_selector(3)))


def _hgrn2_steps(hf_ref, hb_ref, loglb_ref, log1m_ref, onem_ref, e_ref, of_ref, ob_ref,
                  st_ref):
    @pl.when(pl.program_id(1) == 0)
    def _():
        st_ref[...] = jnp.zeros_like(st_ref)

    ri = _iota2((CHUNK, CHUNK), 0)
    ci = _iota2((CHUNK, CHUNK), 1)
    lane = _iota2((CHUNK, LANES), 1)
    rw = _iota2((CHUNK, W_BRANCH), 0)
    blockdiag = (ri >> 6) == (ci >> 6)
    ninf = jnp.float32(-jnp.inf)
    nb = CHUNK // SUB
    rs = _iota2((nb, SUB, W_BRANCH), 1)

    dirs = []
    for d in (0, 1):
        h_ref = (hf_ref, hb_ref)[d]
        tc = jnp.where(_tri_masks(d)[0], 1.0, 0.0).astype(BF16)
        z = h_ref[0, :, W_BRANCH * (1 + d):W_BRANCH * (2 + d)]
        e = jnp.exp(-jnp.abs(z))
        ope = 1.0 + e
        la = loglb_ref[d:d + 1, :]
        lb_ = log1m_ref[d:d + 1, :] + (jnp.minimum(z, 0.0) - jnp.log(ope))
        logf = jnp.maximum(la, lb_) + jnp.log(1.0 + jnp.exp(-jnp.abs(la - lb_)))
        dirs.append(dict(
            q=_silu(h_ref[0, :, 0:W_BRANCH]),
            v=h_ref[0, :, 3 * W_BRANCH:4 * W_BRANCH],
            kk=onem_ref[d:d + 1, :] * (jnp.where(z >= 0.0, e, 1.0) / ope),
            gcum=_mm_exact_l(tc, logf)))

    yield
    for d in (0, 1):
        dd = dirs[d]
        q, kk, gcum = dd["q"], dd["kk"], dd["gcum"]
        last = CHUNK - 1 if d == 0 else 0
        g_last = gcum[last:last + 1, :]
        qg = (q * jnp.exp(gcum)).astype(BF16)
        kg = (kk * jnp.exp(g_last - gcum)).astype(BF16)
        vb = dd["v"].astype(BF16)

        dd["o_inter"] = []
        for p in range(2):
            sl = slice(128 * p, 128 * p + 128)
            st = st_ref[2 * d + p]
            dd["o_inter"].append(_mm_nt(qg[:, sl], st.astype(BF16)))
            dst = _mm_tn(vb[:, sl], kg[:, sl])
            st_ref[2 * d + p] = st * jnp.exp(g_last[:, sl]) + jnp.where(blockdiag, dst, 0.0)

        a_mats = [jnp.zeros((CHUNK, CHUNK), F32) for _ in range(N_HEADS)]
        blk = CHUNK // 2
        while blk >= SUB:
            first = (rw & (2 * blk - 1)) < blk
            edge = (blk - 1) if d == 0 else blk
            gb = gcum.reshape(CHUNK // (2 * blk), 2 * blk, W_BRANCH)[:, edge:edge + 1, :]
            gb = jnp.broadcast_to(gb, (CHUNK // (2 * blk), 2 * blk, W_BRANCH)).reshape(CHUNK, W_BRANCH)
            q_side = jnp.logical_not(first) if d == 0 else first
            qt = q * jnp.exp(jnp.where(q_side, gcum - gb, ninf))
            kt = (kk * jnp.exp(jnp.where(q_side, ninf, gb - gcum))).astype(BF16)
            same = (ri >> int(math.log2(2 * blk))) == (ci >> int(math.log2(2 * blk)))
            for h in range(N_HEADS):
                p, half = divmod(h, 2)
                hmask = (lane >= 64) if half else (lane < 64)
                qh = jnp.where(hmask, qt[:, 128 * p:128 * p + 128], 0.0).astype(BF16)
                a_mats[h] = a_mats[h] + jnp.where(same, _mm_nt(qh, kt[:, 128 * p:128 * p + 128]), 0.0)
            blk //= 2
        dd["a_mats"] = a_mats

        g3 = gcum.reshape(nb, SUB, W_BRANCH)
        q3 = q.reshape(nb, SUB, W_BRANCH)
        k3 = kk.reshape(nb, SUB, W_BRANCH)
        v3 = dd["v"].reshape(nb, SUB, W_BRANCH)
        o3 = None
        for j in range(SUB):
            ok = (rs >= j) if d == 0 else (rs <= j)
            pj = q3 * jnp.exp(jnp.where(ok, g3 - g3[:, j:j + 1, :], ninf)) * k3[:, j:j + 1, :]
            red = _mm(pj.reshape(CHUNK, W_BRANCH).astype(BF16), e_ref[...])
            term = red.reshape(nb, SUB, W_BRANCH) * v3[:, j:j + 1, :]
            o3 = term if o3 is None else o3 + term
        dd["o_diag"] = o3.reshape(CHUNK, W_BRANCH)

    yield
    for d in (0, 1):
        dd = dirs[d]
        out_ref = (of_ref, ob_ref)[d]
        for p in range(2):
            sl = slice(128 * p, 128 * p + 128)
            a_cat = jnp.concatenate([dd["a_mats"][2 * p], dd["a_mats"][2 * p + 1]], axis=1).astype(BF16)
            v_p = dd["v"][:, sl]
            v_cat = jnp.concatenate([jnp.where(lane < 64, v_p, 0.0), jnp.where(lane >= 64, v_p, 0.0)],
                                    axis=0).astype(BF16)
            out_ref[0, :, sl] = (dd["o_inter"][p] + _mm(a_cat, v_cat) + dd["o_diag"][:, sl]).astype(BF16)


def _hgrn2(hg, loglb, log1m, onem, e64, n_ctx_chunks):
    bsz, t, _ = hg.shape
    nc = t // CHUNK
    fwd = lambda b, c: (b, c, 0)
    bwd = lambda b, c: (b, _bwd_chunk(c, n_ctx_chunks, nc), 0)
    const2 = lambda b, c: (0, 0)
    out = jax.ShapeDtypeStruct((bsz, t, W_BRANCH), BF16)
    return dict(
        steps=_hgrn2_steps,
        in_specs=[pl.BlockSpec((1, CHUNK, 1024), fwd), pl.BlockSpec((1, CHUNK, 1024), bwd),
                  pl.BlockSpec((2, W_BRANCH), const2), pl.BlockSpec((2, W_BRANCH), const2),
                  pl.BlockSpec((2, W_BRANCH), const2), pl.BlockSpec((W_BRANCH, W_BRANCH), const2)],
        out_specs=[pl.BlockSpec((1, CHUNK, W_BRANCH), fwd), pl.BlockSpec((1, CHUNK, W_BRANCH), bwd)],
        out_shape=[out, out],
        scratch_shapes=[pltpu.VMEM((4, LANES, LANES), F32)],
        args=(hg, hg, loglb, log1m, onem, e64))


def _scan_kernel(parts, *refs):
    n_in = [len(p["in_specs"]) for p in parts]
    n_out = [len(p["out_specs"]) for p in parts]
    n_scr = [len(p["scratch_shapes"]) for p in parts]
    ins, outs, scr = refs[:sum(n_in)], refs[sum(n_in):sum(n_in) + sum(n_out)], refs[sum(n_in) + sum(n_out):]
    gens = []
    for k, p in enumerate(parts):
        mine = (ins[sum(n_in[:k]):sum(n_in[:k + 1])] + outs[sum(n_out[:k]):sum(n_out[:k + 1])]
                + scr[sum(n_scr[:k]):sum(n_scr[:k + 1])])
        gens.append(p["steps"](*mine))
    while gens:
        for g in list(gens):
            if next(g, gens) is gens:
                gens.remove(g)


def _chunk_scans(parts, bsz, n_chunks):
    return pl.pallas_call(
        functools.partial(_scan_kernel, parts),
        grid=(bsz, n_chunks),
        in_specs=[s for p in parts for s in p["in_specs"]],
        out_specs=[s for p in parts for s in p["out_specs"]],
        out_shape=[s for p in parts for s in p["out_shape"]],
        scratch_shapes=[s for p in parts for s in p["scratch_shapes"]],
        compiler_params=_cparams(("parallel", "arbitrary")),
        name="chunk_scans",
    )(*[a for p in parts for a in p["args"]])


def _attn(daq, kt, vx, lam_vecs, lam_init, n_ctx_tiles, skip_ctx):
    bsz, t, _ = daq.shape
    tq = ROW_TILE
    n_k_tiles = t // KEY_TILE
    n_lat_tiles = n_k_tiles - n_ctx_tiles
    group = math.gcd(KEY_GROUP, n_lat_tiles)
    q_off = n_ctx_tiles if skip_ctx else 0
    nq = t // tq - q_off
    n_maps = W_BRANCH // DA_QK
    rows = n_maps * tq

    def kern(q_ref, kt_ref, vx_ref, lv_ref, o_ref, q8_ref, m_ref, acc_ref, s_ref, mx_ref):
        qi = pl.program_id(1) + q_off
        lane = _iota2((tq, W_BRANCH), 1)
        q = q_ref[0]
        for j in range(n_maps):
            q8_ref[j * tq:(j + 1) * tq, :] = jnp.where((lane >> 5) == j, q, jnp.zeros_like(q))

        def score_tile(t):
            return _mm(q8_ref[...], kt_ref[0, t])
        m_ref[...] = jnp.full_like(m_ref, -jnp.inf)
        acc_ref[...] = jnp.zeros_like(acc_ref)

        def update(t0, n, after_scores=None):
            s = [score_tile(t0 + i) for i in range(n)]
            if after_scores is not None:
                after_scores()
            mx = s[0][:, 0:LANES]
            for i in range(n):
                for c in range(KEY_TILE // LANES):
                    if i or c:
                        mx = jnp.maximum(mx, s[i][:, LANES * c:LANES * (c + 1)])
            m_old = m_ref[...]
            m_new = jnp.maximum(m_old, jnp.max(mx, axis=1, keepdims=True))
            alpha = jnp.exp(m_old - m_new)
            m_ref[...] = m_new
            m2 = jnp.concatenate([m_new] * (KEY_TILE // LANES), axis=1)
            p = [jnp.exp(s[i] - m2).astype(BF16) for i in range(n)]
            for hd in range(N_HEADS):
                r0 = 2 * hd * tq
                pv = _mm(p[0][r0:r0 + 2 * tq], vx_ref[0, t0, hd])
                for i in range(1, n):
                    pv = pv + _mm(p[i][r0:r0 + 2 * tq], vx_ref[0, t0 + i, hd])
                acc_ref[r0:r0 + 2 * tq, :] = alpha[r0:r0 + 2 * tq] * acc_ref[r0:r0 + 2 * tq, :] + pv

        def scores(g, slot):
            t0 = n_ctx_tiles + g * group
            mx = None
            for i in range(group):
                s = score_tile(t0 + i)
                s_ref[slot, :, KEY_TILE * i:KEY_TILE * (i + 1)] = s
                for c in range(KEY_TILE // LANES):
                    part = s[:, LANES * c:LANES * (c + 1)]
                    mx = part if mx is None else jnp.maximum(mx, part)
            mx_ref[slot] = mx

        def absorb(g, slot):
            t0 = n_ctx_tiles + g * group
            m_old = m_ref[...]
            m_new = jnp.maximum(m_old, jnp.max(mx_ref[slot], axis=1, keepdims=True))
            alpha = jnp.exp(m_old - m_new)
            m_ref[...] = m_new
            m2 = jnp.concatenate([m_new] * (KEY_TILE // LANES), axis=1)
            p = [jnp.exp(s_ref[slot, :, KEY_TILE * i:KEY_TILE * (i + 1)] - m2).astype(BF16) for i in range(group)]
            for hd in range(N_HEADS):
                r0 = 2 * hd * tq
                pv = _mm(p[0][r0:r0 + 2 * tq], vx_ref[0, t0, hd])
                for i in range(1, group):
                    pv = pv + _mm(p[i][r0:r0 + 2 * tq], vx_ref[0, t0 + i, hd])
                acc_ref[r0:r0 + 2 * tq, :] = alpha[r0:r0 + 2 * tq] * acc_ref[r0:r0 + 2 * tq, :] + pv

        @pl.when(qi < n_ctx_tiles)
        def _():
            update(0, n_ctx_tiles)

        @pl.when(qi >= n_ctx_tiles)
        def _():
            n_groups = n_lat_tiles // group
            n_pairs = (n_groups - 1) // 2
            update(0, n_ctx_tiles, after_scores=lambda: scores(0, 0))

            def body(k, carry):
                g = 2 * k
                scores(g + 1, 1)
                absorb(g, 0)
                scores(g + 2, 0)
                absorb(g + 1, 1)
                return carry
            for k in range(n_pairs):
                body(k, 0)
            g0 = 2 * n_pairs
            if n_groups - g0 == 2:
                scores(g0 + 1, 1)
            absorb(g0, 0)
            if n_groups - g0 == 2:
                absorb(g0 + 1, 1)

        lv = lv_ref[...]
        lam = (jnp.exp(jnp.sum(lv[0:1, :] * lv[1:2, :], axis=1, keepdims=True))
               - jnp.exp(jnp.sum(lv[2:3, :] * lv[3:4, :], axis=1, keepdims=True)) + lam_init)
        lane1 = _iota2((tq, LANES), 1)
        for pr in range(N_HEADS // 2):
            halves = []
            for half in range(2):
                r0 = 2 * (2 * pr + half) * tq
                one = 0 if half else HEAD_W
                a0 = acc_ref[r0:r0 + tq, :]
                a1 = acc_ref[r0 + tq:r0 + 2 * tq, :]
                halves.append(a0 / a0[:, one:one + 1] - lam * (a1 / a1[:, one:one + 1]))
            o_ref[0, :, LANES * pr:LANES * (pr + 1)] = jnp.where(lane1 < HEAD_W, halves[0], halves[1]).astype(BF16)

    return pl.pallas_call(
        kern,
        grid=(bsz, nq),
        in_specs=[pl.BlockSpec((1, tq, W_BRANCH), lambda b, i: (b, i + q_off, 0)),
                  pl.BlockSpec((1, n_k_tiles, W_BRANCH, KEY_TILE), lambda b, i: (b, 0, 0, 0),
                               pipeline_mode=pl.Buffered(1)),
                  pl.BlockSpec((1, n_k_tiles, N_HEADS, KEY_TILE, LANES), lambda b, i: (b, 0, 0, 0, 0),
                               pipeline_mode=pl.Buffered(1)),
                  pl.BlockSpec(lam_vecs.shape, lambda b, i: (0, 0))],
        out_specs=pl.BlockSpec((1, tq, W_BRANCH), lambda b, i: (b, i, 0)),
        out_shape=jax.ShapeDtypeStruct((bsz, nq * tq, W_BRANCH), BF16),
        scratch_shapes=[pltpu.VMEM((rows, W_BRANCH), BF16),
                        pltpu.VMEM((rows, LANES), F32),
                        pltpu.VMEM((rows, LANES), F32),
                        pltpu.VMEM((2, rows, group * KEY_TILE), F32),
                        pltpu.VMEM((2, rows, LANES), F32)],
        compiler_params=_cparams(("parallel", "parallel")),
        name="diff_attn",
    )(daq, kt, vx, lam_vecs)


def _seg_mean(x, e_ref):
    x0 = x.astype(BF16)
    x1 = (x - x0.astype(F32)).astype(BF16)
    return (_mm(x0, e_ref[...]) + _mm(x1, e_ref[...])) * (1.0 / HEAD_W)


def _out_kernel(n_ctx_tiles, ctx_row, q_off, lam_init, final,
                c_ref, x_ref, mo_ref, mlo_ref, mlz_ref, mhf_ref, mhb_ref, dao_ref, daz_ref,
                syf_ref, syb_ref, ssz_ref, hof_ref, hob_ref, hgz_ref,
                mlg_ref, dag_ref, ssg_ref, hgg_ref, e_ref, w_ref, fg_ref, o_ref):
    b = pl.program_id(0)
    i = pl.program_id(1) + q_off
    d = x_ref.shape[2]
    r = jnp.where(i < n_ctx_tiles, ctx_row, b)
    gate = mo_ref[pl.ds(r, 1), :][:, 2 * d:3 * d]

    u = _sigmoid(mlo_ref[0].astype(F32)) * (mhf_ref[0].astype(F32) + mhb_ref[0].astype(F32))
    dev = u - _seg_mean(u, e_ref)
    y_ml = dev * lax.rsqrt(_seg_mean(dev * dev, e_ref) + EPS) * mlg_ref[...] * _silu(mlz_ref[0].astype(F32))

    o = dao_ref[0].astype(F32)
    y_da = (o * lax.rsqrt(_seg_mean(o * o, e_ref) + EPS) * dag_ref[...]) * (1.0 - lam_init) * _silu(daz_ref[0].astype(F32))

    ys = (syf_ref[0].astype(F32) + syb_ref[0].astype(F32)) * _silu(ssz_ref[0].astype(F32))
    parts = []
    for g in range(2):
        yg = ys[:, 128 * g:128 * g + 128]
        parts.append(yg * lax.rsqrt(jnp.mean(yg * yg, axis=-1, keepdims=True) + EPS))
    y_ss = jnp.concatenate(parts, axis=1) * ssg_ref[...]

    oh = hof_ref[0].astype(F32) + hob_ref[0].astype(F32)
    y_hg = (oh * lax.rsqrt(_seg_mean(oh * oh, e_ref) + EPS) * hgg_ref[...]) * _silu(hgz_ref[0].astype(F32))

    acc = _mm(y_ml.astype(BF16), w_ref[0:W_BRANCH, :])
    acc = acc + _mm(y_da.astype(BF16), w_ref[W_BRANCH:2 * W_BRANCH, :])
    acc = acc + _mm(y_ss.astype(BF16), w_ref[2 * W_BRANCH:3 * W_BRANCH, :])
    acc = acc + _mm(y_hg.astype(BF16), w_ref[3 * W_BRANCH:4 * W_BRANCH, :])
    x_new = jnp.where(i < n_ctx_tiles, c_ref[0], x_ref[0]) + gate * acc
    if final:
        x_new = x_new * lax.rsqrt(jnp.mean(x_new * x_new, axis=-1, keepdims=True) + EPS) * fg_ref[...]
    o_ref[0] = x_new


def _out_proj(stream, mo, zg, mhf, mhb, dao, syf, syb, hof, hob,
              mlg, dag, ssg, hgg, e64, w_out, final_g, lam_init, n_ctx_tiles, ctx_row, final):
    c_src, x_src, x_off, t = stream
    bsz, _, d = x_src.shape
    tm = ROW_TILE
    q_off = n_ctx_tiles if final else 0
    nrow = t // tm - q_off
    row = lambda b, i: (b, i + q_off, 0)
    col = lambda k: (lambda b, i: (b, i + q_off, k))
    const2 = lambda b, i: (0, 0)
    wb = pl.BlockSpec((1, tm, W_BRANCH), row)
    return pl.pallas_call(
        functools.partial(_out_kernel, n_ctx_tiles, ctx_row, q_off, lam_init, final),
        grid=(bsz, nrow),
        in_specs=_stream_specs(stream, n_ctx_tiles, q_off) + [
                  pl.BlockSpec(mo.shape, const2),
                  pl.BlockSpec((1, tm, W_BRANCH), col(0)), pl.BlockSpec((1, tm, W_BRANCH), col(1)),
                  wb, wb,
                  pl.BlockSpec((1, tm, W_BRANCH), lambda b, i: (b, i, 0)) if final else wb,
                  pl.BlockSpec((1, tm, W_BRANCH), col(2)),
                  wb, wb, pl.BlockSpec((1, tm, W_BRANCH), col(3)),
                  wb, wb, pl.BlockSpec((1, tm, W_BRANCH), col(4)),
                  pl.BlockSpec((1, W_BRANCH), const2), pl.BlockSpec((1, W_BRANCH), const2),
                  pl.BlockSpec((1, W_BRANCH), const2), pl.BlockSpec((1, W_BRANCH), const2),
                  pl.BlockSpec((W_BRANCH, W_BRANCH), const2),
                  pl.BlockSpec(w_out.shape, const2),
                  pl.BlockSpec((1, d), const2)],
        out_specs=pl.BlockSpec((1, tm, d), lambda b, i: (b, i, 0)),
        out_shape=jax.ShapeDtypeStruct((bsz, nrow * tm, d), F32),
        compiler_params=_cparams(("parallel", "parallel")),
        name="out_proj",
    )(c_src, x_src, mo, zg, zg, mhf, mhb, dao, zg, syf, syb, zg, hof, hob, zg,
      mlg, dag, ssg, hgg, e64, w_out, final_g.reshape(1, d))


def _relayout_w_in(w_in):
    d = w_in.shape[0]
    o = {}
    off = 0
    for name, n in (("ml_q", 256), ("ml_k", 256), ("ml_v", 256), ("ml_o", 256), ("ml_i", 8), ("ml_f", 8),
                    ("ml_z", 256), ("da_q", 256), ("da_k", 256), ("da_v", 256), ("da_z", 256),
                    ("ss_xbc", 768), ("ss_dt", 8), ("ss_z", 256),
                    ("hg_q", 256), ("hg_f", 512), ("hg_i", 256), ("hg_z", 256)):
        o[name] = w_in[:, off:off + n]
        off += n

    gates = jnp.concatenate([o["ml_i"], o["ml_f"], o["ss_dt"], jnp.zeros((d, GATE_ROWS - 24), w_in.dtype)], axis=1)
    w2 = jnp.concatenate([o["ml_q"], o["ml_v"], o["da_q"], o["da_v"],
                          o["ss_xbc"], o["hg_q"], o["hg_f"], o["hg_i"],
                          o["ml_o"], o["ml_z"], o["da_z"], o["ss_z"], o["hg_z"]], axis=1)
    wt = jnp.concatenate([o["da_k"], gates, o["ml_k"]], axis=1).T
    return w2.astype(BF16), wt.astype(BF16)


def _rope_tables(n_ctx, seq):
    pos = jnp.arange(seq)
    rows = (pos // GRID_W).astype(F32)
    cols = (pos % GRID_W).astype(F32)
    axis = DA_QK // 2
    inv = ROPE_BASE ** (-jnp.arange(0, axis, 2, dtype=F32) / axis)
    ang = jnp.concatenate([rows[:, None] * inv, cols[:, None] * inv], axis=-1)
    cos = jnp.concatenate([jnp.ones((n_ctx, axis), F32), jnp.cos(ang)], axis=0)
    sin = jnp.concatenate([jnp.zeros((n_ctx, axis), F32), jnp.sin(ang)], axis=0)
    reps = W_BRANCH // axis
    return jnp.tile(cos, (1, reps)), jnp.tile(sin, (1, reps))


def _gate_col(vals):
    v = jnp.concatenate([vals.astype(F32), jnp.zeros((GATE_ROWS - vals.shape[0],), F32)])
    return v.reshape(GATE_ROWS, 1)


def kernel(x, c, ctx, c_ctx, w_mod, b_mod, norm_g, w_in, w_out, ml_gate_b, ml_norm_g, da_lambda, da_norm_g,
           ss_conv_w, ss_conv_b, ss_dt_bias, ss_a_log, ss_d, ss_norm_g, hg_lower, hg_norm_g, final_g):
    bsz, seq, d = x.shape
    n_ctx = ctx.shape[1]
    depth = w_mod.shape[0]
    assert n_ctx % ROW_TILE == 0 and seq % ROW_TILE == 0 and bsz < 8
    n_ctx_tiles = n_ctx // ROW_TILE
    n_ctx_chunks = n_ctx // CHUNK
    ctx_row = bsz

    stream = (ctx, x, 0, n_ctx + seq)
    cc = jnp.concatenate([c, c_ctx[None, :], jnp.zeros((8 - bsz - 1, d), F32)], axis=0)
    cos, sin = _rope_tables(n_ctx, seq)
    cos_t, sin_t = cos.T, sin.T
    lb_all = jnp.cumsum(jax.nn.softmax(hg_lower.astype(F32), axis=1), axis=1)
    lb_all = lb_all - lb_all[:, :1]
    hid = _iota2((W_BRANCH, W_BRANCH), 0) // HEAD_W
    e64 = (hid == hid.T).astype(BF16)

    out = None
    for l in range(depth):
        lam_init = 0.8 - 0.6 * math.exp(-0.3 * l)
        final = l == depth - 1
        w2, wt = _relayout_w_in(w_in[l])
        mo = _modulation(cc, w_mod[l], b_mod[l])
        ml, daq, dak, dav, ss, hg, zg, gt, mkt = _in_proj(stream, mo, norm_g[l], cos, sin, cos_t, sin_t, w2, wt,
                                                     n_ctx_tiles, ctx_row)

        gb = ml_gate_b[l]
        zeros8 = jnp.zeros((8,), F32)
        bcol = _gate_col(jnp.concatenate([gb[:, 0].reshape(-1), gb[:, 1].reshape(-1),
                                                ss_dt_bias[l].reshape(-1)]))
        acol = _gate_col(jnp.concatenate([zeros8, zeros8, -jnp.exp(ss_a_log[l].astype(F32)).reshape(-1)]))
        dao = _attn(daq, dak, dav, da_lambda[l].astype(F32), lam_init, n_ctx_tiles, final)
        dskip = jnp.repeat(ss_d[l].astype(F32), HEAD_W).reshape(1, W_BRANCH)
        lbh = lb_all[:, l]
        scans = [_mlstm(ml, mkt, gt, bcol, n_ctx_chunks),
                 _ssd(ss, gt, bcol, acol, ss_conv_w[l], ss_conv_b[l].reshape(1, -1), dskip, n_ctx_chunks),
                 _hgrn2(hg, jnp.log(lbh), jnp.log1p(-lbh), 1.0 - lbh, e64, n_ctx_chunks)]
        mhf, mhb, syf, syb, hof, hob = _chunk_scans(scans, bsz, (n_ctx + seq) // CHUNK)

        res = _out_proj(stream, mo, zg, mhf, mhb, dao, syf, syb, hof, hob,
                        ml_norm_g[l].reshape(1, -1), jnp.tile(da_norm_g[l], N_HEADS).reshape(1, -1),
                        ss_norm_g[l].reshape(1, -1), hg_norm_g[l].reshape(1, -1), e64,
                        w_out[l].astype(BF16), final_g, lam_init, n_ctx_tiles, ctx_row, final)
        if final:
            out = res
        else:
            stream = (res, res, n_ctx_tiles, n_ctx + seq)
    return out
```

```python
import functools
import math

import jax
import jax.numpy as jnp
from jax import lax
from jax.experimental import pallas as pl
from jax.experimental.pallas import tpu as pltpu

F32 = jnp.float32
BF16 = jnp.bfloat16

EPS = 1e-6
GRID_W = 64
ROPE_BASE = 10000.0
N_HEADS = 4
HEAD_W = 64
W_BRANCH = N_HEADS * HEAD_W
DA_QK = 32
SSM_N = 128
CHUNK = 128
ROW_TILE = 256
SUB = 8
LANES = 128
VMEM_LIMIT = 56 * 1024 * 1024

C_ML = 0
C_DAQ = 512
C_DAV = 768
C_SS = 1024
C_HG = 1792
C_ZG = 2816
C_END = 4096
GATE_ROWS = 32
KEY_TILE = 256
KEY_GROUP = 4

NT = (((1,), (1,)), ((), ()))
TN = (((0,), (0,)), ((), ()))


def _mm(a, b):
    return jnp.dot(a, b, preferred_element_type=F32)


def _mm_nt(a, b):
    return lax.dot_general(a, b, NT, preferred_element_type=F32)


def _mm_tn(a, b):
    return lax.dot_general(a, b, TN, preferred_element_type=F32)


def _split3(x):
    x0 = x.astype(BF16)
    r = x - x0.astype(F32)
    x1 = r.astype(BF16)
    x2 = (r - x1.astype(F32)).astype(BF16)
    return x0, x1, x2


def _mm_exact_l(t, x):
    x0, x1, x2 = _split3(x)
    return _mm(t, x0) + _mm(t, x1) + _mm(t, x2)


def _mm_exact_r(x, t):
    x0, x1, x2 = _split3(x)
    return _mm(x0, t) + _mm(x1, t) + _mm(x2, t)


def _sigmoid(x):
    return 1.0 / (1.0 + jnp.exp(-x))


def _silu(x):
    return x * _sigmoid(x)


def _log_sigmoid(x):
    return jnp.minimum(x, 0.0) - jnp.log(1.0 + jnp.exp(-jnp.abs(x)))


def _softplus(x):
    return jnp.maximum(x, 0.0) + jnp.log(1.0 + jnp.exp(-jnp.abs(x)))


def _iota2(shape, axis):
    return lax.broadcasted_iota(jnp.int32, shape, axis)


def _cparams(sem):
    return pltpu.CompilerParams(dimension_semantics=sem, vmem_limit_bytes=VMEM_LIMIT)


def _mod_kernel(c_ref, w_ref, b_ref, o_ref):
    c = c_ref[...]
    o_ref[...] = jnp.dot(_silu(c), w_ref[...], precision=lax.Precision.HIGHEST,
                         preferred_element_type=F32) + b_ref[...]


def _modulation(cc, w_mod, b_mod):
    d = cc.shape[1]
    n = w_mod.shape[1]
    tn = 1536
    assert n % tn == 0
    return pl.pallas_call(
        _mod_kernel,
        grid=(n // tn,),
        in_specs=[pl.BlockSpec((8, d), lambda j: (0, 0)),
                  pl.BlockSpec((d, tn), lambda j: (0, j)),
                  pl.BlockSpec((1, tn), lambda j: (0, j))],
        out_specs=pl.BlockSpec((8, tn), lambda j: (0, j)),
        out_shape=jax.ShapeDtypeStruct((8, n), F32),
        compiler_params=_cparams(("arbitrary",)),
        name="modulation",
    )(cc, w_mod, b_mod.reshape(1, n))


def _stream_specs(stream, n_ctx_tiles, first_tile):
    _, x_src, x_off, _ = stream
    d = x_src.shape[2]
    return [pl.BlockSpec((1, ROW_TILE, d), lambda b, i: (b, jnp.minimum(i + first_tile, n_ctx_tiles - 1), 0)),
            pl.BlockSpec((1, ROW_TILE, d), lambda b, i: (b, jnp.maximum(i + first_tile - n_ctx_tiles, 0) + x_off, 0))]


def _in_kernel(n_ctx_tiles, ctx_row, c_ref, x_ref, mo_ref, g_ref, cos_ref, sin_ref, cost_ref, sint_ref,
               w_ref, wt_ref,
               ml_ref, daq_ref, kt_ref, vx_ref, ss_ref, hg_ref, zg_ref, gt_ref, mkt_ref):
    b = pl.program_id(0)
    i = pl.program_id(1)
    d = x_ref.shape[2]
    r = jnp.where(i < n_ctx_tiles, ctx_row, b)
    mo = mo_ref[pl.ds(r, 1), :]
    shift = mo[:, 0:d]
    scale = mo[:, d:2 * d]
    x = jnp.where(i < n_ctx_tiles, c_ref[0], x_ref[0])
    y = x * lax.rsqrt(jnp.mean(x * x, axis=-1, keepdims=True) + EPS) * g_ref[...]
    h = (y * (1.0 + scale) + shift).astype(BF16)

    ml_ref[0] = _mm(h, w_ref[:, C_ML:C_DAQ])
    q = _mm(h, w_ref[:, C_DAQ:C_DAV])
    half = DA_QK // 2
    lane = _iota2((x.shape[0], LANES), 1)
    q_rot = []
    for c in range(W_BRANCH // LANES):
        qc = q[:, LANES * c:LANES * (c + 1)]
        q_rot.append(jnp.where((lane & half) == 0, -pltpu.roll(qc, LANES - half, 1), pltpu.roll(qc, half, 1)))
    n_rep = W_BRANCH // LANES
    q = (q * jnp.concatenate([cos_ref[...]] * n_rep, axis=1)
         + jnp.concatenate(q_rot, axis=1) * jnp.concatenate([sin_ref[...]] * n_rep, axis=1))
    daq_ref[0] = (q * (DA_QK ** -0.5)).astype(BF16)
    v = _mm(h, w_ref[:, C_DAV:C_SS])
    for hd in range(N_HEADS):
        pair = v[:, LANES * (hd // 2):LANES * (hd // 2 + 1)]
        own = (lane >= HEAD_W) if hd % 2 else (lane < HEAD_W)
        vx_ref[0, 0, hd] = jnp.where(own, pair, jnp.where(lane == (0 if hd % 2 else HEAD_W), 1.0, 0.0)).astype(BF16)
    ss_ref[0] = _mm(h, w_ref[:, C_SS:C_HG])
    hg_ref[0] = _mm(h, w_ref[:, C_HG:C_ZG])
    zg_ref[0] = _mm(h, w_ref[:, C_ZG:C_END]).astype(BF16)
    tr = _mm_nt(wt_ref[...], h)
    kt = tr[0:W_BRANCH]
    kt_rot = []
    for j in range(W_BRANCH // DA_QK):
        kt_rot += [-kt[DA_QK * j + half:DA_QK * (j + 1)], kt[DA_QK * j:DA_QK * j + half]]
    n_grp = W_BRANCH // DA_QK
    kt_ref[0, 0] = (kt * jnp.concatenate([cost_ref[...]] * n_grp, axis=0)
                    + jnp.concatenate(kt_rot, axis=0) * jnp.concatenate([sint_ref[...]] * n_grp, axis=0)).astype(BF16)
    gt_ref[0] = tr[W_BRANCH:W_BRANCH + GATE_ROWS]
    mkt_ref[0] = tr[W_BRANCH + GATE_ROWS:2 * W_BRANCH + GATE_ROWS]


def _in_proj(stream, mo, g, cos, sin, cos_t, sin_t, w2, wt, n_ctx_tiles, ctx_row):
    c_src, x_src, x_off, t = stream
    bsz, _, d = x_src.shape
    tm = ROW_TILE
    assert tm == KEY_TILE
    row = lambda b, i: (b, i, 0)
    const2 = lambda b, i: (0, 0)
    out_shape = [
        jax.ShapeDtypeStruct((bsz, t, 2 * W_BRANCH), F32),
        jax.ShapeDtypeStruct((bsz, t, W_BRANCH), BF16),
        jax.ShapeDtypeStruct((bsz, t // tm, W_BRANCH, tm), BF16),
        jax.ShapeDtypeStruct((bsz, t // tm, N_HEADS, tm, LANES), BF16),
        jax.ShapeDtypeStruct((bsz, t, 768), F32),
        jax.ShapeDtypeStruct((bsz, t, 4 * W_BRANCH), F32),
        jax.ShapeDtypeStruct((bsz, t, 5 * W_BRANCH), BF16),
        jax.ShapeDtypeStruct((bsz, GATE_ROWS, t), F32),
        jax.ShapeDtypeStruct((bsz, W_BRANCH, t), F32),
    ]
    out_specs = [
        pl.BlockSpec((1, tm, 2 * W_BRANCH), row),
        pl.BlockSpec((1, tm, W_BRANCH), row),
        pl.BlockSpec((1, 1, W_BRANCH, tm), lambda b, i: (b, i, 0, 0)),
        pl.BlockSpec((1, 1, N_HEADS, tm, LANES), lambda b, i: (b, i, 0, 0, 0)),
        pl.BlockSpec((1, tm, 768), row),
        pl.BlockSpec((1, tm, 4 * W_BRANCH), row),
        pl.BlockSpec((1, tm, 5 * W_BRANCH), row),
        pl.BlockSpec((1, GATE_ROWS, tm), lambda b, i: (b, 0, i)),
        pl.BlockSpec((1, W_BRANCH, tm), lambda b, i: (b, 0, i)),
    ]
    return pl.pallas_call(
        functools.partial(_in_kernel, n_ctx_tiles, ctx_row),
        grid=(bsz, t // tm),
        in_specs=_stream_specs(stream, n_ctx_tiles, 0) + [
                  pl.BlockSpec(mo.shape, const2),
                  pl.BlockSpec((1, d), const2),
                  pl.BlockSpec((tm, LANES), lambda b, i: (i, 0)),
                  pl.BlockSpec((tm, LANES), lambda b, i: (i, 0)),
                  pl.BlockSpec((DA_QK, tm), lambda b, i: (0, i)),
                  pl.BlockSpec((DA_QK, tm), lambda b, i: (0, i)),
                  pl.BlockSpec(w2.shape, const2),
                  pl.BlockSpec(wt.shape, const2)],
        out_specs=out_specs,
        out_shape=out_shape,
        compiler_params=_cparams(("parallel", "parallel")),
        name="in_proj",
    )(c_src, x_src, mo, g.reshape(1, d), cos, sin, cos_t, sin_t, w2, wt)


def _bwd_chunk(c, n_ctx_chunks, n_chunks):
    return jnp.where(c < n_ctx_chunks, n_ctx_chunks - 1 - c, n_chunks + n_ctx_chunks - 1 - c)


def _tri_masks(d):
    ri = _iota2((CHUNK, CHUNK), 0)
    ci = _iota2((CHUNK, CHUNK), 1)
    if d == 0:
        return ci <= ri, ri <= ci
    return ci >= ri, ri >= ci


def _split_f32(x, n):
    parts = []
    for _ in range(n):
        p = x.astype(BF16).astype(F32)
        parts.append(p)
        x = x - p
    return parts


def _cummax_lanes(u, d):
    lane = _iota2(u.shape, 1)
    ninf = jnp.float32(-jnp.inf)
    s = 1
    while s < CHUNK:
        if d == 0:
            sh = jnp.where(lane >= s, pltpu.roll(u, s, 1), ninf)
        else:
            sh = jnp.where(lane < CHUNK - s, pltpu.roll(u, CHUNK - s, 1), ninf)
        u = jnp.maximum(u, sh)
        s *= 2
    return u


def _state_row_selector(n_tiles):
    keep = (jnp.arange(CHUNK)[None, :] % 8) == jnp.arange(8)[:, None]
    return jnp.broadcast_to(keep[:, :, None], (8, CHUNK, n_tiles * LANES)).astype(BF16)


def _spread_selector(n_spread):
    t = jnp.arange(CHUNK)[:, None] // 16
    tiles = [jnp.broadcast_to(t == 3 + k, (CHUNK, LANES)) for k in range(n_spread)]
    return jnp.concatenate(tiles, axis=1).astype(BF16)


def _spread_operands(col_sum, row_sum, spread, sel_ref, ones_ref):
    z8 = jnp.zeros((8, CHUNK), F32)
    one8 = ones_ref[0:8, :].astype(F32)
    pieces = _split_f32(col_sum, 3) + [one8] * 3
    for x in spread:
        pieces += _split_f32(x, 2)
    assert len(pieces) <= CHUNK // 8
    lhs = jnp.concatenate(pieces + [z8] * (CHUNK // 8 - len(pieces)), axis=0).T.astype(BF16)
    rows = jnp.concatenate([one8] * 3 + _split_f32(row_sum, 3) + [z8] * (CHUNK // 8 - 6), axis=0)
    return lhs, jnp.concatenate([rows.astype(BF16), sel_ref[...]], axis=1)


def _mlstm_steps(qf_ref, vf_ref, ktf_ref, gtf_ref, qb_ref, vb_ref, ktb_ref, gtb_ref, bcol_ref, ones_ref, sel_ref,
                  rsel_ref, hf_ref, hb_ref, cn_ref, m_ref):
    @pl.when(pl.program_id(1) == 0)
    def _():
        cn_ref[...] = jnp.zeros_like(cn_ref)
        m_ref[...] = jnp.zeros_like(m_ref)

    lane = _iota2((CHUNK, LANES), 1)
    feat = _iota2((LANES, CHUNK), 0)
    row8 = _iota2((8, LANES), 0)
    ones_t = ones_ref[...]
    ninf = jnp.float32(-jnp.inf)
    m_all = m_ref[...]
    heads = [None] * 8
    gates = []
    for d in (0, 1):
        q_ref, v_ref, kt_ref = ((qf_ref, vf_ref, ktf_ref), (qb_ref, vb_ref, ktb_ref))[d]
        for p in range(2):
            q_t = q_ref[0, :, 128 * p:128 * p + 128]
            v_t = v_ref[0, :, 128 * p:128 * p + 128]
            kt_p = kt_ref[0, 128 * p:128 * p + 128, :] * (HEAD_W ** -0.5)
            kt_b = kt_p.astype(BF16)
            for half in range(2):
                r = 4 * d + 2 * p + half
                hmask = (lane >= 64) if half else (lane < 64)
                qm = jnp.where(hmask, q_t, 0.0).astype(BF16)
                cn = cn_ref[r]
                heads[r] = dict(
                    cn=cn, kt_p=kt_p,
                    vw=jnp.concatenate([jnp.where(hmask, v_t, 0.0).astype(BF16), ones_t], axis=1),
                    qk=_mm(qm, kt_b),
                    qc=_mm(qm, cn.astype(BF16)))

    yield
    f_cum = []
    for d in (0, 1):
        gt_ref = (gtf_ref, gtb_ref)[d]
        tr = jnp.where(_tri_masks(d)[1], 1.0, 0.0).astype(BF16)
        f_cum.append(_mm_exact_r(_log_sigmoid(gt_ref[0, 8:16, :] + bcol_ref[8:16, :]), tr))

    yield
    for d in (0, 1):
        gt_ref = (gtf_ref, gtb_ref)[d]
        last = CHUNK - 1 if d == 0 else 0
        li = gt_ref[0, 0:8, :] + bcol_ref[0:8, :]
        f = f_cum[d]
        u = li - f
        m_old = m_all
        mx = jnp.maximum(m_old, _cummax_lanes(u, d))
        w_int = jnp.exp(m_old - mx)
        e_mi = jnp.exp(-(f + mx))
        u_max = jnp.max(u, axis=1, keepdims=True)
        e_end = jnp.exp(u - u_max)
        f_last = f[:, last:last + 1]
        b_end = f_last + u_max
        m_new = jnp.maximum(f_last + m_old, b_end)
        a_dec = jnp.exp(f_last + m_old - m_new)
        g_inc = jnp.exp(b_end - m_new)
        m_all = jnp.where((row8 >> 2) == d, m_new, m_all)
        gates.append((e_end, a_dec, g_inc) + _spread_operands(-mx, u, [w_int, e_mi], sel_ref, ones_ref))

    for r in range(8):
        hd = heads[r]
        e_end, a_dec, g_inc, lhs, rhs = gates[r // 4]
        fmask = (feat >= 64) if r % 2 else (feat < 64)
        ke = (jnp.where(fmask, hd["kt_p"], 0.0) * e_end[r:r + 1, :]).astype(BF16)
        a_r = a_dec[r:r + 1, :]
        g_r = g_inc[r:r + 1, :]
        hd["mt"] = _mm(lhs, rhs * rsel_ref[r])
        hd["cn_new"] = (jnp.concatenate([a_r, a_r], axis=1) * hd["cn"]
                        + jnp.concatenate([g_r, g_r], axis=1) * _mm(ke, hd["vw"]))

    yield
    for r in range(8):
        hd = heads[r]
        vis, _ = _tri_masks(r // 4)
        s = hd["qk"] * jnp.exp(jnp.where(vis, hd["mt"][:, 0:128], ninf))
        hd["sv"] = _mm(s.astype(BF16), hd["vw"])

    yield
    for d in (0, 1):
        out_ref = (hf_ref, hb_ref)[d]
        for p in range(2):
            pair = []
            for half in range(2):
                hd = heads[4 * d + 2 * p + half]
                w_b = hd["mt"][:, 128:256]
                tot = hd["sv"] + jnp.concatenate([w_b, w_b], axis=1) * hd["qc"]
                pair.append(tot[:, 0:128] / jnp.maximum(jnp.abs(tot[:, 128:256]), hd["mt"][:, 256:384]))
            out_ref[0, :, 128 * p:128 * p + 128] = jnp.where(lane < 64, pair[0], pair[1]).astype(BF16)
    m_ref[...] = m_all
    for r in range(8):
        cn_ref[r] = heads[r]["cn_new"]


def _mlstm(ml, mkt, gt, bcol, n_ctx_chunks):
    bsz, t, _ = ml.shape
    nc = t // CHUNK
    out = jax.ShapeDtypeStruct((bsz, t, W_BRANCH), BF16)
    const2 = lambda b, c: (0, 0)

    def chunk_specs(chunk):
        return [pl.BlockSpec((1, CHUNK, W_BRANCH), lambda b, c: (b, chunk(c), 0)),
                pl.BlockSpec((1, CHUNK, W_BRANCH), lambda b, c: (b, chunk(c), 1)),
                pl.BlockSpec((1, W_BRANCH, CHUNK), lambda b, c: (b, 0, chunk(c))),
                pl.BlockSpec((1, GATE_ROWS, CHUNK), lambda b, c: (b, 0, chunk(c)))]

    fwd_chunk = lambda c: c
    bwd_chunk = lambda c: _bwd_chunk(c, n_ctx_chunks, nc)
    return dict(
        steps=_mlstm_steps,
        in_specs=chunk_specs(fwd_chunk) + chunk_specs(bwd_chunk) + [
            pl.BlockSpec((GATE_ROWS, 1), const2), pl.BlockSpec((CHUNK, LANES), const2),
            pl.BlockSpec((CHUNK, 2 * LANES), const2),
            pl.BlockSpec((8, CHUNK, 3 * LANES), lambda b, c: (0, 0, 0))],
        out_specs=[pl.BlockSpec((1, CHUNK, W_BRANCH), lambda b, c: (b, c, 0)),
                   pl.BlockSpec((1, CHUNK, W_BRANCH), lambda b, c: (b, bwd_chunk(c), 0))],
        out_shape=[out, out],
        scratch_shapes=[pltpu.VMEM((8, LANES, 2 * LANES), F32), pltpu.VMEM((8, LANES), F32)],
        args=(ml, ml, mkt, gt, ml, ml, mkt, gt, bcol, jnp.ones((CHUNK, LANES), BF16), _spread_selector(2),
              _state_row_selector(3)))


def _ssd_steps(n_ctx_chunks, n_chunks,
                xf_ref, xfp_ref, xfn_ref, xb_ref, xbp_ref, xbn_ref,
                gtf_ref, gtb_ref, bcol_ref, acol_ref, cw_ref, cb_ref, dskip_ref, ones_ref, sel_ref, rsel_ref,
                yf_ref, yb_ref, s_ref):
    c = pl.program_id(1)

    @pl.when(c == 0)
    def _():
        s_ref[...] = jnp.zeros_like(s_ref)

    lane = _iota2((CHUNK, LANES), 1)
    row768 = _iota2((CHUNK, 768), 0)
    ninf = jnp.float32(-jnp.inf)

    dt, a_cum = [], []
    for d in (0, 1):
        gt_ref = (gtf_ref, gtb_ref)[d]
        tr = jnp.where(_tri_masks(d)[1], 1.0, 0.0).astype(BF16)
        dt.append(_softplus(gt_ref[0, 16:24, :] + bcol_ref[16:24, :]))
        a_cum.append(_mm_exact_r(dt[d] * acol_ref[16:24, :], tr))

    yield
    xa = []
    for d in (0, 1):
        x_ref, xp_ref, xn_ref = ((xf_ref, xfp_ref, xfn_ref), (xb_ref, xbp_ref, xbn_ref))[d]
        j = c if d == 0 else _bwd_chunk(c, n_ctx_chunks, n_chunks)
        seg_first = jnp.logical_or(j == 0, j == n_ctx_chunks)
        seg_last = jnp.logical_or(j == n_ctx_chunks - 1, j == n_chunks - 1)
        x = x_ref[0]
        prev = jnp.where(seg_first, 0.0, xp_ref[0, 7:8, :])
        nxt = jnp.where(seg_last, 0.0, xn_ref[0, 0:1, :])
        x_dn = jnp.where(row768 == 0, prev, pltpu.roll(x, 1, 0))
        x_up = jnp.where(row768 == CHUNK - 1, nxt, pltpu.roll(x, CHUNK - 1, 0))
        xa.append(_silu(x_dn * cw_ref[0:1, :] + x * cw_ref[1:2, :] + x_up * cw_ref[2:3, :] + cb_ref[...]))

    heads = [None] * 8
    groups = {}
    for d in (0, 1):
        for g in range(2):
            x_pair = xa[d][:, 128 * g:128 * g + 128]
            b_g = xa[d][:, 256 + 128 * g:256 + 128 * g + 128]
            c_gb = xa[d][:, 512 + 128 * g:512 + 128 * g + 128].astype(BF16)
            groups[d, g] = dict(x_pair=x_pair, bt=b_g.T.astype(BF16), cb=_mm_nt(c_gb, b_g.astype(BF16)))
            for half in range(2):
                r = 4 * d + 2 * g + half
                hmask = (lane >= 64) if half else (lane < 64)
                st = s_ref[r]
                xm = jnp.where(hmask, x_pair, 0.0)
                heads[r] = dict(st=st, xm=xm, xmb=xm.astype(BF16), cs=_mm(c_gb, st.astype(BF16)))

    yield
    for d in (0, 1):
        last = CHUNK - 1 if d == 0 else 0
        a_last = a_cum[d][:, last:last + 1]
        w_in = jnp.exp(a_last - a_cum[d]) * dt[d]
        e_last = jnp.exp(jnp.broadcast_to(a_last, (8, CHUNK)))
        lhs, rhs = _spread_operands(a_cum[d], -a_cum[d], [jnp.exp(a_cum[d]), w_in], sel_ref, ones_ref)
        for h in range(N_HEADS):
            r = 4 * d + h
            heads[r]["mt"] = _mm(lhs, rhs * rsel_ref[r])
            heads[r]["e_last"] = e_last[r:r + 1, :]

    yield
    for r in range(8):
        d, h = divmod(r, 4)
        hd = heads[r]
        ds = _mm(groups[d, h // 2]["bt"], (hd["xm"] * hd["mt"][:, 256:384]).astype(BF16))
        s_ref[r] = hd["e_last"] * hd["st"] + ds
    for r in range(8):
        d, h = divmod(r, 4)
        hd = heads[r]
        vis, _ = _tri_masks(d)
        sc = groups[d, h // 2]["cb"] * jnp.exp(jnp.where(vis, hd["mt"][:, 0:128], ninf)) * dt[d][r:r + 1, :]
        hd["y"] = _mm(sc.astype(BF16), hd["xmb"]) + hd["mt"][:, 128:256] * hd["cs"]

    for d in (0, 1):
        out_ref = (yf_ref, yb_ref)[d]
        for g in range(2):
            y_pair = heads[4 * d + 2 * g]["y"] + heads[4 * d + 2 * g + 1]["y"]
            if d == 0:
                y_pair = y_pair + dskip_ref[:, 128 * g:128 * g + 128] * groups[d, g]["x_pair"]
            out_ref[0, :, 128 * g:128 * g + 128] = y_pair.astype(BF16)


def _ssd(ss, gt, bcol, acol, conv_w, conv_b, dskip, n_ctx_chunks):
    bsz, t, _ = ss.shape
    nc = t // CHUNK
    sub = CHUNK // 8
    nsub = t // 8
    bc = lambda c: _bwd_chunk(c, n_ctx_chunks, nc)
    fwd = lambda b, c: (b, c, 0)
    bwd = lambda b, c: (b, bc(c), 0)
    fwd_p = lambda b, c: (b, jnp.maximum(c * sub - 1, 0), 0)
    fwd_n = lambda b, c: (b, jnp.minimum((c + 1) * sub, nsub - 1), 0)
    bwd_p = lambda b, c: (b, jnp.maximum(bc(c) * sub - 1, 0), 0)
    bwd_n = lambda b, c: (b, jnp.minimum((bc(c) + 1) * sub, nsub - 1), 0)
    fwd_t = lambda b, c: (b, 0, c)
    bwd_t = lambda b, c: (b, 0, bc(c))
    const2 = lambda b, c: (0, 0)
    out = jax.ShapeDtypeStruct((bsz, t, W_BRANCH), BF16)
    return dict(
        steps=functools.partial(_ssd_steps, n_ctx_chunks, nc),
        in_specs=[pl.BlockSpec((1, CHUNK, 768), fwd), pl.BlockSpec((1, 8, 768), fwd_p),
                  pl.BlockSpec((1, 8, 768), fwd_n),
                  pl.BlockSpec((1, CHUNK, 768), bwd), pl.BlockSpec((1, 8, 768), bwd_p),
                  pl.BlockSpec((1, 8, 768), bwd_n),
                  pl.BlockSpec((1, GATE_ROWS, CHUNK), fwd_t), pl.BlockSpec((1, GATE_ROWS, CHUNK), bwd_t),
                  pl.BlockSpec((GATE_ROWS, 1), const2), pl.BlockSpec((GATE_ROWS, 1), const2),
                  pl.BlockSpec((3, 768), const2), pl.BlockSpec((1, 768), const2),
                  pl.BlockSpec((1, W_BRANCH), const2),
                  pl.BlockSpec((CHUNK, LANES), const2), pl.BlockSpec((CHUNK, 2 * LANES), const2),
                  pl.BlockSpec((8, CHUNK, 3 * LANES), lambda b, c: (0, 0, 0))],
        out_specs=[pl.BlockSpec((1, CHUNK, W_BRANCH), fwd), pl.BlockSpec((1, CHUNK, W_BRANCH), bwd)],
        out_shape=[out, out],
        scratch_shapes=[pltpu.VMEM((8, SSM_N, LANES), F32)],
        args=(ss, ss, ss, ss, ss, ss, gt, gt, bcol, acol, conv_w, conv_b, dskip,
              jnp.ones((CHUNK, LANES), BF16), _spread_selector(2), _state_row_selector(3)))


def _hgrn2_steps(hf_ref, hb_ref, loglb_ref, log1m_ref, onem_ref, e_ref, of_ref, ob_ref,
                  st_ref):
    @pl.when(pl.program_id(1) == 0)
    def _():
        st_ref[...] = jnp.zeros_like(st_ref)

    ri = _iota2((CHUNK, CHUNK), 0)
    ci = _iota2((CHUNK, CHUNK), 1)
    lane = _iota2((CHUNK, LANES), 1)
    rw = _iota2((CHUNK, W_BRANCH), 0)
    blockdiag = (ri >> 6) == (ci >> 6)
    ninf = jnp.float32(-jnp.inf)
    nb = CHUNK // SUB
    rs = _iota2((nb, SUB, W_BRANCH), 1)

    dirs = []
    for d in (0, 1):
        h_ref = (hf_ref, hb_ref)[d]
        tc = jnp.where(_tri_masks(d)[0], 1.0, 0.0).astype(BF16)
        z = h_ref[0, :, W_BRANCH * (1 + d):W_BRANCH * (2 + d)]
        e = jnp.exp(-jnp.abs(z))
        ope = 1.0 + e
        la = loglb_ref[d:d + 1, :]
        lb_ = log1m_ref[d:d + 1, :] + (jnp.minimum(z, 0.0) - jnp.log(ope))
        logf = jnp.maximum(la, lb_) + jnp.log(1.0 + jnp.exp(-jnp.abs(la - lb_)))
        dirs.append(dict(
            q=_silu(h_ref[0, :, 0:W_BRANCH]),
            v=h_ref[0, :, 3 * W_BRANCH:4 * W_BRANCH],
            kk=onem_ref[d:d + 1, :] * (jnp.where(z >= 0.0, e, 1.0) / ope),
            gcum=_mm_exact_l(tc, logf)))

    yield
    for d in (0, 1):
        dd = dirs[d]
        q, kk, gcum = dd["q"], dd["kk"], dd["gcum"]
        last = CHUNK - 1 if d == 0 else 0
        g_last = gcum[last:last + 1, :]
        qg = (q * jnp.exp(gcum)).astype(BF16)
        kg = (kk * jnp.exp(g_last - gcum)).astype(BF16)
        vb = dd["v"].astype(BF16)

        dd["o_inter"] = []
        for p in range(2):
            sl = slice(128 * p, 128 * p + 128)
            st = st_ref[2 * d + p]
            dd["o_inter"].append(_mm_nt(qg[:, sl], st.astype(BF16)))
            dst = _mm_tn(vb[:, sl], kg[:, sl])
            st_ref[2 * d + p] = st * jnp.exp(g_last[:, sl]) + jnp.where(blockdiag, dst, 0.0)

        a_mats = [jnp.zeros((CHUNK, CHUNK), F32) for _ in range(N_HEADS)]
        blk = CHUNK // 2
        while blk >= SUB:
            first = (rw & (2 * blk - 1)) < blk
            edge = (blk - 1) if d == 0 else blk
            gb = gcum.reshape(CHUNK // (2 * blk), 2 * blk, W_BRANCH)[:, edge:edge + 1, :]
            gb = jnp.broadcast_to(gb, (CHUNK // (2 * blk), 2 * blk, W_BRANCH)).reshape(CHUNK, W_BRANCH)
            q_side = jnp.logical_not(first) if d == 0 else first
            qt = q * jnp.exp(jnp.where(q_side, gcum - gb, ninf))
            kt = (kk * jnp.exp(jnp.where(q_side, ninf, gb - gcum))).astype(BF16)
            same = (ri >> int(math.log2(2 * blk))) == (ci >> int(math.log2(2 * blk)))
            for h in range(N_HEADS):
                p, half = divmod(h, 2)
                hmask = (lane >= 64) if half else (lane < 64)
                qh = jnp.where(hmask, qt[:, 128 * p:128 * p + 128], 0.0).astype(BF16)
                a_mats[h] = a_mats[h] + jnp.where(same, _mm_nt(qh, kt[:, 128 * p:128 * p + 128]), 0.0)
            blk //= 2
        dd["a_mats"] = a_mats

        g3 = gcum.reshape(nb, SUB, W_BRANCH)
        q3 = q.reshape(nb, SUB, W_BRANCH)
        k3 = kk.reshape(nb, SUB, W_BRANCH)
        v3 = dd["v"].reshape(nb, SUB, W_BRANCH)
        o3 = None
        for j in range(SUB):
            ok = (rs >= j) if d == 0 else (rs <= j)
            pj = q3 * jnp.exp(jnp.where(ok, g3 - g3[:, j:j + 1, :], ninf)) * k3[:, j:j + 1, :]
            red = _mm(pj.reshape(CHUNK, W_BRANCH).astype(BF16), e_ref[...])
            term = red.reshape(nb, SUB, W_BRANCH) * v3[:, j:j + 1, :]
            o3 = term if o3 is None else o3 + term
        dd["o_diag"] = o3.reshape(CHUNK, W_BRANCH)

    yield
    for d in (0, 1):
        dd = dirs[d]
        out_ref = (of_ref, ob_ref)[d]
        for p in range(2):
            sl = slice(128 * p, 128 * p + 128)
            a_cat = jnp.concatenate([dd["a_mats"][2 * p], dd["a_mats"][2 * p + 1]], axis=1).astype(BF16)
            v_p = dd["v"][:, sl]
            v_cat = jnp.concatenate([jnp.where(lane < 64, v_p, 0.0), jnp.where(lane >= 64, v_p, 0.0)],
                                    axis=0).astype(BF16)
            out_ref[0, :, sl] = (dd["o_inter"][p] + _mm(a_cat, v_cat) + dd["o_diag"][:, sl]).astype(BF16)


def _hgrn2(hg, loglb, log1m, onem, e64, n_ctx_chunks):
    bsz, t, _ = hg.shape
    nc = t // CHUNK
    fwd = lambda b, c: (b, c, 0)
    bwd = lambda b, c: (b, _bwd_chunk(c, n_ctx_chunks, nc), 0)
    const2 = lambda b, c: (0, 0)
    out = jax.ShapeDtypeStruct((bsz, t, W_BRANCH), BF16)
    return dict(
        steps=_hgrn2_steps,
        in_specs=[pl.BlockSpec((1, CHUNK, 1024), fwd), pl.BlockSpec((1, CHUNK, 1024), bwd),
                  pl.BlockSpec((2, W_BRANCH), const2), pl.BlockSpec((2, W_BRANCH), const2),
                  pl.BlockSpec((2, W_BRANCH), const2), pl.BlockSpec((W_BRANCH, W_BRANCH), const2)],
        out_specs=[pl.BlockSpec((1, CHUNK, W_BRANCH), fwd), pl.BlockSpec((1, CHUNK, W_BRANCH), bwd)],
        out_shape=[out, out],
        scratch_shapes=[pltpu.VMEM((4, LANES, LANES), F32)],
        args=(hg, hg, loglb, log1m, onem, e64))


def _scan_kernel(parts, *refs):
    n_in = [len(p["in_specs"]) for p in parts]
    n_out = [len(p["out_specs"]) for p in parts]
    n_scr = [len(p["scratch_shapes"]) for p in parts]
    ins, outs, scr = refs[:sum(n_in)], refs[sum(n_in):sum(n_in) + sum(n_out)], refs[sum(n_in) + sum(n_out):]
    gens = []
    for k, p in enumerate(parts):
        mine = (ins[sum(n_in[:k]):sum(n_in[:k + 1])] + outs[sum(n_out[:k]):sum(n_out[:k + 1])]
                + scr[sum(n_scr[:k]):sum(n_scr[:k + 1])])
        gens.append(p["steps"](*mine))
    while gens:
        for g in list(gens):
            if next(g, gens) is gens:
                gens.remove(g)


def _chunk_scans(parts, bsz, n_chunks):
    return pl.pallas_call(
        functools.partial(_scan_kernel, parts),
        grid=(bsz, n_chunks),
        in_specs=[s for p in parts for s in p["in_specs"]],
        out_specs=[s for p in parts for s in p["out_specs"]],
        out_shape=[s for p in parts for s in p["out_shape"]],
        scratch_shapes=[s for p in parts for s in p["scratch_shapes"]],
        compiler_params=_cparams(("parallel", "arbitrary")),
        name="chunk_scans",
    )(*[a for p in parts for a in p["args"]])


def _attn(daq, kt, vx, lam_vecs, lam_init, n_ctx_tiles, skip_ctx):
    bsz, t, _ = daq.shape
    tq = ROW_TILE
    n_k_tiles = t // KEY_TILE
    n_lat_tiles = n_k_tiles - n_ctx_tiles
    group = math.gcd(KEY_GROUP, n_lat_tiles)
    q_off = n_ctx_tiles if skip_ctx else 0
    nq = t // tq - q_off
    n_maps = W_BRANCH // DA_QK
    rows = n_maps * tq

    def kern(q_ref, kt_ref, vx_ref, lv_ref, o_ref, q8_ref, m_ref, acc_ref, s_ref, mx_ref):
        qi = pl.program_id(1) + q_off
        lane = _iota2((tq, W_BRANCH), 1)
        q = q_ref[0]
        for j in range(n_maps):
            q8_ref[j * tq:(j + 1) * tq, :] = jnp.where((lane >> 5) == j, q, jnp.zeros_like(q))

        def score_tile(t):
            return _mm(q8_ref[...], kt_ref[0, t])
        m_ref[...] = jnp.full_like(m_ref, -jnp.inf)
        acc_ref[...] = jnp.zeros_like(acc_ref)

        def update(t0, n, after_scores=None):
            s = [score_tile(t0 + i) for i in range(n)]
            if after_scores is not None:
                after_scores()
            mx = s[0][:, 0:LANES]
            for i in range(n):
                for c in range(KEY_TILE // LANES):
                    if i or c:
                        mx = jnp.maximum(mx, s[i][:, LANES * c:LANES * (c + 1)])
            m_old = m_ref[...]
            m_new = jnp.maximum(m_old, jnp.max(mx, axis=1, keepdims=True))
            alpha = jnp.exp(m_old - m_new)
            m_ref[...] = m_new
            m2 = jnp.concatenate([m_new] * (KEY_TILE // LANES), axis=1)
            p = [jnp.exp(s[i] - m2).astype(BF16) for i in range(n)]
            for hd in range(N_HEADS):
                r0 = 2 * hd * tq
                pv = _mm(p[0][r0:r0 + 2 * tq], vx_ref[0, t0, hd])
                for i in range(1, n):
                    pv = pv + _mm(p[i][r0:r0 + 2 * tq], vx_ref[0, t0 + i, hd])
                acc_ref[r0:r0 + 2 * tq, :] = alpha[r0:r0 + 2 * tq] * acc_ref[r0:r0 + 2 * tq, :] + pv

        def scores(g, slot):
            t0 = n_ctx_tiles + g * group
            mx = None
            for i in range(group):
                s = score_tile(t0 + i)
                s_ref[slot, :, KEY_TILE * i:KEY_TILE * (i + 1)] = s
                for c in range(KEY_TILE // LANES):
                    part = s[:, LANES * c:LANES * (c + 1)]
                    mx = part if mx is None else jnp.maximum(mx, part)
            mx_ref[slot] = mx

        def absorb(g, slot):
            t0 = n_ctx_tiles + g * group
            m_old = m_ref[...]
            m_new = jnp.maximum(m_old, jnp.max(mx_ref[slot], axis=1, keepdims=True))
            alpha = jnp.exp(m_old - m_new)
            m_ref[...] = m_new
            m2 = jnp.concatenate([m_new] * (KEY_TILE // LANES), axis=1)
            p = [jnp.exp(s_ref[slot, :, KEY_TILE * i:KEY_TILE * (i + 1)] - m2).astype(BF16) for i in range(group)]
            for hd in range(N_HEADS):
                r0 = 2 * hd * tq
                pv = _mm(p[0][r0:r0 + 2 * tq], vx_ref[0, t0, hd])
                for i in range(1, group):
                    pv = pv + _mm(p[i][r0:r0 + 2 * tq], vx_ref[0, t0 + i, hd])
                acc_ref[r0:r0 + 2 * tq, :] = alpha[r0:r0 + 2 * tq] * acc_ref[r0:r0 + 2 * tq, :] + pv

        @pl.when(qi < n_ctx_tiles)
        def _():
            update(0, n_ctx_tiles)

        @pl.when(qi >= n_ctx_tiles)
        def _():
            n_groups = n_lat_tiles // group
            n_pairs = (n_groups - 1) // 2
            update(0, n_ctx_tiles, after_scores=lambda: scores(0, 0))

            def body(k, carry):
                g = 2 * k
                scores(g + 1, 1)
                absorb(g, 0)
                scores(g + 2, 0)
                absorb(g + 1, 1)
                return carry
            for k in range(n_pairs):
                body(k, 0)
            g0 = 2 * n_pairs
            if n_groups - g0 == 2:
                scores(g0 + 1, 1)
            absorb(g0, 0)
            if n_groups - g0 == 2:
                absorb(g0 + 1, 1)

        lv = lv_ref[...]
        lam = (jnp.exp(jnp.sum(lv[0:1, :] * lv[1:2, :], axis=1, keepdims=True))
               - jnp.exp(jnp.sum(lv[2:3, :] * lv[3:4, :], axis=1, keepdims=True)) + lam_init)
        lane1 = _iota2((tq, LANES), 1)
        for pr in range(N_HEADS // 2):
            halves = []
            for half in range(2):
                r0 = 2 * (2 * pr + half) * tq
                one = 0 if half else HEAD_W
                a0 = acc_ref[r0:r0 + tq, :]
                a1 = acc_ref[r0 + tq:r0 + 2 * tq, :]
                halves.append(a0 / a0[:, one:one + 1] - lam * (a1 / a1[:, one:one + 1]))
            o_ref[0, :, LANES * pr:LANES * (pr + 1)] = jnp.where(lane1 < HEAD_W, halves[0], halves[1]).astype(BF16)

    return pl.pallas_call(
        kern,
        grid=(bsz, nq),
        in_specs=[pl.BlockSpec((1, tq, W_BRANCH), lambda b, i: (b, i + q_off, 0)),
                  pl.BlockSpec((1, n_k_tiles, W_BRANCH, KEY_TILE), lambda b, i: (b, 0, 0, 0),
                               pipeline_mode=pl.Buffered(1)),
                  pl.BlockSpec((1, n_k_tiles, N_HEADS, KEY_TILE, LANES), lambda b, i: (b, 0, 0, 0, 0),
                               pipeline_mode=pl.Buffered(1)),
                  pl.BlockSpec(lam_vecs.shape, lambda b, i: (0, 0))],
        out_specs=pl.BlockSpec((1, tq, W_BRANCH), lambda b, i: (b, i, 0)),
        out_shape=jax.ShapeDtypeStruct((bsz, nq * tq, W_BRANCH), BF16),
        scratch_shapes=[pltpu.VMEM((rows, W_BRANCH), BF16),
                        pltpu.VMEM((rows, LANES), F32),
                        pltpu.VMEM((rows, LANES), F32),
                        pltpu.VMEM((2, rows, group * KEY_TILE), F32),
                        pltpu.VMEM((2, rows, LANES), F32)],
        compiler_params=_cparams(("parallel", "parallel")),
        name="diff_attn",
    )(daq, kt, vx, lam_vecs)


def _seg_mean(x, e_ref):
    x0 = x.astype(BF16)
    x1 = (x - x0.astype(F32)).astype(BF16)
    return (_mm(x0, e_ref[...]) + _mm(x1, e_ref[...])) * (1.0 / HEAD_W)


def _out_kernel(n_ctx_tiles, ctx_row, q_off, lam_init, final,
                c_ref, x_ref, mo_ref, mlo_ref, mlz_ref, mhf_ref, mhb_ref, dao_ref, daz_ref,
                syf_ref, syb_ref, ssz_ref, hof_ref, hob_ref, hgz_ref,
                mlg_ref, dag_ref, ssg_ref, hgg_ref, e_ref, w_ref, fg_ref, o_ref):
    b = pl.program_id(0)
    i = pl.program_id(1) + q_off
    d = x_ref.shape[2]
    r = jnp.where(i < n_ctx_tiles, ctx_row, b)
    gate = mo_ref[pl.ds(r, 1), :][:, 2 * d:3 * d]

    u = _sigmoid(mlo_ref[0].astype(F32)) * (mhf_ref[0].astype(F32) + mhb_ref[0].astype(F32))
    dev = u - _seg_mean(u, e_ref)
    y_ml = dev * lax.rsqrt(_seg_mean(dev * dev, e_ref) + EPS) * mlg_ref[...] * _silu(mlz_ref[0].astype(F32))

    o = dao_ref[0].astype(F32)
    y_da = (o * lax.rsqrt(_seg_mean(o * o, e_ref) + EPS) * dag_ref[...]) * (1.0 - lam_init) * _silu(daz_ref[0].astype(F32))

    ys = (syf_ref[0].astype(F32) + syb_ref[0].astype(F32)) * _silu(ssz_ref[0].astype(F32))
    parts = []
    for g in range(2):
        yg = ys[:, 128 * g:128 * g + 128]
        parts.append(yg * lax.rsqrt(jnp.mean(yg * yg, axis=-1, keepdims=True) + EPS))
    y_ss = jnp.concatenate(parts, axis=1) * ssg_ref[...]

    oh = hof_ref[0].astype(F32) + hob_ref[0].astype(F32)
    y_hg = (oh * lax.rsqrt(_seg_mean(oh * oh, e_ref) + EPS) * hgg_ref[...]) * _silu(hgz_ref[0].astype(F32))

    acc = _mm(y_ml.astype(BF16), w_ref[0:W_BRANCH, :])
    acc = acc + _mm(y_da.astype(BF16), w_ref[W_BRANCH:2 * W_BRANCH, :])
    acc = acc + _mm(y_ss.astype(BF16), w_ref[2 * W_BRANCH:3 * W_BRANCH, :])
    acc = acc + _mm(y_hg.astype(BF16), w_ref[3 * W_BRANCH:4 * W_BRANCH, :])
    x_new = jnp.where(i < n_ctx_tiles, c_ref[0], x_ref[0]) + gate * acc
    if final:
        x_new = x_new * lax.rsqrt(jnp.mean(x_new * x_new, axis=-1, keepdims=True) + EPS) * fg_ref[...]
    o_ref[0] = x_new


def _out_proj(stream, mo, zg, mhf, mhb, dao, syf, syb, hof, hob,
              mlg, dag, ssg, hgg, e64, w_out, final_g, lam_init, n_ctx_tiles, ctx_row, final):
    c_src, x_src, x_off, t = stream
    bsz, _, d = x_src.shape
    tm = ROW_TILE
    q_off = n_ctx_tiles if final else 0
    nrow = t // tm - q_off
    row = lambda b, i: (b, i + q_off, 0)
    col = lambda k: (lambda b, i: (b, i + q_off, k))
    const2 = lambda b, i: (0, 0)
    wb = pl.BlockSpec((1, tm, W_BRANCH), row)
    return pl.pallas_call(
        functools.partial(_out_kernel, n_ctx_tiles, ctx_row, q_off, lam_init, final),
        grid=(bsz, nrow),
        in_specs=_stream_specs(stream, n_ctx_tiles, q_off) + [
                  pl.BlockSpec(mo.shape, const2),
                  pl.BlockSpec((1, tm, W_BRANCH), col(0)), pl.BlockSpec((1, tm, W_BRANCH), col(1)),
                  wb, wb,
                  pl.BlockSpec((1, tm, W_BRANCH), lambda b, i: (b, i, 0)) if final else wb,
                  pl.BlockSpec((1, tm, W_BRANCH), col(2)),
                  wb, wb, pl.BlockSpec((1, tm, W_BRANCH), col(3)),
                  wb, wb, pl.BlockSpec((1, tm, W_BRANCH), col(4)),
                  pl.BlockSpec((1, W_BRANCH), const2), pl.BlockSpec((1, W_BRANCH), const2),
                  pl.BlockSpec((1, W_BRANCH), const2), pl.BlockSpec((1, W_BRANCH), const2),
                  pl.BlockSpec((W_BRANCH, W_BRANCH), const2),
                  pl.BlockSpec(w_out.shape, const2),
                  pl.BlockSpec((1, d), const2)],
        out_specs=pl.BlockSpec((1, tm, d), lambda b, i: (b, i, 0)),
        out_shape=jax.ShapeDtypeStruct((bsz, nrow * tm, d), F32),
        compiler_params=_cparams(("parallel", "parallel")),
        name="out_proj",
    )(c_src, x_src, mo, zg, zg, mhf, mhb, dao, zg, syf, syb, zg, hof, hob, zg,
      mlg, dag, ssg, hgg, e64, w_out, final_g.reshape(1, d))


def _relayout_w_in(w_in):
    d = w_in.shape[0]
    o = {}
    off = 0
    for name, n in (("ml_q", 256), ("ml_k", 256), ("ml_v", 256), ("ml_o", 256), ("ml_i", 8), ("ml_f", 8),
                    ("ml_z", 256), ("da_q", 256), ("da_k", 256), ("da_v", 256), ("da_z", 256),
                    ("ss_xbc", 768), ("ss_dt", 8), ("ss_z", 256),
                    ("hg_q", 256), ("hg_f", 512), ("hg_i", 256), ("hg_z", 256)):
        o[name] = w_in[:, off:off + n]
        off += n

    gates = jnp.concatenate([o["ml_i"], o["ml_f"], o["ss_dt"], jnp.zeros((d, GATE_ROWS - 24), w_in.dtype)], axis=1)
    w2 = jnp.concatenate([o["ml_q"], o["ml_v"], o["da_q"], o["da_v"],
                          o["ss_xbc"], o["hg_q"], o["hg_f"], o["hg_i"],
                          o["ml_o"], o["ml_z"], o["da_z"], o["ss_z"], o["hg_z"]], axis=1)
    wt = jnp.concatenate([o["da_k"], gates, o["ml_k"]], axis=1).T
    return w2.astype(BF16), wt.astype(BF16)


def _rope_tables(n_ctx, seq):
    pos = jnp.arange(seq)
    rows = (pos // GRID_W).astype(F32)
    cols = (pos % GRID_W).astype(F32)
    axis = DA_QK // 2
    inv = ROPE_BASE ** (-jnp.arange(0, axis, 2, dtype=F32) / axis)
    ang = jnp.concatenate([rows[:, None] * inv, cols[:, None] * inv], axis=-1)
    cos = jnp.concatenate([jnp.ones((n_ctx, axis), F32), jnp.cos(ang)], axis=0)
    sin = jnp.concatenate([jnp.zeros((n_ctx, axis), F32), jnp.sin(ang)], axis=0)
    cos_t = jnp.concatenate([jnp.ones((axis, n_ctx), F32), jnp.cos(ang.T)], axis=1)
    sin_t = jnp.concatenate([jnp.zeros((axis, n_ctx), F32), jnp.sin(ang.T)], axis=1)
    return (jnp.tile(cos, (1, LANES // axis)), jnp.tile(sin, (1, LANES // axis)),
            jnp.tile(cos_t, (2, 1)), jnp.tile(sin_t, (2, 1)))


def _gate_col(vals):
    v = jnp.concatenate([vals.astype(F32), jnp.zeros((GATE_ROWS - vals.shape[0],), F32)])
    return v.reshape(GATE_ROWS, 1)


def kernel(x, c, ctx, c_ctx, w_mod, b_mod, norm_g, w_in, w_out, ml_gate_b, ml_norm_g, da_lambda, da_norm_g,
           ss_conv_w, ss_conv_b, ss_dt_bias, ss_a_log, ss_d, ss_norm_g, hg_lower, hg_norm_g, final_g):
    bsz, seq, d = x.shape
    n_ctx = ctx.shape[1]
    depth = w_mod.shape[0]
    assert n_ctx % ROW_TILE == 0 and seq % ROW_TILE == 0 and bsz < 8
    n_ctx_tiles = n_ctx // ROW_TILE
    n_ctx_chunks = n_ctx // CHUNK
    ctx_row = bsz

    stream = (ctx, x, 0, n_ctx + seq)
    cc = jnp.concatenate([c, c_ctx[None, :], jnp.zeros((8 - bsz - 1, d), F32)], axis=0)
    cos, sin, cos_t, sin_t = _rope_tables(n_ctx, seq)
    lb_all = jnp.cumsum(jax.nn.softmax(hg_lower.astype(F32), axis=1), axis=1)
    lb_all = lb_all - lb_all[:, :1]
    hid = _iota2((W_BRANCH, W_BRANCH), 0) // HEAD_W
    e64 = (hid == hid.T).astype(BF16)

    out = None
    for l in range(depth):
        lam_init = 0.8 - 0.6 * math.exp(-0.3 * l)
        final = l == depth - 1
        w2, wt = _relayout_w_in(w_in[l])
        mo = _modulation(cc, w_mod[l], b_mod[l])
        ml, daq, dak, dav, ss, hg, zg, gt, mkt = _in_proj(stream, mo, norm_g[l], cos, sin, cos_t, sin_t, w2, wt,
                                                     n_ctx_tiles, ctx_row)

        gb = ml_gate_b[l]
        zeros8 = jnp.zeros((8,), F32)
        bcol = _gate_col(jnp.concatenate([gb[:, 0].reshape(-1), gb[:, 1].reshape(-1),
                                                ss_dt_bias[l].reshape(-1)]))
        acol = _gate_col(jnp.concatenate([zeros8, zeros8, -jnp.exp(ss_a_log[l].astype(F32)).reshape(-1)]))
        dao = _attn(daq, dak, dav, da_lambda[l].astype(F32), lam_init, n_ctx_tiles, final)
        dskip = jnp.repeat(ss_d[l].astype(F32), HEAD_W).reshape(1, W_BRANCH)
        lbh = lb_all[:, l]
        scans = [_mlstm(ml, mkt, gt, bcol, n_ctx_chunks),
                 _ssd(ss, gt, bcol, acol, ss_conv_w[l], ss_conv_b[l].reshape(1, -1), dskip, n_ctx_chunks),
                 _hgrn2(hg, jnp.log(lbh), jnp.log1p(-lbh), 1.0 - lbh, e64, n_ctx_chunks)]
        mhf, mhb, syf, syb, hof, hob = _chunk_scans(scans, bsz, (n_ctx + seq) // CHUNK)

        res = _out_proj(stream, mo, zg, mhf, mhb, dao, syf, syb, hof, hob,
                        ml_norm_g[l].reshape(1, -1), jnp.tile(da_norm_g[l], N_HEADS).reshape(1, -1),
                        ss_norm_g[l].reshape(1, -1), hg_norm_g[l].reshape(1, -1), e64,
                        w_out[l].astype(BF16), final_g, lam_init, n_ctx_tiles, ctx_row, final)
        if final:
            out = res
        else:
            stream = (res, res, n_ctx_tiles, n_ctx + seq)
    return out
```

```python
import functools
import math

import jax
import jax.numpy as jnp
from jax import lax
from jax.experimental import pallas as pl
from jax.experimental.pallas import tpu as pltpu

F32 = jnp.float32
BF16 = jnp.bfloat16

EPS = 1e-6
GRID_W = 64
ROPE_BASE = 10000.0
N_HEADS = 4
HEAD_W = 64
W_BRANCH = N_HEADS * HEAD_W
DA_QK = 32
SSM_N = 128
CHUNK = 128
ROW_TILE = 256
SUB = 8
LANES = 128
VMEM_LIMIT = 56 * 1024 * 1024

C_ML = 0
C_DAQ = 512
C_DAV = 768
C_SS = 1024
C_HG = 1792
C_ZG = 2816
C_END = 4096
GATE_ROWS = 32
KEY_TILE = 256
KEY_GROUP = 4

NT = (((1,), (1,)), ((), ()))
TN = (((0,), (0,)), ((), ()))


def _mm(a, b):
    return jnp.dot(a, b, preferred_element_type=F32)


def _mm_nt(a, b):
    return lax.dot_general(a, b, NT, preferred_element_type=F32)


def _mm_tn(a, b):
    return lax.dot_general(a, b, TN, preferred_element_type=F32)


def _split3(x):
    x0 = x.astype(BF16)
    r = x - x0.astype(F32)
    x1 = r.astype(BF16)
    x2 = (r - x1.astype(F32)).astype(BF16)
    return x0, x1, x2


def _mm_exact_l(t, x):
    x0, x1, x2 = _split3(x)
    return _mm(t, x0) + _mm(t, x1) + _mm(t, x2)


def _mm_exact_r(x, t):
    x0, x1, x2 = _split3(x)
    return _mm(x0, t) + _mm(x1, t) + _mm(x2, t)


def _sigmoid(x):
    return 1.0 / (1.0 + jnp.exp(-x))


def _silu(x):
    return x * _sigmoid(x)


def _log_sigmoid(x):
    return jnp.minimum(x, 0.0) - jnp.log(1.0 + jnp.exp(-jnp.abs(x)))


def _softplus(x):
    return jnp.maximum(x, 0.0) + jnp.log(1.0 + jnp.exp(-jnp.abs(x)))


def _iota2(shape, axis):
    return lax.broadcasted_iota(jnp.int32, shape, axis)


def _cparams(sem):
    return pltpu.CompilerParams(dimension_semantics=sem, vmem_limit_bytes=VMEM_LIMIT)


def _mod_kernel(c_ref, w_ref, b_ref, o_ref):
    c = c_ref[...]
    o_ref[...] = jnp.dot(_silu(c), w_ref[...], precision=lax.Precision.HIGHEST,
                         preferred_element_type=F32) + b_ref[...]


def _modulation(cc, w_mod, b_mod):
    d = cc.shape[1]
    n = w_mod.shape[1]
    tn = 1536
    assert n % tn == 0
    return pl.pallas_call(
        _mod_kernel,
        grid=(n // tn,),
        in_specs=[pl.BlockSpec((8, d), lambda j: (0, 0)),
                  pl.BlockSpec((d, tn), lambda j: (0, j)),
                  pl.BlockSpec((1, tn), lambda j: (0, j))],
        out_specs=pl.BlockSpec((8, tn), lambda j: (0, j)),
        out_shape=jax.ShapeDtypeStruct((8, n), F32),
        compiler_params=_cparams(("arbitrary",)),
        name="modulation",
    )(cc, w_mod, b_mod.reshape(1, n))


def _stream_specs(stream, n_ctx_tiles, first_tile):
    _, x_src, x_off, _ = stream
    d = x_src.shape[2]
    return [pl.BlockSpec((1, ROW_TILE, d), lambda b, i: (b, jnp.minimum(i + first_tile, n_ctx_tiles - 1), 0)),
            pl.BlockSpec((1, ROW_TILE, d), lambda b, i: (b, jnp.maximum(i + first_tile - n_ctx_tiles, 0) + x_off, 0))]


def _in_kernel(n_ctx_tiles, ctx_row, c_ref, x_ref, mo_ref, g_ref, cos_ref, sin_ref, cost_ref, sint_ref,
               w_ref, wt_ref,
               ml_ref, daq_ref, kt_ref, vx_ref, ss_ref, hg_ref, zg_ref, gt_ref, mkt_ref):
    b = pl.program_id(0)
    i = pl.program_id(1)
    d = x_ref.shape[2]
    r = jnp.where(i < n_ctx_tiles, ctx_row, b)
    mo = mo_ref[pl.ds(r, 1), :]
    shift = mo[:, 0:d]
    scale = mo[:, d:2 * d]
    x = jnp.where(i < n_ctx_tiles, c_ref[0], x_ref[0])
    y = x * lax.rsqrt(jnp.mean(x * x, axis=-1, keepdims=True) + EPS) * g_ref[...]
    h = (y * (1.0 + scale) + shift).astype(BF16)

    ml_ref[0] = _mm(h, w_ref[:, C_ML:C_DAQ])
    q = _mm(h, w_ref[:, C_DAQ:C_DAV])
    half = DA_QK // 2
    lane = _iota2((x.shape[0], LANES), 1)
    q_rot = []
    for c in range(W_BRANCH // LANES):
        qc = q[:, LANES * c:LANES * (c + 1)]
        q_rot.append(jnp.where((lane & half) == 0, -pltpu.roll(qc, LANES - half, 1), pltpu.roll(qc, half, 1)))
    n_rep = W_BRANCH // LANES
    q = (q * jnp.concatenate([cos_ref[...]] * n_rep, axis=1)
         + jnp.concatenate(q_rot, axis=1) * jnp.concatenate([sin_ref[...]] * n_rep, axis=1))
    daq_ref[0] = (q * (DA_QK ** -0.5)).astype(BF16)
    v = _mm(h, w_ref[:, C_DAV:C_SS])
    for hd in range(N_HEADS):
        pair = v[:, LANES * (hd // 2):LANES * (hd // 2 + 1)]
        own = (lane >= HEAD_W) if hd % 2 else (lane < HEAD_W)
        vx_ref[0, 0, hd] = jnp.where(own, pair, jnp.where(lane == (0 if hd % 2 else HEAD_W), 1.0, 0.0)).astype(BF16)
    ss_ref[0] = _mm(h, w_ref[:, C_SS:C_HG])
    hg_ref[0] = _mm(h, w_ref[:, C_HG:C_ZG])
    zg_ref[0] = _mm(h, w_ref[:, C_ZG:C_END]).astype(BF16)
    tr = _mm_nt(wt_ref[...], h)
    kt = tr[0:W_BRANCH]
    kt_rot = []
    for j in range(W_BRANCH // DA_QK):
        kt_rot += [-kt[DA_QK * j + half:DA_QK * (j + 1)], kt[DA_QK * j:DA_QK * j + half]]
    n_grp = W_BRANCH // DA_QK
    kt_ref[0, 0] = (kt * jnp.concatenate([cost_ref[...]] * n_grp, axis=0)
                    + jnp.concatenate(kt_rot, axis=0) * jnp.concatenate([sint_ref[...]] * n_grp, axis=0)).astype(BF16)
    gt_ref[0] = tr[W_BRANCH:W_BRANCH + GATE_ROWS]
    mkt_ref[0] = tr[W_BRANCH + GATE_ROWS:2 * W_BRANCH + GATE_ROWS]


def _in_proj(stream, mo, g, cos, sin, cos_t, sin_t, w2, wt, n_ctx_tiles, ctx_row):
    c_src, x_src, x_off, t = stream
    bsz, _, d = x_src.shape
    tm = ROW_TILE
    assert tm == KEY_TILE
    row = lambda b, i: (b, i, 0)
    const2 = lambda b, i: (0, 0)
    out_shape = [
        jax.ShapeDtypeStruct((bsz, t, 2 * W_BRANCH), F32),
        jax.ShapeDtypeStruct((bsz, t, W_BRANCH), BF16),
        jax.ShapeDtypeStruct((bsz, t // tm, W_BRANCH, tm), BF16),
        jax.ShapeDtypeStruct((bsz, t // tm, N_HEADS, tm, LANES), BF16),
        jax.ShapeDtypeStruct((bsz, t, 768), F32),
        jax.ShapeDtypeStruct((bsz, t, 4 * W_BRANCH), F32),
        jax.ShapeDtypeStruct((bsz, t, 5 * W_BRANCH), BF16),
        jax.ShapeDtypeStruct((bsz, GATE_ROWS, t), F32),
        jax.ShapeDtypeStruct((bsz, W_BRANCH, t), F32),
    ]
    out_specs = [
        pl.BlockSpec((1, tm, 2 * W_BRANCH), row),
        pl.BlockSpec((1, tm, W_BRANCH), row),
        pl.BlockSpec((1, 1, W_BRANCH, tm), lambda b, i: (b, i, 0, 0)),
        pl.BlockSpec((1, 1, N_HEADS, tm, LANES), lambda b, i: (b, i, 0, 0, 0)),
        pl.BlockSpec((1, tm, 768), row),
        pl.BlockSpec((1, tm, 4 * W_BRANCH), row),
        pl.BlockSpec((1, tm, 5 * W_BRANCH), row),
        pl.BlockSpec((1, GATE_ROWS, tm), lambda b, i: (b, 0, i)),
        pl.BlockSpec((1, W_BRANCH, tm), lambda b, i: (b, 0, i)),
    ]
    return pl.pallas_call(
        functools.partial(_in_kernel, n_ctx_tiles, ctx_row),
        grid=(bsz, t // tm),
        in_specs=_stream_specs(stream, n_ctx_tiles, 0) + [
                  pl.BlockSpec(mo.shape, const2),
                  pl.BlockSpec((1, d), const2),
                  pl.BlockSpec((tm, LANES), lambda b, i: (i, 0)),
                  pl.BlockSpec((tm, LANES), lambda b, i: (i, 0)),
                  pl.BlockSpec((DA_QK, tm), lambda b, i: (0, i)),
                  pl.BlockSpec((DA_QK, tm), lambda b, i: (0, i)),
                  pl.BlockSpec(w2.shape, const2),
                  pl.BlockSpec(wt.shape, const2)],
        out_specs=out_specs,
        out_shape=out_shape,
        compiler_params=_cparams(("parallel", "parallel")),
        name="in_proj",
    )(c_src, x_src, mo, g.reshape(1, d), cos, sin, cos_t, sin_t, w2, wt)


def _bwd_chunk(c, n_ctx_chunks, n_chunks):
    return jnp.where(c < n_ctx_chunks, n_ctx_chunks - 1 - c, n_chunks + n_ctx_chunks - 1 - c)


def _tri_masks(d):
    ri = _iota2((CHUNK, CHUNK), 0)
    ci = _iota2((CHUNK, CHUNK), 1)
    if d == 0:
        return ci <= ri, ri <= ci
    return ci >= ri, ri >= ci


def _split_f32(x, n):
    parts = []
    for _ in range(n):
        p = x.astype(BF16).astype(F32)
        parts.append(p)
        x = x - p
    return parts


def _cummax_lanes(u, d):
    lane = _iota2(u.shape, 1)
    ninf = jnp.float32(-jnp.inf)
    s = 1
    while s < CHUNK:
        if d == 0:
            sh = jnp.where(lane >= s, pltpu.roll(u, s, 1), ninf)
        else:
            sh = jnp.where(lane < CHUNK - s, pltpu.roll(u, CHUNK - s, 1), ninf)
        u = jnp.maximum(u, sh)
        s *= 2
    return u


def _state_row_selector(n_tiles):
    keep = (jnp.arange(CHUNK)[None, :] % 8) == jnp.arange(8)[:, None]
    return jnp.broadcast_to(keep[:, :, None], (8, CHUNK, n_tiles * LANES)).astype(BF16)


def _spread_selector(n_spread):
    t = jnp.arange(CHUNK)[:, None] // 16
    tiles = [jnp.broadcast_to(t == 3 + k, (CHUNK, LANES)) for k in range(n_spread)]
    return jnp.concatenate(tiles, axis=1).astype(BF16)


def _spread_operands(col_sum, row_sum, spread, sel_ref, ones_ref):
    z8 = jnp.zeros((8, CHUNK), F32)
    one8 = ones_ref[0:8, :].astype(F32)
    pieces = _split_f32(col_sum, 3) + [one8] * 3
    for x in spread:
        pieces += _split_f32(x, 2)
    assert len(pieces) <= CHUNK // 8
    lhs = jnp.concatenate(pieces + [z8] * (CHUNK // 8 - len(pieces)), axis=0).T.astype(BF16)
    rows = jnp.concatenate([one8] * 3 + _split_f32(row_sum, 3) + [z8] * (CHUNK // 8 - 6), axis=0)
    return lhs, jnp.concatenate([rows.astype(BF16), sel_ref[...]], axis=1)


def _mlstm_steps(qf_ref, vf_ref, ktf_ref, gtf_ref, qb_ref, vb_ref, ktb_ref, gtb_ref, bcol_ref, ones_ref, sel_ref,
                  rsel_ref, hf_ref, hb_ref, cn_ref, m_ref):
    @pl.when(pl.program_id(1) == 0)
    def _():
        cn_ref[...] = jnp.zeros_like(cn_ref)
        m_ref[...] = jnp.zeros_like(m_ref)

    lane = _iota2((CHUNK, LANES), 1)
    feat = _iota2((LANES, CHUNK), 0)
    row8 = _iota2((8, LANES), 0)
    ones_t = ones_ref[...]
    ninf = jnp.float32(-jnp.inf)
    m_all = m_ref[...]
    heads = [None] * 8
    gates = []
    for d in (0, 1):
        q_ref, v_ref, kt_ref = ((qf_ref, vf_ref, ktf_ref), (qb_ref, vb_ref, ktb_ref))[d]
        for p in range(2):
            q_t = q_ref[0, :, 128 * p:128 * p + 128]
            v_t = v_ref[0, :, 128 * p:128 * p + 128]
            kt_p = kt_ref[0, 128 * p:128 * p + 128, :] * (HEAD_W ** -0.5)
            kt_b = kt_p.astype(BF16)
            for half in range(2):
                r = 4 * d + 2 * p + half
                hmask = (lane >= 64) if half else (lane < 64)
                qm = jnp.where(hmask, q_t, 0.0).astype(BF16)
                cn = cn_ref[r]
                heads[r] = dict(
                    cn=cn, kt_p=kt_p,
                    vw=jnp.concatenate([jnp.where(hmask, v_t, 0.0).astype(BF16), ones_t], axis=1),
                    qk=_mm(qm, kt_b),
                    qc=_mm(qm, cn.astype(BF16)))

    yield
    f_cum = []
    for d in (0, 1):
        gt_ref = (gtf_ref, gtb_ref)[d]
        tr = jnp.where(_tri_masks(d)[1], 1.0, 0.0).astype(BF16)
        f_cum.append(_mm_exact_r(_log_sigmoid(gt_ref[0, 8:16, :] + bcol_ref[8:16, :]), tr))

    yield
    for d in (0, 1):
        gt_ref = (gtf_ref, gtb_ref)[d]
        last = CHUNK - 1 if d == 0 else 0
        li = gt_ref[0, 0:8, :] + bcol_ref[0:8, :]
        f = f_cum[d]
        u = li - f
        m_old = m_all
        mx = jnp.maximum(m_old, _cummax_lanes(u, d))
        w_int = jnp.exp(m_old - mx)
        e_mi = jnp.exp(-(f + mx))
        u_max = jnp.max(u, axis=1, keepdims=True)
        e_end = jnp.exp(u - u_max)
        f_last = f[:, last:last + 1]
        b_end = f_last + u_max
        m_new = jnp.maximum(f_last + m_old, b_end)
        a_dec = jnp.exp(f_last + m_old - m_new)
        g_inc = jnp.exp(b_end - m_new)
        m_all = jnp.where((row8 >> 2) == d, m_new, m_all)
        gates.append((e_end, a_dec, g_inc) + _spread_operands(-mx, u, [w_int, e_mi], sel_ref, ones_ref))

    for r in range(8):
        hd = heads[r]
        e_end, a_dec, g_inc, lhs, rhs = gates[r // 4]
        fmask = (feat >= 64) if r % 2 else (feat < 64)
        ke = (jnp.where(fmask, hd["kt_p"], 0.0) * e_end[r:r + 1, :]).astype(BF16)
        a_r = a_dec[r:r + 1, :]
        g_r = g_inc[r:r + 1, :]
        hd["mt"] = _mm(lhs, rhs * rsel_ref[r])
        hd["cn_new"] = (jnp.concatenate([a_r, a_r], axis=1) * hd["cn"]
                        + jnp.concatenate([g_r, g_r], axis=1) * _mm(ke, hd["vw"]))

    yield
    for r in range(8):
        hd = heads[r]
        vis, _ = _tri_masks(r // 4)
        s = hd["qk"] * jnp.exp(jnp.where(vis, hd["mt"][:, 0:128], ninf))
        hd["sv"] = _mm(s.astype(BF16), hd["vw"])

    yield
    for d in (0, 1):
        out_ref = (hf_ref, hb_ref)[d]
        for p in range(2):
            pair = []
            for half in range(2):
                hd = heads[4 * d + 2 * p + half]
                w_b = hd["mt"][:, 128:256]
                tot = hd["sv"] + jnp.concatenate([w_b, w_b], axis=1) * hd["qc"]
                pair.append(tot[:, 0:128] / jnp.maximum(jnp.abs(tot[:, 128:256]), hd["mt"][:, 256:384]))
            out_ref[0, :, 128 * p:128 * p + 128] = jnp.where(lane < 64, pair[0], pair[1]).astype(BF16)
    m_ref[...] = m_all
    for r in range(8):
        cn_ref[r] = heads[r]["cn_new"]


def _mlstm(ml, mkt, gt, bcol, n_ctx_chunks):
    bsz, t, _ = ml.shape
    nc = t // CHUNK
    out = jax.ShapeDtypeStruct((bsz, t, W_BRANCH), BF16)
    const2 = lambda b, c: (0, 0)

    def chunk_specs(chunk):
        return [pl.BlockSpec((1, CHUNK, W_BRANCH), lambda b, c: (b, chunk(c), 0)),
                pl.BlockSpec((1, CHUNK, W_BRANCH), lambda b, c: (b, chunk(c), 1)),
                pl.BlockSpec((1, W_BRANCH, CHUNK), lambda b, c: (b, 0, chunk(c))),
                pl.BlockSpec((1, GATE_ROWS, CHUNK), lambda b, c: (b, 0, chunk(c)))]

    fwd_chunk = lambda c: c
    bwd_chunk = lambda c: _bwd_chunk(c, n_ctx_chunks, nc)
    return dict(
        steps=_mlstm_steps,
        in_specs=chunk_specs(fwd_chunk) + chunk_specs(bwd_chunk) + [
            pl.BlockSpec((GATE_ROWS, 1), const2), pl.BlockSpec((CHUNK, LANES), const2),
            pl.BlockSpec((CHUNK, 2 * LANES), const2),
            pl.BlockSpec((8, CHUNK, 3 * LANES), lambda b, c: (0, 0, 0))],
        out_specs=[pl.BlockSpec((1, CHUNK, W_BRANCH), lambda b, c: (b, c, 0)),
                   pl.BlockSpec((1, CHUNK, W_BRANCH), lambda b, c: (b, bwd_chunk(c), 0))],
        out_shape=[out, out],
        scratch_shapes=[pltpu.VMEM((8, LANES, 2 * LANES), F32), pltpu.VMEM((8, LANES), F32)],
        args=(ml, ml, mkt, gt, ml, ml, mkt, gt, bcol, jnp.ones((CHUNK, LANES), BF16), _spread_selector(2),
              _state_row_selector(3)))


def _ssd_steps(n_ctx_chunks, n_chunks,
                xf_ref, xfp_ref, xfn_ref, xb_ref, xbp_ref, xbn_ref,
                gtf_ref, gtb_ref, bcol_ref, acol_ref, cw_ref, cb_ref, dskip_ref, ones_ref, sel_ref, rsel_ref,
                yf_ref, yb_ref, s_ref):
    c = pl.program_id(1)

    @pl.when(c == 0)
    def _():
        s_ref[...] = jnp.zeros_like(s_ref)

    lane = _iota2((CHUNK, LANES), 1)
    row768 = _iota2((CHUNK, 768), 0)
    ninf = jnp.float32(-jnp.inf)

    dt, a_cum = [], []
    for d in (0, 1):
        gt_ref = (gtf_ref, gtb_ref)[d]
        tr = jnp.where(_tri_masks(d)[1], 1.0, 0.0).astype(BF16)
        dt.append(_softplus(gt_ref[0, 16:24, :] + bcol_ref[16:24, :]))
        a_cum.append(_mm_exact_r(dt[d] * acol_ref[16:24, :], tr))

    yield
    xa = []
    for d in (0, 1):
        x_ref, xp_ref, xn_ref = ((xf_ref, xfp_ref, xfn_ref), (xb_ref, xbp_ref, xbn_ref))[d]
        j = c if d == 0 else _bwd_chunk(c, n_ctx_chunks, n_chunks)
        seg_first = jnp.logical_or(j == 0, j == n_ctx_chunks)
        seg_last = jnp.logical_or(j == n_ctx_chunks - 1, j == n_chunks - 1)
        x = x_ref[0]
        prev = jnp.where(seg_first, 0.0, xp_ref[0, 7:8, :])
        nxt = jnp.where(seg_last, 0.0, xn_ref[0, 0:1, :])
        x_dn = jnp.where(row768 == 0, prev, pltpu.roll(x, 1, 0))
        x_up = jnp.where(row768 == CHUNK - 1, nxt, pltpu.roll(x, CHUNK - 1, 0))
        xa.append(_silu(x_dn * cw_ref[0:1, :] + x * cw_ref[1:2, :] + x_up * cw_ref[2:3, :] + cb_ref[...]))

    heads = [None] * 8
    groups = {}
    for d in (0, 1):
        for g in range(2):
            x_pair = xa[d][:, 128 * g:128 * g + 128]
            b_g = xa[d][:, 256 + 128 * g:256 + 128 * g + 128]
            c_gb = xa[d][:, 512 + 128 * g:512 + 128 * g + 128].astype(BF16)
            groups[d, g] = dict(x_pair=x_pair, bt=b_g.T.astype(BF16), cb=_mm_nt(c_gb, b_g.astype(BF16)))
            for half in range(2):
                r = 4 * d + 2 * g + half
                hmask = (lane >= 64) if half else (lane < 64)
                st = s_ref[r]
                xm = jnp.where(hmask, x_pair, 0.0)
                heads[r] = dict(st=st, xm=xm, xmb=xm.astype(BF16), cs=_mm(c_gb, st.astype(BF16)))

    yield
    for d in (0, 1):
        last = CHUNK - 1 if d == 0 else 0
        a_last = a_cum[d][:, last:last + 1]
        w_in = jnp.exp(a_last - a_cum[d]) * dt[d]
        e_last = jnp.exp(jnp.broadcast_to(a_last, (8, CHUNK)))
        lhs, rhs = _spread_operands(a_cum[d], -a_cum[d], [jnp.exp(a_cum[d]), w_in], sel_ref, ones_ref)
        for h in range(N_HEADS):
            r = 4 * d + h
            heads[r]["mt"] = _mm(lhs, rhs * rsel_ref[r])
            heads[r]["e_last"] = e_last[r:r + 1, :]

    yield
    for r in range(8):
        d, h = divmod(r, 4)
        hd = heads[r]
        ds = _mm(groups[d, h // 2]["bt"], (hd["xm"] * hd["mt"][:, 256:384]).astype(BF16))
        s_ref[r] = hd["e_last"] * hd["st"] + ds
    for r in range(8):
        d, h = divmod(r, 4)
        hd = heads[r]
        vis, _ = _tri_masks(d)
        sc = groups[d, h // 2]["cb"] * jnp.exp(jnp.where(vis, hd["mt"][:, 0:128], ninf)) * dt[d][r:r + 1, :]
        hd["y"] = _mm(sc.astype(BF16), hd["xmb"]) + hd["mt"][:, 128:256] * hd["cs"]

    for d in (0, 1):
        out_ref = (yf_ref, yb_ref)[d]
        for g in range(2):
            y_pair = heads[4 * d + 2 * g]["y"] + heads[4 * d + 2 * g + 1]["y"]
            if d == 0:
                y_pair = y_pair + dskip_ref[:, 128 * g:128 * g + 128] * groups[d, g]["x_pair"]
            out_ref[0, :, 128 * g:128 * g + 128] = y_pair.astype(BF16)


def _ssd(ss, gt, bcol, acol, conv_w, conv_b, dskip, n_ctx_chunks):
    bsz, t, _ = ss.shape
    nc = t // CHUNK
    sub = CHUNK // 8
    nsub = t // 8
    bc = lambda c: _bwd_chunk(c, n_ctx_chunks, nc)
    fwd = lambda b, c: (b, c, 0)
    bwd = lambda b, c: (b, bc(c), 0)
    fwd_p = lambda b, c: (b, jnp.maximum(c * sub - 1, 0), 0)
    fwd_n = lambda b, c: (b, jnp.minimum((c + 1) * sub, nsub - 1), 0)
    bwd_p = lambda b, c: (b, jnp.maximum(bc(c) * sub - 1, 0), 0)
    bwd_n = lambda b, c: (b, jnp.minimum((bc(c) + 1) * sub, nsub - 1), 0)
    fwd_t = lambda b, c: (b, 0, c)
    bwd_t = lambda b, c: (b, 0, bc(c))
    const2 = lambda b, c: (0, 0)
    out = jax.ShapeDtypeStruct((bsz, t, W_BRANCH), BF16)
    return dict(
        steps=functools.partial(_ssd_steps, n_ctx_chunks, nc),
        in_specs=[pl.BlockSpec((1, CHUNK, 768), fwd), pl.BlockSpec((1, 8, 768), fwd_p),
                  pl.BlockSpec((1, 8, 768), fwd_n),
                  pl.BlockSpec((1, CHUNK, 768), bwd), pl.BlockSpec((1, 8, 768), bwd_p),
                  pl.BlockSpec((1, 8, 768), bwd_n),
                  pl.BlockSpec((1, GATE_ROWS, CHUNK), fwd_t), pl.BlockSpec((1, GATE_ROWS, CHUNK), bwd_t),
                  pl.BlockSpec((GATE_ROWS, 1), const2), pl.BlockSpec((GATE_ROWS, 1), const2),
                  pl.BlockSpec((3, 768), const2), pl.BlockSpec((1, 768), const2),
                  pl.BlockSpec((1, W_BRANCH), const2),
                  pl.BlockSpec((CHUNK, LANES), const2), pl.BlockSpec((CHUNK, 2 * LANES), const2),
                  pl.BlockSpec((8, CHUNK, 3 * LANES), lambda b, c: (0, 0, 0))],
        out_specs=[pl.BlockSpec((1, CHUNK, W_BRANCH), fwd), pl.BlockSpec((1, CHUNK, W_BRANCH), bwd)],
        out_shape=[out, out],
        scratch_shapes=[pltpu.VMEM((8, SSM_N, LANES), F32)],
        args=(ss, ss, ss, ss, ss, ss, gt, gt, bcol, acol, conv_w, conv_b, dskip,
              jnp.ones((CHUNK, LANES), BF16), _spread_selector(2), _state_row_selector(3)))


def _hgrn2_steps(hf_ref, hb_ref, loglb_ref, log1m_ref, onem_ref, e_ref, of_ref, ob_ref,
                  st_ref):
    @pl.when(pl.program_id(1) == 0)
    def _():
        st_ref[...] = jnp.zeros_like(st_ref)

    ri = _iota2((CHUNK, CHUNK), 0)
    ci = _iota2((CHUNK, CHUNK), 1)
    lane = _iota2((CHUNK, LANES), 1)
    rw = _iota2((CHUNK, W_BRANCH), 0)
    blockdiag = (ri >> 6) == (ci >> 6)
    ninf = jnp.float32(-jnp.inf)
    nb = CHUNK // SUB
    rs = _iota2((nb, SUB, W_BRANCH), 1)

    dirs = []
    for d in (0, 1):
        h_ref = (hf_ref, hb_ref)[d]
        tc = jnp.where(_tri_masks(d)[0], 1.0, 0.0).astype(BF16)
        z = h_ref[0, :, W_BRANCH * (1 + d):W_BRANCH * (2 + d)]
        e = jnp.exp(-jnp.abs(z))
        ope = 1.0 + e
        la = loglb_ref[d:d + 1, :]
        lb_ = log1m_ref[d:d + 1, :] + (jnp.minimum(z, 0.0) - jnp.log(ope))
        logf = jnp.maximum(la, lb_) + jnp.log(1.0 + jnp.exp(-jnp.abs(la - lb_)))
        dirs.append(dict(
            q=_silu(h_ref[0, :, 0:W_BRANCH]),
            v=h_ref[0, :, 3 * W_BRANCH:4 * W_BRANCH],
            kk=onem_ref[d:d + 1, :] * (jnp.where(z >= 0.0, e, 1.0) / ope),
            gcum=_mm_exact_l(tc, logf)))

    yield
    for d in (0, 1):
        dd = dirs[d]
        q, kk, gcum = dd["q"], dd["kk"], dd["gcum"]
        last = CHUNK - 1 if d == 0 else 0
        g_last = gcum[last:last + 1, :]
        qg = (q * jnp.exp(gcum)).astype(BF16)
        kg = (kk * jnp.exp(g_last - gcum)).astype(BF16)
        vb = dd["v"].astype(BF16)

        dd["o_inter"] = []
        for p in range(2):
            sl = slice(128 * p, 128 * p + 128)
            st = st_ref[2 * d + p]
            dd["o_inter"].append(_mm_nt(qg[:, sl], st.astype(BF16)))
            dst = _mm_tn(vb[:, sl], kg[:, sl])
            st_ref[2 * d + p] = st * jnp.exp(g_last[:, sl]) + jnp.where(blockdiag, dst, 0.0)

        a_mats = [jnp.zeros((CHUNK, CHUNK), F32) for _ in range(N_HEADS)]
        blk = CHUNK // 2
        while blk >= SUB:
            first = (rw & (2 * blk - 1)) < blk
            edge = (blk - 1) if d == 0 else blk
            gb = gcum.reshape(CHUNK // (2 * blk), 2 * blk, W_BRANCH)[:, edge:edge + 1, :]
            gb = jnp.broadcast_to(gb, (CHUNK // (2 * blk), 2 * blk, W_BRANCH)).reshape(CHUNK, W_BRANCH)
            q_side = jnp.logical_not(first) if d == 0 else first
            qt = q * jnp.exp(jnp.where(q_side, gcum - gb, ninf))
            kt = (kk * jnp.exp(jnp.where(q_side, ninf, gb - gcum))).astype(BF16)
            same = (ri >> int(math.log2(2 * blk))) == (ci >> int(math.log2(2 * blk)))
            for h in range(N_HEADS):
                p, half = divmod(h, 2)
                hmask = (lane >= 64) if half else (lane < 64)
                qh = jnp.where(hmask, qt[:, 128 * p:128 * p + 128], 0.0).astype(BF16)
                a_mats[h] = a_mats[h] + jnp.where(same, _mm_nt(qh, kt[:, 128 * p:128 * p + 128]), 0.0)
            blk //= 2
        dd["a_mats"] = a_mats

        g3 = gcum.reshape(nb, SUB, W_BRANCH)
        q3 = q.reshape(nb, SUB, W_BRANCH)
        k3 = kk.reshape(nb, SUB, W_BRANCH)
        v3 = dd["v"].reshape(nb, SUB, W_BRANCH)
        o3 = None
        for j in range(SUB):
            ok = (rs >= j) if d == 0 else (rs <= j)
            pj = q3 * jnp.exp(jnp.where(ok, g3 - g3[:, j:j + 1, :], ninf)) * k3[:, j:j + 1, :]
            red = _mm(pj.reshape(CHUNK, W_BRANCH).astype(BF16), e_ref[...])
            term = red.reshape(nb, SUB, W_BRANCH) * v3[:, j:j + 1, :]
            o3 = term if o3 is None else o3 + term
        dd["o_diag"] = o3.reshape(CHUNK, W_BRANCH)

    yield
    for d in (0, 1):
        dd = dirs[d]
        out_ref = (of_ref, ob_ref)[d]
        for p in range(2):
            sl = slice(128 * p, 128 * p + 128)
            a_cat = jnp.concatenate([dd["a_mats"][2 * p], dd["a_mats"][2 * p + 1]], axis=1).astype(BF16)
            v_p = dd["v"][:, sl]
            v_cat = jnp.concatenate([jnp.where(lane < 64, v_p, 0.0), jnp.where(lane >= 64, v_p, 0.0)],
                                    axis=0).astype(BF16)
            out_ref[0, :, sl] = (dd["o_inter"][p] + _mm(a_cat, v_cat) + dd["o_diag"][:, sl]).astype(BF16)


def _hgrn2(hg, loglb, log1m, onem, e64, n_ctx_chunks):
    bsz, t, _ = hg.shape
    nc = t // CHUNK
    fwd = lambda b, c: (b, c, 0)
    bwd = lambda b, c: (b, _bwd_chunk(c, n_ctx_chunks, nc), 0)
    const2 = lambda b, c: (0, 0)
    out = jax.ShapeDtypeStruct((bsz, t, W_BRANCH), BF16)
    return dict(
        steps=_hgrn2_steps,
        in_specs=[pl.BlockSpec((1, CHUNK, 1024), fwd), pl.BlockSpec((1, CHUNK, 1024), bwd),
                  pl.BlockSpec((2, W_BRANCH), const2), pl.BlockSpec((2, W_BRANCH), const2),
                  pl.BlockSpec((2, W_BRANCH), const2), pl.BlockSpec((W_BRANCH, W_BRANCH), const2)],
        out_specs=[pl.BlockSpec((1, CHUNK, W_BRANCH), fwd), pl.BlockSpec((1, CHUNK, W_BRANCH), bwd)],
        out_shape=[out, out],
        scratch_shapes=[pltpu.VMEM((4, LANES, LANES), F32)],
        args=(hg, hg, loglb, log1m, onem, e64))


def _scan_kernel(parts, *refs):
    n_in = [len(p["in_specs"]) for p in parts]
    n_out = [len(p["out_specs"]) for p in parts]
    n_scr = [len(p["scratch_shapes"]) for p in parts]
    ins, outs, scr = refs[:sum(n_in)], refs[sum(n_in):sum(n_in) + sum(n_out)], refs[sum(n_in) + sum(n_out):]
    gens = []
    for k, p in enumerate(parts):
        mine = (ins[sum(n_in[:k]):sum(n_in[:k + 1])] + outs[sum(n_out[:k]):sum(n_out[:k + 1])]
                + scr[sum(n_scr[:k]):sum(n_scr[:k + 1])])
        gens.append(p["steps"](*mine))
    while gens:
        for g in list(gens):
            if next(g, gens) is gens:
                gens.remove(g)


def _chunk_scans(parts, bsz, n_chunks):
    return pl.pallas_call(
        functools.partial(_scan_kernel, parts),
        grid=(bsz, n_chunks),
        in_specs=[s for p in parts for s in p["in_specs"]],
        out_specs=[s for p in parts for s in p["out_specs"]],
        out_shape=[s for p in parts for s in p["out_shape"]],
        scratch_shapes=[s for p in parts for s in p["scratch_shapes"]],
        compiler_params=_cparams(("parallel", "arbitrary")),
        name="chunk_scans",
    )(*[a for p in parts for a in p["args"]])


def _attn(daq, kt, vx, lam_vecs, lam_init, n_ctx_tiles, skip_ctx):
    bsz, t, _ = daq.shape
    tq = ROW_TILE
    n_k_tiles = t // KEY_TILE
    n_lat_tiles = n_k_tiles - n_ctx_tiles
    group = math.gcd(KEY_GROUP, n_lat_tiles)
    q_off = n_ctx_tiles if skip_ctx else 0
    nq = t // tq - q_off
    n_maps = W_BRANCH // DA_QK
    rows = n_maps * tq

    def kern(q_ref, kt_ref, vx_ref, lv_ref, o_ref, q8_ref, m_ref, acc_ref, s_ref, mx_ref):
        qi = pl.program_id(1) + q_off
        lane = _iota2((tq, W_BRANCH), 1)
        q = q_ref[0]
        for j in range(n_maps):
            q8_ref[j * tq:(j + 1) * tq, :] = jnp.where((lane >> 5) == j, q, jnp.zeros_like(q))

        def score_tile(t):
            return _mm(q8_ref[...], kt_ref[0, t])
        m_ref[...] = jnp.full_like(m_ref, -jnp.inf)
        acc_ref[...] = jnp.zeros_like(acc_ref)

        def update(t0, n, after_scores=None):
            s = [score_tile(t0 + i) for i in range(n)]
            if after_scores is not None:
                after_scores()
            mx = s[0][:, 0:LANES]
            for i in range(n):
                for c in range(KEY_TILE // LANES):
                    if i or c:
                        mx = jnp.maximum(mx, s[i][:, LANES * c:LANES * (c + 1)])
            m_old = m_ref[...]
            m_new = jnp.maximum(m_old, jnp.max(mx, axis=1, keepdims=True))
            alpha = jnp.exp(m_old - m_new)
            m_ref[...] = m_new
            m2 = jnp.concatenate([m_new] * (KEY_TILE // LANES), axis=1)
            p = [jnp.exp(s[i] - m2).astype(BF16) for i in range(n)]
            for hd in range(N_HEADS):
                r0 = 2 * hd * tq
                pv = _mm(p[0][r0:r0 + 2 * tq], vx_ref[0, t0, hd])
                for i in range(1, n):
                    pv = pv + _mm(p[i][r0:r0 + 2 * tq], vx_ref[0, t0 + i, hd])
                acc_ref[r0:r0 + 2 * tq, :] = alpha[r0:r0 + 2 * tq] * acc_ref[r0:r0 + 2 * tq, :] + pv

        def scores(g, slot):
            t0 = n_ctx_tiles + g * group
            mx = None
            for i in range(group):
                s = score_tile(t0 + i)
                s_ref[slot, :, KEY_TILE * i:KEY_TILE * (i + 1)] = s
                for c in range(KEY_TILE // LANES):
                    part = s[:, LANES * c:LANES * (c + 1)]
                    mx = part if mx is None else jnp.maximum(mx, part)
            mx_ref[slot] = mx

        def absorb(g, slot):
            t0 = n_ctx_tiles + g * group
            m_old = m_ref[...]
            m_new = jnp.maximum(m_old, jnp.max(mx_ref[slot], axis=1, keepdims=True))
            alpha = jnp.exp(m_old - m_new)
            m_ref[...] = m_new
            m2 = jnp.concatenate([m_new] * (KEY_TILE // LANES), axis=1)
            p = [jnp.exp(s_ref[slot, :, KEY_TILE * i:KEY_TILE * (i + 1)] - m2).astype(BF16) for i in range(group)]
            for hd in range(N_HEADS):
                r0 = 2 * hd * tq
                pv = _mm(p[0][r0:r0 + 2 * tq], vx_ref[0, t0, hd])
                for i in range(1, group):
                    pv = pv + _mm(p[i][r0:r0 + 2 * tq], vx_ref[0, t0 + i, hd])
                acc_ref[r0:r0 + 2 * tq, :] = alpha[r0:r0 + 2 * tq] * acc_ref[r0:r0 + 2 * tq, :] + pv

        @pl.when(qi < n_ctx_tiles)
        def _():
            update(0, n_ctx_tiles)

        @pl.when(qi >= n_ctx_tiles)
        def _():
            n_groups = n_lat_tiles // group
            n_pairs = (n_groups - 1) // 2
            update(0, n_ctx_tiles, after_scores=lambda: scores(0, 0))

            def body(k, carry):
                g = 2 * k
                scores(g + 1, 1)
                absorb(g, 0)
                scores(g + 2, 0)
                absorb(g + 1, 1)
                return carry
            for k in range(n_pairs):
                body(k, 0)
            g0 = 2 * n_pairs
            if n_groups - g0 == 2:
                scores(g0 + 1, 1)
            absorb(g0, 0)
            if n_groups - g0 == 2:
                absorb(g0 + 1, 1)

        lv = lv_ref[...]
        lam = (jnp.exp(jnp.sum(lv[0:1, :] * lv[1:2, :], axis=1, keepdims=True))
               - jnp.exp(jnp.sum(lv[2:3, :] * lv[3:4, :], axis=1, keepdims=True)) + lam_init)
        lane1 = _iota2((tq, LANES), 1)
        for pr in range(N_HEADS // 2):
            halves = []
            for half in range(2):
                r0 = 2 * (2 * pr + half) * tq
                one = 0 if half else HEAD_W
                a0 = acc_ref[r0:r0 + tq, :]
                a1 = acc_ref[r0 + tq:r0 + 2 * tq, :]
                halves.append(a0 / a0[:, one:one + 1] - lam * (a1 / a1[:, one:one + 1]))
            o_ref[0, :, LANES * pr:LANES * (pr + 1)] = jnp.where(lane1 < HEAD_W, halves[0], halves[1]).astype(BF16)

    return pl.pallas_call(
        kern,
        grid=(bsz, nq),
        in_specs=[pl.BlockSpec((1, tq, W_BRANCH), lambda b, i: (b, i + q_off, 0)),
                  pl.BlockSpec((1, n_k_tiles, W_BRANCH, KEY_TILE), lambda b, i: (b, 0, 0, 0),
                               pipeline_mode=pl.Buffered(1)),
                  pl.BlockSpec((1, n_k_tiles, N_HEADS, KEY_TILE, LANES), lambda b, i: (b, 0, 0, 0, 0),
                               pipeline_mode=pl.Buffered(1)),
                  pl.BlockSpec(lam_vecs.shape, lambda b, i: (0, 0))],
        out_specs=pl.BlockSpec((1, tq, W_BRANCH), lambda b, i: (b, i, 0)),
        out_shape=jax.ShapeDtypeStruct((bsz, nq * tq, W_BRANCH), BF16),
        scratch_shapes=[pltpu.VMEM((rows, W_BRANCH), BF16),
                        pltpu.VMEM((rows, LANES), F32),
                        pltpu.VMEM((rows, LANES), F32),
                        pltpu.VMEM((2, rows, group * KEY_TILE), F32),
                        pltpu.VMEM((2, rows, LANES), F32)],
        compiler_params=_cparams(("parallel", "parallel")),
        name="diff_attn",
    )(daq, kt, vx, lam_vecs)


def _seg_mean(x, e_ref):
    x0 = x.astype(BF16)
    x1 = (x - x0.astype(F32)).astype(BF16)
    return (_mm(x0, e_ref[...]) + _mm(x1, e_ref[...])) * (1.0 / HEAD_W)


def _out_kernel(n_ctx_tiles, ctx_row, q_off, lam_init, final,
                c_ref, x_ref, mo_ref, mlo_ref, mlz_ref, mhf_ref, mhb_ref, dao_ref, daz_ref,
                syf_ref, syb_ref, ssz_ref, hof_ref, hob_ref, hgz_ref,
                mlg_ref, dag_ref, ssg_ref, hgg_ref, e_ref, w_ref, fg_ref, o_ref):
    b = pl.program_id(0)
    i = pl.program_id(1) + q_off
    d = x_ref.shape[2]
    r = jnp.where(i < n_ctx_tiles, ctx_row, b)
    gate = mo_ref[pl.ds(r, 1), :][:, 2 * d:3 * d]

    u = _sigmoid(mlo_ref[0].astype(F32)) * (mhf_ref[0].astype(F32) + mhb_ref[0].astype(F32))
    dev = u - _seg_mean(u, e_ref)
    y_ml = dev * lax.rsqrt(_seg_mean(dev * dev, e_ref) + EPS) * mlg_ref[...] * _silu(mlz_ref[0].astype(F32))

    o = dao_ref[0].astype(F32)
    y_da = (o * lax.rsqrt(_seg_mean(o * o, e_ref) + EPS) * dag_ref[...]) * (1.0 - lam_init) * _silu(daz_ref[0].astype(F32))

    ys = (syf_ref[0].astype(F32) + syb_ref[0].astype(F32)) * _silu(ssz_ref[0].astype(F32))
    parts = []
    for g in range(2):
        yg = ys[:, 128 * g:128 * g + 128]
        parts.append(yg * lax.rsqrt(jnp.mean(yg * yg, axis=-1, keepdims=True) + EPS))
    y_ss = jnp.concatenate(parts, axis=1) * ssg_ref[...]

    oh = hof_ref[0].astype(F32) + hob_ref[0].astype(F32)
    y_hg = (oh * lax.rsqrt(_seg_mean(oh * oh, e_ref) + EPS) * hgg_ref[...]) * _silu(hgz_ref[0].astype(F32))

    acc = _mm(y_ml.astype(BF16), w_ref[0:W_BRANCH, :])
    acc = acc + _mm(y_da.astype(BF16), w_ref[W_BRANCH:2 * W_BRANCH, :])
    acc = acc + _mm(y_ss.astype(BF16), w_ref[2 * W_BRANCH:3 * W_BRANCH, :])
    acc = acc + _mm(y_hg.astype(BF16), w_ref[3 * W_BRANCH:4 * W_BRANCH, :])
    x_new = jnp.where(i < n_ctx_tiles, c_ref[0], x_ref[0]) + gate * acc
    if final:
        x_new = x_new * lax.rsqrt(jnp.mean(x_new * x_new, axis=-1, keepdims=True) + EPS) * fg_ref[...]
    o_ref[0] = x_new


def _out_proj(stream, mo, zg, mhf, mhb, dao, syf, syb, hof, hob,
              mlg, dag, ssg, hgg, e64, w_out, final_g, lam_init, n_ctx_tiles, ctx_row, final):
    c_src, x_src, x_off, t = stream
    bsz, _, d = x_src.shape
    tm = ROW_TILE
    q_off = n_ctx_tiles if final else 0
    nrow = t // tm - q_off
    row = lambda b, i: (b, i + q_off, 0)
    col = lambda k: (lambda b, i: (b, i + q_off, k))
    const2 = lambda b, i: (0, 0)
    wb = pl.BlockSpec((1, tm, W_BRANCH), row)
    return pl.pallas_call(
        functools.partial(_out_kernel, n_ctx_tiles, ctx_row, q_off, lam_init, final),
        grid=(bsz, nrow),
        in_specs=_stream_specs(stream, n_ctx_tiles, q_off) + [
                  pl.BlockSpec(mo.shape, const2),
                  pl.BlockSpec((1, tm, W_BRANCH), col(0)), pl.BlockSpec((1, tm, W_BRANCH), col(1)),
                  wb, wb,
                  pl.BlockSpec((1, tm, W_BRANCH), lambda b, i: (b, i, 0)) if final else wb,
                  pl.BlockSpec((1, tm, W_BRANCH), col(2)),
                  wb, wb, pl.BlockSpec((1, tm, W_BRANCH), col(3)),
                  wb, wb, pl.BlockSpec((1, tm, W_BRANCH), col(4)),
                  pl.BlockSpec((1, W_BRANCH), const2), pl.BlockSpec((1, W_BRANCH), const2),
                  pl.BlockSpec((1, W_BRANCH), const2), pl.BlockSpec((1, W_BRANCH), const2),
                  pl.BlockSpec((W_BRANCH, W_BRANCH), const2),
                  pl.BlockSpec(w_out.shape, const2),
                  pl.BlockSpec((1, d), const2)],
        out_specs=pl.BlockSpec((1, tm, d), lambda b, i: (b, i, 0)),
        out_shape=jax.ShapeDtypeStruct((bsz, nrow * tm, d), F32),
        compiler_params=_cparams(("parallel", "parallel")),
        name="out_proj",
    )(c_src, x_src, mo, zg, zg, mhf, mhb, dao, zg, syf, syb, zg, hof, hob, zg,
      mlg, dag, ssg, hgg, e64, w_out, final_g.reshape(1, d))


def _relayout_w_in(w_in):
    d = w_in.shape[0]
    o = {}
    off = 0
    for name, n in (("ml_q", 256), ("ml_k", 256), ("ml_v", 256), ("ml_o", 256), ("ml_i", 8), ("ml_f", 8),
                    ("ml_z", 256), ("da_q", 256), ("da_k", 256), ("da_v", 256), ("da_z", 256),
                    ("ss_xbc", 768), ("ss_dt", 8), ("ss_z", 256),
                    ("hg_q", 256), ("hg_f", 512), ("hg_i", 256), ("hg_z", 256)):
        o[name] = w_in[:, off:off + n]
        off += n

    gates = jnp.concatenate([o["ml_i"], o["ml_f"], o["ss_dt"], jnp.zeros((d, GATE_ROWS - 24), w_in.dtype)], axis=1)
    w2 = jnp.concatenate([o["ml_q"], o["ml_v"], o["da_q"], o["da_v"],
                          o["ss_xbc"], o["hg_q"], o["hg_f"], o["hg_i"],
                          o["ml_o"], o["ml_z"], o["da_z"], o["ss_z"], o["hg_z"]], axis=1)
    wt = jnp.concatenate([o["da_k"], gates, o["ml_k"]], axis=1).T
    return w2.astype(BF16), wt.astype(BF16)


def _rope_tables(n_ctx, seq):
    pos = jnp.arange(seq)
    rows = (pos // GRID_W).astype(F32)
    cols = (pos % GRID_W).astype(F32)
    axis = DA_QK // 2
    inv = ROPE_BASE ** (-jnp.arange(0, axis, 2, dtype=F32) / axis)
    ang = jnp.concatenate([rows[:, None] * inv, cols[:, None] * inv], axis=-1)
    cos = jnp.concatenate([jnp.ones((n_ctx, axis), F32), jnp.cos(ang)], axis=0)
    sin = jnp.concatenate([jnp.zeros((n_ctx, axis), F32), jnp.sin(ang)], axis=0)
    cos_t = jnp.concatenate([jnp.ones((axis, n_ctx), F32), jnp.cos(ang.T)], axis=1)
    sin_t = jnp.concatenate([jnp.zeros((axis, n_ctx), F32), jnp.sin(ang.T)], axis=1)
    return (jnp.tile(cos, (1, LANES // axis)), jnp.tile(sin, (1, LANES // axis)),
            jnp.tile(cos_t, (2, 1)), jnp.tile(sin_t, (2, 1)))


def _gate_col(vals):
    v = jnp.concatenate([vals.astype(F32), jnp.zeros((GATE_ROWS - vals.shape[0],), F32)])
    return v.reshape(GATE_ROWS, 1)


def kernel(x, c, ctx, c_ctx, w_mod, b_mod, norm_g, w_in, w_out, ml_gate_b, ml_norm_g, da_lambda, da_norm_g,
           ss_conv_w, ss_conv_b, ss_dt_bias, ss_a_log, ss_d, ss_norm_g, hg_lower, hg_norm_g, final_g):
    bsz, seq, d = x.shape
    n_ctx = ctx.shape[1]
    depth = w_mod.shape[0]
    assert n_ctx % ROW_TILE == 0 and seq % ROW_TILE == 0 and bsz < 8
    n_ctx_tiles = n_ctx // ROW_TILE
    n_ctx_chunks = n_ctx // CHUNK
    ctx_row = bsz

    stream = (ctx, x, 0, n_ctx + seq)
    cc = jnp.concatenate([c, c_ctx[None, :], jnp.zeros((8 - bsz - 1, d), F32)], axis=0)
    cos, sin, cos_t, sin_t = _rope_tables(n_ctx, seq)
    lb_all = jnp.cumsum(jax.nn.softmax(hg_lower.astype(F32), axis=1), axis=1)
    lb_all = lb_all - lb_all[:, :1]
    hid = _iota2((W_BRANCH, W_BRANCH), 0) // HEAD_W
    e64 = (hid == hid.T).astype(BF16)

    out = None
    for l in range(depth):
        lam_init = 0.8 - 0.6 * math.exp(-0.3 * l)
        final = l == depth - 1
        w2, wt = _relayout_w_in(w_in[l])
        mo = _modulation(cc, w_mod[l], b_mod[l])
        ml, daq, dak, dav, ss, hg, zg, gt, mkt = _in_proj(stream, mo, norm_g[l], cos, sin, cos_t, sin_t, w2, wt,
                                                     n_ctx_tiles, ctx_row)

        gb = ml_gate_b[l]
        zeros8 = jnp.zeros((8,), F32)
        bcol = _gate_col(jnp.concatenate([gb[:, 0].reshape(-1), gb[:, 1].reshape(-1),
                                                ss_dt_bias[l].reshape(-1)]))
        acol = _gate_col(jnp.concatenate([zeros8, zeros8, -jnp.exp(ss_a_log[l].astype(F32)).reshape(-1)]))
        dao = _attn(daq, dak, dav, da_lambda[l].astype(F32), lam_init, n_ctx_tiles, final)
        dskip = jnp.repeat(ss_d[l].astype(F32), HEAD_W).reshape(1, W_BRANCH)
        lbh = lb_all[:, l]
        scans = [_ssd(ss, gt, bcol, acol, ss_conv_w[l], ss_conv_b[l].reshape(1, -1), dskip, n_ctx_chunks),
                 _mlstm(ml, mkt, gt, bcol, n_ctx_chunks),
                 _hgrn2(hg, jnp.log(lbh), jnp.log1p(-lbh), 1.0 - lbh, e64, n_ctx_chunks)]
        syf, syb, mhf, mhb, hof, hob = _chunk_scans(scans, bsz, (n_ctx + seq) // CHUNK)

        res = _out_proj(stream, mo, zg, mhf, mhb, dao, syf, syb, hof, hob,
                        ml_norm_g[l].reshape(1, -1), jnp.tile(da_norm_g[l], N_HEADS).reshape(1, -1),
                        ss_norm_g[l].reshape(1, -1), hg_norm_g[l].reshape(1, -1), e64,
                        w_out[l].astype(BF16), final_g, lam_init, n_ctx_tiles, ctx_row, final)
        if final:
            out = res
        else:
            stream = (res, res, n_ctx_tiles, n_ctx + seq)
    return out
```

```python
import functools
import math

import jax
import jax.numpy as jnp
from jax import lax
from jax.experimental import pallas as pl
from jax.experimental.pallas import tpu as pltpu

F32 = jnp.float32
BF16 = jnp.bfloat16

EPS = 1e-6
GRID_W = 64
ROPE_BASE = 10000.0
N_HEADS = 4
HEAD_W = 64
W_BRANCH = N_HEADS * HEAD_W
DA_QK = 32
SSM_N = 128
CHUNK = 128
ROW_TILE = 256
SUB = 8
LANES = 128
VMEM_LIMIT = 56 * 1024 * 1024

C_ML = 0
C_DAQ = 512
C_DAV = 768
C_SS = 1024
C_HG = 1792
C_ZG = 2816
C_END = 4096
GATE_ROWS = 32
KEY_TILE = 256
KEY_GROUP = 4

NT = (((1,), (1,)), ((), ()))
TN = (((0,), (0,)), ((), ()))


def _mm(a, b):
    return jnp.dot(a, b, preferred_element_type=F32)


def _mm_nt(a, b):
    return lax.dot_general(a, b, NT, preferred_element_type=F32)


def _mm_tn(a, b):
    return lax.dot_general(a, b, TN, preferred_element_type=F32)


def _split3(x):
    x0 = x.astype(BF16)
    r = x - x0.astype(F32)
    x1 = r.astype(BF16)
    x2 = (r - x1.astype(F32)).astype(BF16)
    return x0, x1, x2


def _mm_exact_l(t, x):
    x0, x1, x2 = _split3(x)
    return _mm(t, x0) + _mm(t, x1) + _mm(t, x2)


def _mm_exact_r(x, t):
    x0, x1, x2 = _split3(x)
    return _mm(x0, t) + _mm(x1, t) + _mm(x2, t)


def _sigmoid(x):
    return 1.0 / (1.0 + jnp.exp(-x))


def _silu(x):
    return x * _sigmoid(x)


def _log_sigmoid(x):
    return jnp.minimum(x, 0.0) - jnp.log(1.0 + jnp.exp(-jnp.abs(x)))


def _softplus(x):
    return jnp.maximum(x, 0.0) + jnp.log(1.0 + jnp.exp(-jnp.abs(x)))


def _iota2(shape, axis):
    return lax.broadcasted_iota(jnp.int32, shape, axis)


def _cparams(sem):
    return pltpu.CompilerParams(dimension_semantics=sem, vmem_limit_bytes=VMEM_LIMIT)


def _mod_kernel(c_ref, w_ref, b_ref, o_ref):
    c = c_ref[...]
    o_ref[...] = jnp.dot(_silu(c), w_ref[...], precision=lax.Precision.HIGHEST,
                         preferred_element_type=F32) + b_ref[...]


def _modulation(cc, w_mod, b_mod):
    d = cc.shape[1]
    n = w_mod.shape[1]
    tn = 1536
    assert n % tn == 0
    return pl.pallas_call(
        _mod_kernel,
        grid=(n // tn,),
        in_specs=[pl.BlockSpec((8, d), lambda j: (0, 0)),
                  pl.BlockSpec((d, tn), lambda j: (0, j)),
                  pl.BlockSpec((1, tn), lambda j: (0, j))],
        out_specs=pl.BlockSpec((8, tn), lambda j: (0, j)),
        out_shape=jax.ShapeDtypeStruct((8, n), F32),
        compiler_params=_cparams(("arbitrary",)),
        name="modulation",
    )(cc, w_mod, b_mod.reshape(1, n))


def _stream_specs(stream, n_ctx_tiles, first_tile):
    _, x_src, x_off, _ = stream
    d = x_src.shape[2]
    return [pl.BlockSpec((1, ROW_TILE, d), lambda b, i: (b, jnp.minimum(i + first_tile, n_ctx_tiles - 1), 0)),
            pl.BlockSpec((1, ROW_TILE, d), lambda b, i: (b, jnp.maximum(i + first_tile - n_ctx_tiles, 0) + x_off, 0))]


def _in_kernel(n_ctx_tiles, ctx_row, c_ref, x_ref, mo_ref, g_ref, cos_ref, sin_ref, cost_ref, sint_ref,
               w_ref, wt_ref,
               ml_ref, daq_ref, kt_ref, vx_ref, ss_ref, hg_ref, zg_ref, gt_ref, mkt_ref):
    b = pl.program_id(0)
    i = pl.program_id(1)
    d = x_ref.shape[2]
    r = jnp.where(i < n_ctx_tiles, ctx_row, b)
    mo = mo_ref[pl.ds(r, 1), :]
    shift = mo[:, 0:d]
    scale = mo[:, d:2 * d]
    x = jnp.where(i < n_ctx_tiles, c_ref[0], x_ref[0])
    y = x * lax.rsqrt(jnp.mean(x * x, axis=-1, keepdims=True) + EPS) * g_ref[...]
    h = (y * (1.0 + scale) + shift).astype(BF16)

    ml_ref[0] = _mm(h, w_ref[:, C_ML:C_DAQ])
    q = _mm(h, w_ref[:, C_DAQ:C_DAV])
    half = DA_QK // 2
    lane = _iota2((x.shape[0], LANES), 1)
    q_rot = []
    for c in range(W_BRANCH // LANES):
        qc = q[:, LANES * c:LANES * (c + 1)]
        q_rot.append(jnp.where((lane & half) == 0, -pltpu.roll(qc, LANES - half, 1), pltpu.roll(qc, half, 1)))
    n_rep = W_BRANCH // LANES
    q = (q * jnp.concatenate([cos_ref[...]] * n_rep, axis=1)
         + jnp.concatenate(q_rot, axis=1) * jnp.concatenate([sin_ref[...]] * n_rep, axis=1))
    daq_ref[0] = (q * (DA_QK ** -0.5)).astype(BF16)
    v = _mm(h, w_ref[:, C_DAV:C_SS])
    for hd in range(N_HEADS):
        pair = v[:, LANES * (hd // 2):LANES * (hd // 2 + 1)]
        own = (lane >= HEAD_W) if hd % 2 else (lane < HEAD_W)
        vx_ref[0, 0, hd] = jnp.where(own, pair, jnp.where(lane == (0 if hd % 2 else HEAD_W), 1.0, 0.0)).astype(BF16)
    ss_ref[0] = _mm(h, w_ref[:, C_SS:C_HG])
    hg_ref[0] = _mm(h, w_ref[:, C_HG:C_ZG])
    zg_ref[0] = _mm(h, w_ref[:, C_ZG:C_END]).astype(BF16)
    tr = _mm_nt(wt_ref[...], h)
    kt = tr[0:W_BRANCH]
    kt_rot = []
    for j in range(W_BRANCH // DA_QK):
        kt_rot += [-kt[DA_QK * j + half:DA_QK * (j + 1)], kt[DA_QK * j:DA_QK * j + half]]
    n_grp = W_BRANCH // DA_QK
    kt_ref[0, 0] = (kt * jnp.concatenate([cost_ref[...]] * n_grp, axis=0)
                    + jnp.concatenate(kt_rot, axis=0) * jnp.concatenate([sint_ref[...]] * n_grp, axis=0)).astype(BF16)
    gt_ref[0] = tr[W_BRANCH:W_BRANCH + GATE_ROWS]
    mkt_ref[0] = tr[W_BRANCH + GATE_ROWS:2 * W_BRANCH + GATE_ROWS]


def _in_proj(stream, mo, g, cos, sin, cos_t, sin_t, w2, wt, n_ctx_tiles, ctx_row):
    c_src, x_src, x_off, t = stream
    bsz, _, d = x_src.shape
    tm = ROW_TILE
    assert tm == KEY_TILE
    row = lambda b, i: (b, i, 0)
    const2 = lambda b, i: (0, 0)
    out_shape = [
        jax.ShapeDtypeStruct((bsz, t, 2 * W_BRANCH), F32),
        jax.ShapeDtypeStruct((bsz, t, W_BRANCH), BF16),
        jax.ShapeDtypeStruct((bsz, t // tm, W_BRANCH, tm), BF16),
        jax.ShapeDtypeStruct((bsz, t // tm, N_HEADS, tm, LANES), BF16),
        jax.ShapeDtypeStruct((bsz, t, 768), F32),
        jax.ShapeDtypeStruct((bsz, t, 4 * W_BRANCH), F32),
        jax.ShapeDtypeStruct((bsz, t, 5 * W_BRANCH), BF16),
        jax.ShapeDtypeStruct((bsz, GATE_ROWS, t), F32),
        jax.ShapeDtypeStruct((bsz, W_BRANCH, t), F32),
    ]
    out_specs = [
        pl.BlockSpec((1, tm, 2 * W_BRANCH), row),
        pl.BlockSpec((1, tm, W_BRANCH), row),
        pl.BlockSpec((1, 1, W_BRANCH, tm), lambda b, i: (b, i, 0, 0)),
        pl.BlockSpec((1, 1, N_HEADS, tm, LANES), lambda b, i: (b, i, 0, 0, 0)),
        pl.BlockSpec((1, tm, 768), row),
        pl.BlockSpec((1, tm, 4 * W_BRANCH), row),
        pl.BlockSpec((1, tm, 5 * W_BRANCH), row),
        pl.BlockSpec((1, GATE_ROWS, tm), lambda b, i: (b, 0, i)),
        pl.BlockSpec((1, W_BRANCH, tm), lambda b, i: (b, 0, i)),
    ]
    return pl.pallas_call(
        functools.partial(_in_kernel, n_ctx_tiles, ctx_row),
        grid=(bsz, t // tm),
        in_specs=_stream_specs(stream, n_ctx_tiles, 0) + [
                  pl.BlockSpec(mo.shape, const2),
                  pl.BlockSpec((1, d), const2),
                  pl.BlockSpec((tm, LANES), lambda b, i: (i, 0)),
                  pl.BlockSpec((tm, LANES), lambda b, i: (i, 0)),
                  pl.BlockSpec((DA_QK, tm), lambda b, i: (0, i)),
                  pl.BlockSpec((DA_QK, tm), lambda b, i: (0, i)),
                  pl.BlockSpec(w2.shape, const2),
                  pl.BlockSpec(wt.shape, const2)],
        out_specs=out_specs,
        out_shape=out_shape,
        compiler_params=_cparams(("parallel", "parallel")),
        name="in_proj",
    )(c_src, x_src, mo, g.reshape(1, d), cos, sin, cos_t, sin_t, w2, wt)


def _bwd_chunk(c, n_ctx_chunks, n_chunks):
    return jnp.where(c < n_ctx_chunks, n_ctx_chunks - 1 - c, n_chunks + n_ctx_chunks - 1 - c)


def _tri_masks(d):
    ri = _iota2((CHUNK, CHUNK), 0)
    ci = _iota2((CHUNK, CHUNK), 1)
    if d == 0:
        return ci <= ri, ri <= ci
    return ci >= ri, ri >= ci


def _split_f32(x, n):
    parts = []
    for _ in range(n):
        p = x.astype(BF16).astype(F32)
        parts.append(p)
        x = x - p
    return parts


def _cummax_lanes(u, d):
    lane = _iota2(u.shape, 1)
    ninf = jnp.float32(-jnp.inf)
    s = 1
    while s < CHUNK:
        if d == 0:
            sh = jnp.where(lane >= s, pltpu.roll(u, s, 1), ninf)
        else:
            sh = jnp.where(lane < CHUNK - s, pltpu.roll(u, CHUNK - s, 1), ninf)
        u = jnp.maximum(u, sh)
        s *= 2
    return u


def _state_row_selector(n_tiles):
    keep = (jnp.arange(CHUNK)[None, :] % 8) == jnp.arange(8)[:, None]
    return jnp.broadcast_to(keep[:, :, None], (8, CHUNK, n_tiles * LANES)).astype(BF16)


def _spread_selector(n_spread):
    t = jnp.arange(CHUNK)[:, None] // 16
    tiles = [jnp.broadcast_to(t == 3 + k, (CHUNK, LANES)) for k in range(n_spread)]
    return jnp.concatenate(tiles, axis=1).astype(BF16)


def _spread_operands(col_sum, row_sum, spread, sel_ref, ones_ref):
    z8 = jnp.zeros((8, CHUNK), F32)
    one8 = ones_ref[0:8, :].astype(F32)
    pieces = _split_f32(col_sum, 3) + [one8] * 3
    for x in spread:
        pieces += _split_f32(x, 2)
    assert len(pieces) <= CHUNK // 8
    lhs = jnp.concatenate(pieces + [z8] * (CHUNK // 8 - len(pieces)), axis=0).T.astype(BF16)
    rows = jnp.concatenate([one8] * 3 + _split_f32(row_sum, 3) + [z8] * (CHUNK // 8 - 6), axis=0)
    return lhs, jnp.concatenate([rows.astype(BF16), sel_ref[...]], axis=1)


def _mlstm_steps(qf_ref, vf_ref, ktf_ref, gtf_ref, qb_ref, vb_ref, ktb_ref, gtb_ref, bcol_ref, ones_ref, sel_ref,
                  rsel_ref, hf_ref, hb_ref, cn_ref, m_ref):
    @pl.when(pl.program_id(1) == 0)
    def _():
        cn_ref[...] = jnp.zeros_like(cn_ref)
        m_ref[...] = jnp.zeros_like(m_ref)

    lane = _iota2((CHUNK, LANES), 1)
    feat = _iota2((LANES, CHUNK), 0)
    row8 = _iota2((8, LANES), 0)
    ones_t = ones_ref[...]
    ninf = jnp.float32(-jnp.inf)
    m_all = m_ref[...]
    heads = [None] * 8
    gates = []
    for d in (0, 1):
        q_ref, v_ref, kt_ref = ((qf_ref, vf_ref, ktf_ref), (qb_ref, vb_ref, ktb_ref))[d]
        for p in range(2):
            q_t = q_ref[0, :, 128 * p:128 * p + 128]
            v_t = v_ref[0, :, 128 * p:128 * p + 128]
            kt_p = kt_ref[0, 128 * p:128 * p + 128, :] * (HEAD_W ** -0.5)
            kt_b = kt_p.astype(BF16)
            for half in range(2):
                r = 4 * d + 2 * p + half
                hmask = (lane >= 64) if half else (lane < 64)
                qm = jnp.where(hmask, q_t, 0.0).astype(BF16)
                cn = cn_ref[r]
                heads[r] = dict(
                    cn=cn, kt_p=kt_p,
                    vw=jnp.concatenate([jnp.where(hmask, v_t, 0.0).astype(BF16), ones_t], axis=1),
                    qk=_mm(qm, kt_b),
                    qc=_mm(qm, cn.astype(BF16)))

    yield
    f_cum = []
    for d in (0, 1):
        gt_ref = (gtf_ref, gtb_ref)[d]
        tr = jnp.where(_tri_masks(d)[1], 1.0, 0.0).astype(BF16)
        f_cum.append(_mm_exact_r(_log_sigmoid(gt_ref[0, 8:16, :] + bcol_ref[8:16, :]), tr))

    yield
    for d in (0, 1):
        gt_ref = (gtf_ref, gtb_ref)[d]
        last = CHUNK - 1 if d == 0 else 0
        li = gt_ref[0, 0:8, :] + bcol_ref[0:8, :]
        f = f_cum[d]
        u = li - f
        m_old = m_all
        mx = jnp.maximum(m_old, _cummax_lanes(u, d))
        w_int = jnp.exp(m_old - mx)
        e_mi = jnp.exp(-(f + mx))
        u_max = jnp.max(u, axis=1, keepdims=True)
        e_end = jnp.exp(u - u_max)
        f_last = f[:, last:last + 1]
        b_end = f_last + u_max
        m_new = jnp.maximum(f_last + m_old, b_end)
        a_dec = jnp.exp(f_last + m_old - m_new)
        g_inc = jnp.exp(b_end - m_new)
        m_all = jnp.where((row8 >> 2) == d, m_new, m_all)
        gates.append((e_end, a_dec, g_inc) + _spread_operands(-mx, u, [w_int, e_mi], sel_ref, ones_ref))

    for r in range(8):
        hd = heads[r]
        e_end, a_dec, g_inc, lhs, rhs = gates[r // 4]
        fmask = (feat >= 64) if r % 2 else (feat < 64)
        ke = (jnp.where(fmask, hd["kt_p"], 0.0) * e_end[r:r + 1, :]).astype(BF16)
        a_r = a_dec[r:r + 1, :]
        g_r = g_inc[r:r + 1, :]
        hd["mt"] = _mm(lhs, rhs * rsel_ref[r])
        hd["cn_new"] = (jnp.concatenate([a_r, a_r], axis=1) * hd["cn"]
                        + jnp.concatenate([g_r, g_r], axis=1) * _mm(ke, hd["vw"]))

    yield
    for r in range(8):
        hd = heads[r]
        vis, _ = _tri_masks(r // 4)
        s = hd["qk"] * jnp.exp(jnp.where(vis, hd["mt"][:, 0:128], ninf))
        hd["sv"] = _mm(s.astype(BF16), hd["vw"])

    yield
    for d in (0, 1):
        out_ref = (hf_ref, hb_ref)[d]
        for p in range(2):
            pair = []
            for half in range(2):
                hd = heads[4 * d + 2 * p + half]
                w_b = hd["mt"][:, 128:256]
                tot = hd["sv"] + jnp.concatenate([w_b, w_b], axis=1) * hd["qc"]
                pair.append(tot[:, 0:128] / jnp.maximum(jnp.abs(tot[:, 128:256]), hd["mt"][:, 256:384]))
            out_ref[0, :, 128 * p:128 * p + 128] = jnp.where(lane < 64, pair[0], pair[1]).astype(BF16)
    m_ref[...] = m_all
    for r in range(8):
        cn_ref[r] = heads[r]["cn_new"]


def _mlstm(ml, mkt, gt, bcol, n_ctx_chunks):
    bsz, t, _ = ml.shape
    nc = t // CHUNK
    const2 = lambda b, c: (0, 0)

    def chunk_specs(chunk):
        return [pl.BlockSpec((1, CHUNK, W_BRANCH), lambda b, c: (b, chunk(c), 0)),
                pl.BlockSpec((1, CHUNK, W_BRANCH), lambda b, c: (b, chunk(c), 1)),
                pl.BlockSpec((1, W_BRANCH, CHUNK), lambda b, c: (b, 0, chunk(c))),
                pl.BlockSpec((1, GATE_ROWS, CHUNK), lambda b, c: (b, 0, chunk(c)))]

    fwd_chunk = lambda c: c
    bwd_chunk = lambda c: _bwd_chunk(c, n_ctx_chunks, nc)
    return dict(
        steps=_mlstm_steps,
        in_specs=chunk_specs(fwd_chunk) + chunk_specs(bwd_chunk) + [
            pl.BlockSpec((GATE_ROWS, 1), const2), pl.BlockSpec((CHUNK, LANES), const2),
            pl.BlockSpec((CHUNK, 2 * LANES), const2),
            pl.BlockSpec((8, CHUNK, 3 * LANES), lambda b, c: (0, 0, 0))],
        scratch_shapes=[pltpu.VMEM((8, LANES, 2 * LANES), F32), pltpu.VMEM((8, LANES), F32)],
        args=(ml, ml, mkt, gt, ml, ml, mkt, gt, bcol, jnp.ones((CHUNK, LANES), BF16), _spread_selector(2),
              _state_row_selector(3)))


def _ssd_steps(n_ctx_chunks, n_chunks,
                xf_ref, xfp_ref, xfn_ref, xb_ref, xbp_ref, xbn_ref,
                gtf_ref, gtb_ref, bcol_ref, acol_ref, cw_ref, cb_ref, dskip_ref, ones_ref, sel_ref, rsel_ref,
                yf_ref, yb_ref, s_ref):
    c = pl.program_id(1)

    @pl.when(c == 0)
    def _():
        s_ref[...] = jnp.zeros_like(s_ref)

    lane = _iota2((CHUNK, LANES), 1)
    row768 = _iota2((CHUNK, 768), 0)
    ninf = jnp.float32(-jnp.inf)

    dt, a_cum = [], []
    for d in (0, 1):
        gt_ref = (gtf_ref, gtb_ref)[d]
        tr = jnp.where(_tri_masks(d)[1], 1.0, 0.0).astype(BF16)
        dt.append(_softplus(gt_ref[0, 16:24, :] + bcol_ref[16:24, :]))
        a_cum.append(_mm_exact_r(dt[d] * acol_ref[16:24, :], tr))

    yield
    xa = []
    for d in (0, 1):
        x_ref, xp_ref, xn_ref = ((xf_ref, xfp_ref, xfn_ref), (xb_ref, xbp_ref, xbn_ref))[d]
        j = c if d == 0 else _bwd_chunk(c, n_ctx_chunks, n_chunks)
        seg_first = jnp.logical_or(j == 0, j == n_ctx_chunks)
        seg_last = jnp.logical_or(j == n_ctx_chunks - 1, j == n_chunks - 1)
        x = x_ref[0]
        prev = jnp.where(seg_first, 0.0, xp_ref[0, 7:8, :])
        nxt = jnp.where(seg_last, 0.0, xn_ref[0, 0:1, :])
        x_dn = jnp.where(row768 == 0, prev, pltpu.roll(x, 1, 0))
        x_up = jnp.where(row768 == CHUNK - 1, nxt, pltpu.roll(x, CHUNK - 1, 0))
        xa.append(_silu(x_dn * cw_ref[0:1, :] + x * cw_ref[1:2, :] + x_up * cw_ref[2:3, :] + cb_ref[...]))

    heads = [None] * 8
    groups = {}
    for d in (0, 1):
        for g in range(2):
            x_pair = xa[d][:, 128 * g:128 * g + 128]
            b_g = xa[d][:, 256 + 128 * g:256 + 128 * g + 128]
            c_gb = xa[d][:, 512 + 128 * g:512 + 128 * g + 128].astype(BF16)
            groups[d, g] = dict(x_pair=x_pair, bt=b_g.T.astype(BF16), cb=_mm_nt(c_gb, b_g.astype(BF16)))
            for half in range(2):
                r = 4 * d + 2 * g + half
                hmask = (lane >= 64) if half else (lane < 64)
                st = s_ref[r]
                xm = jnp.where(hmask, x_pair, 0.0)
                heads[r] = dict(st=st, xm=xm, xmb=xm.astype(BF16), cs=_mm(c_gb, st.astype(BF16)))

    yield
    for d in (0, 1):
        last = CHUNK - 1 if d == 0 else 0
        a_last = a_cum[d][:, last:last + 1]
        w_in = jnp.exp(a_last - a_cum[d]) * dt[d]
        e_last = jnp.exp(jnp.broadcast_to(a_last, (8, CHUNK)))
        lhs, rhs = _spread_operands(a_cum[d], -a_cum[d], [jnp.exp(a_cum[d]), w_in], sel_ref, ones_ref)
        for h in range(N_HEADS):
            r = 4 * d + h
            heads[r]["mt"] = _mm(lhs, rhs * rsel_ref[r])
            heads[r]["e_last"] = e_last[r:r + 1, :]

    yield
    for r in range(8):
        d, h = divmod(r, 4)
        hd = heads[r]
        ds = _mm(groups[d, h // 2]["bt"], (hd["xm"] * hd["mt"][:, 256:384]).astype(BF16))
        s_ref[r] = hd["e_last"] * hd["st"] + ds
    for r in range(8):
        d, h = divmod(r, 4)
        hd = heads[r]
        vis, _ = _tri_masks(d)
        sc = groups[d, h // 2]["cb"] * jnp.exp(jnp.where(vis, hd["mt"][:, 0:128], ninf)) * dt[d][r:r + 1, :]
        hd["y"] = _mm(sc.astype(BF16), hd["xmb"]) + hd["mt"][:, 128:256] * hd["cs"]

    for d in (0, 1):
        out_ref = (yf_ref, yb_ref)[d]
        for g in range(2):
            y_pair = heads[4 * d + 2 * g]["y"] + heads[4 * d + 2 * g + 1]["y"]
            if d == 0:
                y_pair = y_pair + dskip_ref[:, 128 * g:128 * g + 128] * groups[d, g]["x_pair"]
            out_ref[0, :, 128 * g:128 * g + 128] = y_pair.astype(BF16)


def _ssd(ss, gt, bcol, acol, conv_w, conv_b, dskip, n_ctx_chunks):
    bsz, t, _ = ss.shape
    nc = t // CHUNK
    sub = CHUNK // 8
    nsub = t // 8
    bc = lambda c: _bwd_chunk(c, n_ctx_chunks, nc)
    fwd = lambda b, c: (b, c, 0)
    bwd = lambda b, c: (b, bc(c), 0)
    fwd_p = lambda b, c: (b, jnp.maximum(c * sub - 1, 0), 0)
    fwd_n = lambda b, c: (b, jnp.minimum((c + 1) * sub, nsub - 1), 0)
    bwd_p = lambda b, c: (b, jnp.maximum(bc(c) * sub - 1, 0), 0)
    bwd_n = lambda b, c: (b, jnp.minimum((bc(c) + 1) * sub, nsub - 1), 0)
    fwd_t = lambda b, c: (b, 0, c)
    bwd_t = lambda b, c: (b, 0, bc(c))
    const2 = lambda b, c: (0, 0)
    return dict(
        steps=functools.partial(_ssd_steps, n_ctx_chunks, nc),
        in_specs=[pl.BlockSpec((1, CHUNK, 768), fwd), pl.BlockSpec((1, 8, 768), fwd_p),
                  pl.BlockSpec((1, 8, 768), fwd_n),
                  pl.BlockSpec((1, CHUNK, 768), bwd), pl.BlockSpec((1, 8, 768), bwd_p),
                  pl.BlockSpec((1, 8, 768), bwd_n),
                  pl.BlockSpec((1, GATE_ROWS, CHUNK), fwd_t), pl.BlockSpec((1, GATE_ROWS, CHUNK), bwd_t),
                  pl.BlockSpec((GATE_ROWS, 1), const2), pl.BlockSpec((GATE_ROWS, 1), const2),
                  pl.BlockSpec((3, 768), const2), pl.BlockSpec((1, 768), const2),
                  pl.BlockSpec((1, W_BRANCH), const2),
                  pl.BlockSpec((CHUNK, LANES), const2), pl.BlockSpec((CHUNK, 2 * LANES), const2),
                  pl.BlockSpec((8, CHUNK, 3 * LANES), lambda b, c: (0, 0, 0))],
        scratch_shapes=[pltpu.VMEM((8, SSM_N, LANES), F32)],
        args=(ss, ss, ss, ss, ss, ss, gt, gt, bcol, acol, conv_w, conv_b, dskip,
              jnp.ones((CHUNK, LANES), BF16), _spread_selector(2), _state_row_selector(3)))


def _hgrn2_steps(hf_ref, hb_ref, loglb_ref, log1m_ref, onem_ref, e_ref, of_ref, ob_ref,
                  st_ref):
    @pl.when(pl.program_id(1) == 0)
    def _():
        st_ref[...] = jnp.zeros_like(st_ref)

    ri = _iota2((CHUNK, CHUNK), 0)
    ci = _iota2((CHUNK, CHUNK), 1)
    lane = _iota2((CHUNK, LANES), 1)
    rw = _iota2((CHUNK, W_BRANCH), 0)
    blockdiag = (ri >> 6) == (ci >> 6)
    ninf = jnp.float32(-jnp.inf)
    nb = CHUNK // SUB
    rs = _iota2((nb, SUB, W_BRANCH), 1)

    dirs = []
    for d in (0, 1):
        h_ref = (hf_ref, hb_ref)[d]
        tc = jnp.where(_tri_masks(d)[0], 1.0, 0.0).astype(BF16)
        z = h_ref[0, :, W_BRANCH * (1 + d):W_BRANCH * (2 + d)]
        e = jnp.exp(-jnp.abs(z))
        ope = 1.0 + e
        la = loglb_ref[d:d + 1, :]
        lb_ = log1m_ref[d:d + 1, :] + (jnp.minimum(z, 0.0) - jnp.log(ope))
        logf = jnp.maximum(la, lb_) + jnp.log(1.0 + jnp.exp(-jnp.abs(la - lb_)))
        dirs.append(dict(
            q=_silu(h_ref[0, :, 0:W_BRANCH]),
            v=h_ref[0, :, 3 * W_BRANCH:4 * W_BRANCH],
            kk=onem_ref[d:d + 1, :] * (jnp.where(z >= 0.0, e, 1.0) / ope),
            gcum=_mm_exact_l(tc, logf)))

    yield
    for d in (0, 1):
        dd = dirs[d]
        q, kk, gcum = dd["q"], dd["kk"], dd["gcum"]
        last = CHUNK - 1 if d == 0 else 0
        g_last = gcum[last:last + 1, :]
        qg = (q * jnp.exp(gcum)).astype(BF16)
        kg = (kk * jnp.exp(g_last - gcum)).astype(BF16)
        vb = dd["v"].astype(BF16)

        dd["o_inter"] = []
        for p in range(2):
            sl = slice(128 * p, 128 * p + 128)
            st = st_ref[2 * d + p]
            dd["o_inter"].append(_mm_nt(qg[:, sl], st.astype(BF16)))
            dst = _mm_tn(vb[:, sl], kg[:, sl])
            st_ref[2 * d + p] = st * jnp.exp(g_last[:, sl]) + jnp.where(blockdiag, dst, 0.0)

        a_mats = [jnp.zeros((CHUNK, CHUNK), F32) for _ in range(N_HEADS)]
        blk = CHUNK // 2
        while blk >= SUB:
            first = (rw & (2 * blk - 1)) < blk
            edge = (blk - 1) if d == 0 else blk
            gb = gcum.reshape(CHUNK // (2 * blk), 2 * blk, W_BRANCH)[:, edge:edge + 1, :]
            gb = jnp.broadcast_to(gb, (CHUNK // (2 * blk), 2 * blk, W_BRANCH)).reshape(CHUNK, W_BRANCH)
            q_side = jnp.logical_not(first) if d == 0 else first
            qt = q * jnp.exp(jnp.where(q_side, gcum - gb, ninf))
            kt = (kk * jnp.exp(jnp.where(q_side, ninf, gb - gcum))).astype(BF16)
            same = (ri >> int(math.log2(2 * blk))) == (ci >> int(math.log2(2 * blk)))
            for h in range(N_HEADS):
                p, half = divmod(h, 2)
                hmask = (lane >= 64) if half else (lane < 64)
                qh = jnp.where(hmask, qt[:, 128 * p:128 * p + 128], 0.0).astype(BF16)
                a_mats[h] = a_mats[h] + jnp.where(same, _mm_nt(qh, kt[:, 128 * p:128 * p + 128]), 0.0)
            blk //= 2
        dd["a_mats"] = a_mats

        g3 = gcum.reshape(nb, SUB, W_BRANCH)
        q3 = q.reshape(nb, SUB, W_BRANCH)
        k3 = kk.reshape(nb, SUB, W_BRANCH)
        v3 = dd["v"].reshape(nb, SUB, W_BRANCH)
        o3 = None
        for j in range(SUB):
            ok = (rs >= j) if d == 0 else (rs <= j)
            pj = q3 * jnp.exp(jnp.where(ok, g3 - g3[:, j:j + 1, :], ninf)) * k3[:, j:j + 1, :]
            red = _mm(pj.reshape(CHUNK, W_BRANCH).astype(BF16), e_ref[...])
            term = red.reshape(nb, SUB, W_BRANCH) * v3[:, j:j + 1, :]
            o3 = term if o3 is None else o3 + term
        dd["o_diag"] = o3.reshape(CHUNK, W_BRANCH)

    yield
    for d in (0, 1):
        dd = dirs[d]
        out_ref = (of_ref, ob_ref)[d]
        for p in range(2):
            sl = slice(128 * p, 128 * p + 128)
            a_cat = jnp.concatenate([dd["a_mats"][2 * p], dd["a_mats"][2 * p + 1]], axis=1).astype(BF16)
            v_p = dd["v"][:, sl]
            v_cat = jnp.concatenate([jnp.where(lane < 64, v_p, 0.0), jnp.where(lane >= 64, v_p, 0.0)],
                                    axis=0).astype(BF16)
            out_ref[0, :, sl] = (dd["o_inter"][p] + _mm(a_cat, v_cat) + dd["o_diag"][:, sl]).astype(BF16)


def _hgrn2(hg, loglb, log1m, onem, e64, n_ctx_chunks):
    bsz, t, _ = hg.shape
    nc = t // CHUNK
    fwd = lambda b, c: (b, c, 0)
    bwd = lambda b, c: (b, _bwd_chunk(c, n_ctx_chunks, nc), 0)
    const2 = lambda b, c: (0, 0)
    return dict(
        steps=_hgrn2_steps,
        in_specs=[pl.BlockSpec((1, CHUNK, 1024), fwd), pl.BlockSpec((1, CHUNK, 1024), bwd),
                  pl.BlockSpec((2, W_BRANCH), const2), pl.BlockSpec((2, W_BRANCH), const2),
                  pl.BlockSpec((2, W_BRANCH), const2), pl.BlockSpec((W_BRANCH, W_BRANCH), const2)],
        scratch_shapes=[pltpu.VMEM((4, LANES, LANES), F32)],
        args=(hg, hg, loglb, log1m, onem, e64))


def _scan_kernel(parts, *refs):
    n_in = [len(p["in_specs"]) for p in parts]
    n_scr = [len(p["scratch_shapes"]) for p in parts]
    ins, (out_f, out_b), scr = refs[:sum(n_in)], refs[sum(n_in):sum(n_in) + 2], refs[sum(n_in) + 2:]
    gens = []
    for k, p in enumerate(parts):
        cols = slice(W_BRANCH * k, W_BRANCH * (k + 1))
        mine = (ins[sum(n_in[:k]):sum(n_in[:k + 1])] + (out_f.at[:, :, cols], out_b.at[:, :, cols])
                + scr[sum(n_scr[:k]):sum(n_scr[:k + 1])])
        gens.append(p["steps"](*mine))
    while gens:
        for g in list(gens):
            if next(g, gens) is gens:
                gens.remove(g)


def _chunk_scans(parts, bsz, n_chunks, n_ctx_chunks):
    width = W_BRANCH * len(parts)
    out = jax.ShapeDtypeStruct((bsz, n_chunks * CHUNK, width), BF16)
    return pl.pallas_call(
        functools.partial(_scan_kernel, parts),
        grid=(bsz, n_chunks),
        in_specs=[s for p in parts for s in p["in_specs"]],
        out_specs=[pl.BlockSpec((1, CHUNK, width), lambda b, c: (b, c, 0)),
                   pl.BlockSpec((1, CHUNK, width), lambda b, c: (b, _bwd_chunk(c, n_ctx_chunks, n_chunks), 0))],
        out_shape=[out, out],
        scratch_shapes=[s for p in parts for s in p["scratch_shapes"]],
        compiler_params=_cparams(("parallel", "arbitrary")),
        name="chunk_scans",
    )(*[a for p in parts for a in p["args"]])


def _attn(daq, kt, vx, lam_vecs, lam_init, n_ctx_tiles, skip_ctx):
    bsz, t, _ = daq.shape
    tq = ROW_TILE
    n_k_tiles = t // KEY_TILE
    n_lat_tiles = n_k_tiles - n_ctx_tiles
    group = math.gcd(KEY_GROUP, n_lat_tiles)
    q_off = n_ctx_tiles if skip_ctx else 0
    nq = t // tq - q_off
    n_maps = W_BRANCH // DA_QK
    rows = n_maps * tq

    def kern(q_ref, kt_ref, vx_ref, lv_ref, o_ref, q8_ref, m_ref, acc_ref, s_ref, mx_ref):
        qi = pl.program_id(1) + q_off
        lane = _iota2((tq, W_BRANCH), 1)
        q = q_ref[0]
        for j in range(n_maps):
            q8_ref[j * tq:(j + 1) * tq, :] = jnp.where((lane >> 5) == j, q, jnp.zeros_like(q))

        def score_tile(t):
            return _mm(q8_ref[...], kt_ref[0, t])
        m_ref[...] = jnp.full_like(m_ref, -jnp.inf)
        acc_ref[...] = jnp.zeros_like(acc_ref)

        def update(t0, n, after_scores=None):
            s = [score_tile(t0 + i) for i in range(n)]
            if after_scores is not None:
                after_scores()
            mx = s[0][:, 0:LANES]
            for i in range(n):
                for c in range(KEY_TILE // LANES):
                    if i or c:
                        mx = jnp.maximum(mx, s[i][:, LANES * c:LANES * (c + 1)])
            m_old = m_ref[...]
            m_new = jnp.maximum(m_old, jnp.max(mx, axis=1, keepdims=True))
            alpha = jnp.exp(m_old - m_new)
            m_ref[...] = m_new
            m2 = jnp.concatenate([m_new] * (KEY_TILE // LANES), axis=1)
            p = [jnp.exp(s[i] - m2).astype(BF16) for i in range(n)]
            for hd in range(N_HEADS):
                r0 = 2 * hd * tq
                pv = _mm(p[0][r0:r0 + 2 * tq], vx_ref[0, t0, hd])
                for i in range(1, n):
                    pv = pv + _mm(p[i][r0:r0 + 2 * tq], vx_ref[0, t0 + i, hd])
                acc_ref[r0:r0 + 2 * tq, :] = alpha[r0:r0 + 2 * tq] * acc_ref[r0:r0 + 2 * tq, :] + pv

        def scores(g, slot):
            t0 = n_ctx_tiles + g * group
            mx = None
            for i in range(group):
                s = score_tile(t0 + i)
                s_ref[slot, :, KEY_TILE * i:KEY_TILE * (i + 1)] = s
                for c in range(KEY_TILE // LANES):
                    part = s[:, LANES * c:LANES * (c + 1)]
                    mx = part if mx is None else jnp.maximum(mx, part)
            mx_ref[slot] = mx

        def absorb(g, slot):
            t0 = n_ctx_tiles + g * group
            m_old = m_ref[...]
            m_new = jnp.maximum(m_old, jnp.max(mx_ref[slot], axis=1, keepdims=True))
            alpha = jnp.exp(m_old - m_new)
            m_ref[...] = m_new
            m2 = jnp.concatenate([m_new] * (KEY_TILE // LANES), axis=1)
            p = [jnp.exp(s_ref[slot, :, KEY_TILE * i:KEY_TILE * (i + 1)] - m2).astype(BF16) for i in range(group)]
            for hd in range(N_HEADS):
                r0 = 2 * hd * tq
                pv = _mm(p[0][r0:r0 + 2 * tq], vx_ref[0, t0, hd])
                for i in range(1, group):
                    pv = pv + _mm(p[i][r0:r0 + 2 * tq], vx_ref[0, t0 + i, hd])
                acc_ref[r0:r0 + 2 * tq, :] = alpha[r0:r0 + 2 * tq] * acc_ref[r0:r0 + 2 * tq, :] + pv

        @pl.when(qi < n_ctx_tiles)
        def _():
            update(0, n_ctx_tiles)

        @pl.when(qi >= n_ctx_tiles)
        def _():
            n_groups = n_lat_tiles // group
            n_pairs = (n_groups - 1) // 2
            update(0, n_ctx_tiles, after_scores=lambda: scores(0, 0))

            def body(k, carry):
                g = 2 * k
                scores(g + 1, 1)
                absorb(g, 0)
                scores(g + 2, 0)
                absorb(g + 1, 1)
                return carry
            for k in range(n_pairs):
                body(k, 0)
            g0 = 2 * n_pairs
            if n_groups - g0 == 2:
                scores(g0 + 1, 1)
            absorb(g0, 0)
            if n_groups - g0 == 2:
                absorb(g0 + 1, 1)

        lv = lv_ref[...]
        lam = (jnp.exp(jnp.sum(lv[0:1, :] * lv[1:2, :], axis=1, keepdims=True))
               - jnp.exp(jnp.sum(lv[2:3, :] * lv[3:4, :], axis=1, keepdims=True)) + lam_init)
        lane1 = _iota2((tq, LANES), 1)
        for pr in range(N_HEADS // 2):
            halves = []
            for half in range(2):
                r0 = 2 * (2 * pr + half) * tq
                one = 0 if half else HEAD_W
                a0 = acc_ref[r0:r0 + tq, :]
                a1 = acc_ref[r0 + tq:r0 + 2 * tq, :]
                halves.append(a0 / a0[:, one:one + 1] - lam * (a1 / a1[:, one:one + 1]))
            o_ref[0, :, LANES * pr:LANES * (pr + 1)] = jnp.where(lane1 < HEAD_W, halves[0], halves[1]).astype(BF16)

    return pl.pallas_call(
        kern,
        grid=(bsz, nq),
        in_specs=[pl.BlockSpec((1, tq, W_BRANCH), lambda b, i: (b, i + q_off, 0)),
                  pl.BlockSpec((1, n_k_tiles, W_BRANCH, KEY_TILE), lambda b, i: (b, 0, 0, 0),
                               pipeline_mode=pl.Buffered(1)),
                  pl.BlockSpec((1, n_k_tiles, N_HEADS, KEY_TILE, LANES), lambda b, i: (b, 0, 0, 0, 0),
                               pipeline_mode=pl.Buffered(1)),
                  pl.BlockSpec(lam_vecs.shape, lambda b, i: (0, 0))],
        out_specs=pl.BlockSpec((1, tq, W_BRANCH), lambda b, i: (b, i, 0)),
        out_shape=jax.ShapeDtypeStruct((bsz, nq * tq, W_BRANCH), BF16),
        scratch_shapes=[pltpu.VMEM((rows, W_BRANCH), BF16),
                        pltpu.VMEM((rows, LANES), F32),
                        pltpu.VMEM((rows, LANES), F32),
                        pltpu.VMEM((2, rows, group * KEY_TILE), F32),
                        pltpu.VMEM((2, rows, LANES), F32)],
        compiler_params=_cparams(("parallel", "parallel")),
        name="diff_attn",
    )(daq, kt, vx, lam_vecs)


def _seg_mean(x, e_ref):
    x0 = x.astype(BF16)
    x1 = (x - x0.astype(F32)).astype(BF16)
    return (_mm(x0, e_ref[...]) + _mm(x1, e_ref[...])) * (1.0 / HEAD_W)


def _out_kernel(n_ctx_tiles, ctx_row, q_off, lam_init, final,
                c_ref, x_ref, mo_ref, zg_ref, yf_ref, yb_ref, dao_ref,
                mlg_ref, dag_ref, ssg_ref, hgg_ref, e_ref, w_ref, fg_ref, o_ref):
    b = pl.program_id(0)
    i = pl.program_id(1) + q_off
    d = x_ref.shape[2]
    r = jnp.where(i < n_ctx_tiles, ctx_row, b)
    gate = mo_ref[pl.ds(r, 1), :][:, 2 * d:3 * d]

    def gate_block(k):
        return zg_ref[0, :, W_BRANCH * k:W_BRANCH * (k + 1)].astype(F32)

    def scan_sum(k):
        cols = slice(W_BRANCH * k, W_BRANCH * (k + 1))
        return yf_ref[0, :, cols].astype(F32) + yb_ref[0, :, cols].astype(F32)

    u = _sigmoid(gate_block(0)) * scan_sum(0)
    dev = u - _seg_mean(u, e_ref)
    y_ml = dev * lax.rsqrt(_seg_mean(dev * dev, e_ref) + EPS) * mlg_ref[...] * _silu(gate_block(1))

    o = dao_ref[0].astype(F32)
    y_da = (o * lax.rsqrt(_seg_mean(o * o, e_ref) + EPS) * dag_ref[...]) * (1.0 - lam_init) * _silu(gate_block(2))

    ys = scan_sum(1) * _silu(gate_block(3))
    parts = []
    for g in range(2):
        yg = ys[:, 128 * g:128 * g + 128]
        parts.append(yg * lax.rsqrt(jnp.mean(yg * yg, axis=-1, keepdims=True) + EPS))
    y_ss = jnp.concatenate(parts, axis=1) * ssg_ref[...]

    oh = scan_sum(2)
    y_hg = (oh * lax.rsqrt(_seg_mean(oh * oh, e_ref) + EPS) * hgg_ref[...]) * _silu(gate_block(4))

    acc = _mm(y_ml.astype(BF16), w_ref[0:W_BRANCH, :])
    acc = acc + _mm(y_da.astype(BF16), w_ref[W_BRANCH:2 * W_BRANCH, :])
    acc = acc + _mm(y_ss.astype(BF16), w_ref[2 * W_BRANCH:3 * W_BRANCH, :])
    acc = acc + _mm(y_hg.astype(BF16), w_ref[3 * W_BRANCH:4 * W_BRANCH, :])
    x_new = jnp.where(i < n_ctx_tiles, c_ref[0], x_ref[0]) + gate * acc
    if final:
        x_new = x_new * lax.rsqrt(jnp.mean(x_new * x_new, axis=-1, keepdims=True) + EPS) * fg_ref[...]
    o_ref[0] = x_new


def _out_proj(stream, mo, zg, yf, yb, dao,
              mlg, dag, ssg, hgg, e64, w_out, final_g, lam_init, n_ctx_tiles, ctx_row, final):
    c_src, x_src, x_off, t = stream
    bsz, _, d = x_src.shape
    tm = ROW_TILE
    q_off = n_ctx_tiles if final else 0
    nrow = t // tm - q_off
    row = lambda b, i: (b, i + q_off, 0)
    const2 = lambda b, i: (0, 0)
    wb = pl.BlockSpec((1, tm, W_BRANCH), row)
    return pl.pallas_call(
        functools.partial(_out_kernel, n_ctx_tiles, ctx_row, q_off, lam_init, final),
        grid=(bsz, nrow),
        in_specs=_stream_specs(stream, n_ctx_tiles, q_off) + [
                  pl.BlockSpec(mo.shape, const2),
                  pl.BlockSpec((1, tm, zg.shape[2]), row),
                  pl.BlockSpec((1, tm, yf.shape[2]), row), pl.BlockSpec((1, tm, yb.shape[2]), row),
                  pl.BlockSpec((1, tm, W_BRANCH), lambda b, i: (b, i, 0)) if final else wb,
                  pl.BlockSpec((1, W_BRANCH), const2), pl.BlockSpec((1, W_BRANCH), const2),
                  pl.BlockSpec((1, W_BRANCH), const2), pl.BlockSpec((1, W_BRANCH), const2),
                  pl.BlockSpec((W_BRANCH, W_BRANCH), const2),
                  pl.BlockSpec(w_out.shape, const2),
                  pl.BlockSpec((1, d), const2)],
        out_specs=pl.BlockSpec((1, tm, d), lambda b, i: (b, i, 0)),
        out_shape=jax.ShapeDtypeStruct((bsz, nrow * tm, d), F32),
        compiler_params=_cparams(("parallel", "parallel")),
        name="out_proj",
    )(c_src, x_src, mo, zg, yf, yb, dao,
      mlg, dag, ssg, hgg, e64, w_out, final_g.reshape(1, d))


def _relayout_w_in(w_in):
    d = w_in.shape[0]
    o = {}
    off = 0
    for name, n in (("ml_q", 256), ("ml_k", 256), ("ml_v", 256), ("ml_o", 256), ("ml_i", 8), ("ml_f", 8),
                    ("ml_z", 256), ("da_q", 256), ("da_k", 256), ("da_v", 256), ("da_z", 256),
                    ("ss_xbc", 768), ("ss_dt", 8), ("ss_z", 256),
                    ("hg_q", 256), ("hg_f", 512), ("hg_i", 256), ("hg_z", 256)):
        o[name] = w_in[:, off:off + n]
        off += n

    gates = jnp.concatenate([o["ml_i"], o["ml_f"], o["ss_dt"], jnp.zeros((d, GATE_ROWS - 24), w_in.dtype)], axis=1)
    w2 = jnp.concatenate([o["ml_q"], o["ml_v"], o["da_q"], o["da_v"],
                          o["ss_xbc"], o["hg_q"], o["hg_f"], o["hg_i"],
                          o["ml_o"], o["ml_z"], o["da_z"], o["ss_z"], o["hg_z"]], axis=1)
    wt = jnp.concatenate([o["da_k"], gates, o["ml_k"]], axis=1).T
    return w2.astype(BF16), wt.astype(BF16)


def _rope_tables(n_ctx, seq):
    pos = jnp.arange(seq)
    rows = (pos // GRID_W).astype(F32)
    cols = (pos % GRID_W).astype(F32)
    axis = DA_QK // 2
    inv = ROPE_BASE ** (-jnp.arange(0, axis, 2, dtype=F32) / axis)
    ang = jnp.concatenate([rows[:, None] * inv, cols[:, None] * inv], axis=-1)
    cos = jnp.concatenate([jnp.ones((n_ctx, axis), F32), jnp.cos(ang)], axis=0)
    sin = jnp.concatenate([jnp.zeros((n_ctx, axis), F32), jnp.sin(ang)], axis=0)
    cos_t = jnp.concatenate([jnp.ones((axis, n_ctx), F32), jnp.cos(ang.T)], axis=1)
    sin_t = jnp.concatenate([jnp.zeros((axis, n_ctx), F32), jnp.sin(ang.T)], axis=1)
    return (jnp.tile(cos, (1, LANES // axis)), jnp.tile(sin, (1, LANES // axis)),
            jnp.tile(cos_t, (2, 1)), jnp.tile(sin_t, (2, 1)))


def _gate_col(vals):
    v = jnp.concatenate([vals.astype(F32), jnp.zeros((GATE_ROWS - vals.shape[0],), F32)])
    return v.reshape(GATE_ROWS, 1)


def kernel(x, c, ctx, c_ctx, w_mod, b_mod, norm_g, w_in, w_out, ml_gate_b, ml_norm_g, da_lambda, da_norm_g,
           ss_conv_w, ss_conv_b, ss_dt_bias, ss_a_log, ss_d, ss_norm_g, hg_lower, hg_norm_g, final_g):
    bsz, seq, d = x.shape
    n_ctx = ctx.shape[1]
    depth = w_mod.shape[0]
    assert n_ctx % ROW_TILE == 0 and seq % ROW_TILE == 0 and bsz < 8
    n_ctx_tiles = n_ctx // ROW_TILE
    n_ctx_chunks = n_ctx // CHUNK
    ctx_row = bsz

    stream = (ctx, x, 0, n_ctx + seq)
    cc = jnp.concatenate([c, c_ctx[None, :], jnp.zeros((8 - bsz - 1, d), F32)], axis=0)
    cos, sin, cos_t, sin_t = _rope_tables(n_ctx, seq)
    lb_all = jnp.cumsum(jax.nn.softmax(hg_lower.astype(F32), axis=1), axis=1)
    lb_all = lb_all - lb_all[:, :1]
    hid = _iota2((W_BRANCH, W_BRANCH), 0) // HEAD_W
    e64 = (hid == hid.T).astype(BF16)

    out = None
    for l in range(depth):
        lam_init = 0.8 - 0.6 * math.exp(-0.3 * l)
        final = l == depth - 1
        w2, wt = _relayout_w_in(w_in[l])
        mo = _modulation(cc, w_mod[l], b_mod[l])
        ml, daq, dak, dav, ss, hg, zg, gt, mkt = _in_proj(stream, mo, norm_g[l], cos, sin, cos_t, sin_t, w2, wt,
                                                     n_ctx_tiles, ctx_row)

        gb = ml_gate_b[l]
        zeros8 = jnp.zeros((8,), F32)
        bcol = _gate_col(jnp.concatenate([gb[:, 0].reshape(-1), gb[:, 1].reshape(-1),
                                                ss_dt_bias[l].reshape(-1)]))
        acol = _gate_col(jnp.concatenate([zeros8, zeros8, -jnp.exp(ss_a_log[l].astype(F32)).reshape(-1)]))
        dao = _attn(daq, dak, dav, da_lambda[l].astype(F32), lam_init, n_ctx_tiles, final)
        dskip = jnp.repeat(ss_d[l].astype(F32), HEAD_W).reshape(1, W_BRANCH)
        lbh = lb_all[:, l]
        scans = [_mlstm(ml, mkt, gt, bcol, n_ctx_chunks),
                 _ssd(ss, gt, bcol, acol, ss_conv_w[l], ss_conv_b[l].reshape(1, -1), dskip, n_ctx_chunks),
                 _hgrn2(hg, jnp.log(lbh), jnp.log1p(-lbh), 1.0 - lbh, e64, n_ctx_chunks)]
        yf, yb = _chunk_scans(scans, bsz, (n_ctx + seq) // CHUNK, n_ctx_chunks)

        res = _out_proj(stream, mo, zg, yf, yb, dao,
                        ml_norm_g[l].reshape(1, -1), jnp.tile(da_norm_g[l], N_HEADS).reshape(1, -1),
                        ss_norm_g[l].reshape(1, -1), hg_norm_g[l].reshape(1, -1), e64,
                        w_out[l].astype(BF16), final_g, lam_init, n_ctx_tiles, ctx_row, final)
        if final:
            out = res
        else:
            stream = (res, res, n_ctx_tiles, n_ctx + seq)
    return out
```

```python
import functools
import math

import jax
import jax.numpy as jnp
from jax import lax
from jax.experimental import pallas as pl
from jax.experimental.pallas import tpu as pltpu

F32 = jnp.float32
BF16 = jnp.bfloat16

EPS = 1e-6
GRID_W = 64
ROPE_BASE = 10000.0
N_HEADS = 4
HEAD_W = 64
W_BRANCH = N_HEADS * HEAD_W
DA_QK = 32
SSM_N = 128
CHUNK = 128
ROW_TILE = 256
SUB = 8
LANES = 128
VMEM_LIMIT = 56 * 1024 * 1024

C_ML = 0
C_DAQ = 512
C_DAV = 768
C_SS = 1024
C_HG = 1792
C_ZG = 2816
C_END = 4096
GATE_ROWS = 32
KEY_TILE = 256
KEY_GROUP = 4

NT = (((1,), (1,)), ((), ()))
TN = (((0,), (0,)), ((), ()))


def _mm(a, b):
    return jnp.dot(a, b, preferred_element_type=F32)


def _mm_nt(a, b):
    return lax.dot_general(a, b, NT, preferred_element_type=F32)


def _mm_tn(a, b):
    return lax.dot_general(a, b, TN, preferred_element_type=F32)


def _split3(x):
    x0 = x.astype(BF16)
    r = x - x0.astype(F32)
    x1 = r.astype(BF16)
    x2 = (r - x1.astype(F32)).astype(BF16)
    return x0, x1, x2


def _mm_exact_l(t, x):
    x0, x1, x2 = _split3(x)
    return _mm(t, x0) + _mm(t, x1) + _mm(t, x2)


def _mm_exact_r(x, t):
    x0, x1, x2 = _split3(x)
    return _mm(x0, t) + _mm(x1, t) + _mm(x2, t)


def _sigmoid(x):
    return 1.0 / (1.0 + jnp.exp(-x))


def _silu(x):
    return x * _sigmoid(x)


def _log_sigmoid(x):
    return jnp.minimum(x, 0.0) - jnp.log(1.0 + jnp.exp(-jnp.abs(x)))


def _softplus(x):
    return jnp.maximum(x, 0.0) + jnp.log(1.0 + jnp.exp(-jnp.abs(x)))


def _iota2(shape, axis):
    return lax.broadcasted_iota(jnp.int32, shape, axis)


def _cparams(sem):
    return pltpu.CompilerParams(dimension_semantics=sem, vmem_limit_bytes=VMEM_LIMIT)


def _mod_kernel(c_ref, w_ref, b_ref, o_ref):
    c = c_ref[...]
    o_ref[...] = jnp.dot(_silu(c), w_ref[...], precision=lax.Precision.HIGHEST,
                         preferred_element_type=F32) + b_ref[...]


def _modulation(cc, w_mod, b_mod):
    d = cc.shape[1]
    n = w_mod.shape[1]
    tn = 1536
    assert n % tn == 0
    return pl.pallas_call(
        _mod_kernel,
        grid=(n // tn,),
        in_specs=[pl.BlockSpec((8, d), lambda j: (0, 0)),
                  pl.BlockSpec((d, tn), lambda j: (0, j)),
                  pl.BlockSpec((1, tn), lambda j: (0, j))],
        out_specs=pl.BlockSpec((8, tn), lambda j: (0, j)),
        out_shape=jax.ShapeDtypeStruct((8, n), F32),
        compiler_params=_cparams(("arbitrary",)),
        name="modulation",
    )(cc, w_mod, b_mod.reshape(1, n))


def _stream_specs(stream, n_ctx_tiles, first_tile):
    _, x_src, x_off, _ = stream
    d = x_src.shape[2]
    return [pl.BlockSpec((1, ROW_TILE, d), lambda b, i: (b, jnp.minimum(i + first_tile, n_ctx_tiles - 1), 0)),
            pl.BlockSpec((1, ROW_TILE, d), lambda b, i: (b, jnp.maximum(i + first_tile - n_ctx_tiles, 0) + x_off, 0))]


def _in_kernel(n_ctx_tiles, ctx_row, c_ref, x_ref, mo_ref, g_ref, cos_ref, sin_ref, cost_ref, sint_ref,
               w_ref, wt_ref,
               ml_ref, daq_ref, kt_ref, vx_ref, ss_ref, hg_ref, zg_ref, gt_ref, mkt_ref):
    b = pl.program_id(0)
    i = pl.program_id(1)
    d = x_ref.shape[2]
    r = jnp.where(i < n_ctx_tiles, ctx_row, b)
    mo = mo_ref[pl.ds(r, 1), :]
    shift = mo[:, 0:d]
    scale = mo[:, d:2 * d]
    x = jnp.where(i < n_ctx_tiles, c_ref[0], x_ref[0])
    y = x * lax.rsqrt(jnp.mean(x * x, axis=-1, keepdims=True) + EPS) * g_ref[...]
    h = (y * (1.0 + scale) + shift).astype(BF16)

    ml_ref[0] = _mm(h, w_ref[:, C_ML:C_DAQ])
    q = _mm(h, w_ref[:, C_DAQ:C_DAV])
    half = DA_QK // 2
    lane = _iota2((x.shape[0], LANES), 1)
    q_rot = []
    for c in range(W_BRANCH // LANES):
        qc = q[:, LANES * c:LANES * (c + 1)]
        q_rot.append(jnp.where((lane & half) == 0, -pltpu.roll(qc, LANES - half, 1), pltpu.roll(qc, half, 1)))
    n_rep = W_BRANCH // LANES
    q = (q * jnp.concatenate([cos_ref[...]] * n_rep, axis=1)
         + jnp.concatenate(q_rot, axis=1) * jnp.concatenate([sin_ref[...]] * n_rep, axis=1))
    daq_ref[0] = (q * (DA_QK ** -0.5)).astype(BF16)
    v = _mm(h, w_ref[:, C_DAV:C_SS])
    for hd in range(N_HEADS):
        pair = v[:, LANES * (hd // 2):LANES * (hd // 2 + 1)]
        own = (lane >= HEAD_W) if hd % 2 else (lane < HEAD_W)
        vx_ref[0, 0, hd] = jnp.where(own, pair, jnp.where(lane == (0 if hd % 2 else HEAD_W), 1.0, 0.0)).astype(BF16)
    ss_ref[0] = _mm(h, w_ref[:, C_SS:C_HG])
    hg_ref[0] = _mm(h, w_ref[:, C_HG:C_ZG])
    zg_ref[0] = _mm(h, w_ref[:, C_ZG:C_END]).astype(BF16)
    tr = _mm_nt(wt_ref[...], h)
    kt = tr[0:W_BRANCH]
    kt_rot = []
    for j in range(W_BRANCH // DA_QK):
        kt_rot += [-kt[DA_QK * j + half:DA_QK * (j + 1)], kt[DA_QK * j:DA_QK * j + half]]
    n_grp = W_BRANCH // DA_QK
    kt_ref[0, 0] = (kt * jnp.concatenate([cost_ref[...]] * n_grp, axis=0)
                    + jnp.concatenate(kt_rot, axis=0) * jnp.concatenate([sint_ref[...]] * n_grp, axis=0)).astype(BF16)
    gt_ref[0] = tr[W_BRANCH:W_BRANCH + GATE_ROWS]
    mkt_ref[0] = tr[W_BRANCH + GATE_ROWS:2 * W_BRANCH + GATE_ROWS]


def _in_proj(stream, mo, g, cos, sin, cos_t, sin_t, w2, wt, n_ctx_tiles, ctx_row):
    c_src, x_src, x_off, t = stream
    bsz, _, d = x_src.shape
    tm = ROW_TILE
    assert tm == KEY_TILE
    row = lambda b, i: (b, i, 0)
    const2 = lambda b, i: (0, 0)
    out_shape = [
        jax.ShapeDtypeStruct((bsz, t, 2 * W_BRANCH), F32),
        jax.ShapeDtypeStruct((bsz, t, W_BRANCH), BF16),
        jax.ShapeDtypeStruct((bsz, t // tm, W_BRANCH, tm), BF16),
        jax.ShapeDtypeStruct((bsz, t // tm, N_HEADS, tm, LANES), BF16),
        jax.ShapeDtypeStruct((bsz, t, 768), F32),
        jax.ShapeDtypeStruct((bsz, t, 4 * W_BRANCH), F32),
        jax.ShapeDtypeStruct((bsz, t, 5 * W_BRANCH), BF16),
        jax.ShapeDtypeStruct((bsz, GATE_ROWS, t), F32),
        jax.ShapeDtypeStruct((bsz, W_BRANCH, t), F32),
    ]
    out_specs = [
        pl.BlockSpec((1, tm, 2 * W_BRANCH), row),
        pl.BlockSpec((1, tm, W_BRANCH), row),
        pl.BlockSpec((1, 1, W_BRANCH, tm), lambda b, i: (b, i, 0, 0)),
        pl.BlockSpec((1, 1, N_HEADS, tm, LANES), lambda b, i: (b, i, 0, 0, 0)),
        pl.BlockSpec((1, tm, 768), row),
        pl.BlockSpec((1, tm, 4 * W_BRANCH), row),
        pl.BlockSpec((1, tm, 5 * W_BRANCH), row),
        pl.BlockSpec((1, GATE_ROWS, tm), lambda b, i: (b, 0, i)),
        pl.BlockSpec((1, W_BRANCH, tm), lambda b, i: (b, 0, i)),
    ]
    return pl.pallas_call(
        functools.partial(_in_kernel, n_ctx_tiles, ctx_row),
        grid=(bsz, t // tm),
        in_specs=_stream_specs(stream, n_ctx_tiles, 0) + [
                  pl.BlockSpec(mo.shape, const2),
                  pl.BlockSpec((1, d), const2),
                  pl.BlockSpec((tm, LANES), lambda b, i: (i, 0)),
                  pl.BlockSpec((tm, LANES), lambda b, i: (i, 0)),
                  pl.BlockSpec((DA_QK, tm), lambda b, i: (0, i)),
                  pl.BlockSpec((DA_QK, tm), lambda b, i: (0, i)),
                  pl.BlockSpec(w2.shape, const2),
                  pl.BlockSpec(wt.shape, const2)],
        out_specs=out_specs,
        out_shape=out_shape,
        compiler_params=_cparams(("parallel", "parallel")),
        name="in_proj",
    )(c_src, x_src, mo, g.reshape(1, d), cos, sin, cos_t, sin_t, w2, wt)


def _bwd_chunk(c, n_ctx_chunks, n_chunks):
    return jnp.where(c < n_ctx_chunks, n_ctx_chunks - 1 - c, n_chunks + n_ctx_chunks - 1 - c)


def _tri_masks(d):
    ri = _iota2((CHUNK, CHUNK), 0)
    ci = _iota2((CHUNK, CHUNK), 1)
    if d == 0:
        return ci <= ri, ri <= ci
    return ci >= ri, ri >= ci


def _split_f32(x, n):
    parts = []
    for _ in range(n):
        p = x.astype(BF16).astype(F32)
        parts.append(p)
        x = x - p
    return parts


def _cummax_lanes(u, d):
    lane = _iota2(u.shape, 1)
    ninf = jnp.float32(-jnp.inf)
    s = 1
    while s < CHUNK:
        if d == 0:
            sh = jnp.where(lane >= s, pltpu.roll(u, s, 1), ninf)
        else:
            sh = jnp.where(lane < CHUNK - s, pltpu.roll(u, CHUNK - s, 1), ninf)
        u = jnp.maximum(u, sh)
        s *= 2
    return u


def _state_row_selector(n_tiles):
    keep = (jnp.arange(CHUNK)[None, :] % 8) == jnp.arange(8)[:, None]
    return jnp.broadcast_to(keep[:, :, None], (8, CHUNK, n_tiles * LANES)).astype(BF16)


def _spread_selector(n_spread):
    t = jnp.arange(CHUNK)[:, None] // 16
    tiles = [jnp.broadcast_to(t == 3 + k, (CHUNK, LANES)) for k in range(n_spread)]
    return jnp.concatenate(tiles, axis=1).astype(BF16)


def _spread_operands(col_sum, row_sum, spread, sel_ref, ones_ref):
    z8 = jnp.zeros((8, CHUNK), F32)
    one8 = ones_ref[0:8, :].astype(F32)
    pieces = _split_f32(col_sum, 3) + [one8] * 3
    for x in spread:
        pieces += _split_f32(x, 2)
    assert len(pieces) <= CHUNK // 8
    lhs = jnp.concatenate(pieces + [z8] * (CHUNK // 8 - len(pieces)), axis=0).T.astype(BF16)
    rows = jnp.concatenate([one8] * 3 + _split_f32(row_sum, 3) + [z8] * (CHUNK // 8 - 6), axis=0)
    return lhs, jnp.concatenate([rows.astype(BF16), sel_ref[...]], axis=1)


def _mlstm_steps(qf_ref, vf_ref, ktf_ref, gtf_ref, qb_ref, vb_ref, ktb_ref, gtb_ref, bcol_ref, ones_ref, sel_ref,
                  rsel_ref, hf_ref, hb_ref, cn_ref, m_ref):
    @pl.when(pl.program_id(1) == 0)
    def _():
        cn_ref[...] = jnp.zeros_like(cn_ref)
        m_ref[...] = jnp.zeros_like(m_ref)

    lane = _iota2((CHUNK, LANES), 1)
    feat = _iota2((LANES, CHUNK), 0)
    row8 = _iota2((8, LANES), 0)
    ones_t = ones_ref[...]
    ninf = jnp.float32(-jnp.inf)
    m_all = m_ref[...]
    heads = [None] * 8
    gates = []
    for d in (0, 1):
        q_ref, v_ref, kt_ref = ((qf_ref, vf_ref, ktf_ref), (qb_ref, vb_ref, ktb_ref))[d]
        for p in range(2):
            q_t = q_ref[0, :, 128 * p:128 * p + 128]
            v_t = v_ref[0, :, 128 * p:128 * p + 128]
            kt_p = kt_ref[0, 128 * p:128 * p + 128, :] * (HEAD_W ** -0.5)
            kt_b = kt_p.astype(BF16)
            for half in range(2):
                r = 4 * d + 2 * p + half
                hmask = (lane >= 64) if half else (lane < 64)
                qm = jnp.where(hmask, q_t, 0.0).astype(BF16)
                cn = cn_ref[r]
                heads[r] = dict(
                    cn=cn, kt_p=kt_p,
                    vw=jnp.concatenate([jnp.where(hmask, v_t, 0.0).astype(BF16), ones_t], axis=1),
                    qk=_mm(qm, kt_b),
                    qc=_mm(qm, cn.astype(BF16)))

    yield
    f_cum = []
    for d in (0, 1):
        gt_ref = (gtf_ref, gtb_ref)[d]
        tr = jnp.where(_tri_masks(d)[1], 1.0, 0.0).astype(BF16)
        f_cum.append(_mm_exact_r(_log_sigmoid(gt_ref[0, 8:16, :] + bcol_ref[8:16, :]), tr))

    yield
    for d in (0, 1):
        gt_ref = (gtf_ref, gtb_ref)[d]
        last = CHUNK - 1 if d == 0 else 0
        li = gt_ref[0, 0:8, :] + bcol_ref[0:8, :]
        f = f_cum[d]
        u = li - f
        m_old = m_all
        mx = jnp.maximum(m_old, _cummax_lanes(u, d))
        w_int = jnp.exp(m_old - mx)
        e_mi = jnp.exp(-(f + mx))
        u_max = jnp.max(u, axis=1, keepdims=True)
        e_end = jnp.exp(u - u_max)
        f_last = f[:, last:last + 1]
        b_end = f_last + u_max
        m_new = jnp.maximum(f_last + m_old, b_end)
        a_dec = jnp.exp(f_last + m_old - m_new)
        g_inc = jnp.exp(b_end - m_new)
        m_all = jnp.where((row8 >> 2) == d, m_new, m_all)
        gates.append((e_end, a_dec, g_inc) + _spread_operands(-mx, u, [w_int, e_mi], sel_ref, ones_ref))

    for r in range(8):
        hd = heads[r]
        e_end, a_dec, g_inc, lhs, rhs = gates[r // 4]
        fmask = (feat >= 64) if r % 2 else (feat < 64)
        ke = (jnp.where(fmask, hd["kt_p"], 0.0) * e_end[r:r + 1, :]).astype(BF16)
        a_r = a_dec[r:r + 1, :]
        g_r = g_inc[r:r + 1, :]
        hd["mt"] = _mm(lhs, rhs * rsel_ref[r])
        hd["cn_new"] = (jnp.concatenate([a_r, a_r], axis=1) * hd["cn"]
                        + jnp.concatenate([g_r, g_r], axis=1) * _mm(ke, hd["vw"]))

    yield
    for r in range(8):
        hd = heads[r]
        vis, _ = _tri_masks(r // 4)
        s = hd["qk"] * jnp.exp(jnp.where(vis, hd["mt"][:, 0:128], ninf))
        hd["sv"] = _mm(s.astype(BF16), hd["vw"])

    yield
    for d in (0, 1):
        out_ref = (hf_ref, hb_ref)[d]
        for p in range(2):
            pair = []
            for half in range(2):
                hd = heads[4 * d + 2 * p + half]
                w_b = hd["mt"][:, 128:256]
                tot = hd["sv"] + jnp.concatenate([w_b, w_b], axis=1) * hd["qc"]
                pair.append(tot[:, 0:128] / jnp.maximum(jnp.abs(tot[:, 128:256]), hd["mt"][:, 256:384]))
            out_ref[0, :, 128 * p:128 * p + 128] = jnp.where(lane < 64, pair[0], pair[1]).astype(BF16)
    m_ref[...] = m_all
    for r in range(8):
        cn_ref[r] = heads[r]["cn_new"]


def _mlstm(ml, mkt, gt, bcol, n_ctx_chunks):
    bsz, t, _ = ml.shape
    nc = t // CHUNK
    out = jax.ShapeDtypeStruct((bsz, t, W_BRANCH), BF16)
    const2 = lambda b, c: (0, 0)

    def chunk_specs(chunk):
        return [pl.BlockSpec((1, CHUNK, W_BRANCH), lambda b, c: (b, chunk(c), 0)),
                pl.BlockSpec((1, CHUNK, W_BRANCH), lambda b, c: (b, chunk(c), 1)),
                pl.BlockSpec((1, W_BRANCH, CHUNK), lambda b, c: (b, 0, chunk(c))),
                pl.BlockSpec((1, GATE_ROWS, CHUNK), lambda b, c: (b, 0, chunk(c)))]

    fwd_chunk = lambda c: c
    bwd_chunk = lambda c: _bwd_chunk(c, n_ctx_chunks, nc)
    return dict(
        steps=_mlstm_steps,
        in_specs=chunk_specs(fwd_chunk) + chunk_specs(bwd_chunk) + [
            pl.BlockSpec((GATE_ROWS, 1), const2), pl.BlockSpec((CHUNK, LANES), const2),
            pl.BlockSpec((CHUNK, 2 * LANES), const2),
            pl.BlockSpec((8, CHUNK, 3 * LANES), lambda b, c: (0, 0, 0))],
        out_specs=[pl.BlockSpec((1, CHUNK, W_BRANCH), lambda b, c: (b, c, 0)),
                   pl.BlockSpec((1, CHUNK, W_BRANCH), lambda b, c: (b, bwd_chunk(c), 0))],
        out_shape=[out, out],
        scratch_shapes=[pltpu.VMEM((8, LANES, 2 * LANES), F32), pltpu.VMEM((8, LANES), F32)],
        args=(ml, ml, mkt, gt, ml, ml, mkt, gt, bcol, jnp.ones((CHUNK, LANES), BF16), _spread_selector(2),
              _state_row_selector(3)))


def _ssd_steps(n_ctx_chunks, n_chunks,
                xf_ref, xfp_ref, xfn_ref, xb_ref, xbp_ref, xbn_ref,
                gtf_ref, gtb_ref, bcol_ref, acol_ref, cw_ref, cb_ref, dskip_ref, ones_ref, sel_ref, rsel_ref,
                yf_ref, yb_ref, s_ref):
    c = pl.program_id(1)

    @pl.when(c == 0)
    def _():
        s_ref[...] = jnp.zeros_like(s_ref)

    lane = _iota2((CHUNK, LANES), 1)
    row768 = _iota2((CHUNK, 768), 0)
    ninf = jnp.float32(-jnp.inf)

    dt, a_cum = [], []
    for d in (0, 1):
        gt_ref = (gtf_ref, gtb_ref)[d]
        tr = jnp.where(_tri_masks(d)[1], 1.0, 0.0).astype(BF16)
        dt.append(_softplus(gt_ref[0, 16:24, :] + bcol_ref[16:24, :]))
        a_cum.append(_mm_exact_r(dt[d] * acol_ref[16:24, :], tr))

    yield
    xa = []
    for d in (0, 1):
        x_ref, xp_ref, xn_ref = ((xf_ref, xfp_ref, xfn_ref), (xb_ref, xbp_ref, xbn_ref))[d]
        j = c if d == 0 else _bwd_chunk(c, n_ctx_chunks, n_chunks)
        seg_first = jnp.logical_or(j == 0, j == n_ctx_chunks)
        seg_last = jnp.logical_or(j == n_ctx_chunks - 1, j == n_chunks - 1)
        x = x_ref[0]
        prev = jnp.where(seg_first, 0.0, xp_ref[0, 7:8, :])
        nxt = jnp.where(seg_last, 0.0, xn_ref[0, 0:1, :])
        x_dn = jnp.where(row768 == 0, prev, pltpu.roll(x, 1, 0))
        x_up = jnp.where(row768 == CHUNK - 1, nxt, pltpu.roll(x, CHUNK - 1, 0))
        xa.append(_silu(x_dn * cw_ref[0:1, :] + x * cw_ref[1:2, :] + x_up * cw_ref[2:3, :] + cb_ref[...]))

    heads = [None] * 8
    groups = {}
    for d in (0, 1):
        for g in range(2):
            x_pair = xa[d][:, 128 * g:128 * g + 128]
            b_g = xa[d][:, 256 + 128 * g:256 + 128 * g + 128]
            c_gb = xa[d][:, 512 + 128 * g:512 + 128 * g + 128].astype(BF16)
            groups[d, g] = dict(x_pair=x_pair, bt=b_g.T.astype(BF16), cb=_mm_nt(c_gb, b_g.astype(BF16)))
            for half in range(2):
                r = 4 * d + 2 * g + half
                hmask = (lane >= 64) if half else (lane < 64)
                st = s_ref[r]
                xm = jnp.where(hmask, x_pair, 0.0)
                heads[r] = dict(st=st, xm=xm, xmb=xm.astype(BF16), cs=_mm(c_gb, st.astype(BF16)))

    yield
    for d in (0, 1):
        last = CHUNK - 1 if d == 0 else 0
        a_last = a_cum[d][:, last:last + 1]
        w_in = jnp.exp(a_last - a_cum[d]) * dt[d]
        e_last = jnp.exp(jnp.broadcast_to(a_last, (8, CHUNK)))
        lhs, rhs = _spread_operands(a_cum[d], -a_cum[d], [jnp.exp(a_cum[d]), w_in], sel_ref, ones_ref)
        for h in range(N_HEADS):
            r = 4 * d + h
            heads[r]["mt"] = _mm(lhs, rhs * rsel_ref[r])
            heads[r]["e_last"] = e_last[r:r + 1, :]

    yield
    for r in range(8):
        d, h = divmod(r, 4)
        hd = heads[r]
        ds = _mm(groups[d, h // 2]["bt"], (hd["xm"] * hd["mt"][:, 256:384]).astype(BF16))
        s_ref[r] = hd["e_last"] * hd["st"] + ds
    for r in range(8):
        d, h = divmod(r, 4)
        hd = heads[r]
        vis, _ = _tri_masks(d)
        sc = groups[d, h // 2]["cb"] * jnp.exp(jnp.where(vis, hd["mt"][:, 0:128], ninf)) * dt[d][r:r + 1, :]
        hd["y"] = _mm(sc.astype(BF16), hd["xmb"]) + hd["mt"][:, 128:256] * hd["cs"]

    for d in (0, 1):
        out_ref = (yf_ref, yb_ref)[d]
        for g in range(2):
            y_pair = heads[4 * d + 2 * g]["y"] + heads[4 * d + 2 * g + 1]["y"]
            if d == 0:
                y_pair = y_pair + dskip_ref[:, 128 * g:128 * g + 128] * groups[d, g]["x_pair"]
            out_ref[0, :, 128 * g:128 * g + 128] = y_pair.astype(BF16)


def _ssd(ss, gt, bcol, acol, conv_w, conv_b, dskip, n_ctx_chunks):
    bsz, t, _ = ss.shape
    nc = t // CHUNK
    sub = CHUNK // 8
    nsub = t // 8
    bc = lambda c: _bwd_chunk(c, n_ctx_chunks, nc)
    fwd = lambda b, c: (b, c, 0)
    bwd = lambda b, c: (b, bc(c), 0)
    fwd_p = lambda b, c: (b, jnp.maximum(c * sub - 1, 0), 0)
    fwd_n = lambda b, c: (b, jnp.minimum((c + 1) * sub, nsub - 1), 0)
    bwd_p = lambda b, c: (b, jnp.maximum(bc(c) * sub - 1, 0), 0)
    bwd_n = lambda b, c: (b, jnp.minimum((bc(c) + 1) * sub, nsub - 1), 0)
    fwd_t = lambda b, c: (b, 0, c)
    bwd_t = lambda b, c: (b, 0, bc(c))
    const2 = lambda b, c: (0, 0)
    out = jax.ShapeDtypeStruct((bsz, t, W_BRANCH), BF16)
    return dict(
        steps=functools.partial(_ssd_steps, n_ctx_chunks, nc),
        in_specs=[pl.BlockSpec((1, CHUNK, 768), fwd), pl.BlockSpec((1, 8, 768), fwd_p),
                  pl.BlockSpec((1, 8, 768), fwd_n),
                  pl.BlockSpec((1, CHUNK, 768), bwd), pl.BlockSpec((1, 8, 768), bwd_p),
                  pl.BlockSpec((1, 8, 768), bwd_n),
                  pl.BlockSpec((1, GATE_ROWS, CHUNK), fwd_t), pl.BlockSpec((1, GATE_ROWS, CHUNK), bwd_t),
                  pl.BlockSpec((GATE_ROWS, 1), const2), pl.BlockSpec((GATE_ROWS, 1), const2),
                  pl.BlockSpec((3, 768), const2), pl.BlockSpec((1, 768), const2),
                  pl.BlockSpec((1, W_BRANCH), const2),
                  pl.BlockSpec((CHUNK, LANES), const2), pl.BlockSpec((CHUNK, 2 * LANES), const2),
                  pl.BlockSpec((8, CHUNK, 3 * LANES), lambda b, c: (0, 0, 0))],
        out_specs=[pl.BlockSpec((1, CHUNK, W_BRANCH), fwd), pl.BlockSpec((1, CHUNK, W_BRANCH), bwd)],
        out_shape=[out, out],
        scratch_shapes=[pltpu.VMEM((8, SSM_N, LANES), F32)],
        args=(ss, ss, ss, ss, ss, ss, gt, gt, bcol, acol, conv_w, conv_b, dskip,
              jnp.ones((CHUNK, LANES), BF16), _spread_selector(2), _state_row_selector(3)))


def _hgrn2_steps(hf_ref, hb_ref, loglb_ref, log1m_ref, onem_ref, e_ref, of_ref, ob_ref,
                  st_ref):
    @pl.when(pl.program_id(1) == 0)
    def _():
        st_ref[...] = jnp.zeros_like(st_ref)

    ri = _iota2((CHUNK, CHUNK), 0)
    ci = _iota2((CHUNK, CHUNK), 1)
    lane = _iota2((CHUNK, LANES), 1)
    rw = _iota2((CHUNK, W_BRANCH), 0)
    blockdiag = (ri >> 6) == (ci >> 6)
    ninf = jnp.float32(-jnp.inf)
    nb = CHUNK // SUB
    rs = _iota2((nb, SUB, W_BRANCH), 1)

    dirs = []
    for d in (0, 1):
        h_ref = (hf_ref, hb_ref)[d]
        tc = jnp.where(_tri_masks(d)[0], 1.0, 0.0).astype(BF16)
        z = h_ref[0, :, W_BRANCH * (1 + d):W_BRANCH * (2 + d)]
        e = jnp.exp(-jnp.abs(z))
        ope = 1.0 + e
        la = loglb_ref[d:d + 1, :]
        lb_ = log1m_ref[d:d + 1, :] + (jnp.minimum(z, 0.0) - jnp.log(ope))
        logf = jnp.maximum(la, lb_) + jnp.log(1.0 + jnp.exp(-jnp.abs(la - lb_)))
        dirs.append(dict(
            q=_silu(h_ref[0, :, 0:W_BRANCH]),
            v=h_ref[0, :, 3 * W_BRANCH:4 * W_BRANCH],
            kk=onem_ref[d:d + 1, :] * (jnp.where(z >= 0.0, e, 1.0) / ope),
            gcum=_mm_exact_l(tc, logf)))

    yield
    for d in (0, 1):
        dd = dirs[d]
        q, kk, gcum = dd["q"], dd["kk"], dd["gcum"]
        last = CHUNK - 1 if d == 0 else 0
        g_last = gcum[last:last + 1, :]
        qg = (q * jnp.exp(gcum)).astype(BF16)
        kg = (kk * jnp.exp(g_last - gcum)).astype(BF16)
        vb = dd["v"].astype(BF16)

        dd["o_inter"] = []
        for p in range(2):
            sl = slice(128 * p, 128 * p + 128)
            st = st_ref[2 * d + p]
            dd["o_inter"].append(_mm_nt(qg[:, sl], st.astype(BF16)))
            dst = _mm_tn(vb[:, sl], kg[:, sl])
            st_ref[2 * d + p] = st * jnp.exp(g_last[:, sl]) + jnp.where(blockdiag, dst, 0.0)

        a_mats = [jnp.zeros((CHUNK, CHUNK), F32) for _ in range(N_HEADS)]
        blk = CHUNK // 2
        while blk >= SUB:
            first = (rw & (2 * blk - 1)) < blk
            edge = (blk - 1) if d == 0 else blk
            gb = gcum.reshape(CHUNK // (2 * blk), 2 * blk, W_BRANCH)[:, edge:edge + 1, :]
            gb = jnp.broadcast_to(gb, (CHUNK // (2 * blk), 2 * blk, W_BRANCH)).reshape(CHUNK, W_BRANCH)
            q_side = jnp.logical_not(first) if d == 0 else first
            qt = q * jnp.exp(jnp.where(q_side, gcum - gb, ninf))
            kt = (kk * jnp.exp(jnp.where(q_side, ninf, gb - gcum))).astype(BF16)
            same = (ri >> int(math.log2(2 * blk))) == (ci >> int(math.log2(2 * blk)))
            for h in range(N_HEADS):
                p, half = divmod(h, 2)
                hmask = (lane >= 64) if half else (lane < 64)
                qh = jnp.where(hmask, qt[:, 128 * p:128 * p + 128], 0.0).astype(BF16)
                a_mats[h] = a_mats[h] + jnp.where(same, _mm_nt(qh, kt[:, 128 * p:128 * p + 128]), 0.0)
            blk //= 2
        dd["a_mats"] = a_mats

        g3 = gcum.reshape(nb, SUB, W_BRANCH)
        q3 = q.reshape(nb, SUB, W_BRANCH)
        k3 = kk.reshape(nb, SUB, W_BRANCH)
        v3 = dd["v"].reshape(nb, SUB, W_BRANCH)
        o3 = None
        for j in range(SUB):
            ok = (rs >= j) if d == 0 else (rs <= j)
            pj = q3 * jnp.exp(jnp.where(ok, g3 - g3[:, j:j + 1, :], ninf)) * k3[:, j:j + 1, :]
            red = _mm(pj.reshape(CHUNK, W_BRANCH).astype(BF16), e_ref[...])
            term = red.reshape(nb, SUB, W_BRANCH) * v3[:, j:j + 1, :]
            o3 = term if o3 is None else o3 + term
        dd["o_diag"] = o3.reshape(CHUNK, W_BRANCH)

    yield
    for d in (0, 1):
        dd = dirs[d]
        out_ref = (of_ref, ob_ref)[d]
        for p in range(2):
            sl = slice(128 * p, 128 * p + 128)
            a_cat = jnp.concatenate([dd["a_mats"][2 * p], dd["a_mats"][2 * p + 1]], axis=1).astype(BF16)
            v_p = dd["v"][:, sl]
            v_cat = jnp.concatenate([jnp.where(lane < 64, v_p, 0.0), jnp.where(lane >= 64, v_p, 0.0)],
                                    axis=0).astype(BF16)
            out_ref[0, :, sl] = (dd["o_inter"][p] + _mm(a_cat, v_cat) + dd["o_diag"][:, sl]).astype(BF16)


def _hgrn2(hg, loglb, log1m, onem, e64, n_ctx_chunks):
    bsz, t, _ = hg.shape
    nc = t // CHUNK
    fwd = lambda b, c: (b, c, 0)
    bwd = lambda b, c: (b, _bwd_chunk(c, n_ctx_chunks, nc), 0)
    const2 = lambda b, c: (0, 0)
    out = jax.ShapeDtypeStruct((bsz, t, W_BRANCH), BF16)
    return dict(
        steps=_hgrn2_steps,
        in_specs=[pl.BlockSpec((1, CHUNK, 1024), fwd), pl.BlockSpec((1, CHUNK, 1024), bwd),
                  pl.BlockSpec((2, W_BRANCH), const2), pl.BlockSpec((2, W_BRANCH), const2),
                  pl.BlockSpec((2, W_BRANCH), const2), pl.BlockSpec((W_BRANCH, W_BRANCH), const2)],
        out_specs=[pl.BlockSpec((1, CHUNK, W_BRANCH), fwd), pl.BlockSpec((1, CHUNK, W_BRANCH), bwd)],
        out_shape=[out, out],
        scratch_shapes=[pltpu.VMEM((4, LANES, LANES), F32)],
        args=(hg, hg, loglb, log1m, onem, e64))


def _scan_kernel(parts, *refs):
    n_in = [len(p["in_specs"]) for p in parts]
    n_out = [len(p["out_specs"]) for p in parts]
    n_scr = [len(p["scratch_shapes"]) for p in parts]
    ins, outs, scr = refs[:sum(n_in)], refs[sum(n_in):sum(n_in) + sum(n_out)], refs[sum(n_in) + sum(n_out):]
    gens = []
    for k, p in enumerate(parts):
        mine = (ins[sum(n_in[:k]):sum(n_in[:k + 1])] + outs[sum(n_out[:k]):sum(n_out[:k + 1])]
                + scr[sum(n_scr[:k]):sum(n_scr[:k + 1])])
        gens.append(p["steps"](*mine))
    while gens:
        for g in list(gens):
            if next(g, gens) is gens:
                gens.remove(g)


def _chunk_scans(parts, bsz, n_chunks):
    return pl.pallas_call(
        functools.partial(_scan_kernel, parts),
        grid=(bsz, n_chunks),
        in_specs=[s for p in parts for s in p["in_specs"]],
        out_specs=[s for p in parts for s in p["out_specs"]],
        out_shape=[s for p in parts for s in p["out_shape"]],
        scratch_shapes=[s for p in parts for s in p["scratch_shapes"]],
        compiler_params=_cparams(("parallel", "arbitrary")),
        name="chunk_scans",
    )(*[a for p in parts for a in p["args"]])


def _attn(daq, kt, vx, lam_vecs, lam_init, n_ctx_tiles, skip_ctx):
    bsz, t, _ = daq.shape
    tq = ROW_TILE
    n_k_tiles = t // KEY_TILE
    n_lat_tiles = n_k_tiles - n_ctx_tiles
    group = math.gcd(KEY_GROUP, n_lat_tiles)
    q_off = n_ctx_tiles if skip_ctx else 0
    nq = t // tq - q_off
    n_maps = W_BRANCH // DA_QK
    rows = n_maps * tq

    def kern(q_ref, kt_ref, vx_ref, lv_ref, o_ref, q8_ref, m_ref, acc_ref, s_ref, mx_ref):
        qi = pl.program_id(1) + q_off
        lane = _iota2((tq, W_BRANCH), 1)
        q = q_ref[0]
        for j in range(n_maps):
            q8_ref[j * tq:(j + 1) * tq, :] = jnp.where((lane >> 5) == j, q, jnp.zeros_like(q))

        def score_tile(t):
            return _mm(q8_ref[...], kt_ref[0, t])

        def update(t0, n, after_scores=None):
            s = [score_tile(t0 + i) for i in range(n)]
            if after_scores is not None:
                after_scores()
            mx = s[0][:, 0:LANES]
            for i in range(n):
                for c in range(KEY_TILE // LANES):
                    if i or c:
                        mx = jnp.maximum(mx, s[i][:, LANES * c:LANES * (c + 1)])
            m_new = jnp.broadcast_to(jnp.max(mx, axis=1, keepdims=True), mx.shape)
            m_ref[...] = m_new
            m2 = jnp.concatenate([m_new] * (KEY_TILE // LANES), axis=1)
            p = [jnp.exp(s[i] - m2).astype(BF16) for i in range(n)]
            for hd in range(N_HEADS):
                r0 = 2 * hd * tq
                pv = _mm(p[0][r0:r0 + 2 * tq], vx_ref[0, t0, hd])
                for i in range(1, n):
                    pv = pv + _mm(p[i][r0:r0 + 2 * tq], vx_ref[0, t0 + i, hd])
                acc_ref[r0:r0 + 2 * tq, :] = pv

        def scores(g, slot):
            t0 = n_ctx_tiles + g * group
            mx = None
            for i in range(group):
                s = score_tile(t0 + i)
                s_ref[slot, :, KEY_TILE * i:KEY_TILE * (i + 1)] = s
                for c in range(KEY_TILE // LANES):
                    part = s[:, LANES * c:LANES * (c + 1)]
                    mx = part if mx is None else jnp.maximum(mx, part)
            mx_ref[slot] = mx

        def absorb(g, slot):
            t0 = n_ctx_tiles + g * group
            m_old = m_ref[...]
            m_new = jnp.maximum(m_old, jnp.max(mx_ref[slot], axis=1, keepdims=True))
            alpha = jnp.exp(m_old - m_new)
            m_ref[...] = m_new
            m2 = jnp.concatenate([m_new] * (KEY_TILE // LANES), axis=1)
            p = [jnp.exp(s_ref[slot, :, KEY_TILE * i:KEY_TILE * (i + 1)] - m2).astype(BF16) for i in range(group)]
            for hd in range(N_HEADS):
                r0 = 2 * hd * tq
                pv = _mm(p[0][r0:r0 + 2 * tq], vx_ref[0, t0, hd])
                for i in range(1, group):
                    pv = pv + _mm(p[i][r0:r0 + 2 * tq], vx_ref[0, t0 + i, hd])
                acc_ref[r0:r0 + 2 * tq, :] = alpha[r0:r0 + 2 * tq] * acc_ref[r0:r0 + 2 * tq, :] + pv

        @pl.when(qi < n_ctx_tiles)
        def _():
            update(0, n_ctx_tiles)

        @pl.when(qi >= n_ctx_tiles)
        def _():
            n_groups = n_lat_tiles // group
            n_pairs = (n_groups - 1) // 2
            update(0, n_ctx_tiles, after_scores=lambda: scores(0, 0))

            def body(k, carry):
                g = 2 * k
                scores(g + 1, 1)
                absorb(g, 0)
                scores(g + 2, 0)
                absorb(g + 1, 1)
                return carry
            for k in range(n_pairs):
                body(k, 0)
            g0 = 2 * n_pairs
            if n_groups - g0 == 2:
                scores(g0 + 1, 1)
            absorb(g0, 0)
            if n_groups - g0 == 2:
                absorb(g0 + 1, 1)

        lv = lv_ref[...]
        lam = (jnp.exp(jnp.sum(lv[0:1, :] * lv[1:2, :], axis=1, keepdims=True))
               - jnp.exp(jnp.sum(lv[2:3, :] * lv[3:4, :], axis=1, keepdims=True)) + lam_init)
        lane1 = _iota2((tq, LANES), 1)
        for pr in range(N_HEADS // 2):
            halves = []
            for half in range(2):
                r0 = 2 * (2 * pr + half) * tq
                one = 0 if half else HEAD_W
                a0 = acc_ref[r0:r0 + tq, :]
                a1 = acc_ref[r0 + tq:r0 + 2 * tq, :]
                halves.append(a0 / a0[:, one:one + 1] - lam * (a1 / a1[:, one:one + 1]))
            o_ref[0, :, LANES * pr:LANES * (pr + 1)] = jnp.where(lane1 < HEAD_W, halves[0], halves[1]).astype(BF16)

    return pl.pallas_call(
        kern,
        grid=(bsz, nq),
        in_specs=[pl.BlockSpec((1, tq, W_BRANCH), lambda b, i: (b, i + q_off, 0)),
                  pl.BlockSpec((1, n_k_tiles, W_BRANCH, KEY_TILE), lambda b, i: (b, 0, 0, 0),
                               pipeline_mode=pl.Buffered(1)),
                  pl.BlockSpec((1, n_k_tiles, N_HEADS, KEY_TILE, LANES), lambda b, i: (b, 0, 0, 0, 0),
                               pipeline_mode=pl.Buffered(1)),
                  pl.BlockSpec(lam_vecs.shape, lambda b, i: (0, 0))],
        out_specs=pl.BlockSpec((1, tq, W_BRANCH), lambda b, i: (b, i, 0)),
        out_shape=jax.ShapeDtypeStruct((bsz, nq * tq, W_BRANCH), BF16),
        scratch_shapes=[pltpu.VMEM((rows, W_BRANCH), BF16),
                        pltpu.VMEM((rows, LANES), F32),
                        pltpu.VMEM((rows, LANES), F32),
                        pltpu.VMEM((2, rows, group * KEY_TILE), F32),
                        pltpu.VMEM((2, rows, LANES), F32)],
        compiler_params=_cparams(("parallel", "parallel")),
        name="diff_attn",
    )(daq, kt, vx, lam_vecs)


def _seg_mean(x, e_ref):
    x0 = x.astype(BF16)
    x1 = (x - x0.astype(F32)).astype(BF16)
    return (_mm(x0, e_ref[...]) + _mm(x1, e_ref[...])) * (1.0 / HEAD_W)


def _out_kernel(n_ctx_tiles, ctx_row, q_off, lam_init, final,
                c_ref, x_ref, mo_ref, mlo_ref, mlz_ref, mhf_ref, mhb_ref, dao_ref, daz_ref,
                syf_ref, syb_ref, ssz_ref, hof_ref, hob_ref, hgz_ref,
                mlg_ref, dag_ref, ssg_ref, hgg_ref, e_ref, w_ref, fg_ref, o_ref):
    b = pl.program_id(0)
    i = pl.program_id(1) + q_off
    d = x_ref.shape[2]
    r = jnp.where(i < n_ctx_tiles, ctx_row, b)
    gate = mo_ref[pl.ds(r, 1), :][:, 2 * d:3 * d]

    u = _sigmoid(mlo_ref[0].astype(F32)) * (mhf_ref[0].astype(F32) + mhb_ref[0].astype(F32))
    dev = u - _seg_mean(u, e_ref)
    y_ml = dev * lax.rsqrt(_seg_mean(dev * dev, e_ref) + EPS) * mlg_ref[...] * _silu(mlz_ref[0].astype(F32))

    o = dao_ref[0].astype(F32)
    y_da = (o * lax.rsqrt(_seg_mean(o * o, e_ref) + EPS) * dag_ref[...]) * (1.0 - lam_init) * _silu(daz_ref[0].astype(F32))

    ys = (syf_ref[0].astype(F32) + syb_ref[0].astype(F32)) * _silu(ssz_ref[0].astype(F32))
    parts = []
    for g in range(2):
        yg = ys[:, 128 * g:128 * g + 128]
        parts.append(yg * lax.rsqrt(jnp.mean(yg * yg, axis=-1, keepdims=True) + EPS))
    y_ss = jnp.concatenate(parts, axis=1) * ssg_ref[...]

    oh = hof_ref[0].astype(F32) + hob_ref[0].astype(F32)
    y_hg = (oh * lax.rsqrt(_seg_mean(oh * oh, e_ref) + EPS) * hgg_ref[...]) * _silu(hgz_ref[0].astype(F32))

    acc = _mm(y_ml.astype(BF16), w_ref[0:W_BRANCH, :])
    acc = acc + _mm(y_da.astype(BF16), w_ref[W_BRANCH:2 * W_BRANCH, :])
    acc = acc + _mm(y_ss.astype(BF16), w_ref[2 * W_BRANCH:3 * W_BRANCH, :])
    acc = acc + _mm(y_hg.astype(BF16), w_ref[3 * W_BRANCH:4 * W_BRANCH, :])
    x_new = jnp.where(i < n_ctx_tiles, c_ref[0], x_ref[0]) + gate * acc
    if final:
        x_new = x_new * lax.rsqrt(jnp.mean(x_new * x_new, axis=-1, keepdims=True) + EPS) * fg_ref[...]
    o_ref[0] = x_new


def _out_proj(stream, mo, zg, mhf, mhb, dao, syf, syb, hof, hob,
              mlg, dag, ssg, hgg, e64, w_out, final_g, lam_init, n_ctx_tiles, ctx_row, final):
    c_src, x_src, x_off, t = stream
    bsz, _, d = x_src.shape
    tm = ROW_TILE
    q_off = n_ctx_tiles if final else 0
    nrow = t // tm - q_off
    row = lambda b, i: (b, i + q_off, 0)
    col = lambda k: (lambda b, i: (b, i + q_off, k))
    const2 = lambda b, i: (0, 0)
    wb = pl.BlockSpec((1, tm, W_BRANCH), row)
    return pl.pallas_call(
        functools.partial(_out_kernel, n_ctx_tiles, ctx_row, q_off, lam_init, final),
        grid=(bsz, nrow),
        in_specs=_stream_specs(stream, n_ctx_tiles, q_off) + [
                  pl.BlockSpec(mo.shape, const2),
                  pl.BlockSpec((1, tm, W_BRANCH), col(0)), pl.BlockSpec((1, tm, W_BRANCH), col(1)),
                  wb, wb,
                  pl.BlockSpec((1, tm, W_BRANCH), lambda b, i: (b, i, 0)) if final else wb,
                  pl.BlockSpec((1, tm, W_BRANCH), col(2)),
                  wb, wb, pl.BlockSpec((1, tm, W_BRANCH), col(3)),
                  wb, wb, pl.BlockSpec((1, tm, W_BRANCH), col(4)),
                  pl.BlockSpec((1, W_BRANCH), const2), pl.BlockSpec((1, W_BRANCH), const2),
                  pl.BlockSpec((1, W_BRANCH), const2), pl.BlockSpec((1, W_BRANCH), const2),
                  pl.BlockSpec((W_BRANCH, W_BRANCH), const2),
                  pl.BlockSpec(w_out.shape, const2),
                  pl.BlockSpec((1, d), const2)],
        out_specs=pl.BlockSpec((1, tm, d), lambda b, i: (b, i, 0)),
        out_shape=jax.ShapeDtypeStruct((bsz, nrow * tm, d), F32),
        compiler_params=_cparams(("parallel", "parallel")),
        name="out_proj",
    )(c_src, x_src, mo, zg, zg, mhf, mhb, dao, zg, syf, syb, zg, hof, hob, zg,
      mlg, dag, ssg, hgg, e64, w_out, final_g.reshape(1, d))


def _relayout_w_in(w_in):
    d = w_in.shape[0]
    o = {}
    off = 0
    for name, n in (("ml_q", 256), ("ml_k", 256), ("ml_v", 256), ("ml_o", 256), ("ml_i", 8), ("ml_f", 8),
                    ("ml_z", 256), ("da_q", 256), ("da_k", 256), ("da_v", 256), ("da_z", 256),
                    ("ss_xbc", 768), ("ss_dt", 8), ("ss_z", 256),
                    ("hg_q", 256), ("hg_f", 512), ("hg_i", 256), ("hg_z", 256)):
        o[name] = w_in[:, off:off + n]
        off += n

    gates = jnp.concatenate([o["ml_i"], o["ml_f"], o["ss_dt"], jnp.zeros((d, GATE_ROWS - 24), w_in.dtype)], axis=1)
    w2 = jnp.concatenate([o["ml_q"], o["ml_v"], o["da_q"], o["da_v"],
                          o["ss_xbc"], o["hg_q"], o["hg_f"], o["hg_i"],
                          o["ml_o"], o["ml_z"], o["da_z"], o["ss_z"], o["hg_z"]], axis=1)
    wt = jnp.concatenate([o["da_k"], gates, o["ml_k"]], axis=1).T
    return w2.astype(BF16), wt.astype(BF16)


def _rope_tables(n_ctx, seq):
    pos = jnp.arange(seq)
    rows = (pos // GRID_W).astype(F32)
    cols = (pos % GRID_W).astype(F32)
    axis = DA_QK // 2
    inv = ROPE_BASE ** (-jnp.arange(0, axis, 2, dtype=F32) / axis)
    ang = jnp.concatenate([rows[:, None] * inv, cols[:, None] * inv], axis=-1)
    cos = jnp.concatenate([jnp.ones((n_ctx, axis), F32), jnp.cos(ang)], axis=0)
    sin = jnp.concatenate([jnp.zeros((n_ctx, axis), F32), jnp.sin(ang)], axis=0)
    cos_t = jnp.concatenate([jnp.ones((axis, n_ctx), F32), jnp.cos(ang.T)], axis=1)
    sin_t = jnp.concatenate([jnp.zeros((axis, n_ctx), F32), jnp.sin(ang.T)], axis=1)
    return (jnp.tile(cos, (1, LANES // axis)), jnp.tile(sin, (1, LANES // axis)),
            jnp.tile(cos_t, (2, 1)), jnp.tile(sin_t, (2, 1)))


def _gate_col(vals):
    v = jnp.concatenate([vals.astype(F32), jnp.zeros((GATE_ROWS - vals.shape[0],), F32)])
    return v.reshape(GATE_ROWS, 1)


def kernel(x, c, ctx, c_ctx, w_mod, b_mod, norm_g, w_in, w_out, ml_gate_b, ml_norm_g, da_lambda, da_norm_g,
           ss_conv_w, ss_conv_b, ss_dt_bias, ss_a_log, ss_d, ss_norm_g, hg_lower, hg_norm_g, final_g):
    bsz, seq, d = x.shape
    n_ctx = ctx.shape[1]
    depth = w_mod.shape[0]
    assert n_ctx % ROW_TILE == 0 and seq % ROW_TILE == 0 and bsz < 8
    n_ctx_tiles = n_ctx // ROW_TILE
    n_ctx_chunks = n_ctx // CHUNK
    ctx_row = bsz

    stream = (ctx, x, 0, n_ctx + seq)
    cc = jnp.concatenate([c, c_ctx[None, :], jnp.zeros((8 - bsz - 1, d), F32)], axis=0)
    cos, sin, cos_t, sin_t = _rope_tables(n_ctx, seq)
    lb_all = jnp.cumsum(jax.nn.softmax(hg_lower.astype(F32), axis=1), axis=1)
    lb_all = lb_all - lb_all[:, :1]
    hid = _iota2((W_BRANCH, W_BRANCH), 0) // HEAD_W
    e64 = (hid == hid.T).astype(BF16)

    out = None
    for l in range(depth):
        lam_init = 0.8 - 0.6 * math.exp(-0.3 * l)
        final = l == depth - 1
        w2, wt = _relayout_w_in(w_in[l])
        mo = _modulation(cc, w_mod[l], b_mod[l])
        ml, daq, dak, dav, ss, hg, zg, gt, mkt = _in_proj(stream, mo, norm_g[l], cos, sin, cos_t, sin_t, w2, wt,
                                                     n_ctx_tiles, ctx_row)

        gb = ml_gate_b[l]
        zeros8 = jnp.zeros((8,), F32)
        bcol = _gate_col(jnp.concatenate([gb[:, 0].reshape(-1), gb[:, 1].reshape(-1),
                                                ss_dt_bias[l].reshape(-1)]))
        acol = _gate_col(jnp.concatenate([zeros8, zeros8, -jnp.exp(ss_a_log[l].astype(F32)).reshape(-1)]))
        dao = _attn(daq, dak, dav, da_lambda[l].astype(F32), lam_init, n_ctx_tiles, final)
        dskip = jnp.repeat(ss_d[l].astype(F32), HEAD_W).reshape(1, W_BRANCH)
        lbh = lb_all[:, l]
        scans = [_mlstm(ml, mkt, gt, bcol, n_ctx_chunks),
                 _ssd(ss, gt, bcol, acol, ss_conv_w[l], ss_conv_b[l].reshape(1, -1), dskip, n_ctx_chunks),
                 _hgrn2(hg, jnp.log(lbh), jnp.log1p(-lbh), 1.0 - lbh, e64, n_ctx_chunks)]
        mhf, mhb, syf, syb, hof, hob = _chunk_scans(scans, bsz, (n_ctx + seq) // CHUNK)

        res = _out_proj(stream, mo, zg, mhf, mhb, dao, syf, syb, hof, hob,
                        ml_norm_g[l].reshape(1, -1), jnp.tile(da_norm_g[l], N_HEADS).reshape(1, -1),
                        ss_norm_g[l].reshape(1, -1), hg_norm_g[l].reshape(1, -1), e64,
                        w_out[l].astype(BF16), final_g, lam_init, n_ctx_tiles, ctx_row, final)
        if final:
            out = res
        else:
            stream = (res, res, n_ctx_tiles, n_ctx + seq)
    return out
```

```python
import functools
import math

import jax
import jax.numpy as jnp
from jax import lax
from jax.experimental import pallas as pl
from jax.experimental.pallas import tpu as pltpu

F32 = jnp.float32
BF16 = jnp.bfloat16

EPS = 1e-6
GRID_W = 64
ROPE_BASE = 10000.0
N_HEADS = 4
HEAD_W = 64
W_BRANCH = N_HEADS * HEAD_W
DA_QK = 32
SSM_N = 128
CHUNK = 128
ROW_TILE = 256
SUB = 8
LANES = 128
VMEM_LIMIT = 56 * 1024 * 1024

C_ML = 0
C_DAQ = 512
C_DAV = 768
C_SS = 1024
C_HG = 1792
C_ZG = 2816
C_END = 4096
GATE_ROWS = 32
KEY_TILE = 256
KEY_GROUP = 4

NT = (((1,), (1,)), ((), ()))
TN = (((0,), (0,)), ((), ()))


def _mm(a, b):
    return jnp.dot(a, b, preferred_element_type=F32)


def _mm_nt(a, b):
    return lax.dot_general(a, b, NT, preferred_element_type=F32)


def _mm_tn(a, b):
    return lax.dot_general(a, b, TN, preferred_element_type=F32)


def _split3(x):
    x0 = x.astype(BF16)
    r = x - x0.astype(F32)
    x1 = r.astype(BF16)
    x2 = (r - x1.astype(F32)).astype(BF16)
    return x0, x1, x2


def _mm_exact_l(t, x):
    x0, x1, x2 = _split3(x)
    return _mm(t, x0) + _mm(t, x1) + _mm(t, x2)


def _mm_exact_r(x, t):
    x0, x1, x2 = _split3(x)
    return _mm(x0, t) + _mm(x1, t) + _mm(x2, t)


def _sigmoid(x):
    return 1.0 / (1.0 + jnp.exp(-x))


def _silu(x):
    return x * _sigmoid(x)


def _log_sigmoid(x):
    return jnp.minimum(x, 0.0) - jnp.log(1.0 + jnp.exp(-jnp.abs(x)))


def _softplus(x):
    return jnp.maximum(x, 0.0) + jnp.log(1.0 + jnp.exp(-jnp.abs(x)))


def _iota2(shape, axis):
    return lax.broadcasted_iota(jnp.int32, shape, axis)


def _cparams(sem):
    return pltpu.CompilerParams(dimension_semantics=sem, vmem_limit_bytes=VMEM_LIMIT)


def _mod_kernel(c_ref, w_ref, b_ref, o_ref):
    c = c_ref[...]
    o_ref[...] = jnp.dot(_silu(c), w_ref[...], precision=lax.Precision.HIGHEST,
                         preferred_element_type=F32) + b_ref[...]


def _modulation(cc, w_mod, b_mod):
    d = cc.shape[1]
    n = w_mod.shape[1]
    tn = 1536
    assert n % tn == 0
    return pl.pallas_call(
        _mod_kernel,
        grid=(n // tn,),
        in_specs=[pl.BlockSpec((8, d), lambda j: (0, 0)),
                  pl.BlockSpec((d, tn), lambda j: (0, j)),
                  pl.BlockSpec((1, tn), lambda j: (0, j))],
        out_specs=pl.BlockSpec((8, tn), lambda j: (0, j)),
        out_shape=jax.ShapeDtypeStruct((8, n), F32),
        compiler_params=_cparams(("arbitrary",)),
        name="modulation",
    )(cc, w_mod, b_mod.reshape(1, n))


def _stream_specs(stream, n_ctx_tiles, first_tile):
    _, x_src, x_off, _ = stream
    d = x_src.shape[2]
    return [pl.BlockSpec((1, ROW_TILE, d), lambda b, i: (b, jnp.minimum(i + first_tile, n_ctx_tiles - 1), 0)),
            pl.BlockSpec((1, ROW_TILE, d), lambda b, i: (b, jnp.maximum(i + first_tile - n_ctx_tiles, 0) + x_off, 0))]


def _in_kernel(n_ctx_tiles, ctx_row, c_ref, x_ref, mo_ref, g_ref, cos_ref, sin_ref, cost_ref, sint_ref,
               w_ref, wt_ref,
               ml_ref, daq_ref, kt_ref, vx_ref, ss_ref, hg_ref, zg_ref, gt_ref, mkt_ref):
    b = pl.program_id(0)
    i = pl.program_id(1)
    d = x_ref.shape[2]
    r = jnp.where(i < n_ctx_tiles, ctx_row, b)
    mo = mo_ref[pl.ds(r, 1), :]
    shift = mo[:, 0:d]
    scale = mo[:, d:2 * d]
    x = jnp.where(i < n_ctx_tiles, c_ref[0], x_ref[0])
    y = x * lax.rsqrt(jnp.mean(x * x, axis=-1, keepdims=True) + EPS) * g_ref[...]
    h = (y * (1.0 + scale) + shift).astype(BF16)

    ml_ref[0] = _mm(h, w_ref[:, C_ML:C_DAQ])
    q = _mm(h, w_ref[:, C_DAQ:C_DAV])
    half = DA_QK // 2
    lane = _iota2((x.shape[0], LANES), 1)
    q_rot = []
    for c in range(W_BRANCH // LANES):
        qc = q[:, LANES * c:LANES * (c + 1)]
        q_rot.append(jnp.where((lane & half) == 0, -pltpu.roll(qc, LANES - half, 1), pltpu.roll(qc, half, 1)))
    n_rep = W_BRANCH // LANES
    q = (q * jnp.concatenate([cos_ref[...]] * n_rep, axis=1)
         + jnp.concatenate(q_rot, axis=1) * jnp.concatenate([sin_ref[...]] * n_rep, axis=1))
    daq_ref[0] = (q * (DA_QK ** -0.5)).astype(BF16)
    v = _mm(h, w_ref[:, C_DAV:C_SS])
    for hd in range(N_HEADS):
        pair = v[:, LANES * (hd // 2):LANES * (hd // 2 + 1)]
        own = (lane >= HEAD_W) if hd % 2 else (lane < HEAD_W)
        vx_ref[0, 0, hd] = jnp.where(own, pair, jnp.where(lane == (0 if hd % 2 else HEAD_W), 1.0, 0.0)).astype(BF16)
    ss_ref[0] = _mm(h, w_ref[:, C_SS:C_HG])
    hg_ref[0] = _mm(h, w_ref[:, C_HG:C_ZG])
    zg_ref[0] = _mm(h, w_ref[:, C_ZG:C_END]).astype(BF16)
    tr = _mm_nt(wt_ref[...], h)
    kt = tr[0:W_BRANCH]
    kt_rot = []
    for j in range(W_BRANCH // DA_QK):
        kt_rot += [-kt[DA_QK * j + half:DA_QK * (j + 1)], kt[DA_QK * j:DA_QK * j + half]]
    n_grp = W_BRANCH // DA_QK
    kt_ref[0, 0] = (kt * jnp.concatenate([cost_ref[...]] * n_grp, axis=0)
                    + jnp.concatenate(kt_rot, axis=0) * jnp.concatenate([sint_ref[...]] * n_grp, axis=0)).astype(BF16)
    gt_ref[0] = tr[W_BRANCH:W_BRANCH + GATE_ROWS]
    mkt_ref[0] = tr[W_BRANCH + GATE_ROWS:2 * W_BRANCH + GATE_ROWS]


def _in_proj(stream, mo, g, cos, sin, cos_t, sin_t, w2, wt, n_ctx_tiles, ctx_row):
    c_src, x_src, x_off, t = stream
    bsz, _, d = x_src.shape
    tm = ROW_TILE
    assert tm == KEY_TILE
    row = lambda b, i: (b, i, 0)
    const2 = lambda b, i: (0, 0)
    out_shape = [
        jax.ShapeDtypeStruct((bsz, t, 2 * W_BRANCH), F32),
        jax.ShapeDtypeStruct((bsz, t, W_BRANCH), BF16),
        jax.ShapeDtypeStruct((bsz, t // tm, W_BRANCH, tm), BF16),
        jax.ShapeDtypeStruct((bsz, t // tm, N_HEADS, tm, LANES), BF16),
        jax.ShapeDtypeStruct((bsz, t, 768), F32),
        jax.ShapeDtypeStruct((bsz, t, 4 * W_BRANCH), F32),
        jax.ShapeDtypeStruct((bsz, t, 5 * W_BRANCH), BF16),
        jax.ShapeDtypeStruct((bsz, GATE_ROWS, t), F32),
        jax.ShapeDtypeStruct((bsz, W_BRANCH, t), F32),
    ]
    out_specs = [
        pl.BlockSpec((1, tm, 2 * W_BRANCH), row),
        pl.BlockSpec((1, tm, W_BRANCH), row),
        pl.BlockSpec((1, 1, W_BRANCH, tm), lambda b, i: (b, i, 0, 0)),
        pl.BlockSpec((1, 1, N_HEADS, tm, LANES), lambda b, i: (b, i, 0, 0, 0)),
        pl.BlockSpec((1, tm, 768), row),
        pl.BlockSpec((1, tm, 4 * W_BRANCH), row),
        pl.BlockSpec((1, tm, 5 * W_BRANCH), row),
        pl.BlockSpec((1, GATE_ROWS, tm), lambda b, i: (b, 0, i)),
        pl.BlockSpec((1, W_BRANCH, tm), lambda b, i: (b, 0, i)),
    ]
    return pl.pallas_call(
        functools.partial(_in_kernel, n_ctx_tiles, ctx_row),
        grid=(bsz, t // tm),
        in_specs=_stream_specs(stream, n_ctx_tiles, 0) + [
                  pl.BlockSpec(mo.shape, const2),
                  pl.BlockSpec((1, d), const2),
                  pl.BlockSpec((tm, LANES), lambda b, i: (i, 0)),
                  pl.BlockSpec((tm, LANES), lambda b, i: (i, 0)),
                  pl.BlockSpec((DA_QK, tm), lambda b, i: (0, i)),
                  pl.BlockSpec((DA_QK, tm), lambda b, i: (0, i)),
                  pl.BlockSpec(w2.shape, const2),
                  pl.BlockSpec(wt.shape, const2)],
        out_specs=out_specs,
        out_shape=out_shape,
        compiler_params=_cparams(("parallel", "parallel")),
        name="in_proj",
    )(c_src, x_src, mo, g.reshape(1, d), cos, sin, cos_t, sin_t, w2, wt)


def _bwd_chunk(c, n_ctx_chunks, n_chunks):
    return jnp.where(c < n_ctx_chunks, n_ctx_chunks - 1 - c, n_chunks + n_ctx_chunks - 1 - c)


def _tri_masks(d):
    ri = _iota2((CHUNK, CHUNK), 0)
    ci = _iota2((CHUNK, CHUNK), 1)
    if d == 0:
        return ci <= ri, ri <= ci
    return ci >= ri, ri >= ci


def _split_f32(x, n):
    parts = []
    for _ in range(n):
        p = x.astype(BF16).astype(F32)
        parts.append(p)
        x = x - p
    return parts


def _cummax_lanes(u, d):
    lane = _iota2(u.shape, 1)
    ninf = jnp.float32(-jnp.inf)
    s = 1
    while s < CHUNK:
        if d == 0:
            sh = jnp.where(lane >= s, pltpu.roll(u, s, 1), ninf)
        else:
            sh = jnp.where(lane < CHUNK - s, pltpu.roll(u, CHUNK - s, 1), ninf)
        u = jnp.maximum(u, sh)
        s *= 2
    return u


def _state_row_selector(n_tiles):
    keep = (jnp.arange(CHUNK)[None, :] % 8) == jnp.arange(8)[:, None]
    return jnp.broadcast_to(keep[:, :, None], (8, CHUNK, n_tiles * LANES)).astype(BF16)


def _spread_selector(n_spread):
    t = jnp.arange(CHUNK)[:, None] // 16
    tiles = [jnp.broadcast_to(t == 3 + k, (CHUNK, LANES)) for k in range(n_spread)]
    return jnp.concatenate(tiles, axis=1).astype(BF16)


def _spread_operands(col_sum, row_sum, spread, sel_ref, ones_ref):
    z8 = jnp.zeros((8, CHUNK), F32)
    one8 = ones_ref[0:8, :].astype(F32)
    pieces = _split_f32(col_sum, 3) + [one8] * 3
    for x in spread:
        pieces += _split_f32(x, 2)
    assert len(pieces) <= CHUNK // 8
    lhs = jnp.concatenate(pieces + [z8] * (CHUNK // 8 - len(pieces)), axis=0).T.astype(BF16)
    rows = jnp.concatenate([one8] * 3 + _split_f32(row_sum, 3) + [z8] * (CHUNK // 8 - 6), axis=0)
    return lhs, jnp.concatenate([rows.astype(BF16), sel_ref[...]], axis=1)


def _mlstm_steps(qf_ref, vf_ref, ktf_ref, gtf_ref, qb_ref, vb_ref, ktb_ref, gtb_ref, bcol_ref, ones_ref, sel_ref,
                  rsel_ref, hf_ref, hb_ref, cn_ref, m_ref):
    @pl.when(pl.program_id(1) == 0)
    def _():
        cn_ref[...] = jnp.zeros_like(cn_ref)
        m_ref[...] = jnp.zeros_like(m_ref)

    lane = _iota2((CHUNK, LANES), 1)
    feat = _iota2((LANES, CHUNK), 0)
    row8 = _iota2((8, LANES), 0)
    ones_t = ones_ref[...]
    ninf = jnp.float32(-jnp.inf)
    m_all = m_ref[...]
    heads = [None] * 8
    gates = []
    for d in (0, 1):
        q_ref, v_ref, kt_ref = ((qf_ref, vf_ref, ktf_ref), (qb_ref, vb_ref, ktb_ref))[d]
        for p in range(2):
            q_t = q_ref[0, :, 128 * p:128 * p + 128]
            v_t = v_ref[0, :, 128 * p:128 * p + 128]
            kt_p = kt_ref[0, 128 * p:128 * p + 128, :] * (HEAD_W ** -0.5)
            kt_b = kt_p.astype(BF16)
            for half in range(2):
                r = 4 * d + 2 * p + half
                hmask = (lane >= 64) if half else (lane < 64)
                qm = jnp.where(hmask, q_t, 0.0).astype(BF16)
                cn = cn_ref[r]
                heads[r] = dict(
                    cn=cn, kt_p=kt_p,
                    vw=jnp.concatenate([jnp.where(hmask, v_t, 0.0).astype(BF16), ones_t], axis=1),
                    qk=_mm(qm, kt_b),
                    qc=_mm(qm, cn.astype(BF16)))

    yield
    f_cum = []
    for d in (0, 1):
        gt_ref = (gtf_ref, gtb_ref)[d]
        tr = jnp.where(_tri_masks(d)[1], 1.0, 0.0).astype(BF16)
        f_cum.append(_mm_exact_r(_log_sigmoid(gt_ref[0, 8:16, :] + bcol_ref[8:16, :]), tr))

    yield
    for d in (0, 1):
        gt_ref = (gtf_ref, gtb_ref)[d]
        last = CHUNK - 1 if d == 0 else 0
        li = gt_ref[0, 0:8, :] + bcol_ref[0:8, :]
        f = f_cum[d]
        u = li - f
        m_old = m_all
        mx = jnp.maximum(m_old, _cummax_lanes(u, d))
        w_int = jnp.exp(m_old - mx)
        e_mi = jnp.exp(-(f + mx))
        u_max = jnp.max(u, axis=1, keepdims=True)
        e_end = jnp.exp(u - u_max)
        f_last = f[:, last:last + 1]
        b_end = f_last + u_max
        m_new = jnp.maximum(f_last + m_old, b_end)
        a_dec = jnp.exp(f_last + m_old - m_new)
        g_inc = jnp.exp(b_end - m_new)
        m_all = jnp.where((row8 >> 2) == d, m_new, m_all)
        gates.append((e_end, a_dec, g_inc) + _spread_operands(-mx, u, [w_int, e_mi], sel_ref, ones_ref))

    for r in range(8):
        hd = heads[r]
        e_end, a_dec, g_inc, lhs, rhs = gates[r // 4]
        fmask = (feat >= 64) if r % 2 else (feat < 64)
        ke = (jnp.where(fmask, hd["kt_p"], 0.0) * e_end[r:r + 1, :]).astype(BF16)
        a_r = a_dec[r:r + 1, :]
        g_r = g_inc[r:r + 1, :]
        hd["mt"] = _mm(lhs, rhs * rsel_ref[r])
        hd["cn_new"] = (jnp.concatenate([a_r, a_r], axis=1) * hd["cn"]
                        + jnp.concatenate([g_r, g_r], axis=1) * _mm(ke, hd["vw"]))

    yield
    for r in range(8):
        hd = heads[r]
        vis, _ = _tri_masks(r // 4)
        s = hd["qk"] * jnp.exp(jnp.where(vis, hd["mt"][:, 0:128], ninf))
        hd["sv"] = _mm(s.astype(BF16), hd["vw"])

    yield
    for d in (0, 1):
        out_ref = (hf_ref, hb_ref)[d]
        for p in range(2):
            pair = []
            for half in range(2):
                hd = heads[4 * d + 2 * p + half]
                w_b = hd["mt"][:, 128:256]
                tot = hd["sv"] + jnp.concatenate([w_b, w_b], axis=1) * hd["qc"]
                pair.append(tot[:, 0:128] / jnp.maximum(jnp.abs(tot[:, 128:256]), hd["mt"][:, 256:384]))
            out_ref[0, :, 128 * p:128 * p + 128] = jnp.where(lane < 64, pair[0], pair[1]).astype(BF16)
    m_ref[...] = m_all
    for r in range(8):
        cn_ref[r] = heads[r]["cn_new"]


def _mlstm(ml, mkt, gt, bcol, n_ctx_chunks):
    bsz, t, _ = ml.shape
    nc = t // CHUNK
    out = jax.ShapeDtypeStruct((bsz, t, W_BRANCH), BF16)
    const2 = lambda b, c: (0, 0)

    def chunk_specs(chunk):
        return [pl.BlockSpec((1, CHUNK, W_BRANCH), lambda b, c: (b, chunk(c), 0)),
                pl.BlockSpec((1, CHUNK, W_BRANCH), lambda b, c: (b, chunk(c), 1)),
                pl.BlockSpec((1, W_BRANCH, CHUNK), lambda b, c: (b, 0, chunk(c))),
                pl.BlockSpec((1, GATE_ROWS, CHUNK), lambda b, c: (b, 0, chunk(c)))]

    fwd_chunk = lambda c: c
    bwd_chunk = lambda c: _bwd_chunk(c, n_ctx_chunks, nc)
    return dict(
        steps=_mlstm_steps,
        in_specs=chunk_specs(fwd_chunk) + chunk_specs(bwd_chunk) + [
            pl.BlockSpec((GATE_ROWS, 1), const2), pl.BlockSpec((CHUNK, LANES), const2),
            pl.BlockSpec((CHUNK, 2 * LANES), const2),
            pl.BlockSpec((8, CHUNK, 3 * LANES), lambda b, c: (0, 0, 0))],
        out_specs=[pl.BlockSpec((1, CHUNK, W_BRANCH), lambda b, c: (b, c, 0)),
                   pl.BlockSpec((1, CHUNK, W_BRANCH), lambda b, c: (b, bwd_chunk(c), 0))],
        out_shape=[out, out],
        scratch_shapes=[pltpu.VMEM((8, LANES, 2 * LANES), F32), pltpu.VMEM((8, LANES), F32)],
        args=(ml, ml, mkt, gt, ml, ml, mkt, gt, bcol, jnp.ones((CHUNK, LANES), BF16), _spread_selector(2),
              _state_row_selector(3)))


def _ssd_steps(n_ctx_chunks, n_chunks,
                xf_ref, xfp_ref, xfn_ref, xb_ref, xbp_ref, xbn_ref,
                gtf_ref, gtb_ref, bcol_ref, acol_ref, cw_ref, cb_ref, dskip_ref, ones_ref, sel_ref, rsel_ref,
                yf_ref, yb_ref, s_ref):
    c = pl.program_id(1)

    @pl.when(c == 0)
    def _():
        s_ref[...] = jnp.zeros_like(s_ref)

    lane = _iota2((CHUNK, LANES), 1)
    row768 = _iota2((CHUNK, 768), 0)
    ninf = jnp.float32(-jnp.inf)

    dt, a_cum = [], []
    for d in (0, 1):
        gt_ref = (gtf_ref, gtb_ref)[d]
        tr = jnp.where(_tri_masks(d)[1], 1.0, 0.0).astype(BF16)
        dt.append(_softplus(gt_ref[0, 16:24, :] + bcol_ref[16:24, :]))
        a_cum.append(_mm_exact_r(dt[d] * acol_ref[16:24, :], tr))

    yield
    xa = []
    for d in (0, 1):
        x_ref, xp_ref, xn_ref = ((xf_ref, xfp_ref, xfn_ref), (xb_ref, xbp_ref, xbn_ref))[d]
        j = c if d == 0 else _bwd_chunk(c, n_ctx_chunks, n_chunks)
        seg_first = jnp.logical_or(j == 0, j == n_ctx_chunks)
        seg_last = jnp.logical_or(j == n_ctx_chunks - 1, j == n_chunks - 1)
        x = x_ref[0]
        prev = jnp.where(seg_first, 0.0, xp_ref[0, 7:8, :])
        nxt = jnp.where(seg_last, 0.0, xn_ref[0, 0:1, :])
        x_dn = jnp.where(row768 == 0, prev, pltpu.roll(x, 1, 0))
        x_up = jnp.where(row768 == CHUNK - 1, nxt, pltpu.roll(x, CHUNK - 1, 0))
        xa.append(_silu(x_dn * cw_ref[0:1, :] + x * cw_ref[1:2, :] + x_up * cw_ref[2:3, :] + cb_ref[...]))

    heads = [None] * 8
    groups = {}
    for d in (0, 1):
        for g in range(2):
            x_pair = xa[d][:, 128 * g:128 * g + 128]
            b_g = xa[d][:, 256 + 128 * g:256 + 128 * g + 128]
            c_gb = xa[d][:, 512 + 128 * g:512 + 128 * g + 128].astype(BF16)
            groups[d, g] = dict(x_pair=x_pair, bt=b_g.T.astype(BF16), cb=_mm_nt(c_gb, b_g.astype(BF16)))
            for half in range(2):
                r = 4 * d + 2 * g + half
                hmask = (lane >= 64) if half else (lane < 64)
                st = s_ref[r]
                xm = jnp.where(hmask, x_pair, 0.0)
                heads[r] = dict(st=st, xm=xm, xmb=xm.astype(BF16), cs=_mm(c_gb, st.astype(BF16)))

    yield
    for d in (0, 1):
        last = CHUNK - 1 if d == 0 else 0
        a_last = a_cum[d][:, last:last + 1]
        w_in = jnp.exp(a_last - a_cum[d]) * dt[d]
        e_last = jnp.exp(jnp.broadcast_to(a_last, (8, CHUNK)))
        lhs, rhs = _spread_operands(a_cum[d], -a_cum[d], [jnp.exp(a_cum[d]), w_in], sel_ref, ones_ref)
        for h in range(N_HEADS):
            r = 4 * d + h
            heads[r]["mt"] = _mm(lhs, rhs * rsel_ref[r])
            heads[r]["e_last"] = e_last[r:r + 1, :]

    yield
    for r in range(8):
        d, h = divmod(r, 4)
        hd = heads[r]
        ds = _mm(groups[d, h // 2]["bt"], (hd["xm"] * hd["mt"][:, 256:384]).astype(BF16))
        s_ref[r] = hd["e_last"] * hd["st"] + ds
    for r in range(8):
        d, h = divmod(r, 4)
        hd = heads[r]
        vis, _ = _tri_masks(d)
        sc = groups[d, h // 2]["cb"] * jnp.exp(jnp.where(vis, hd["mt"][:, 0:128], ninf)) * dt[d][r:r + 1, :]
        hd["y"] = _mm(sc.astype(BF16), hd["xmb"]) + hd["mt"][:, 128:256] * hd["cs"]

    for d in (0, 1):
        out_ref = (yf_ref, yb_ref)[d]
        for g in range(2):
            y_pair = heads[4 * d + 2 * g]["y"] + heads[4 * d + 2 * g + 1]["y"]
            if d == 0:
                y_pair = y_pair + dskip_ref[:, 128 * g:128 * g + 128] * groups[d, g]["x_pair"]
            out_ref[0, :, 128 * g:128 * g + 128] = y_pair.astype(BF16)


def _ssd(ss, gt, bcol, acol, conv_w, conv_b, dskip, n_ctx_chunks):
    bsz, t, _ = ss.shape
    nc = t // CHUNK
    sub = CHUNK // 8
    nsub = t // 8
    bc = lambda c: _bwd_chunk(c, n_ctx_chunks, nc)
    fwd = lambda b, c: (b, c, 0)
    bwd = lambda b, c: (b, bc(c), 0)
    fwd_p = lambda b, c: (b, jnp.maximum(c * sub - 1, 0), 0)
    fwd_n = lambda b, c: (b, jnp.minimum((c + 1) * sub, nsub - 1), 0)
    bwd_p = lambda b, c: (b, jnp.maximum(bc(c) * sub - 1, 0), 0)
    bwd_n = lambda b, c: (b, jnp.minimum((bc(c) + 1) * sub, nsub - 1), 0)
    fwd_t = lambda b, c: (b, 0, c)
    bwd_t = lambda b, c: (b, 0, bc(c))
    const2 = lambda b, c: (0, 0)
    out = jax.ShapeDtypeStruct((bsz, t, W_BRANCH), BF16)
    return dict(
        steps=functools.partial(_ssd_steps, n_ctx_chunks, nc),
        in_specs=[pl.BlockSpec((1, CHUNK, 768), fwd), pl.BlockSpec((1, 8, 768), fwd_p),
                  pl.BlockSpec((1, 8, 768), fwd_n),
                  pl.BlockSpec((1, CHUNK, 768), bwd), pl.BlockSpec((1, 8, 768), bwd_p),
                  pl.BlockSpec((1, 8, 768), bwd_n),
                  pl.BlockSpec((1, GATE_ROWS, CHUNK), fwd_t), pl.BlockSpec((1, GATE_ROWS, CHUNK), bwd_t),
                  pl.BlockSpec((GATE_ROWS, 1), const2), pl.BlockSpec((GATE_ROWS, 1), const2),
                  pl.BlockSpec((3, 768), const2), pl.BlockSpec((1, 768), const2),
                  pl.BlockSpec((1, W_BRANCH), const2),
                  pl.BlockSpec((CHUNK, LANES), const2), pl.BlockSpec((CHUNK, 2 * LANES), const2),
                  pl.BlockSpec((8, CHUNK, 3 * LANES), lambda b, c: (0, 0, 0))],
        out_specs=[pl.BlockSpec((1, CHUNK, W_BRANCH), fwd), pl.BlockSpec((1, CHUNK, W_BRANCH), bwd)],
        out_shape=[out, out],
        scratch_shapes=[pltpu.VMEM((8, SSM_N, LANES), F32)],
        args=(ss, ss, ss, ss, ss, ss, gt, gt, bcol, acol, conv_w, conv_b, dskip,
              jnp.ones((CHUNK, LANES), BF16), _spread_selector(2), _state_row_selector(3)))


def _hgrn2_steps(hf_ref, hb_ref, loglb_ref, log1m_ref, onem_ref, e_ref, of_ref, ob_ref,
                  st_ref):
    @pl.when(pl.program_id(1) == 0)
    def _():
        st_ref[...] = jnp.zeros_like(st_ref)

    ri = _iota2((CHUNK, CHUNK), 0)
    ci = _iota2((CHUNK, CHUNK), 1)
    lane = _iota2((CHUNK, LANES), 1)
    rw = _iota2((CHUNK, W_BRANCH), 0)
    blockdiag = (ri >> 6) == (ci >> 6)
    ninf = jnp.float32(-jnp.inf)
    nb = CHUNK // SUB
    rs = _iota2((nb, SUB, W_BRANCH), 1)

    dirs = []
    for d in (0, 1):
        h_ref = (hf_ref, hb_ref)[d]
        tc = jnp.where(_tri_masks(d)[0], 1.0, 0.0).astype(BF16)
        z = h_ref[0, :, W_BRANCH * (1 + d):W_BRANCH * (2 + d)]
        e = jnp.exp(-jnp.abs(z))
        ope = 1.0 + e
        la = loglb_ref[d:d + 1, :]
        lb_ = log1m_ref[d:d + 1, :] + (jnp.minimum(z, 0.0) - jnp.log(ope))
        logf = jnp.maximum(la, lb_) + jnp.log(1.0 + jnp.exp(-jnp.abs(la - lb_)))
        dirs.append(dict(
            q=_silu(h_ref[0, :, 0:W_BRANCH]),
            v=h_ref[0, :, 3 * W_BRANCH:4 * W_BRANCH],
            kk=onem_ref[d:d + 1, :] * (jnp.where(z >= 0.0, e, 1.0) / ope),
            gcum=_mm_exact_l(tc, logf)))

    yield
    for d in (0, 1):
        dd = dirs[d]
        q, kk, gcum = dd["q"], dd["kk"], dd["gcum"]
        last = CHUNK - 1 if d == 0 else 0
        g_last = gcum[last:last + 1, :]
        qg = (q * jnp.exp(gcum)).astype(BF16)
        kg = (kk * jnp.exp(g_last - gcum)).astype(BF16)
        vb = dd["v"].astype(BF16)

        dd["o_inter"] = []
        for p in range(2):
            sl = slice(128 * p, 128 * p + 128)
            st = st_ref[2 * d + p]
            dd["o_inter"].append(_mm_nt(qg[:, sl], st.astype(BF16)))
            dst = _mm_tn(vb[:, sl], kg[:, sl])
            st_ref[2 * d + p] = st * jnp.exp(g_last[:, sl]) + jnp.where(blockdiag, dst, 0.0)

        a_mats = [jnp.zeros((CHUNK, CHUNK), F32) for _ in range(N_HEADS)]
        blk = CHUNK // 2
        while blk >= SUB:
            first = (rw & (2 * blk - 1)) < blk
            edge = (blk - 1) if d == 0 else blk
            gb = gcum.reshape(CHUNK // (2 * blk), 2 * blk, W_BRANCH)[:, edge:edge + 1, :]
            gb = jnp.broadcast_to(gb, (CHUNK // (2 * blk), 2 * blk, W_BRANCH)).reshape(CHUNK, W_BRANCH)
            q_side = jnp.logical_not(first) if d == 0 else first
            qt = q * jnp.exp(jnp.where(q_side, gcum - gb, ninf))
            kt = (kk * jnp.exp(jnp.where(q_side, ninf, gb - gcum))).astype(BF16)
            same = (ri >> int(math.log2(2 * blk))) == (ci >> int(math.log2(2 * blk)))
            for h in range(N_HEADS):
                p, half = divmod(h, 2)
                hmask = (lane >= 64) if half else (lane < 64)
                qh = jnp.where(hmask, qt[:, 128 * p:128 * p + 128], 0.0).astype(BF16)
                a_mats[h] = a_mats[h] + jnp.where(same, _mm_nt(qh, kt[:, 128 * p:128 * p + 128]), 0.0)
            blk //= 2
        dd["a_mats"] = a_mats

        g3 = gcum.reshape(nb, SUB, W_BRANCH)
        q3 = q.reshape(nb, SUB, W_BRANCH)
        k3 = kk.reshape(nb, SUB, W_BRANCH)
        v3 = dd["v"].reshape(nb, SUB, W_BRANCH)
        o3 = None
        for j in range(SUB):
            ok = (rs >= j) if d == 0 else (rs <= j)
            pj = q3 * jnp.exp(jnp.where(ok, g3 - g3[:, j:j + 1, :], ninf)) * k3[:, j:j + 1, :]
            red = _mm(pj.reshape(CHUNK, W_BRANCH).astype(BF16), e_ref[...])
            term = red.reshape(nb, SUB, W_BRANCH) * v3[:, j:j + 1, :]
            o3 = term if o3 is None else o3 + term
        dd["o_diag"] = o3.reshape(CHUNK, W_BRANCH)

    yield
    for d in (0, 1):
        dd = dirs[d]
        out_ref = (of_ref, ob_ref)[d]
        for p in range(2):
            sl = slice(128 * p, 128 * p + 128)
            a_cat = jnp.concatenate([dd["a_mats"][2 * p], dd["a_mats"][2 * p + 1]], axis=1).astype(BF16)
            v_p = dd["v"][:, sl]
            v_cat = jnp.concatenate([jnp.where(lane < 64, v_p, 0.0), jnp.where(lane >= 64, v_p, 0.0)],
                                    axis=0).astype(BF16)
            out_ref[0, :, sl] = (dd["o_inter"][p] + _mm(a_cat, v_cat) + dd["o_diag"][:, sl]).astype(BF16)


def _hgrn2(hg, loglb, log1m, onem, e64, n_ctx_chunks):
    bsz, t, _ = hg.shape
    nc = t // CHUNK
    fwd = lambda b, c: (b, c, 0)
    bwd = lambda b, c: (b, _bwd_chunk(c, n_ctx_chunks, nc), 0)
    const2 = lambda b, c: (0, 0)
    out = jax.ShapeDtypeStruct((bsz, t, W_BRANCH), BF16)
    return dict(
        steps=_hgrn2_steps,
        in_specs=[pl.BlockSpec((1, CHUNK, 1024), fwd), pl.BlockSpec((1, CHUNK, 1024), bwd),
                  pl.BlockSpec((2, W_BRANCH), const2), pl.BlockSpec((2, W_BRANCH), const2),
                  pl.BlockSpec((2, W_BRANCH), const2), pl.BlockSpec((W_BRANCH, W_BRANCH), const2)],
        out_specs=[pl.BlockSpec((1, CHUNK, W_BRANCH), fwd), pl.BlockSpec((1, CHUNK, W_BRANCH), bwd)],
        out_shape=[out, out],
        scratch_shapes=[pltpu.VMEM((4, LANES, LANES), F32)],
        args=(hg, hg, loglb, log1m, onem, e64))


def _scan_kernel(parts, *refs):
    n_in = [len(p["in_specs"]) for p in parts]
    n_out = [len(p["out_specs"]) for p in parts]
    n_scr = [len(p["scratch_shapes"]) for p in parts]
    ins, outs, scr = refs[:sum(n_in)], refs[sum(n_in):sum(n_in) + sum(n_out)], refs[sum(n_in) + sum(n_out):]
    gens = []
    for k, p in enumerate(parts):
        mine = (ins[sum(n_in[:k]):sum(n_in[:k + 1])] + outs[sum(n_out[:k]):sum(n_out[:k + 1])]
                + scr[sum(n_scr[:k]):sum(n_scr[:k + 1])])
        gens.append(p["steps"](*mine))
    while gens:
        for g in list(gens):
            if next(g, gens) is gens:
                gens.remove(g)


def _chunk_scans(parts, bsz, n_chunks):
    return pl.pallas_call(
        functools.partial(_scan_kernel, parts),
        grid=(bsz, n_chunks),
        in_specs=[s for p in parts for s in p["in_specs"]],
        out_specs=[s for p in parts for s in p["out_specs"]],
        out_shape=[s for p in parts for s in p["out_shape"]],
        scratch_shapes=[s for p in parts for s in p["scratch_shapes"]],
        compiler_params=_cparams(("parallel", "arbitrary")),
        name="chunk_scans",
    )(*[a for p in parts for a in p["args"]])


def _attn(daq, kt, vx, lam_vecs, lam_init, n_ctx_tiles, skip_ctx):
    bsz, t, _ = daq.shape
    tq = ROW_TILE
    n_k_tiles = t // KEY_TILE
    n_lat_tiles = n_k_tiles - n_ctx_tiles
    group = math.gcd(KEY_GROUP, n_lat_tiles)
    q_off = n_ctx_tiles if skip_ctx else 0
    nq = t // tq - q_off
    n_maps = W_BRANCH // DA_QK
    rows = n_maps * tq

    def kern(q_ref, kt_ref, vx_ref, lv_ref, qsel_ref, o_ref, q8_ref, m_ref, acc_ref, s_ref, mx_ref):
        qi = pl.program_id(1) + q_off
        q = q_ref[0]
        for j in range(n_maps):
            q8_ref[j * tq:(j + 1) * tq, :] = q * qsel_ref[j:j + 1, :]

        def score_tile(t):
            return _mm(q8_ref[...], kt_ref[0, t])

        def update(t0, n, after_scores=None):
            s = [score_tile(t0 + i) for i in range(n)]
            if after_scores is not None:
                after_scores()
            mx = s[0][:, 0:LANES]
            for i in range(n):
                for c in range(KEY_TILE // LANES):
                    if i or c:
                        mx = jnp.maximum(mx, s[i][:, LANES * c:LANES * (c + 1)])
            m_new = jnp.broadcast_to(jnp.max(mx, axis=1, keepdims=True), mx.shape)
            m_ref[...] = m_new
            m2 = jnp.concatenate([m_new] * (KEY_TILE // LANES), axis=1)
            p = [jnp.exp(s[i] - m2).astype(BF16) for i in range(n)]
            for hd in range(N_HEADS):
                r0 = 2 * hd * tq
                pv = _mm(p[0][r0:r0 + 2 * tq], vx_ref[0, t0, hd])
                for i in range(1, n):
                    pv = pv + _mm(p[i][r0:r0 + 2 * tq], vx_ref[0, t0 + i, hd])
                acc_ref[r0:r0 + 2 * tq, :] = pv

        def scores(g, slot):
            t0 = n_ctx_tiles + g * group
            mx = None
            for i in range(group):
                s = score_tile(t0 + i)
                s_ref[slot, :, KEY_TILE * i:KEY_TILE * (i + 1)] = s
                for c in range(KEY_TILE // LANES):
                    part = s[:, LANES * c:LANES * (c + 1)]
                    mx = part if mx is None else jnp.maximum(mx, part)
            mx_ref[slot] = mx

        def absorb(g, slot):
            t0 = n_ctx_tiles + g * group
            m_old = m_ref[...]
            m_new = jnp.maximum(m_old, jnp.max(mx_ref[slot], axis=1, keepdims=True))
            alpha = jnp.exp(m_old - m_new)
            m_ref[...] = m_new
            m2 = jnp.concatenate([m_new] * (KEY_TILE // LANES), axis=1)
            p = [jnp.exp(s_ref[slot, :, KEY_TILE * i:KEY_TILE * (i + 1)] - m2).astype(BF16) for i in range(group)]
            for hd in range(N_HEADS):
                r0 = 2 * hd * tq
                pv = _mm(p[0][r0:r0 + 2 * tq], vx_ref[0, t0, hd])
                for i in range(1, group):
                    pv = pv + _mm(p[i][r0:r0 + 2 * tq], vx_ref[0, t0 + i, hd])
                acc_ref[r0:r0 + 2 * tq, :] = alpha[r0:r0 + 2 * tq] * acc_ref[r0:r0 + 2 * tq, :] + pv

        @pl.when(qi < n_ctx_tiles)
        def _():
            update(0, n_ctx_tiles)

        @pl.when(qi >= n_ctx_tiles)
        def _():
            n_groups = n_lat_tiles // group
            n_pairs = (n_groups - 1) // 2
            update(0, n_ctx_tiles, after_scores=lambda: scores(0, 0))

            def body(k, carry):
                g = 2 * k
                scores(g + 1, 1)
                absorb(g, 0)
                scores(g + 2, 0)
                absorb(g + 1, 1)
                return carry
            for k in range(n_pairs):
                body(k, 0)
            g0 = 2 * n_pairs
            if n_groups - g0 == 2:
                scores(g0 + 1, 1)
            absorb(g0, 0)
            if n_groups - g0 == 2:
                absorb(g0 + 1, 1)

        lv = lv_ref[...]
        lam = (jnp.exp(jnp.sum(lv[0:1, :] * lv[1:2, :], axis=1, keepdims=True))
               - jnp.exp(jnp.sum(lv[2:3, :] * lv[3:4, :], axis=1, keepdims=True)) + lam_init)
        lane1 = _iota2((tq, LANES), 1)
        for pr in range(N_HEADS // 2):
            halves = []
            for half in range(2):
                r0 = 2 * (2 * pr + half) * tq
                one = 0 if half else HEAD_W
                a0 = acc_ref[r0:r0 + tq, :]
                a1 = acc_ref[r0 + tq:r0 + 2 * tq, :]
                halves.append(a0 / a0[:, one:one + 1] - lam * (a1 / a1[:, one:one + 1]))
            o_ref[0, :, LANES * pr:LANES * (pr + 1)] = jnp.where(lane1 < HEAD_W, halves[0], halves[1]).astype(BF16)

    return pl.pallas_call(
        kern,
        grid=(bsz, nq),
        in_specs=[pl.BlockSpec((1, tq, W_BRANCH), lambda b, i: (b, i + q_off, 0)),
                  pl.BlockSpec((1, n_k_tiles, W_BRANCH, KEY_TILE), lambda b, i: (b, 0, 0, 0),
                               pipeline_mode=pl.Buffered(1)),
                  pl.BlockSpec((1, n_k_tiles, N_HEADS, KEY_TILE, LANES), lambda b, i: (b, 0, 0, 0, 0),
                               pipeline_mode=pl.Buffered(1)),
                  pl.BlockSpec(lam_vecs.shape, lambda b, i: (0, 0)),
                  pl.BlockSpec((n_maps, W_BRANCH), lambda b, i: (0, 0))],
        out_specs=pl.BlockSpec((1, tq, W_BRANCH), lambda b, i: (b, i, 0)),
        out_shape=jax.ShapeDtypeStruct((bsz, nq * tq, W_BRANCH), BF16),
        scratch_shapes=[pltpu.VMEM((rows, W_BRANCH), BF16),
                        pltpu.VMEM((rows, LANES), F32),
                        pltpu.VMEM((rows, LANES), F32),
                        pltpu.VMEM((2, rows, group * KEY_TILE), F32),
                        pltpu.VMEM((2, rows, LANES), F32)],
        compiler_params=_cparams(("parallel", "parallel")),
        name="diff_attn",
    )(daq, kt, vx, lam_vecs, (jnp.arange(W_BRANCH)[None, :] // DA_QK == jnp.arange(n_maps)[:, None]).astype(BF16))


def _seg_mean(x, e_ref):
    x0 = x.astype(BF16)
    x1 = (x - x0.astype(F32)).astype(BF16)
    return (_mm(x0, e_ref[...]) + _mm(x1, e_ref[...])) * (1.0 / HEAD_W)


def _out_kernel(n_ctx_tiles, ctx_row, q_off, lam_init, final,
                c_ref, x_ref, mo_ref, mlo_ref, mlz_ref, mhf_ref, mhb_ref, dao_ref, daz_ref,
                syf_ref, syb_ref, ssz_ref, hof_ref, hob_ref, hgz_ref,
                mlg_ref, dag_ref, ssg_ref, hgg_ref, e_ref, w_ref, fg_ref, o_ref):
    b = pl.program_id(0)
    i = pl.program_id(1) + q_off
    d = x_ref.shape[2]
    r = jnp.where(i < n_ctx_tiles, ctx_row, b)
    gate = mo_ref[pl.ds(r, 1), :][:, 2 * d:3 * d]

    u = _sigmoid(mlo_ref[0].astype(F32)) * (mhf_ref[0].astype(F32) + mhb_ref[0].astype(F32))
    dev = u - _seg_mean(u, e_ref)
    y_ml = dev * lax.rsqrt(_seg_mean(dev * dev, e_ref) + EPS) * mlg_ref[...] * _silu(mlz_ref[0].astype(F32))

    o = dao_ref[0].astype(F32)
    y_da = (o * lax.rsqrt(_seg_mean(o * o, e_ref) + EPS) * dag_ref[...]) * (1.0 - lam_init) * _silu(daz_ref[0].astype(F32))

    ys = (syf_ref[0].astype(F32) + syb_ref[0].astype(F32)) * _silu(ssz_ref[0].astype(F32))
    parts = []
    for g in range(2):
        yg = ys[:, 128 * g:128 * g + 128]
        parts.append(yg * lax.rsqrt(jnp.mean(yg * yg, axis=-1, keepdims=True) + EPS))
    y_ss = jnp.concatenate(parts, axis=1) * ssg_ref[...]

    oh = hof_ref[0].astype(F32) + hob_ref[0].astype(F32)
    y_hg = (oh * lax.rsqrt(_seg_mean(oh * oh, e_ref) + EPS) * hgg_ref[...]) * _silu(hgz_ref[0].astype(F32))

    acc = _mm(y_ml.astype(BF16), w_ref[0:W_BRANCH, :])
    acc = acc + _mm(y_da.astype(BF16), w_ref[W_BRANCH:2 * W_BRANCH, :])
    acc = acc + _mm(y_ss.astype(BF16), w_ref[2 * W_BRANCH:3 * W_BRANCH, :])
    acc = acc + _mm(y_hg.astype(BF16), w_ref[3 * W_BRANCH:4 * W_BRANCH, :])
    x_new = jnp.where(i < n_ctx_tiles, c_ref[0], x_ref[0]) + gate * acc
    if final:
        x_new = x_new * lax.rsqrt(jnp.mean(x_new * x_new, axis=-1, keepdims=True) + EPS) * fg_ref[...]
    o_ref[0] = x_new


def _out_proj(stream, mo, zg, mhf, mhb, dao, syf, syb, hof, hob,
              mlg, dag, ssg, hgg, e64, w_out, final_g, lam_init, n_ctx_tiles, ctx_row, final):
    c_src, x_src, x_off, t = stream
    bsz, _, d = x_src.shape
    tm = ROW_TILE
    q_off = n_ctx_tiles if final else 0
    nrow = t // tm - q_off
    row = lambda b, i: (b, i + q_off, 0)
    col = lambda k: (lambda b, i: (b, i + q_off, k))
    const2 = lambda b, i: (0, 0)
    wb = pl.BlockSpec((1, tm, W_BRANCH), row)
    return pl.pallas_call(
        functools.partial(_out_kernel, n_ctx_tiles, ctx_row, q_off, lam_init, final),
        grid=(bsz, nrow),
        in_specs=_stream_specs(stream, n_ctx_tiles, q_off) + [
                  pl.BlockSpec(mo.shape, const2),
                  pl.BlockSpec((1, tm, W_BRANCH), col(0)), pl.BlockSpec((1, tm, W_BRANCH), col(1)),
                  wb, wb,
                  pl.BlockSpec((1, tm, W_BRANCH), lambda b, i: (b, i, 0)) if final else wb,
                  pl.BlockSpec((1, tm, W_BRANCH), col(2)),
                  wb, wb, pl.BlockSpec((1, tm, W_BRANCH), col(3)),
                  wb, wb, pl.BlockSpec((1, tm, W_BRANCH), col(4)),
                  pl.BlockSpec((1, W_BRANCH), const2), pl.BlockSpec((1, W_BRANCH), const2),
                  pl.BlockSpec((1, W_BRANCH), const2), pl.BlockSpec((1, W_BRANCH), const2),
                  pl.BlockSpec((W_BRANCH, W_BRANCH), const2),
                  pl.BlockSpec(w_out.shape, const2),
                  pl.BlockSpec((1, d), const2)],
        out_specs=pl.BlockSpec((1, tm, d), lambda b, i: (b, i, 0)),
        out_shape=jax.ShapeDtypeStruct((bsz, nrow * tm, d), F32),
        compiler_params=_cparams(("parallel", "parallel")),
        name="out_proj",
    )(c_src, x_src, mo, zg, zg, mhf, mhb, dao, zg, syf, syb, zg, hof, hob, zg,
      mlg, dag, ssg, hgg, e64, w_out, final_g.reshape(1, d))


def _relayout_w_in(w_in):
    d = w_in.shape[0]
    o = {}
    off = 0
    for name, n in (("ml_q", 256), ("ml_k", 256), ("ml_v", 256), ("ml_o", 256), ("ml_i", 8), ("ml_f", 8),
                    ("ml_z", 256), ("da_q", 256), ("da_k", 256), ("da_v", 256), ("da_z", 256),
                    ("ss_xbc", 768), ("ss_dt", 8), ("ss_z", 256),
                    ("hg_q", 256), ("hg_f", 512), ("hg_i", 256), ("hg_z", 256)):
        o[name] = w_in[:, off:off + n]
        off += n

    gates = jnp.concatenate([o["ml_i"], o["ml_f"], o["ss_dt"], jnp.zeros((d, GATE_ROWS - 24), w_in.dtype)], axis=1)
    w2 = jnp.concatenate([o["ml_q"], o["ml_v"], o["da_q"], o["da_v"],
                          o["ss_xbc"], o["hg_q"], o["hg_f"], o["hg_i"],
                          o["ml_o"], o["ml_z"], o["da_z"], o["ss_z"], o["hg_z"]], axis=1)
    wt = jnp.concatenate([o["da_k"], gates, o["ml_k"]], axis=1).T
    return w2.astype(BF16), wt.astype(BF16)


def _rope_tables(n_ctx, seq):
    pos = jnp.arange(seq)
    rows = (pos // GRID_W).astype(F32)
    cols = (pos % GRID_W).astype(F32)
    axis = DA_QK // 2
    inv = ROPE_BASE ** (-jnp.arange(0, axis, 2, dtype=F32) / axis)
    ang = jnp.concatenate([rows[:, None] * inv, cols[:, None] * inv], axis=-1)
    cos = jnp.concatenate([jnp.ones((n_ctx, axis), F32), jnp.cos(ang)], axis=0)
    sin = jnp.concatenate([jnp.zeros((n_ctx, axis), F32), jnp.sin(ang)], axis=0)
    cos_t = jnp.concatenate([jnp.ones((axis, n_ctx), F32), jnp.cos(ang.T)], axis=1)
    sin_t = jnp.concatenate([jnp.zeros((axis, n_ctx), F32), jnp.sin(ang.T)], axis=1)
    return (jnp.tile(cos, (1, LANES // axis)), jnp.tile(sin, (1, LANES // axis)),
            jnp.tile(cos_t, (2, 1)), jnp.tile(sin_t, (2, 1)))


def _gate_col(vals):
    v = jnp.concatenate([vals.astype(F32), jnp.zeros((GATE_ROWS - vals.shape[0],), F32)])
    return v.reshape(GATE_ROWS, 1)


def kernel(x, c, ctx, c_ctx, w_mod, b_mod, norm_g, w_in, w_out, ml_gate_b, ml_norm_g, da_lambda, da_norm_g,
           ss_conv_w, ss_conv_b, ss_dt_bias, ss_a_log, ss_d, ss_norm_g, hg_lower, hg_norm_g, final_g):
    bsz, seq, d = x.shape
    n_ctx = ctx.shape[1]
    depth = w_mod.shape[0]
    assert n_ctx % ROW_TILE == 0 and seq % ROW_TILE == 0 and bsz < 8
    n_ctx_tiles = n_ctx // ROW_TILE
    n_ctx_chunks = n_ctx // CHUNK
    ctx_row = bsz

    stream = (ctx, x, 0, n_ctx + seq)
    cc = jnp.concatenate([c, c_ctx[None, :], jnp.zeros((8 - bsz - 1, d), F32)], axis=0)
    cos, sin, cos_t, sin_t = _rope_tables(n_ctx, seq)
    lb_all = jnp.cumsum(jax.nn.softmax(hg_lower.astype(F32), axis=1), axis=1)
    lb_all = lb_all - lb_all[:, :1]
    hid = _iota2((W_BRANCH, W_BRANCH), 0) // HEAD_W
    e64 = (hid == hid.T).astype(BF16)

    out = None
    for l in range(depth):
        lam_init = 0.8 - 0.6 * math.exp(-0.3 * l)
        final = l == depth - 1
        w2, wt = _relayout_w_in(w_in[l])
        mo = _modulation(cc, w_mod[l], b_mod[l])
        ml, daq, dak, dav, ss, hg, zg, gt, mkt = _in_proj(stream, mo, norm_g[l], cos, sin, cos_t, sin_t, w2, wt,
                                                     n_ctx_tiles, ctx_row)

        gb = ml_gate_b[l]
        zeros8 = jnp.zeros((8,), F32)
        bcol = _gate_col(jnp.concatenate([gb[:, 0].reshape(-1), gb[:, 1].reshape(-1),
                                                ss_dt_bias[l].reshape(-1)]))
        acol = _gate_col(jnp.concatenate([zeros8, zeros8, -jnp.exp(ss_a_log[l].astype(F32)).reshape(-1)]))
        dao = _attn(daq, dak, dav, da_lambda[l].astype(F32), lam_init, n_ctx_tiles, final)
        dskip = jnp.repeat(ss_d[l].astype(F32), HEAD_W).reshape(1, W_BRANCH)
        lbh = lb_all[:, l]
        scans = [_mlstm(ml, mkt, gt, bcol, n_ctx_chunks),
                 _ssd(ss, gt, bcol, acol, ss_conv_w[l], ss_conv_b[l].reshape(1, -1), dskip, n_ctx_chunks),
                 _hgrn2(hg, jnp.log(lbh), jnp.log1p(-lbh), 1.0 - lbh, e64, n_ctx_chunks)]
        mhf, mhb, syf, syb, hof, hob = _chunk_scans(scans, bsz, (n_ctx + seq) // CHUNK)

        res = _out_proj(stream, mo, zg, mhf, mhb, dao, syf, syb, hof, hob,
                        ml_norm_g[l].reshape(1, -1), jnp.tile(da_norm_g[l], N_HEADS).reshape(1, -1),
                        ss_norm_g[l].reshape(1, -1), hg_norm_g[l].reshape(1, -1), e64,
                        w_out[l].astype(BF16), final_g, lam_init, n_ctx_tiles, ctx_row, final)
        if final:
            out = res
        else:
            stream = (res, res, n_ctx_tiles, n_ctx + seq)
    return out
```
